```python
import jax, jax.numpy as jnp
from jax import lax
import numpy as np

D_MODEL = 1024
BATCH = 8
SEQ = 8192
DEPTH = 2

N_A = DEPTH // 2
N_B = DEPTH - N_A
D_FF = 2816
CONV_WIDTH = 31
N_HEADS = 16
Q_LORA = 512
KV_LORA = 256
QK_NOPE = 64
QK_ROPE = 32
V_HEAD = 64
ROPE_THETA = 10000.0
Q_BLOCK = 128
EPS = 1e-6
N_MOD = 9

kernel_name = "yoco_conformer_mla_macaron_adaln"


def rmsnorm(x, g):
    xf = x.astype(jnp.float32)
    y = xf * lax.rsqrt(jnp.mean(xf * xf, axis=-1, keepdims=True) + EPS)
    return (y * g.astype(jnp.float32)).astype(x.dtype)


def modulate(xn, shift, scale):
    return xn * (1 + scale[:, None, :]) + shift[:, None, :]


def swiglu(x, w13, w2):
    a, b = jnp.split(x @ w13, 2, axis=-1)
    return (jax.nn.silu(a) * b) @ w2


def conv_module(u, w_pw1, b_pw1, w_dw, b_dw, ln_g, ln_b, w_pw2, b_pw2):
    a, g = jnp.split(u @ w_pw1 + b_pw1, 2, axis=-1)
    u = a * jax.nn.sigmoid(g)
    u = lax.conv_general_dilated(
        u, w_dw[:, None, :], window_strides=(1,),
        padding=[(CONV_WIDTH - 1, 0)],
        dimension_numbers=("NWC", "WIO", "NWC"),
        feature_group_count=D_MODEL) + b_dw
    uf = u.astype(jnp.float32)
    mu = jnp.mean(uf, axis=-1, keepdims=True)
    var = jnp.mean(jnp.square(uf - mu), axis=-1, keepdims=True)
    un = ((uf - mu) * lax.rsqrt(var + EPS) * ln_g.astype(jnp.float32)
          + ln_b.astype(jnp.float32)).astype(u.dtype)
    return jax.nn.silu(un) @ w_pw2 + b_pw2


def rope(x, cos, sin):
    x1, x2 = jnp.split(x, 2, axis=-1)
    out = jnp.concatenate([x1 * cos - x2 * sin, x2 * cos + x1 * sin], axis=-1)
    return out.astype(x.dtype)


def mla_shared_kv(h, shift, scale, kv_norm_g, w_kv_a, kv_a_norm_g, w_kv_b, cos, sin):
    b, s, _ = h.shape
    hn = modulate(rmsnorm(h, kv_norm_g), shift, scale)
    c_kv, k_pe = jnp.split(hn @ w_kv_a, [KV_LORA], axis=-1)
    c_kv = rmsnorm(c_kv, kv_a_norm_g)
    kvb = (c_kv @ w_kv_b).reshape(b, s, N_HEADS, QK_NOPE + V_HEAD)
    k_nope, v = jnp.split(kvb, [QK_NOPE], axis=-1)
    k_pe = rope(k_pe, cos[:, :, 0, :], sin[:, :, 0, :])
    return k_nope, k_pe, v


def mla_attention(hn, w_q_a, q_a_norm_g, w_q_b, w_o, k_nope, k_pe, v, cos, sin):
    b, s, _ = hn.shape
    q = (rmsnorm(hn @ w_q_a, q_a_norm_g) @ w_q_b).reshape(b, s, N_HEADS, QK_NOPE + QK_ROPE)
    q_nope, q_pe = jnp.split(q, [QK_NOPE], axis=-1)
    q_pe = rope(q_pe, cos, sin)
    n_blk = s // Q_BLOCK
    qn = q_nope.reshape(b, n_blk, Q_BLOCK, N_HEADS, QK_NOPE).transpose(1, 0, 2, 3, 4)
    qp = q_pe.reshape(b, n_blk, Q_BLOCK, N_HEADS, QK_ROPE).transpose(1, 0, 2, 3, 4)
    k_idx = jnp.arange(s)
    sm_scale = (QK_NOPE + QK_ROPE) ** -0.5
    neg = jnp.finfo(jnp.float32).min

    def block(args):
        qn_b, qp_b, i = args
        sc = (jnp.einsum("bqhd,bkhd->bhqk", qn_b, k_nope)
              + jnp.einsum("bqhr,bkr->bhqk", qp_b, k_pe)).astype(jnp.float32) * sm_scale
        q_idx = i * Q_BLOCK + jnp.arange(Q_BLOCK)
        sc = jnp.where(k_idx[None, :] <= q_idx[:, None], sc, neg)
        p = jax.nn.softmax(sc, axis=-1).astype(v.dtype)
        return jnp.einsum("bhqk,bkhd->bqhd", p, v)

    o = lax.map(block, (qn, qp, jnp.arange(n_blk, dtype=jnp.int32)))
    o = o.transpose(1, 0, 2, 3, 4).reshape(b, s, N_HEADS * V_HEAD)
    return o @ w_o


def _fwd_setup_inputs(seed: int = 0) -> dict:
    key = jax.random.key(seed)
    ks = jax.random.split(key, 32)
    f32 = jnp.float32

    def nrm(k, shape, fan_in, mult=1.0):
        return jax.random.normal(k, shape, f32) * (mult * fan_in ** -0.5)

    def gain(k, shape):
        return 1.0 + 0.02 * jax.random.normal(k, shape, f32)

    def bias(k, shape):
        return 0.01 * jax.random.normal(k, shape, f32)

    D = D_MODEL
    return {
        "x": jax.random.normal(ks[0], (BATCH, SEQ, D), f32),
        "c": jax.random.normal(ks[1], (BATCH, D), f32),
        "positions": (jnp.arange(SEQ, dtype=jnp.int32)[None, :]
                      + jax.random.randint(ks[2], (BATCH, 1), 0, 1024, dtype=jnp.int32)),
        "ada_w": nrm(ks[3], (DEPTH, D, N_MOD * D), D, 0.5),
        "ada_b": bias(ks[4], (DEPTH, N_MOD * D)),
        "norm_g": gain(ks[5], (DEPTH, 3, D)),
        "ffn_w13": nrm(ks[6], (DEPTH, 2, D, 2 * D_FF), D),
        "ffn_w2": nrm(ks[7], (DEPTH, 2, D_FF, D), D_FF),
        "conv_w_pw1": nrm(ks[8], (N_A, D, 2 * D), D),
        "conv_b_pw1": bias(ks[9], (N_A, 2 * D)),
        "conv_w_dw": nrm(ks[10], (N_A, CONV_WIDTH, D), CONV_WIDTH),
        "conv_b_dw": bias(ks[11], (N_A, D)),
        "conv_ln_g": gain(ks[12], (N_A, D)),
        "conv_ln_b": bias(ks[13], (N_A, D)),
        "conv_w_pw2": nrm(ks[14], (N_A, D, D), D),
        "conv_b_pw2": bias(ks[15], (N_A, D)),
        "kv_ada_w": nrm(ks[16], (D, 2 * D), D, 0.5),
        "kv_ada_b": bias(ks[17], (2 * D,)),
        "kv_norm_g": gain(ks[18], (D,)),
        "w_kv_a": nrm(ks[19], (D, KV_LORA + QK_ROPE), D),
        "kv_a_norm_g": gain(ks[20], (KV_LORA,)),
        "w_kv_b": nrm(ks[21], (KV_LORA, N_HEADS * (QK_NOPE + V_HEAD)), KV_LORA),
        "w_q_a": nrm(ks[22], (N_B, D, Q_LORA), D),
        "q_a_norm_g": gain(ks[23], (N_B, Q_LORA)),
        "w_q_b": nrm(ks[24], (N_B, Q_LORA, N_HEADS * (QK_NOPE + QK_ROPE)), Q_LORA),
        "w_o": nrm(ks[25], (N_B, N_HEADS * V_HEAD, D), N_HEADS * V_HEAD),
        "final_norm_g": gain(ks[26], (D,)),
    }


def _fwd_reference(x, c, positions, ada_w, ada_b, norm_g, ffn_w13, ffn_w2,
              conv_w_pw1, conv_b_pw1, conv_w_dw, conv_b_dw, conv_ln_g, conv_ln_b,
              conv_w_pw2, conv_b_pw2, kv_ada_w, kv_ada_b, kv_norm_g, w_kv_a,
              kv_a_norm_g, w_kv_b, w_q_a, q_a_norm_g, w_q_b, w_o, final_norm_g):
    silu_c = jax.nn.silu(c)
    inv_freq = ROPE_THETA ** (-jnp.arange(0, QK_ROPE, 2, dtype=jnp.float32) / QK_ROPE)
    ang = positions.astype(jnp.float32)[..., None] * inv_freq
    cos = jnp.cos(ang)[:, :, None, :]
    sin = jnp.sin(ang)[:, :, None, :]

    h = x
    k_nope = k_pe = v = None
    for l in range(DEPTH):
        if l == N_A:
            kv_shift, kv_scale = jnp.split(silu_c @ kv_ada_w + kv_ada_b, 2, axis=-1)
            k_nope, k_pe, v = mla_shared_kv(h, kv_shift, kv_scale, kv_norm_g, w_kv_a,
                                            kv_a_norm_g, w_kv_b, cos, sin)
        (sh1, sc1, g1, shm, scm, gm, sh2, sc2, g2) = jnp.split(
            silu_c @ ada_w[l] + ada_b[l], N_MOD, axis=-1)
        hn = modulate(rmsnorm(h, norm_g[l, 0]), sh1, sc1)
        h = h + 0.5 * g1[:, None, :] * swiglu(hn, ffn_w13[l, 0], ffn_w2[l, 0])
        hn = modulate(rmsnorm(h, norm_g[l, 1]), shm, scm)
        if l < N_A:
            y = conv_module(hn, conv_w_pw1[l], conv_b_pw1[l], conv_w_dw[l], conv_b_dw[l],
                            conv_ln_g[l], conv_ln_b[l], conv_w_pw2[l], conv_b_pw2[l])
        else:
            j = l - N_A
            y = mla_attention(hn, w_q_a[j], q_a_norm_g[j], w_q_b[j], w_o[j],
                              k_nope, k_pe, v, cos, sin)
        h = h + gm[:, None, :] * y
        hn = modulate(rmsnorm(h, norm_g[l, 2]), sh2, sc2)
        h = h + 0.5 * g2[:, None, :] * swiglu(hn, ffn_w13[l, 1], ffn_w2[l, 1])
    return rmsnorm(h, final_norm_g)


import jax as _jax
import jax.numpy as _jnp

TWIN_FORMAT = 'train_step'
FWD_PARAMS = ['x', 'c', 'positions', 'ada_w', 'ada_b', 'norm_g', 'ffn_w13', 'ffn_w2', 'conv_w_pw1', 'conv_b_pw1', 'conv_w_dw', 'conv_b_dw', 'conv_ln_g', 'conv_ln_b', 'conv_w_pw2', 'conv_b_pw2', 'kv_ada_w', 'kv_ada_b', 'kv_norm_g', 'w_kv_a', 'kv_a_norm_g', 'w_kv_b', 'w_q_a', 'q_a_norm_g', 'w_q_b', 'w_o', 'final_norm_g']
TWIN_WEIGHTS = ['ada_w', 'ada_b', 'norm_g', 'ffn_w13', 'ffn_w2', 'conv_w_pw1', 'conv_b_pw1', 'conv_w_dw', 'conv_b_dw', 'conv_ln_g', 'conv_ln_b', 'conv_w_pw2', 'conv_b_pw2', 'kv_ada_w', 'kv_ada_b', 'kv_norm_g', 'w_kv_a', 'kv_a_norm_g', 'w_kv_b', 'w_q_a', 'q_a_norm_g', 'w_q_b', 'w_o', 'final_norm_g']
TWIN_DIFF_INPUT = 'x'
TWIN_INPUTS = ['x', 'c', 'positions', 'ada_w', 'ada_b', 'norm_g', 'ffn_w13', 'ffn_w2', 'conv_w_pw1', 'conv_b_pw1', 'conv_w_dw', 'conv_b_dw', 'conv_ln_g', 'conv_ln_b', 'conv_w_pw2', 'conv_b_pw2', 'kv_ada_w', 'kv_ada_b', 'kv_norm_g', 'w_kv_a', 'kv_a_norm_g', 'w_kv_b', 'w_q_a', 'q_a_norm_g', 'w_q_b', 'w_o', 'final_norm_g', 'loss_target', 'm_ada_w', 'm_ada_b', 'm_norm_g', 'm_ffn_w13', 'm_ffn_w2', 'm_conv_w_pw1', 'm_conv_b_pw1', 'm_conv_w_dw', 'm_conv_b_dw', 'm_conv_ln_g', 'm_conv_ln_b', 'm_conv_w_pw2', 'm_conv_b_pw2', 'm_kv_ada_w', 'm_kv_ada_b', 'm_kv_norm_g', 'm_w_kv_a', 'm_kv_a_norm_g', 'm_w_kv_b', 'm_w_q_a', 'm_q_a_norm_g', 'm_w_q_b', 'm_w_o', 'm_final_norm_g', 'v_ada_w', 'v_ada_b', 'v_norm_g', 'v_ffn_w13', 'v_ffn_w2', 'v_conv_w_pw1', 'v_conv_b_pw1', 'v_conv_w_dw', 'v_conv_b_dw', 'v_conv_ln_g', 'v_conv_ln_b', 'v_conv_w_pw2', 'v_conv_b_pw2', 'v_kv_ada_w', 'v_kv_ada_b', 'v_kv_norm_g', 'v_w_kv_a', 'v_kv_a_norm_g', 'v_w_kv_b', 'v_w_q_a', 'v_q_a_norm_g', 'v_w_q_b', 'v_w_o', 'v_final_norm_g']
TWIN_OUTPUTS = ['loss', 'grad_x', 'grad_ada_w', 'grad_ada_b', 'grad_norm_g', 'grad_ffn_w13', 'grad_ffn_w2', 'grad_conv_w_pw1', 'grad_conv_b_pw1', 'grad_conv_w_dw', 'grad_conv_b_dw', 'grad_conv_ln_g', 'grad_conv_ln_b', 'grad_conv_w_pw2', 'grad_conv_b_pw2', 'grad_kv_ada_w', 'grad_kv_ada_b', 'grad_kv_norm_g', 'grad_w_kv_a', 'grad_kv_a_norm_g', 'grad_w_kv_b', 'grad_w_q_a', 'grad_q_a_norm_g', 'grad_w_q_b', 'grad_w_o', 'grad_final_norm_g', 'delta_ada_w', 'delta_ada_b', 'delta_norm_g', 'delta_ffn_w13', 'delta_ffn_w2', 'delta_conv_w_pw1', 'delta_conv_b_pw1', 'delta_conv_w_dw', 'delta_conv_b_dw', 'delta_conv_ln_g', 'delta_conv_ln_b', 'delta_conv_w_pw2', 'delta_conv_b_pw2', 'delta_kv_ada_w', 'delta_kv_ada_b', 'delta_kv_norm_g', 'delta_w_kv_a', 'delta_kv_a_norm_g', 'delta_w_kv_b', 'delta_w_q_a', 'delta_q_a_norm_g', 'delta_w_q_b', 'delta_w_o', 'delta_final_norm_g', 'new_m_ada_w', 'new_m_ada_b', 'new_m_norm_g', 'new_m_ffn_w13', 'new_m_ffn_w2', 'new_m_conv_w_pw1', 'new_m_conv_b_pw1', 'new_m_conv_w_dw', 'new_m_conv_b_dw', 'new_m_conv_ln_g', 'new_m_conv_ln_b', 'new_m_conv_w_pw2', 'new_m_conv_b_pw2', 'new_m_kv_ada_w', 'new_m_kv_ada_b', 'new_m_kv_norm_g', 'new_m_w_kv_a', 'new_m_kv_a_norm_g', 'new_m_w_kv_b', 'new_m_w_q_a', 'new_m_q_a_norm_g', 'new_m_w_q_b', 'new_m_w_o', 'new_m_final_norm_g', 'new_v_ada_w', 'new_v_ada_b', 'new_v_norm_g', 'new_v_ffn_w13', 'new_v_ffn_w2', 'new_v_conv_w_pw1', 'new_v_conv_b_pw1', 'new_v_conv_w_dw', 'new_v_conv_b_dw', 'new_v_conv_ln_g', 'new_v_conv_ln_b', 'new_v_conv_w_pw2', 'new_v_conv_b_pw2', 'new_v_kv_ada_w', 'new_v_kv_ada_b', 'new_v_kv_norm_g', 'new_v_w_kv_a', 'new_v_kv_a_norm_g', 'new_v_w_kv_b', 'new_v_w_q_a', 'new_v_q_a_norm_g', 'new_v_w_q_b', 'new_v_w_o', 'new_v_final_norm_g']
TWIN_LEAF_KINDS = {'loss': 'loss', 'grad_x': 'grad_x', 'grad_ada_w': 'grad_w', 'grad_ada_b': 'grad_w', 'grad_norm_g': 'grad_w', 'grad_ffn_w13': 'grad_w', 'grad_ffn_w2': 'grad_w', 'grad_conv_w_pw1': 'grad_w', 'grad_conv_b_pw1': 'grad_w', 'grad_conv_w_dw': 'grad_w', 'grad_conv_b_dw': 'grad_w', 'grad_conv_ln_g': 'grad_w', 'grad_conv_ln_b': 'grad_w', 'grad_conv_w_pw2': 'grad_w', 'grad_conv_b_pw2': 'grad_w', 'grad_kv_ada_w': 'grad_w', 'grad_kv_ada_b': 'grad_w', 'grad_kv_norm_g': 'grad_w', 'grad_w_kv_a': 'grad_w', 'grad_kv_a_norm_g': 'grad_w', 'grad_w_kv_b': 'grad_w', 'grad_w_q_a': 'grad_w', 'grad_q_a_norm_g': 'grad_w', 'grad_w_q_b': 'grad_w', 'grad_w_o': 'grad_w', 'grad_final_norm_g': 'grad_w', 'delta_ada_w': 'delta_w', 'delta_ada_b': 'delta_w', 'delta_norm_g': 'delta_w', 'delta_ffn_w13': 'delta_w', 'delta_ffn_w2': 'delta_w', 'delta_conv_w_pw1': 'delta_w', 'delta_conv_b_pw1': 'delta_w', 'delta_conv_w_dw': 'delta_w', 'delta_conv_b_dw': 'delta_w', 'delta_conv_ln_g': 'delta_w', 'delta_conv_ln_b': 'delta_w', 'delta_conv_w_pw2': 'delta_w', 'delta_conv_b_pw2': 'delta_w', 'delta_kv_ada_w': 'delta_w', 'delta_kv_ada_b': 'delta_w', 'delta_kv_norm_g': 'delta_w', 'delta_w_kv_a': 'delta_w', 'delta_kv_a_norm_g': 'delta_w', 'delta_w_kv_b': 'delta_w', 'delta_w_q_a': 'delta_w', 'delta_q_a_norm_g': 'delta_w', 'delta_w_q_b': 'delta_w', 'delta_w_o': 'delta_w', 'delta_final_norm_g': 'delta_w', 'new_m_ada_w': 'new_m', 'new_m_ada_b': 'new_m', 'new_m_norm_g': 'new_m', 'new_m_ffn_w13': 'new_m', 'new_m_ffn_w2': 'new_m', 'new_m_conv_w_pw1': 'new_m', 'new_m_conv_b_pw1': 'new_m', 'new_m_conv_w_dw': 'new_m', 'new_m_conv_b_dw': 'new_m', 'new_m_conv_ln_g': 'new_m', 'new_m_conv_ln_b': 'new_m', 'new_m_conv_w_pw2': 'new_m', 'new_m_conv_b_pw2': 'new_m', 'new_m_kv_ada_w': 'new_m', 'new_m_kv_ada_b': 'new_m', 'new_m_kv_norm_g': 'new_m', 'new_m_w_kv_a': 'new_m', 'new_m_kv_a_norm_g': 'new_m', 'new_m_w_kv_b': 'new_m', 'new_m_w_q_a': 'new_m', 'new_m_q_a_norm_g': 'new_m', 'new_m_w_q_b': 'new_m', 'new_m_w_o': 'new_m', 'new_m_final_norm_g': 'new_m', 'new_v_ada_w': 'new_v', 'new_v_ada_b': 'new_v', 'new_v_norm_g': 'new_v', 'new_v_ffn_w13': 'new_v', 'new_v_ffn_w2': 'new_v', 'new_v_conv_w_pw1': 'new_v', 'new_v_conv_b_pw1': 'new_v', 'new_v_conv_w_dw': 'new_v', 'new_v_conv_b_dw': 'new_v', 'new_v_conv_ln_g': 'new_v', 'new_v_conv_ln_b': 'new_v', 'new_v_conv_w_pw2': 'new_v', 'new_v_conv_b_pw2': 'new_v', 'new_v_kv_ada_w': 'new_v', 'new_v_kv_ada_b': 'new_v', 'new_v_kv_norm_g': 'new_v', 'new_v_w_kv_a': 'new_v', 'new_v_kv_a_norm_g': 'new_v', 'new_v_w_kv_b': 'new_v', 'new_v_w_q_a': 'new_v', 'new_v_q_a_norm_g': 'new_v', 'new_v_w_q_b': 'new_v', 'new_v_w_o': 'new_v', 'new_v_final_norm_g': 'new_v'}


def _forward(args):
    return _fwd_reference(*[args[k] for k in FWD_PARAMS])


def _output_shape():
    def fwd():
        inp = _fwd_setup_inputs(0)
        return _fwd_reference(*[inp[k] for k in FWD_PARAMS])
    out = _jax.eval_shape(fwd)
    return out.shape, out.dtype

N_MICROBATCH = 1
ADAM_LR = 0.001
ADAM_B1 = 0.9
ADAM_B2 = 0.999
ADAM_EPS = 1e-08
ADAM_WD = 0.01
ADAM_STEP = 10
PER_EXAMPLE_BATCH_AXIS = {'x': 0, 'c': 0, 'positions': 0, 'loss_target': 0}
SHARED_INPUTS = []
_WEIGHT_DTYPES = {'ada_w': _jnp.float32, 'ada_b': _jnp.float32, 'norm_g': _jnp.float32, 'ffn_w13': _jnp.float32, 'ffn_w2': _jnp.float32, 'conv_w_pw1': _jnp.float32, 'conv_b_pw1': _jnp.float32, 'conv_w_dw': _jnp.float32, 'conv_b_dw': _jnp.float32, 'conv_ln_g': _jnp.float32, 'conv_ln_b': _jnp.float32, 'conv_w_pw2': _jnp.float32, 'conv_b_pw2': _jnp.float32, 'kv_ada_w': _jnp.float32, 'kv_ada_b': _jnp.float32, 'kv_norm_g': _jnp.float32, 'w_kv_a': _jnp.float32, 'kv_a_norm_g': _jnp.float32, 'w_kv_b': _jnp.float32, 'w_q_a': _jnp.float32, 'q_a_norm_g': _jnp.float32, 'w_q_b': _jnp.float32, 'w_o': _jnp.float32, 'final_norm_g': _jnp.float32}
MOMENT_SCALE = {'ada_w': 3.923966e-02, 'ada_b': 6.584723e-02, 'norm_g': 3.630068e-02, 'ffn_w13': 1.608545e-02, 'ffn_w2': 2.619522e-02, 'conv_w_pw1': 3.589347e-02, 'conv_b_pw1': 4.068364e-02, 'conv_w_dw': 4.706816e-02, 'conv_b_dw': 9.108474e-02, 'conv_ln_g': 5.755820e-02, 'conv_ln_b': 5.273910e-02, 'conv_w_pw2': 4.582465e-02, 'conv_b_pw2': 8.671941e-02, 'kv_ada_w': 3.109222e-02, 'kv_ada_b': 5.512784e-02, 'kv_norm_g': 2.044779e-02, 'w_kv_a': 5.564096e-02, 'kv_a_norm_g': 5.622708e-02, 'w_kv_b': 2.006038e-02, 'w_q_a': 1.804380e-02, 'q_a_norm_g': 1.751314e-02, 'w_q_b': 1.041786e-02, 'w_o': 2.606820e-02, 'final_norm_g': 6.398530e+01}


def _to_microbatches(a, axis):
    t = _jnp.moveaxis(a, axis, 0)
    t = t.reshape((N_MICROBATCH, t.shape[0] // N_MICROBATCH) + t.shape[1:])
    return _jnp.moveaxis(t, 1, axis + 1)


def setup_inputs(seed: int = 0) -> dict:
    inp = _fwd_setup_inputs(seed)
    key = _jax.random.fold_in(_jax.random.key(seed), 7919)
    shape, _ = _output_shape()
    out = dict(inp)
    out["loss_target"] = _jax.random.normal(_jax.random.fold_in(key, 0), shape, _jnp.float32)
    for i, name in enumerate(TWIN_WEIGHTS):
        w = inp[name].astype(_jnp.float32)
        if MOMENT_SCALE is None:
            s = _jnp.sqrt(_jnp.mean(_jnp.square(w)) + 1e-30)
        else:
            s = MOMENT_SCALE[name]
        km, kv = _jax.random.split(_jax.random.fold_in(key, i + 1))
        out[name] = w
        out["m_" + name] = s * _jax.random.normal(km, w.shape, _jnp.float32)
        out["v_" + name] = (s * s) * _jax.random.uniform(kv, w.shape, _jnp.float32, 0.5, 1.5)
    if N_MICROBATCH > 1:
        for name, axis in PER_EXAMPLE_BATCH_AXIS.items():
            out[name] = _to_microbatches(out[name], axis)
    return {'x': out['x'], 'c': out['c'], 'positions': out['positions'], 'ada_w': out['ada_w'], 'ada_b': out['ada_b'], 'norm_g': out['norm_g'], 'ffn_w13': out['ffn_w13'], 'ffn_w2': out['ffn_w2'], 'conv_w_pw1': out['conv_w_pw1'], 'conv_b_pw1': out['conv_b_pw1'], 'conv_w_dw': out['conv_w_dw'], 'conv_b_dw': out['conv_b_dw'], 'conv_ln_g': out['conv_ln_g'], 'conv_ln_b': out['conv_ln_b'], 'conv_w_pw2': out['conv_w_pw2'], 'conv_b_pw2': out['conv_b_pw2'], 'kv_ada_w': out['kv_ada_w'], 'kv_ada_b': out['kv_ada_b'], 'kv_norm_g': out['kv_norm_g'], 'w_kv_a': out['w_kv_a'], 'kv_a_norm_g': out['kv_a_norm_g'], 'w_kv_b': out['w_kv_b'], 'w_q_a': out['w_q_a'], 'q_a_norm_g': out['q_a_norm_g'], 'w_q_b': out['w_q_b'], 'w_o': out['w_o'], 'final_norm_g': out['final_norm_g'], 'loss_target': out['loss_target'], 'm_ada_w': out['m_ada_w'], 'm_ada_b': out['m_ada_b'], 'm_norm_g': out['m_norm_g'], 'm_ffn_w13': out['m_ffn_w13'], 'm_ffn_w2': out['m_ffn_w2'], 'm_conv_w_pw1': out['m_conv_w_pw1'], 'm_conv_b_pw1': out['m_conv_b_pw1'], 'm_conv_w_dw': out['m_conv_w_dw'], 'm_conv_b_dw': out['m_conv_b_dw'], 'm_conv_ln_g': out['m_conv_ln_g'], 'm_conv_ln_b': out['m_conv_ln_b'], 'm_conv_w_pw2': out['m_conv_w_pw2'], 'm_conv_b_pw2': out['m_conv_b_pw2'], 'm_kv_ada_w': out['m_kv_ada_w'], 'm_kv_ada_b': out['m_kv_ada_b'], 'm_kv_norm_g': out['m_kv_norm_g'], 'm_w_kv_a': out['m_w_kv_a'], 'm_kv_a_norm_g': out['m_kv_a_norm_g'], 'm_w_kv_b': out['m_w_kv_b'], 'm_w_q_a': out['m_w_q_a'], 'm_q_a_norm_g': out['m_q_a_norm_g'], 'm_w_q_b': out['m_w_q_b'], 'm_w_o': out['m_w_o'], 'm_final_norm_g': out['m_final_norm_g'], 'v_ada_w': out['v_ada_w'], 'v_ada_b': out['v_ada_b'], 'v_norm_g': out['v_norm_g'], 'v_ffn_w13': out['v_ffn_w13'], 'v_ffn_w2': out['v_ffn_w2'], 'v_conv_w_pw1': out['v_conv_w_pw1'], 'v_conv_b_pw1': out['v_conv_b_pw1'], 'v_conv_w_dw': out['v_conv_w_dw'], 'v_conv_b_dw': out['v_conv_b_dw'], 'v_conv_ln_g': out['v_conv_ln_g'], 'v_conv_ln_b': out['v_conv_ln_b'], 'v_conv_w_pw2': out['v_conv_w_pw2'], 'v_conv_b_pw2': out['v_conv_b_pw2'], 'v_kv_ada_w': out['v_kv_ada_w'], 'v_kv_ada_b': out['v_kv_ada_b'], 'v_kv_norm_g': out['v_kv_norm_g'], 'v_w_kv_a': out['v_w_kv_a'], 'v_kv_a_norm_g': out['v_kv_a_norm_g'], 'v_w_kv_b': out['v_w_kv_b'], 'v_w_q_a': out['v_w_q_a'], 'v_q_a_norm_g': out['v_q_a_norm_g'], 'v_w_q_b': out['v_w_q_b'], 'v_w_o': out['v_w_o'], 'v_final_norm_g': out['v_final_norm_g']}


def _loss(weights, diff, rest, loss_target):
    with _jax.named_scope("forward"):
        args = {**rest, TWIN_DIFF_INPUT: diff, **{k: w.astype(_WEIGHT_DTYPES[k]) for k, w in weights.items()}}
        y = _forward(args)
    with _jax.named_scope("loss_head"):
        err = _jnp.square(y.astype(_jnp.float32) - loss_target)
        return 0.5 * _jnp.sum(_jnp.mean(err, axis=-1)) if err.ndim else 0.5 * err


def _adamw(w, g, m, v):
    m = ADAM_B1 * m + (1.0 - ADAM_B1) * g
    v = ADAM_B2 * v + (1.0 - ADAM_B2) * _jnp.square(g)
    m_hat = m / (1.0 - ADAM_B1 ** ADAM_STEP)
    v_hat = v / (1.0 - ADAM_B2 ** ADAM_STEP)
    delta = -ADAM_LR * (m_hat / (_jnp.sqrt(v_hat) + ADAM_EPS) + ADAM_WD * w)
    return delta, m, v


def reference(x, c, positions, ada_w, ada_b, norm_g, ffn_w13, ffn_w2, conv_w_pw1, conv_b_pw1, conv_w_dw, conv_b_dw, conv_ln_g, conv_ln_b, conv_w_pw2, conv_b_pw2, kv_ada_w, kv_ada_b, kv_norm_g, w_kv_a, kv_a_norm_g, w_kv_b, w_q_a, q_a_norm_g, w_q_b, w_o, final_norm_g, loss_target, m_ada_w, m_ada_b, m_norm_g, m_ffn_w13, m_ffn_w2, m_conv_w_pw1, m_conv_b_pw1, m_conv_w_dw, m_conv_b_dw, m_conv_ln_g, m_conv_ln_b, m_conv_w_pw2, m_conv_b_pw2, m_kv_ada_w, m_kv_ada_b, m_kv_norm_g, m_w_kv_a, m_kv_a_norm_g, m_w_kv_b, m_w_q_a, m_q_a_norm_g, m_w_q_b, m_w_o, m_final_norm_g, v_ada_w, v_ada_b, v_norm_g, v_ffn_w13, v_ffn_w2, v_conv_w_pw1, v_conv_b_pw1, v_conv_w_dw, v_conv_b_dw, v_conv_ln_g, v_conv_ln_b, v_conv_w_pw2, v_conv_b_pw2, v_kv_ada_w, v_kv_ada_b, v_kv_norm_g, v_w_kv_a, v_kv_a_norm_g, v_w_kv_b, v_w_q_a, v_q_a_norm_g, v_w_q_b, v_w_o, v_final_norm_g):
    given = dict(x=x, c=c, positions=positions, ada_w=ada_w, ada_b=ada_b, norm_g=norm_g, ffn_w13=ffn_w13, ffn_w2=ffn_w2, conv_w_pw1=conv_w_pw1, conv_b_pw1=conv_b_pw1, conv_w_dw=conv_w_dw, conv_b_dw=conv_b_dw, conv_ln_g=conv_ln_g, conv_ln_b=conv_ln_b, conv_w_pw2=conv_w_pw2, conv_b_pw2=conv_b_pw2, kv_ada_w=kv_ada_w, kv_ada_b=kv_ada_b, kv_norm_g=kv_norm_g, w_kv_a=w_kv_a, kv_a_norm_g=kv_a_norm_g, w_kv_b=w_kv_b, w_q_a=w_q_a, q_a_norm_g=q_a_norm_g, w_q_b=w_q_b, w_o=w_o, final_norm_g=final_norm_g, loss_target=loss_target, m_ada_w=m_ada_w, m_ada_b=m_ada_b, m_norm_g=m_norm_g, m_ffn_w13=m_ffn_w13, m_ffn_w2=m_ffn_w2, m_conv_w_pw1=m_conv_w_pw1, m_conv_b_pw1=m_conv_b_pw1, m_conv_w_dw=m_conv_w_dw, m_conv_b_dw=m_conv_b_dw, m_conv_ln_g=m_conv_ln_g, m_conv_ln_b=m_conv_ln_b, m_conv_w_pw2=m_conv_w_pw2, m_conv_b_pw2=m_conv_b_pw2, m_kv_ada_w=m_kv_ada_w, m_kv_ada_b=m_kv_ada_b, m_kv_norm_g=m_kv_norm_g, m_w_kv_a=m_w_kv_a, m_kv_a_norm_g=m_kv_a_norm_g, m_w_kv_b=m_w_kv_b, m_w_q_a=m_w_q_a, m_q_a_norm_g=m_q_a_norm_g, m_w_q_b=m_w_q_b, m_w_o=m_w_o, m_final_norm_g=m_final_norm_g, v_ada_w=v_ada_w, v_ada_b=v_ada_b, v_norm_g=v_norm_g, v_ffn_w13=v_ffn_w13, v_ffn_w2=v_ffn_w2, v_conv_w_pw1=v_conv_w_pw1, v_conv_b_pw1=v_conv_b_pw1, v_conv_w_dw=v_conv_w_dw, v_conv_b_dw=v_conv_b_dw, v_conv_ln_g=v_conv_ln_g, v_conv_ln_b=v_conv_ln_b, v_conv_w_pw2=v_conv_w_pw2, v_conv_b_pw2=v_conv_b_pw2, v_kv_ada_w=v_kv_ada_w, v_kv_ada_b=v_kv_ada_b, v_kv_norm_g=v_kv_norm_g, v_w_kv_a=v_w_kv_a, v_kv_a_norm_g=v_kv_a_norm_g, v_w_kv_b=v_w_kv_b, v_w_q_a=v_w_q_a, v_q_a_norm_g=v_q_a_norm_g, v_w_q_b=v_w_q_b, v_w_o=v_w_o, v_final_norm_g=v_final_norm_g)
    weights = {n: given[n] for n in TWIN_WEIGHTS}
    shared = {n: given[n] for n in SHARED_INPUTS}
    per_example = {n: given[n] for n in ['x', 'c', 'positions']}
    grad_fn = _jax.value_and_grad(_loss, argnums=(0, 1))

    def one_microbatch(ex, loss_target):
        ex = dict(ex)
        diff = ex.pop(TWIN_DIFF_INPUT)
        return grad_fn(weights, diff, {**shared, **ex}, loss_target)

    if N_MICROBATCH == 1:
        loss, (grad_w, grad_x) = one_microbatch(per_example, given["loss_target"])
    else:
        def body(carry, xs):
            loss_sum, grad_sum = carry
            l_k, (gw_k, gx_k) = one_microbatch(xs[0], xs[1])
            with _jax.named_scope("update"):
                return (loss_sum + l_k, _jax.tree.map(_jnp.add, grad_sum, gw_k)), gx_k

        init = (_jnp.zeros((), _jnp.float32), _jax.tree.map(_jnp.zeros_like, weights))
        (loss, grad_w), grad_x = _jax.lax.scan(body, init, (per_example, given["loss_target"]))
    with _jax.named_scope("update"):
        delta_w, new_m, new_v = {}, {}, {}
        for n in TWIN_WEIGHTS:
            delta_w[n], new_m[n], new_v[n] = _adamw(weights[n], grad_w[n], given["m_" + n], given["v_" + n])
    return (loss, grad_x, *[grad_w[n] for n in TWIN_WEIGHTS], *[delta_w[n] for n in TWIN_WEIGHTS],
            *[new_m[n] for n in TWIN_WEIGHTS], *[new_v[n] for n in TWIN_WEIGHTS])
```

```python
import functools

import numpy as np
import jax
import jax.numpy as jnp
from jax import lax
from jax.experimental import pallas as pl
from jax.experimental.pallas import tpu as pltpu

F32, BF = jnp.float32, jnp.bfloat16

D = 1024
DFF = 2816
CH = 1408
NH = 16
HP = 128
KV_LORA, Q_LORA, ROPE = 256, 512, 32
KVA_P = 384
CONV_W = 31
HALO = 32
EPS = 1e-6
SM_SCALE = float((64 + 32) ** -0.5)
NEG = -1e30
N_DEV = 8
MESH = pl.DeviceIdType.MESH

TS = 256
TA = 512
TW = 512
VMEM_LIMIT = 56 * 1024 * 1024

LR, B1, B2, EPS_ADAM, WD, STEP = 0.001, 0.9, 0.999, 1e-08, 0.01, 10

VMEM_FULL = pl.BlockSpec(memory_space=pltpu.VMEM)
HBM_FULL = pl.BlockSpec(memory_space=pltpu.HBM)


def _params(n_grid):
    return pltpu.CompilerParams(dimension_semantics=("arbitrary",) * n_grid, vmem_limit_bytes=VMEM_LIMIT)


def _dot(a, b):
    return jnp.dot(a, b, preferred_element_type=F32)


def _dot_nt(a, b):
    return lax.dot_general(a, b, (((1,), (1,)), ((), ())), preferred_element_type=F32)


def _dot_tn(a, b):
    return lax.dot_general(a, b, (((0,), (0,)), ((), ())), preferred_element_type=F32)


def _sum0(x):
    return jnp.sum(x, axis=0, keepdims=True)


def _mean1(x):
    return jnp.mean(x, axis=-1, keepdims=True)


def _sigmoid(x):
    return jax.nn.sigmoid(x)


def _rows(shape, imap):
    return pl.BlockSpec(shape, imap)


def _norm_mod(h, vec):
    r = lax.rsqrt(_mean1(h * h) + EPS)
    xhat = h * r
    u = (xhat * vec[0:1]) * (1.0 + vec[2:3]) + vec[1:2]
    return u, xhat, r


def _norm_mod_bwd(du, xhat, r, vec):
    g = vec[0:1]
    dxn = du * (1.0 + vec[2:3])
    dsh = _sum0(du)
    dsc = _sum0(du * (xhat * g))
    dg = _sum0(dxn * xhat)
    dxhat = dxn * g
    dh = r * (dxhat - xhat * _mean1(dxhat * xhat))
    return dh, dg, dsh, dsc


def _rms_bwd(dy, x, r, g):
    xhat = x * r
    dg = _sum0(dy * xhat)
    dxhat = dy * g
    return r * (dxhat - xhat * _mean1(dxhat * xhat)), dg


def _accumulate(ref, first, rows):
    @pl.when(first)
    def _():
        ref[...] = jnp.zeros(ref.shape, ref.dtype)

    for i, row in enumerate(rows):
        ref[i:i + 1, :] += row


def _rope(x, tab):
    return x * tab[:, 0:HP] + pltpu.roll(x, 16, 1) * tab[:, HP:2 * HP] + pltpu.roll(x, HP - 16, 1) * tab[:, 2 * HP:3 * HP]


def _rope_t(dy, tab):
    return (dy * tab[:, 0:HP] + pltpu.roll(dy * tab[:, HP:2 * HP], HP - 16, 1)
            + pltpu.roll(dy * tab[:, 2 * HP:3 * HP], 16, 1))


def _exchange(items, name):
    n = len(items)

    def body(*refs):
        ins, outs = refs[:n], refs[n:2 * n]
        send_sems, recv_sems, local_sems = refs[2 * n:]
        x, y, c = lax.axis_index("x"), lax.axis_index("y"), lax.axis_index("c")
        me = 4 * x + 2 * y + c

        def source(j, dev):
            return ins[j] if items[j][1] == "gather" else ins[j].at[dev]

        own = [pltpu.make_async_copy(source(j, me), outs[j].at[me], local_sems.at[j]) for j in range(n)]
        for cp in own:
            cp.start()
        remote = []
        for d in range(1, N_DEV):
            px = 1 - x if d & 4 else x
            py = 1 - y if d & 2 else y
            pc = 1 - c if d & 1 else c
            peer = 4 * px + 2 * py + pc
            for j in range(n):
                pltpu.make_async_remote_copy(
                    src_ref=source(j, peer), dst_ref=outs[j].at[me], send_sem=send_sems.at[j, d - 1],
                    recv_sem=recv_sems.at[j, d - 1], device_id=(px, py, pc), device_id_type=MESH).start()
                remote.append(pltpu.make_async_remote_copy(
                    src_ref=source(j, peer), dst_ref=outs[j].at[peer], send_sem=send_sems.at[j, d - 1],
                    recv_sem=recv_sems.at[j, d - 1], device_id=(px, py, pc), device_id_type=MESH))
        for cp in remote:
            cp.wait_send()
            cp.wait_recv()
        for cp in own:
            cp.wait()

    out_shape = []
    for arr, mode in items:
        shp = (N_DEV,) + tuple(arr.shape) if mode == "gather" else tuple(arr.shape)
        out_shape.append(jax.ShapeDtypeStruct(shp, arr.dtype))
    return pl.pallas_call(
        body, name=name, out_shape=out_shape,
        in_specs=[HBM_FULL] * n, out_specs=[HBM_FULL] * n,
        scratch_shapes=[pltpu.SemaphoreType.DMA((n, N_DEV - 1)), pltpu.SemaphoreType.DMA((n, N_DEV - 1)),
                        pltpu.SemaphoreType.DMA((n,))],
        compiler_params=pltpu.CompilerParams(has_side_effects=True),
    )(*[a for a, _ in items])


def _mod_fwd(c_all, w0, w1, wkv):
    n0, n1, n2 = w0.shape[1], w1.shape[1], wkv.shape[1]

    def body(c_ref, w0_ref, w1_ref, w2_ref, o_ref):
        cc = c_ref[...]
        s = cc * _sigmoid(cc)
        o_ref[:, 0:n0] = _dot(s, w0_ref[...])
        o_ref[:, n0:n0 + n1] = _dot(s, w1_ref[...])
        o_ref[:, n0 + n1:n0 + n1 + n2] = _dot(s, w2_ref[...])

    return pl.pallas_call(
        body, name="mod_fwd", out_shape=jax.ShapeDtypeStruct((N_DEV, n0 + n1 + n2), F32),
        in_specs=[VMEM_FULL] * 4, out_specs=VMEM_FULL,
        compiler_params=pltpu.CompilerParams(vmem_limit_bytes=VMEM_LIMIT),
    )(c_all, w0, w1, wkv)


def _mod_wgrad(c_t, dm):
    C = dm.shape[1]
    tr = 256

    def body(ct_ref, dm_ref, o_ref):
        ct = ct_ref[...]
        s = ct * _sigmoid(ct)
        dmv = dm_ref[...]
        lane = lax.broadcasted_iota(jnp.int32, (tr, N_DEV), 1)
        acc = jnp.zeros((tr, C), F32)
        for r in range(N_DEV):
            col = jnp.sum(jnp.where(lane == r, s, 0.0), axis=1, keepdims=True)
            acc = acc + col * dmv[r:r + 1, :]
        o_ref[...] = acc

    return pl.pallas_call(
        body, name="mod_wgrad", grid=(D // tr,), out_shape=jax.ShapeDtypeStruct((D, C), F32),
        in_specs=[_rows((tr, N_DEV), lambda i: (i, 0)), _rows((N_DEV, C), lambda i: (0, 0))],
        out_specs=_rows((tr, C), lambda i: (i, 0)), compiler_params=_params(1),
    )(c_t, dm)


def _ffn_fwd(h, vec, w13, w2):
    S = h.shape[0]

    def body(h_ref, vec_ref, w13_ref, w2_ref, ho_ref, u_ref, a_ref, b_ref):
        hv, vec = h_ref[...], vec_ref[...]
        u, _, _ = _norm_mod(hv, vec)
        ub = u.astype(BF)
        u_ref[...] = ub
        y = jnp.zeros((TS, D), F32)
        for k in range(DFF // CH):
            c0 = k * CH
            a = _dot(ub, w13_ref[:, c0:c0 + CH])
            b = _dot(ub, w13_ref[:, DFF + c0:DFF + c0 + CH])
            a_ref[:, c0:c0 + CH] = a.astype(BF)
            b_ref[:, c0:c0 + CH] = b.astype(BF)
            t = (a * _sigmoid(a)) * b
            y = y + _dot(t.astype(BF), w2_ref[c0:c0 + CH, :])
        ho_ref[...] = hv + (0.5 * vec[3:4]) * y

    tok = lambda i: (i, 0)
    return pl.pallas_call(
        body, name="ffn_fwd", grid=(S // TS,),
        out_shape=[jax.ShapeDtypeStruct((S, D), F32), jax.ShapeDtypeStruct((S, D), BF),
                   jax.ShapeDtypeStruct((S, DFF), BF), jax.ShapeDtypeStruct((S, DFF), BF)],
        in_specs=[_rows((TS, D), tok), _rows((8, D), lambda i: (0, 0)), VMEM_FULL, VMEM_FULL],
        out_specs=[_rows((TS, D), tok), _rows((TS, D), tok), _rows((TS, DFF), tok), _rows((TS, DFF), tok)],
        compiler_params=_params(1),
    )(h, vec, w13, w2)


def _ffn_bwd(dho, h, a, b, vec, w13, w2):
    S = h.shape[0]

    def body(dho_ref, h_ref, a_ref, b_ref, vec_ref, w13_ref, w2_ref,
             dh_ref, da_ref, db_ref, t_ref, dhb_ref, ps_ref):
        dho_v, vec = dho_ref[...], vec_ref[...]
        dhb_ref[...] = dho_v.astype(BF)
        dyb = ((0.5 * vec[3:4]) * dho_v).astype(BF)
        du = jnp.zeros((TS, D), F32)
        for k in range(DFF // CH):
            c0 = k * CH
            av = a_ref[:, c0:c0 + CH].astype(F32)
            bv = b_ref[:, c0:c0 + CH].astype(F32)
            dt = _dot_nt(dyb, w2_ref[c0:c0 + CH, :])
            sig = _sigmoid(av)
            sl = av * sig
            t_ref[:, c0:c0 + CH] = (sl * bv).astype(BF)
            dab = (dt * bv * (sig * (1.0 + av * (1.0 - sig)))).astype(BF)
            dbb = (dt * sl).astype(BF)
            da_ref[:, c0:c0 + CH] = dab
            db_ref[:, c0:c0 + CH] = dbb
            du = du + _dot_nt(dab, w13_ref[:, c0:c0 + CH]) + _dot_nt(dbb, w13_ref[:, DFF + c0:DFF + c0 + CH])
        _, xhat, r = _norm_mod(h_ref[...], vec)
        dhn, dg, dsh, dsc = _norm_mod_bwd(du, xhat, r, vec)
        dh_ref[...] = dho_v + dhn
        _accumulate(ps_ref, pl.program_id(0) == 0, [dg, dsh, dsc])

    tok = lambda i: (i, 0)
    return pl.pallas_call(
        body, name="ffn_bwd", grid=(S // TS,),
        out_shape=[jax.ShapeDtypeStruct((S, D), F32), jax.ShapeDtypeStruct((S, DFF), BF),
                   jax.ShapeDtypeStruct((S, DFF), BF), jax.ShapeDtypeStruct((S, DFF), BF),
                   jax.ShapeDtypeStruct((S, D), BF), jax.ShapeDtypeStruct((8, D), F32)],
        in_specs=[_rows((TS, D), tok), _rows((TS, D), tok), _rows((TS, DFF), tok), _rows((TS, DFF), tok),
                  _rows((8, D), lambda i: (0, 0)), VMEM_FULL, VMEM_FULL],
        out_specs=[_rows((TS, D), tok), _rows((TS, DFF), tok), _rows((TS, DFF), tok), _rows((TS, DFF), tok),
                   _rows((TS, D), tok), _rows((8, D), lambda i: (0, 0))],
        compiler_params=_params(1),
    )(dho, h, a, b, vec, w13, w2)


def _wgrad(a, b, tm, tn, name, gate=None):
    S, M = a.shape
    N = b.shape[1]
    n_s = S // TW

    def body(*refs):
        if gate is None:
            a_ref, b_ref, o_ref, acc_ref = refs
        else:
            a_ref, b_ref, w_ref, sc_ref, o_ref, gs_ref, acc_ref = refs
        s = pl.program_id(2)

        @pl.when(s == 0)
        def _():
            acc_ref[...] = jnp.zeros((tm, tn), F32)

        acc_ref[...] += _dot_tn(a_ref[...], b_ref[...])

        @pl.when(s == n_s - 1)
        def _():
            acc = acc_ref[...]
            if gate is None:
                o_ref[...] = acc.astype(BF)
            else:
                o_ref[...] = (acc * sc_ref[0:1, :]).astype(BF)
                gs_ref[...] = jnp.broadcast_to(_sum0(acc * w_ref[...].astype(F32)), (8, tn))

    in_specs = [_rows((TW, tm), lambda m, n, s: (s, m)), _rows((TW, tn), lambda m, n, s: (s, n))]
    out_shape = [jax.ShapeDtypeStruct((M, N), BF)]
    out_specs = [_rows((tm, tn), lambda m, n, s: (m, n))]
    args = [a, b]
    if gate is not None:
        in_specs += [_rows((tm, tn), lambda m, n, s: (m, n)), _rows((8, tn), lambda m, n, s: (0, n))]
        out_shape.append(jax.ShapeDtypeStruct((M // tm, 8, N), F32))
        out_specs.append(_rows((None, 8, tn), lambda m, n, s: (m, 0, n)))
        args += list(gate)
    res = pl.pallas_call(
        body, name=name, grid=(M // tm, N // tn, n_s), out_shape=out_shape, in_specs=in_specs,
        out_specs=out_specs, scratch_shapes=[pltpu.VMEM((tm, tn), F32)], compiler_params=_params(3),
    )(*args)
    return res[0] if gate is None else (res[0], res[1])


def _conv_in_fwd(h, vec, w1, b1):
    S = h.shape[0]

    def body(h_ref, vec_ref, w_ref, b1_ref, hn_ref, pre_ref):
        u, _, _ = _norm_mod(h_ref[...], vec_ref[...])
        ub = u.astype(BF)
        hn_ref[...] = ub
        pre_ref[...] = _dot(ub, w_ref[...]) + b1_ref[0:1, :]

    tok = lambda i: (i, 0)
    return pl.pallas_call(
        body, name="conv_in_fwd", grid=(S // TS,),
        out_shape=[jax.ShapeDtypeStruct((S, D), BF), jax.ShapeDtypeStruct((S, 2 * D), F32)],
        in_specs=[_rows((TS, D), tok), _rows((8, D), lambda i: (0, 0)), VMEM_FULL, _rows((8, 2 * D), lambda i: (0, 0))],
        out_specs=[_rows((TS, D), tok), _rows((TS, 2 * D), tok)], compiler_params=_params(1),
    )(h, vec, w1, b1)


def _glu(pre):
    return pre[:, :D] * _sigmoid(pre[:, D:])


def _layernorm(u2):
    mu = _mean1(u2)
    xc = u2 - mu
    rstd = lax.rsqrt(_mean1(xc * xc) + EPS)
    return xc * rstd, rstd


def _conv_out_fwd(pre, h, cw, w2):
    S = h.shape[0]
    hb = TS // HALO

    def body(pre_ref, ph_ref, h_ref, cw_ref, w2_ref, u2_ref, z_ref, ho_ref, win_ref):
        i = pl.program_id(0)
        cwv = cw_ref[...]
        win_ref[0:HALO, :] = jnp.where(i > 0, _glu(ph_ref[...]), 0.0)
        win_ref[HALO:HALO + TS, :] = _glu(pre_ref[...])
        u2 = jnp.broadcast_to(cwv[31:32], (TS, D))
        for j in range(CONV_W):
            off = HALO - (CONV_W - 1) + j
            u2 = u2 + cwv[j:j + 1] * win_ref[off:off + TS, :]
        u2_ref[...] = u2
        xh, _ = _layernorm(u2)
        un = xh * cwv[32:33] + cwv[33:34]
        zb = (un * _sigmoid(un)).astype(BF)
        z_ref[...] = zb
        y = _dot(zb, w2_ref[...]) + cwv[34:35]
        ho_ref[...] = h_ref[...] + cwv[35:36] * y

    tok = lambda i: (i, 0)
    return pl.pallas_call(
        body, name="conv_out_fwd", grid=(S // TS,),
        out_shape=[jax.ShapeDtypeStruct((S, D), F32), jax.ShapeDtypeStruct((S, D), BF), jax.ShapeDtypeStruct((S, D), F32)],
        in_specs=[_rows((TS, 2 * D), tok), _rows((HALO, 2 * D), lambda i: (jnp.maximum(i * hb - 1, 0), 0)),
                  _rows((TS, D), tok), _rows((40, D), lambda i: (0, 0)), VMEM_FULL],
        out_specs=[_rows((TS, D), tok), _rows((TS, D), tok), _rows((TS, D), tok)],
        scratch_shapes=[pltpu.VMEM((TS + HALO, D), F32)], compiler_params=_params(1),
    )(pre, pre, h, cw, w2)


def _conv_out_bwd(dho, u2, cw, w2):
    S = dho.shape[0]

    def body(dho_ref, u2_ref, cw_ref, w2_ref, du2_ref, dhb_ref, ps_ref):
        dho_v, cwv = dho_ref[...], cw_ref[...]
        dhb_ref[...] = dho_v.astype(BF)
        dz = _dot_nt((cwv[35:36] * dho_v).astype(BF), w2_ref[...])
        xh, rstd = _layernorm(u2_ref[...])
        un = xh * cwv[32:33] + cwv[33:34]
        sig = _sigmoid(un)
        dun = dz * (sig * (1.0 + un * (1.0 - sig)))
        dxh = dun * cwv[32:33]
        du2_ref[...] = rstd * (dxh - _mean1(dxh) - xh * _mean1(dxh * xh))
        _accumulate(ps_ref, pl.program_id(0) == 0, [_sum0(dun * xh), _sum0(dun), _sum0(dho_v)])

    tok = lambda i: (i, 0)
    return pl.pallas_call(
        body, name="conv_out_bwd", grid=(S // TS,),
        out_shape=[jax.ShapeDtypeStruct((S, D), F32), jax.ShapeDtypeStruct((S, D), BF), jax.ShapeDtypeStruct((8, D), F32)],
        in_specs=[_rows((TS, D), tok), _rows((TS, D), tok), _rows((40, D), lambda i: (0, 0)), VMEM_FULL],
        out_specs=[_rows((TS, D), tok), _rows((TS, D), tok), _rows((8, D), lambda i: (0, 0))],
        compiler_params=_params(1),
    )(dho, u2, cw, w2)


def _conv_in_bwd(du2, pre, h, dho, vec, cw, w1):
    S = h.shape[0]
    n_t = S // TS
    hb = TS // HALO

    def body(du2_ref, dh2h_ref, pre_ref, ph_ref, h_ref, dho_ref, vec_ref, cw_ref, w1_ref,
             dh_ref, dpre_ref, ps_ref, pw_ref, pb_ref, winu_ref, wind_ref):
        i = pl.program_id(0)
        vec, cwv = vec_ref[...], cw_ref[...]
        pre = pre_ref[...]
        av, sg = pre[:, :D], _sigmoid(pre[:, D:])
        winu_ref[0:HALO, :] = jnp.where(i > 0, _glu(ph_ref[...]), 0.0)
        winu_ref[HALO:HALO + TS, :] = av * sg
        du2v = du2_ref[...]
        wind_ref[0:TS, :] = du2v
        wind_ref[TS:TS + HALO, :] = jnp.where(i < n_t - 1, dh2h_ref[...], 0.0)

        @pl.when(i == 0)
        def _():
            pw_ref[...] = jnp.zeros((32, D), F32)

        du1 = jnp.zeros((TS, D), F32)
        for j in range(CONV_W):
            off = HALO - (CONV_W - 1) + j
            pw_ref[j:j + 1, :] += _sum0(du2v * winu_ref[off:off + TS, :])
            du1 = du1 + cwv[j:j + 1] * wind_ref[CONV_W - 1 - j:CONV_W - 1 - j + TS, :]
        pw_ref[31:32, :] += _sum0(du2v)
        da = du1 * sg
        dg = du1 * av * sg * (1.0 - sg)
        dab, dgb = da.astype(BF), dg.astype(BF)
        dpre_ref[:, :D] = dab
        dpre_ref[:, D:] = dgb

        @pl.when(i == 0)
        def _():
            pb_ref[...] = jnp.zeros((8, 2 * D), F32)

        pb_ref[0:1, :D] += _sum0(da)
        pb_ref[0:1, D:] += _sum0(dg)
        du = _dot_nt(dab, w1_ref[:, :D]) + _dot_nt(dgb, w1_ref[:, D:])
        _, xhat, r = _norm_mod(h_ref[...], vec)
        dhn, dgn, dsh, dsc = _norm_mod_bwd(du, xhat, r, vec)
        dh_ref[...] = dho_ref[...] + dhn
        _accumulate(ps_ref, i == 0, [dgn, dsh, dsc])

    tok = lambda i: (i, 0)
    fixed = lambda i: (0, 0)
    return pl.pallas_call(
        body, name="conv_in_bwd", grid=(n_t,),
        out_shape=[jax.ShapeDtypeStruct((S, D), F32), jax.ShapeDtypeStruct((S, 2 * D), BF),
                   jax.ShapeDtypeStruct((8, D), F32), jax.ShapeDtypeStruct((32, D), F32),
                   jax.ShapeDtypeStruct((8, 2 * D), F32)],
        in_specs=[_rows((TS, D), tok), _rows((HALO, D), lambda i: (jnp.minimum((i + 1) * hb, S // HALO - 1), 0)),
                  _rows((TS, 2 * D), tok), _rows((HALO, 2 * D), lambda i: (jnp.maximum(i * hb - 1, 0), 0)),
                  _rows((TS, D), tok), _rows((TS, D), tok), _rows((8, D), fixed), _rows((40, D), fixed), VMEM_FULL],
        out_specs=[_rows((TS, D), tok), _rows((TS, 2 * D), tok), _rows((8, D), fixed), _rows((32, D), fixed),
                   _rows((8, 2 * D), fixed)],
        scratch_shapes=[pltpu.VMEM((TS + HALO, D), F32), pltpu.VMEM((TS + HALO, D), F32)],
        compiler_params=_params(1),
    )(du2, du2, pre, pre, h, dho, vec, cw, w1)


def _lane():
    return lax.broadcasted_iota(jnp.int32, (TS, HP), 1)


def _kv_fwd(h, vec, wkva, g2, wkvb, rope):
    S = h.shape[0]

    def body(h_ref, vec_ref, wa_ref, g2_ref, wb_ref, rope_ref, hn_ref, ckv_ref, ckn_ref, k_ref, v_ref):
        u, _, _ = _norm_mod(h_ref[...], vec_ref[...])
        ub = u.astype(BF)
        hn_ref[...] = ub
        kva = _dot(ub, wa_ref[...])
        ckv = kva[:, :KV_LORA]
        ckv_ref[...] = ckv
        r2 = lax.rsqrt(_mean1(ckv * ckv) + EPS)
        cknb = ((ckv * r2) * g2_ref[0:1, :]).astype(BF)
        ckn_ref[...] = cknb
        kvb = _dot(cknb, wb_ref[...])
        kpe = _rope(kva[:, KV_LORA:KVA_P], rope_ref[...])
        lane = _lane()
        for hd in range(NH):
            blk = kvb[:, hd * HP:(hd + 1) * HP]
            k_ref[:, hd * HP:(hd + 1) * HP] = jnp.where(lane < 64, blk, kpe).astype(BF)
            v_ref[:, hd * HP:(hd + 1) * HP] = jnp.where(lane >= 64, blk, 0.0).astype(BF)

    tok = lambda i: (i, 0)
    fixed = lambda i: (0, 0)
    return pl.pallas_call(
        body, name="kv_fwd", grid=(S // TS,),
        out_shape=[jax.ShapeDtypeStruct((S, D), BF), jax.ShapeDtypeStruct((S, KV_LORA), F32),
                   jax.ShapeDtypeStruct((S, KV_LORA), BF), jax.ShapeDtypeStruct((S, NH * HP), BF),
                   jax.ShapeDtypeStruct((S, NH * HP), BF)],
        in_specs=[_rows((TS, D), tok), _rows((8, D), fixed), VMEM_FULL, _rows((8, KV_LORA), fixed), VMEM_FULL,
                  _rows((TS, 3 * HP), tok)],
        out_specs=[_rows((TS, D), tok), _rows((TS, KV_LORA), tok), _rows((TS, KV_LORA), tok),
                   _rows((TS, NH * HP), tok), _rows((TS, NH * HP), tok)],
        compiler_params=_params(1),
    )(h, vec, wkva, g2, wkvb, rope)


def _kv_bwd(dk, dv, h, ckv, dho, vec, wkva, g2, wkvb, rope):
    S = h.shape[0]

    def body(dk_ref, dv_ref, h_ref, ckv_ref, dho_ref, vec_ref, wa_ref, g2_ref, wb_ref, rope_ref,
             dh_ref, dkva_ref, dkvb_ref, ps_ref, ps2_ref):
        i = pl.program_id(0)
        vec = vec_ref[...]
        lane = _lane()
        dkpe = jnp.zeros((TS, HP), F32)
        for hd in range(NH):
            dkh = dk_ref[:, hd * HP:(hd + 1) * HP]
            dvh = dv_ref[:, hd * HP:(hd + 1) * HP]
            dkvb_ref[:, hd * HP:(hd + 1) * HP] = jnp.where(lane < 64, dkh, dvh).astype(BF)
            dkpe = dkpe + jnp.where(lane >= 64, dkh, 0.0)
        dkpe = _rope_t(dkpe, rope_ref[...])
        dckn = _dot_nt(dkvb_ref[...], wb_ref[...])
        ckv = ckv_ref[...]
        r2 = lax.rsqrt(_mean1(ckv * ckv) + EPS)
        dckv, dg2 = _rms_bwd(dckn, ckv, r2, g2_ref[0:1, :])
        dkva_ref[:, :KV_LORA] = dckv.astype(BF)
        dkva_ref[:, KV_LORA:KVA_P] = dkpe.astype(BF)
        du = _dot_nt(dkva_ref[...], wa_ref[...])
        _, xhat, r = _norm_mod(h_ref[...], vec)
        dhn, dgn, dsh, dsc = _norm_mod_bwd(du, xhat, r, vec)
        dh_ref[...] = dho_ref[...] + dhn
        _accumulate(ps_ref, i == 0, [dgn, dsh, dsc])
        _accumulate(ps2_ref, i == 0, [dg2])

    tok = lambda i: (i, 0)
    fixed = lambda i: (0, 0)
    return pl.pallas_call(
        body, name="kv_bwd", grid=(S // TS,),
        out_shape=[jax.ShapeDtypeStruct((S, D), F32), jax.ShapeDtypeStruct((S, KVA_P), BF),
                   jax.ShapeDtypeStruct((S, NH * HP), BF), jax.ShapeDtypeStruct((8, D), F32),
                   jax.ShapeDtypeStruct((8, KV_LORA), F32)],
        in_specs=[_rows((TS, NH * HP), tok), _rows((TS, NH * HP), tok), _rows((TS, D), tok), _rows((TS, KV_LORA), tok),
                  _rows((TS, D), tok), _rows((8, D), fixed), VMEM_FULL, _rows((8, KV_LORA), fixed), VMEM_FULL,
                  _rows((TS, 3 * HP), tok)],
        out_specs=[_rows((TS, D), tok), _rows((TS, KVA_P), tok), _rows((TS, NH * HP), tok), _rows((8, D), fixed),
                   _rows((8, KV_LORA), fixed)],
        compiler_params=_params(1),
    )(dk, dv, h, ckv, dho, vec, wkva, g2, wkvb, rope)


def _q_fwd(h, vec, wqa, g2, wqb, rope):
    S = h.shape[0]

    def body(h_ref, vec_ref, wa_ref, g2_ref, wb_ref, rope_ref, hn_ref, qa_ref, qan_ref, q_ref):
        u, _, _ = _norm_mod(h_ref[...], vec_ref[...])
        ub = u.astype(BF)
        hn_ref[...] = ub
        qa = _dot(ub, wa_ref[...])
        qa_ref[...] = qa
        r2 = lax.rsqrt(_mean1(qa * qa) + EPS)
        qanb = ((qa * r2) * g2_ref[0:1, :]).astype(BF)
        qan_ref[...] = qanb
        q = _dot(qanb, wb_ref[...])
        tab = rope_ref[...]
        for hd in range(NH):
            q_ref[:, hd * HP:(hd + 1) * HP] = _rope(q[:, hd * HP:(hd + 1) * HP], tab).astype(BF)

    tok = lambda i: (i, 0)
    fixed = lambda i: (0, 0)
    return pl.pallas_call(
        body, name="q_fwd", grid=(S // TS,),
        out_shape=[jax.ShapeDtypeStruct((S, D), BF), jax.ShapeDtypeStruct((S, Q_LORA), F32),
                   jax.ShapeDtypeStruct((S, Q_LORA), BF), jax.ShapeDtypeStruct((S, NH * HP), BF)],
        in_specs=[_rows((TS, D), tok), _rows((8, D), fixed), VMEM_FULL, _rows((8, Q_LORA), fixed), VMEM_FULL,
                  _rows((TS, 3 * HP), tok)],
        out_specs=[_rows((TS, D), tok), _rows((TS, Q_LORA), tok), _rows((TS, Q_LORA), tok), _rows((TS, NH * HP), tok)],
        compiler_params=_params(1),
    )(h, vec, wqa, g2, wqb, rope)


def _q_bwd(dq, h, qa, dho, vec, wqa, g2, wqb, rope):
    S = h.shape[0]

    def body(dq_ref, h_ref, qa_ref, dho_ref, vec_ref, wa_ref, g2_ref, wb_ref, rope_ref,
             dh_ref, dqb_ref, dqa_ref, ps_ref, ps2_ref):
        i = pl.program_id(0)
        vec, tab = vec_ref[...], rope_ref[...]
        for hd in range(NH):
            dqb_ref[:, hd * HP:(hd + 1) * HP] = _rope_t(dq_ref[:, hd * HP:(hd + 1) * HP], tab).astype(BF)
        dqan = _dot_nt(dqb_ref[...], wb_ref[...])
        qa = qa_ref[...]
        r2 = lax.rsqrt(_mean1(qa * qa) + EPS)
        dqa, dg2 = _rms_bwd(dqan, qa, r2, g2_ref[0:1, :])
        dqab = dqa.astype(BF)
        dqa_ref[...] = dqab
        du = _dot_nt(dqab, wa_ref[...])
        _, xhat, r = _norm_mod(h_ref[...], vec)
        dhn, dgn, dsh, dsc = _norm_mod_bwd(du, xhat, r, vec)
        dh_ref[...] = dho_ref[...] + dhn
        _accumulate(ps_ref, i == 0, [dgn, dsh, dsc])
        _accumulate(ps2_ref, i == 0, [dg2])

    tok = lambda i: (i, 0)
    fixed = lambda i: (0, 0)
    return pl.pallas_call(
        body, name="q_bwd", grid=(S // TS,),
        out_shape=[jax.ShapeDtypeStruct((S, D), F32), jax.ShapeDtypeStruct((S, NH * HP), BF),
                   jax.ShapeDtypeStruct((S, Q_LORA), BF), jax.ShapeDtypeStruct((8, D), F32),
                   jax.ShapeDtypeStruct((8, Q_LORA), F32)],
        in_specs=[_rows((TS, NH * HP), tok), _rows((TS, D), tok), _rows((TS, Q_LORA), tok), _rows((TS, D), tok),
                  _rows((8, D), fixed), VMEM_FULL, _rows((8, Q_LORA), fixed), VMEM_FULL, _rows((TS, 3 * HP), tok)],
        out_specs=[_rows((TS, D), tok), _rows((TS, NH * HP), tok), _rows((TS, Q_LORA), tok), _rows((8, D), fixed),
                   _rows((8, Q_LORA), fixed)],
        compiler_params=_params(1),
    )(dq, h, qa, dho, vec, wqa, g2, wqb, rope)


def _causal_mask(s, qi, kj):
    row = lax.broadcasted_iota(jnp.int32, (TA, TA), 0) + qi * TA
    col = lax.broadcasted_iota(jnp.int32, (TA, TA), 1) + kj * TA
    return jnp.where(col <= row, s, NEG)


def _attn_fwd(q, k, v):
    S = q.shape[0]
    nq = S // TA
    pairs = [(qi, kj) for qi in range(nq) for kj in range(qi + 1)]
    qt = jnp.asarray(np.array([p[0] for p in pairs], np.int32))
    kt = jnp.asarray(np.array([p[1] for p in pairs], np.int32))

    def body(qt_ref, kt_ref, q_ref, k_ref, v_ref, o_ref, lse_ref, m_ref, l_ref, acc_ref):
        p = pl.program_id(1)
        qi, kj = qt_ref[p], kt_ref[p]

        @pl.when(kj == 0)
        def _():
            m_ref[...] = jnp.full((TA, 1), NEG, F32)
            l_ref[...] = jnp.zeros((TA, 1), F32)
            acc_ref[...] = jnp.zeros((TA, HP), F32)

        s = _causal_mask(_dot_nt(q_ref[...], k_ref[...]) * SM_SCALE, qi, kj)
        m_prev = m_ref[...]
        m_new = jnp.maximum(m_prev, jnp.max(s, axis=1, keepdims=True))
        alpha = jnp.exp(m_prev - m_new)
        pm = jnp.exp(s - m_new)
        l_ref[...] = alpha * l_ref[...] + jnp.sum(pm, axis=1, keepdims=True)
        acc_ref[...] = alpha * acc_ref[...] + _dot(pm.astype(BF), v_ref[...])
        m_ref[...] = m_new

        @pl.when(kj == qi)
        def _():
            o_ref[...] = (acc_ref[...] / l_ref[...]).astype(BF)
            lse_ref[...] = m_ref[...] + jnp.log(l_ref[...])

    grid_spec = pltpu.PrefetchScalarGridSpec(
        num_scalar_prefetch=2, grid=(NH, len(pairs)),
        in_specs=[_rows((TA, HP), lambda h, p, qt, kt: (qt[p], h)), _rows((TA, HP), lambda h, p, qt, kt: (kt[p], h)),
                  _rows((TA, HP), lambda h, p, qt, kt: (kt[p], h))],
        out_specs=[_rows((TA, HP), lambda h, p, qt, kt: (qt[p], h)),
                   _rows((None, TA, 1), lambda h, p, qt, kt: (h, qt[p], 0))],
        scratch_shapes=[pltpu.VMEM((TA, 1), F32), pltpu.VMEM((TA, 1), F32), pltpu.VMEM((TA, HP), F32)])
    return pl.pallas_call(
        body, name="attn_fwd", grid_spec=grid_spec,
        out_shape=[jax.ShapeDtypeStruct((S, NH * HP), BF), jax.ShapeDtypeStruct((NH, S, 1), F32)],
        compiler_params=_params(2),
    )(qt, kt, q, k, v)


def _attn_bwd(q, k, v, do, lse, delta):
    S = q.shape[0]
    nq = S // TA
    pairs = [(kj, qi) for kj in range(nq) for qi in range(kj, nq)]
    kt = jnp.asarray(np.array([p[0] for p in pairs], np.int32))
    qt = jnp.asarray(np.array([p[1] for p in pairs], np.int32))

    def body(kt_ref, qt_ref, q_ref, k_ref, v_ref, do_ref, lse_ref, dl_ref, dq_ref, dk_ref, dv_ref, dka_ref, dva_ref):
        p = pl.program_id(1)
        kj, qi = kt_ref[p], qt_ref[p]

        @pl.when(p == 0)
        def _():
            dq_ref[...] = jnp.zeros((S, HP), F32)

        @pl.when(qi == kj)
        def _():
            dka_ref[...] = jnp.zeros((TA, HP), F32)
            dva_ref[...] = jnp.zeros((TA, HP), F32)

        qv, kv, dov = q_ref[...], k_ref[...], do_ref[...]
        s = _causal_mask(_dot_nt(qv, kv) * SM_SCALE, qi, kj)
        pm = jnp.exp(s - lse_ref[...])
        dp = _dot_nt(dov, v_ref[...])
        ds = pm * (dp - dl_ref[...]) * SM_SCALE
        pb, dsb = pm.astype(BF), ds.astype(BF)
        dva_ref[...] += _dot_tn(pb, dov)
        dka_ref[...] += _dot_tn(dsb, qv)
        r0 = pl.multiple_of(qi * TA, TA)
        dq_ref[pl.ds(r0, TA), :] += _dot(dsb, kv)

        @pl.when(qi == nq - 1)
        def _():
            dk_ref[...] = dka_ref[...]
            dv_ref[...] = dva_ref[...]

    qmap = lambda h, p, kt, qt: (qt[p], h)
    kmap = lambda h, p, kt, qt: (kt[p], h)
    grid_spec = pltpu.PrefetchScalarGridSpec(
        num_scalar_prefetch=2, grid=(NH, len(pairs)),
        in_specs=[_rows((TA, HP), qmap), _rows((TA, HP), kmap), _rows((TA, HP), kmap), _rows((TA, HP), qmap),
                  _rows((None, TA, 1), lambda h, p, kt, qt: (h, qt[p], 0)),
                  _rows((None, TA, 1), lambda h, p, kt, qt: (h, qt[p], 0))],
        out_specs=[_rows((S, HP), lambda h, p, kt, qt: (0, h)), _rows((TA, HP), kmap), _rows((TA, HP), kmap)],
        scratch_shapes=[pltpu.VMEM((TA, HP), F32), pltpu.VMEM((TA, HP), F32)])
    return pl.pallas_call(
        body, name="attn_bwd", grid_spec=grid_spec,
        out_shape=[jax.ShapeDtypeStruct((S, NH * HP), F32)] * 3, compiler_params=_params(2),
    )(kt, qt, q, k, v, do, lse, delta)


def _attn_out_fwd(o, h, wo, vec):
    S = h.shape[0]

    def body(o_ref, h_ref, wo_ref, vec_ref, ho_ref):
        ho_ref[...] = h_ref[...] + vec_ref[3:4, :] * _dot(o_ref[...], wo_ref[...])

    tok = lambda i: (i, 0)
    return pl.pallas_call(
        body, name="attn_out_fwd", grid=(S // TS,), out_shape=jax.ShapeDtypeStruct((S, D), F32),
        in_specs=[_rows((TS, NH * HP), tok), _rows((TS, D), tok), VMEM_FULL, _rows((8, D), lambda i: (0, 0))],
        out_specs=_rows((TS, D), tok), compiler_params=_params(1),
    )(o, h, wo, vec)


def _attn_out_bwd(dho, o, wo, vec):
    S = dho.shape[0]

    def body(dho_ref, o_ref, wo_ref, vec_ref, do_ref, dl_ref, dhb_ref):
        dho_v = dho_ref[...]
        dhb_ref[...] = dho_v.astype(BF)
        do = _dot_nt((vec_ref[3:4, :] * dho_v).astype(BF), wo_ref[...])
        do_ref[...] = do.astype(BF)
        prod = do * o_ref[...].astype(F32)
        for hd in range(NH):
            dl_ref[hd] = jnp.sum(prod[:, hd * HP:(hd + 1) * HP], axis=1, keepdims=True)

    tok = lambda i: (i, 0)
    return pl.pallas_call(
        body, name="attn_out_bwd", grid=(S // TS,),
        out_shape=[jax.ShapeDtypeStruct((S, NH * HP), BF), jax.ShapeDtypeStruct((NH, S, 1), F32),
                   jax.ShapeDtypeStruct((S, D), BF)],
        in_specs=[_rows((TS, D), tok), _rows((TS, NH * HP), tok), VMEM_FULL, _rows((8, D), lambda i: (0, 0))],
        out_specs=[_rows((TS, NH * HP), tok), _rows((NH, TS, 1), lambda i: (0, i, 0)), _rows((TS, D), tok)],
        compiler_params=_params(1),
    )(dho, o, wo, vec)


def _final(h, target, fg):
    S = h.shape[0]

    def body(h_ref, t_ref, g_ref, dh_ref, ps_ref):
        hv, g = h_ref[...], g_ref[0:1, :]
        r = lax.rsqrt(_mean1(hv * hv) + EPS)
        xhat = hv * r
        err = xhat * g - t_ref[...]
        loss = 0.5 * jnp.sum(_mean1(err * err), axis=0, keepdims=True)
        dy = err * (1.0 / D)
        dxhat = dy * g
        dh_ref[...] = r * (dxhat - xhat * _mean1(dxhat * xhat))
        _accumulate(ps_ref, pl.program_id(0) == 0, [_sum0(dy * xhat), jnp.broadcast_to(loss, (1, D))])

    tok = lambda i: (i, 0)
    return pl.pallas_call(
        body, name="final_loss", grid=(S // TS,),
        out_shape=[jax.ShapeDtypeStruct((S, D), F32), jax.ShapeDtypeStruct((8, D), F32)],
        in_specs=[_rows((TS, D), tok), _rows((TS, D), tok), _rows((8, D), lambda i: (0, 0))],
        out_specs=[_rows((TS, D), tok), _rows((8, D), lambda i: (0, 0))], compiler_params=_params(1),
    )(h, target, fg)


def _row_tile(r):
    if r <= 256:
        return r
    for t in range(256, 7, -8):
        if r % t == 0:
            return t
    return r


def _adamw(parts, w, m, v):
    P, R, C = parts.shape
    tr = _row_tile(R)

    def body(p_ref, w_ref, m_ref, v_ref, g_ref, d_ref, mo_ref, vo_ref):
        g = p_ref[0].astype(F32)
        for k in range(1, P):
            g = g + p_ref[k].astype(F32)
        g_ref[...] = g
        m2 = B1 * m_ref[...] + (1.0 - B1) * g
        v2 = B2 * v_ref[...] + (1.0 - B2) * (g * g)
        mo_ref[...] = m2
        vo_ref[...] = v2
        m_hat = m2 / (1.0 - B1 ** STEP)
        v_hat = v2 / (1.0 - B2 ** STEP)
        d_ref[...] = -LR * (m_hat / (jnp.sqrt(v_hat) + EPS_ADAM) + WD * w_ref[...])

    blk = _rows((tr, C), lambda i: (i, 0))
    return pl.pallas_call(
        body, name="adamw", grid=(R // tr,), out_shape=[jax.ShapeDtypeStruct((R, C), F32)] * 4,
        in_specs=[_rows((P, tr, C), lambda i: (0, i, 0)), blk, blk, blk], out_specs=[blk] * 4,
        compiler_params=_params(1),
    )(parts, w, m, v)


_WEIGHTS = ['ada_w', 'ada_b', 'norm_g', 'ffn_w13', 'ffn_w2', 'conv_w_pw1', 'conv_b_pw1', 'conv_w_dw', 'conv_b_dw',
            'conv_ln_g', 'conv_ln_b', 'conv_w_pw2', 'conv_b_pw2', 'kv_ada_w', 'kv_ada_b', 'kv_norm_g', 'w_kv_a',
            'kv_a_norm_g', 'w_kv_b', 'w_q_a', 'q_a_norm_g', 'w_q_b', 'w_o', 'final_norm_g']


def _vec(rows):
    rows = [r.reshape(1, -1).astype(F32) for r in rows]
    return jnp.concatenate(rows + [jnp.zeros((8 - len(rows), rows[0].shape[1]), F32)], axis=0)


def kernel(x, c, positions, ada_w, ada_b, norm_g, ffn_w13, ffn_w2, conv_w_pw1, conv_b_pw1, conv_w_dw, conv_b_dw, conv_ln_g, conv_ln_b, conv_w_pw2, conv_b_pw2, kv_ada_w, kv_ada_b, kv_norm_g, w_kv_a, kv_a_norm_g, w_kv_b, w_q_a, q_a_norm_g, w_q_b, w_o, final_norm_g, loss_target, m_ada_w, m_ada_b, m_norm_g, m_ffn_w13, m_ffn_w2, m_conv_w_pw1, m_conv_b_pw1, m_conv_w_dw, m_conv_b_dw, m_conv_ln_g, m_conv_ln_b, m_conv_w_pw2, m_conv_b_pw2, m_kv_ada_w, m_kv_ada_b, m_kv_norm_g, m_w_kv_a, m_kv_a_norm_g, m_w_kv_b, m_w_q_a, m_q_a_norm_g, m_w_q_b, m_w_o, m_final_norm_g, v_ada_w, v_ada_b, v_norm_g, v_ffn_w13, v_ffn_w2, v_conv_w_pw1, v_conv_b_pw1, v_conv_w_dw, v_conv_b_dw, v_conv_ln_g, v_conv_ln_b, v_conv_w_pw2, v_conv_b_pw2, v_kv_ada_w, v_kv_ada_b, v_kv_norm_g, v_w_kv_a, v_kv_a_norm_g, v_w_kv_b, v_w_q_a, v_q_a_norm_g, v_w_q_b, v_w_o, v_final_norm_g):
    given = dict(locals())
    S = x.shape[1]
    me = 4 * lax.axis_index("x") + 2 * lax.axis_index("y") + lax.axis_index("c")

    small = jnp.concatenate([
        conv_w_dw[0], conv_b_dw, conv_ln_g, conv_ln_b, conv_b_pw2,
        norm_g.reshape(6, 128), conv_b_pw1.reshape(2, 128),
        c.reshape(8, 128), jnp.zeros((5, 128), F32)], axis=0)
    big = [ffn_w13, ffn_w2, conv_w_pw1, conv_w_pw2, w_kv_a, w_kv_b, w_q_a, w_q_b, w_o]
    got = _exchange([(small, "gather")] + [(w.astype(BF), "gather") for w in big], "gather_weights")
    sm = got[0]
    chan = lambda lo, hi: jnp.moveaxis(sm[:, lo:hi, :], 0, 1).reshape(hi - lo, D)
    w_dw_f, b_dw_f, ln_g_f, ln_b_f, b_pw2_f = chan(0, 31), chan(31, 32), chan(32, 33), chan(33, 34), chan(34, 35)
    norm_f = chan(35, 41).reshape(2, 3, D)
    b_pw1_f = sm[:, 41:43, :].reshape(1, 2 * D)
    c_all = sm[:, 43:51, :].reshape(N_DEV, D)
    w13_f = jnp.transpose(got[1], (1, 2, 3, 0, 4)).reshape(2, 2, D, 2 * DFF)
    w2_f = jnp.transpose(got[2], (1, 2, 0, 3, 4)).reshape(2, 2, DFF, D)
    pw1_f = jnp.transpose(got[3][:, 0], (1, 0, 2)).reshape(D, 2 * D)
    pw2_f = got[4].reshape(D, D)
    wkva = got[5].reshape(D, KV_LORA + ROPE)
    wkva_f = jnp.concatenate([wkva[:, :KV_LORA], jnp.zeros((D, 64), BF), wkva[:, KV_LORA:], jnp.zeros((D, 32), BF)], axis=1)
    wkvb_f = jnp.transpose(got[6], (1, 0, 2)).reshape(KV_LORA, NH * HP)
    wqa_f = got[7].reshape(D, Q_LORA)
    wqb = jnp.transpose(got[8][:, 0], (1, 0, 2)).reshape(Q_LORA, NH, 96)
    wqb_f = jnp.pad(wqb, ((0, 0), (0, 0), (0, HP - 96))).reshape(Q_LORA, NH * HP)
    wo_f = jnp.pad(got[9].reshape(NH, 64, D), ((0, 0), (64, 0), (0, 0))).reshape(NH * HP, D)

    n_ada = ada_w.shape[2]
    n_kva = kv_ada_w.shape[1]
    modp = _mod_fwd(c_all, ada_w[0], ada_w[1], kv_ada_w)
    (modr,) = _exchange([(modp.reshape(N_DEV, 1, 2 * n_ada + n_kva), "scatter")], "scatter_mod")
    modr = modr[:, 0, :]
    mod = jnp.transpose(modr[:, :2 * n_ada].reshape(N_DEV, 2, n_ada), (1, 0, 2)).reshape(2, 9 * D) + ada_b
    mod = mod.reshape(2, 9, D)
    kvmod = (modr[:, 2 * n_ada:].reshape(2 * D) + kv_ada_b).reshape(2, D)

    def sub_vec(l, idx):
        return _vec([norm_f[l, idx], mod[l, 3 * idx], mod[l, 3 * idx + 1], mod[l, 3 * idx + 2]])

    vec_kv = _vec([kv_norm_g, kvmod[0], kvmod[1]])
    cw = jnp.concatenate([w_dw_f, b_dw_f, ln_g_f, ln_b_f, b_pw2_f, mod[0, 5].reshape(1, D), jnp.zeros((4, D), F32)], axis=0)
    b1v = _vec([b_pw1_f])
    g_kva = _vec([kv_a_norm_g])
    g_qa = _vec([q_a_norm_g[0]])
    fgv = _vec([final_norm_g])

    inv_freq = 10000.0 ** (-jnp.arange(0, ROPE, 2, dtype=F32) / ROPE)
    ang = positions[0].astype(F32)[:, None] * inv_freq
    cs, sn = jnp.cos(ang), jnp.sin(ang)
    z16, z32, z64 = jnp.zeros((S, 16), F32), jnp.zeros((S, 32), F32), jnp.zeros((S, 64), F32)
    rope = jnp.concatenate([jnp.ones((S, 64), F32), cs, cs, z32,
                            z64, z16, sn, z32,
                            z64, -sn, z16, z32], axis=1)

    h0 = x[0]
    h1, u00, a00, b00 = _ffn_fwd(h0, sub_vec(0, 0), w13_f[0, 0], w2_f[0, 0])
    hn_c, pre = _conv_in_fwd(h1, sub_vec(0, 1), pw1_f, b1v)
    u2, z_c, h2 = _conv_out_fwd(pre, h1, cw, pw2_f)
    h3, u01, a01, b01 = _ffn_fwd(h2, sub_vec(0, 2), w13_f[0, 1], w2_f[0, 1])
    hn_kv, ckv, ckn, k_all, v_all = _kv_fwd(h3, vec_kv, wkva_f, g_kva, wkvb_f, rope)
    h4, u10, a10, b10 = _ffn_fwd(h3, sub_vec(1, 0), w13_f[1, 0], w2_f[1, 0])
    hn_q, qa, qan, q_all = _q_fwd(h4, sub_vec(1, 1), wqa_f, g_qa, wqb_f, rope)
    o_all, lse = _attn_fwd(q_all, k_all, v_all)
    h5 = _attn_out_fwd(o_all, h4, wo_f, sub_vec(1, 1))
    h6, u11, a11, b11 = _ffn_fwd(h5, sub_vec(1, 2), w13_f[1, 1], w2_f[1, 1])

    dh6, ps_fin = _final(h6, loss_target[0], fgv)
    loss = lax.psum(ps_fin[1, 0], ("x", "y", "c"))

    def ffn_back(dho, h_in, u, a, b, l, i):
        vec = sub_vec(l, 2 * i)
        dh, da, db, t, dhb, ps = _ffn_bwd(dho, h_in, a, b, vec, w13_f[l, i], w2_f[l, i])
        dwa = _wgrad(u, da, D, CH, "wgrad_w13")
        dwb = _wgrad(u, db, D, CH, "wgrad_w13")
        dw2, gs = _wgrad(t, dhb, CH, D, "wgrad_w2", gate=(w2_f[l, i], _vec([0.5 * vec[3]])))
        dgate = 0.5 * jnp.sum(gs[:, 0, :], axis=0)
        return dh, jnp.concatenate([dwa, dwb], axis=1), dw2, ps, dgate

    dh5, dw13_11, dw2_11, ps11, dg11 = ffn_back(dh6, h5, u11, a11, b11, 1, 1)
    vec_m1 = sub_vec(1, 1)
    do_all, delta, dhb5 = _attn_out_bwd(dh5, o_all, wo_f, vec_m1)
    dwo_p, gs_o = _wgrad(o_all, dhb5, D, D, "wgrad_wo", gate=(wo_f, _vec([vec_m1[3]])))
    dgm1 = jnp.sum(gs_o[:, 0, :], axis=0)
    dq_all, dk_all, dv_all = _attn_bwd(q_all, k_all, v_all, do_all, lse, delta)
    dh4, dqb, dqab, ps_q, ps_q2 = _q_bwd(dq_all, h4, qa, dh5, vec_m1, wqa_f, g_qa, wqb_f, rope)
    dwqb_p = _wgrad(qan, dqb, Q_LORA, D, "wgrad_wqb")
    dwqa = _wgrad(hn_q, dqab, D, Q_LORA, "wgrad_wqa")
    dh3a, dw13_10, dw2_10, ps10, dg10 = ffn_back(dh4, h3, u10, a10, b10, 1, 0)
    dh3, dkva, dkvb, ps_kv, ps_kv2 = _kv_bwd(dk_all, dv_all, h3, ckv, dh3a, vec_kv, wkva_f, g_kva, wkvb_f, rope)
    dwkva_p = _wgrad(hn_kv, dkva, D, KVA_P, "wgrad_wkva")
    dwkvb = _wgrad(ckn, dkvb, KV_LORA, D, "wgrad_wkvb")
    dh2, dw13_01, dw2_01, ps01, dg01 = ffn_back(dh3, h2, u01, a01, b01, 0, 1)
    vec_m0 = sub_vec(0, 1)
    du2, dhb2, ps_co = _conv_out_bwd(dh2, u2, cw, pw2_f)
    dpw2, gs_c = _wgrad(z_c, dhb2, D, D, "wgrad_pw2", gate=(pw2_f, _vec([vec_m0[3]])))
    dgm0 = jnp.sum(gs_c[:, 0, :], axis=0) + b_pw2_f[0] * ps_co[2]
    dh1, dpre, ps_ci, ps_dw, ps_b1 = _conv_in_bwd(du2, pre, h1, dh2, vec_m0, cw, pw1_f)
    dpw1 = _wgrad(hn_c, dpre, D, D, "wgrad_pw1")
    dh0, dw13_00, dw2_00, ps00, dg00 = ffn_back(dh1, h0, u00, a00, b00, 0, 0)

    dw13 = jnp.stack([jnp.stack([dw13_00, dw13_01]), jnp.stack([dw13_10, dw13_11])])
    dw2 = jnp.stack([jnp.stack([dw2_00, dw2_01]), jnp.stack([dw2_10, dw2_11])])
    send = [
        jnp.transpose(dw13.reshape(2, 2, D, N_DEV, 704), (3, 0, 1, 2, 4)),
        jnp.transpose(dw2.reshape(2, 2, N_DEV, 352, D), (2, 0, 1, 3, 4)),
        jnp.transpose(dpw1.reshape(D, N_DEV, 256), (1, 0, 2)),
        dpw2.reshape(N_DEV, 128, D),
        jnp.concatenate([dwkva_p[:, :KV_LORA], dwkva_p[:, KV_LORA + 64:KV_LORA + 96]], axis=1).reshape(N_DEV, 128, KV_LORA + ROPE),
        jnp.transpose(dwkvb.reshape(KV_LORA, N_DEV, 256), (1, 0, 2)),
        dwqa.reshape(N_DEV, 128, Q_LORA),
        jnp.transpose(dwqb_p.reshape(Q_LORA, NH, HP)[:, :, :96].reshape(Q_LORA, N_DEV, 192), (1, 0, 2)),
        dwo_p.reshape(NH, HP, D)[:, 64:, :].reshape(N_DEV, 128, D),
    ]
    dmod = jnp.stack([
        jnp.stack([ps00[1], ps00[2], dg00, ps_ci[1], ps_ci[2], dgm0, ps01[1], ps01[2], dg01]),
        jnp.stack([ps10[1], ps10[2], dg10, ps_q[1], ps_q[2], dgm1, ps11[1], ps11[2], dg11])])
    dnorm = jnp.stack([ps00[0], ps_ci[0], ps01[0], ps10[0], ps_q[0], ps11[0]])
    pieces = [dnorm, ps_b1[0], ps_dw[0:31], ps_dw[31], ps_co[0], ps_co[1], vec_m0[3] * ps_co[2],
              ps_kv[0], ps_kv2[0], ps_q2[0], ps_fin[0], dmod, ps_kv[1], ps_kv[2]]
    sizes = [int(np.prod(p.shape)) for p in pieces]
    offs = np.concatenate([[0], np.cumsum(sizes)]).astype(int)
    flat = jnp.concatenate([p.reshape(-1) for p in pieces]).reshape(-1, 128)
    got2 = _exchange([(flat, "gather")] + [(s, "scatter") for s in send], "exchange_grads")
    part = got2[0].reshape(N_DEV, -1)

    def piece(i, rows, cols):
        return part[:, offs[i]:offs[i + 1]].reshape(N_DEV, rows, cols)

    def mine(i, rows):
        return lax.dynamic_slice_in_dim(piece(i, rows, D), me * 128, 128, axis=2)

    dmod_all = piece(11, 2, 9 * D)
    c_t = jnp.transpose(c_all)
    g_ada = jnp.stack([_mod_wgrad(c_t, lax.dynamic_slice_in_dim(dmod_all[:, l], me * n_ada, n_ada, axis=1))
                       for l in range(2)])
    dkvmod_all = jnp.concatenate([piece(12, 1, D)[:, 0], piece(13, 1, D)[:, 0]], axis=1)
    g_kvada = _mod_wgrad(c_t, lax.dynamic_slice_in_dim(dkvmod_all, me * n_kva, n_kva, axis=1))

    parts = {
        'ada_w': g_ada.reshape(1, 2 * D, n_ada),
        'ada_b': dmod_all,
        'norm_g': mine(0, 6),
        'ffn_w13': got2[1].reshape(N_DEV, 4 * D, 704),
        'ffn_w2': got2[2].reshape(N_DEV, 4 * 352, D),
        'conv_w_pw1': got2[3],
        'conv_b_pw1': lax.dynamic_slice_in_dim(piece(1, 1, 2 * D), me * 256, 256, axis=2),
        'conv_w_dw': mine(2, 31),
        'conv_b_dw': mine(3, 1),
        'conv_ln_g': mine(4, 1),
        'conv_ln_b': mine(5, 1),
        'conv_w_pw2': got2[4],
        'conv_b_pw2': mine(6, 1),
        'kv_ada_w': g_kvada.reshape(1, D, n_kva),
        'kv_ada_b': dkvmod_all.reshape(N_DEV, 1, 2 * D),
        'kv_norm_g': piece(7, 1, D),
        'w_kv_a': got2[5],
        'kv_a_norm_g': piece(8, 1, KV_LORA),
        'w_kv_b': got2[6],
        'w_q_a': got2[7],
        'q_a_norm_g': piece(9, 1, Q_LORA),
        'w_q_b': got2[8],
        'w_o': got2[9],
        'final_norm_g': piece(10, 1, D),
    }
    grads, deltas, new_m, new_v = [], [], [], []
    for name in _WEIGHTS:
        w = given[name]
        p = parts[name]
        shape2 = p.shape[1:]
        g, dlt, m2, v2 = _adamw(p, w.reshape(shape2), given['m_' + name].reshape(shape2), given['v_' + name].reshape(shape2))
        grads.append(g.reshape(w.shape))
        deltas.append(dlt.reshape(w.shape))
        new_m.append(m2.reshape(w.shape))
        new_v.append(v2.reshape(w.shape))
    return (loss, dh0.reshape(1, S, D), *grads, *deltas, *new_m, *new_v)
```

```python
import functools

import numpy as np
import jax
import jax.numpy as jnp
from jax import lax
from jax.experimental import pallas as pl
from jax.experimental.pallas import tpu as pltpu

F32, BF = jnp.float32, jnp.bfloat16

D = 1024
DFF = 2816
CH = 1408
NH = 16
HP = 128
KV_LORA, Q_LORA, ROPE = 256, 512, 32
KVA_P = 384
CONV_W = 31
HALO = 32
EPS = 1e-6
SM_SCALE = float((64 + 32) ** -0.5)
LOG2E = 1.4426950408889634
EXP2_SCALE = SM_SCALE * LOG2E
NEG = -1e30
N_DEV = 8
MESH = pl.DeviceIdType.MESH

TS = 256
TA = 512
TW = 512
VMEM_LIMIT = 56 * 1024 * 1024

LR, B1, B2, EPS_ADAM, WD, STEP = 0.001, 0.9, 0.999, 1e-08, 0.01, 10

VMEM_FULL = pl.BlockSpec(memory_space=pltpu.VMEM)
HBM_FULL = pl.BlockSpec(memory_space=pltpu.HBM)


def _params(n_grid):
    return pltpu.CompilerParams(dimension_semantics=("arbitrary",) * n_grid, vmem_limit_bytes=VMEM_LIMIT)


def _dot(a, b):
    return jnp.dot(a, b, preferred_element_type=F32)


def _dot_nt(a, b):
    return lax.dot_general(a, b, (((1,), (1,)), ((), ())), preferred_element_type=F32)


def _dot_tn(a, b):
    return lax.dot_general(a, b, (((0,), (0,)), ((), ())), preferred_element_type=F32)


def _sum0(x):
    return jnp.sum(x, axis=0, keepdims=True)


def _mean1(x):
    return jnp.mean(x, axis=-1, keepdims=True)


def _sigmoid(x):
    return jax.nn.sigmoid(x)


def _rows(shape, imap):
    return pl.BlockSpec(shape, imap)


def _norm_mod(h, vec):
    r = lax.rsqrt(_mean1(h * h) + EPS)
    xhat = h * r
    u = (xhat * vec[0:1]) * (1.0 + vec[2:3]) + vec[1:2]
    return u, xhat, r


def _norm_mod_bwd(du, xhat, r, vec):
    g = vec[0:1]
    dxn = du * (1.0 + vec[2:3])
    dsh = _sum0(du)
    dsc = _sum0(du * (xhat * g))
    dg = _sum0(dxn * xhat)
    dxhat = dxn * g
    dh = r * (dxhat - xhat * _mean1(dxhat * xhat))
    return dh, dg, dsh, dsc


def _rms_bwd(dy, x, r, g):
    xhat = x * r
    dg = _sum0(dy * xhat)
    dxhat = dy * g
    return r * (dxhat - xhat * _mean1(dxhat * xhat)), dg


def _accumulate(ref, first, rows):
    @pl.when(first)
    def _():
        ref[...] = jnp.zeros(ref.shape, ref.dtype)

    for i, row in enumerate(rows):
        ref[i:i + 1, :] += row


def _rope(x, tab):
    return x * tab[:, 0:HP] + pltpu.roll(x, 16, 1) * tab[:, HP:2 * HP] + pltpu.roll(x, HP - 16, 1) * tab[:, 2 * HP:3 * HP]


def _rope_t(dy, tab):
    return (dy * tab[:, 0:HP] + pltpu.roll(dy * tab[:, HP:2 * HP], HP - 16, 1)
            + pltpu.roll(dy * tab[:, 2 * HP:3 * HP], 16, 1))


def _exchange(items, name):
    n = len(items)

    def body(*refs):
        ins, outs = refs[:n], refs[n:2 * n]
        send_sems, recv_sems, local_sems = refs[2 * n:]
        x, y, c = lax.axis_index("x"), lax.axis_index("y"), lax.axis_index("c")
        me = 4 * x + 2 * y + c

        def source(j, dev):
            return ins[j] if items[j][1] == "gather" else ins[j].at[dev]

        own = [pltpu.make_async_copy(source(j, me), outs[j].at[me], local_sems.at[j]) for j in range(n)]
        for cp in own:
            cp.start()
        remote = []
        for d in range(1, N_DEV):
            px = 1 - x if d & 4 else x
            py = 1 - y if d & 2 else y
            pc = 1 - c if d & 1 else c
            peer = 4 * px + 2 * py + pc
            for j in range(n):
                pltpu.make_async_remote_copy(
                    src_ref=source(j, peer), dst_ref=outs[j].at[me], send_sem=send_sems.at[j, d - 1],
                    recv_sem=recv_sems.at[j, d - 1], device_id=(px, py, pc), device_id_type=MESH).start()
                remote.append(pltpu.make_async_remote_copy(
                    src_ref=source(j, peer), dst_ref=outs[j].at[peer], send_sem=send_sems.at[j, d - 1],
                    recv_sem=recv_sems.at[j, d - 1], device_id=(px, py, pc), device_id_type=MESH))
        for cp in remote:
            cp.wait_send()
            cp.wait_recv()
        for cp in own:
            cp.wait()

    out_shape = []
    for arr, mode in items:
        shp = (N_DEV,) + tuple(arr.shape) if mode == "gather" else tuple(arr.shape)
        out_shape.append(jax.ShapeDtypeStruct(shp, arr.dtype))
    return pl.pallas_call(
        body, name=name, out_shape=out_shape,
        in_specs=[HBM_FULL] * n, out_specs=[HBM_FULL] * n,
        scratch_shapes=[pltpu.SemaphoreType.DMA((n, N_DEV - 1)), pltpu.SemaphoreType.DMA((n, N_DEV - 1)),
                        pltpu.SemaphoreType.DMA((n,))],
        compiler_params=pltpu.CompilerParams(has_side_effects=True),
    )(*[a for a, _ in items])


def _mod_fwd(c_all, w0, w1, wkv):
    n0, n1, n2 = w0.shape[1], w1.shape[1], wkv.shape[1]

    def body(c_ref, w0_ref, w1_ref, w2_ref, o_ref):
        cc = c_ref[...]
        s = cc * _sigmoid(cc)
        o_ref[:, 0:n0] = _dot(s, w0_ref[...])
        o_ref[:, n0:n0 + n1] = _dot(s, w1_ref[...])
        o_ref[:, n0 + n1:n0 + n1 + n2] = _dot(s, w2_ref[...])

    return pl.pallas_call(
        body, name="mod_fwd", out_shape=jax.ShapeDtypeStruct((N_DEV, n0 + n1 + n2), F32),
        in_specs=[VMEM_FULL] * 4, out_specs=VMEM_FULL,
        compiler_params=pltpu.CompilerParams(vmem_limit_bytes=VMEM_LIMIT),
    )(c_all, w0, w1, wkv)


def _mod_wgrad(c_t, dm):
    C = dm.shape[1]
    tr = 256

    def body(ct_ref, dm_ref, o_ref):
        ct = ct_ref[...]
        s = ct * _sigmoid(ct)
        dmv = dm_ref[...]
        lane = lax.broadcasted_iota(jnp.int32, (tr, N_DEV), 1)
        acc = jnp.zeros((tr, C), F32)
        for r in range(N_DEV):
            col = jnp.sum(jnp.where(lane == r, s, 0.0), axis=1, keepdims=True)
            acc = acc + col * dmv[r:r + 1, :]
        o_ref[...] = acc

    return pl.pallas_call(
        body, name="mod_wgrad", grid=(D // tr,), out_shape=jax.ShapeDtypeStruct((D, C), F32),
        in_specs=[_rows((tr, N_DEV), lambda i: (i, 0)), _rows((N_DEV, C), lambda i: (0, 0))],
        out_specs=_rows((tr, C), lambda i: (i, 0)), compiler_params=_params(1),
    )(c_t, dm)


def _ffn_fwd(h, vec, w13, w2):
    S = h.shape[0]

    def body(h_ref, vec_ref, w13_ref, w2_ref, ho_ref, u_ref, a_ref, b_ref):
        hv, vec = h_ref[...], vec_ref[...]
        u, _, _ = _norm_mod(hv, vec)
        ub = u.astype(BF)
        u_ref[...] = ub
        y = jnp.zeros((TS, D), F32)
        for k in range(DFF // CH):
            c0 = k * CH
            a = _dot(ub, w13_ref[:, c0:c0 + CH])
            b = _dot(ub, w13_ref[:, DFF + c0:DFF + c0 + CH])
            a_ref[:, c0:c0 + CH] = a.astype(BF)
            b_ref[:, c0:c0 + CH] = b.astype(BF)
            t = (a * _sigmoid(a)) * b
            y = y + _dot(t.astype(BF), w2_ref[c0:c0 + CH, :])
        ho_ref[...] = hv + (0.5 * vec[3:4]) * y

    tok = lambda i: (i, 0)
    return pl.pallas_call(
        body, name="ffn_fwd", grid=(S // TS,),
        out_shape=[jax.ShapeDtypeStruct((S, D), F32), jax.ShapeDtypeStruct((S, D), BF),
                   jax.ShapeDtypeStruct((S, DFF), BF), jax.ShapeDtypeStruct((S, DFF), BF)],
        in_specs=[_rows((TS, D), tok), _rows((8, D), lambda i: (0, 0)), VMEM_FULL, VMEM_FULL],
        out_specs=[_rows((TS, D), tok), _rows((TS, D), tok), _rows((TS, DFF), tok), _rows((TS, DFF), tok)],
        compiler_params=_params(1),
    )(h, vec, w13, w2)


def _ffn_bwd(dho, h, a, b, vec, w13, w2):
    S = h.shape[0]

    def body(dho_ref, h_ref, a_ref, b_ref, vec_ref, w13_ref, w2_ref,
             dh_ref, da_ref, db_ref, t_ref, dhb_ref, ps_ref):
        dho_v, vec = dho_ref[...], vec_ref[...]
        dhb_ref[...] = dho_v.astype(BF)
        dyb = ((0.5 * vec[3:4]) * dho_v).astype(BF)
        du = jnp.zeros((TS, D), F32)
        for k in range(DFF // CH):
            c0 = k * CH
            av = a_ref[:, c0:c0 + CH].astype(F32)
            bv = b_ref[:, c0:c0 + CH].astype(F32)
            dt = _dot_nt(dyb, w2_ref[c0:c0 + CH, :])
            sig = _sigmoid(av)
            sl = av * sig
            t_ref[:, c0:c0 + CH] = (sl * bv).astype(BF)
            dab = (dt * bv * (sig * (1.0 + av * (1.0 - sig)))).astype(BF)
            dbb = (dt * sl).astype(BF)
            da_ref[:, c0:c0 + CH] = dab
            db_ref[:, c0:c0 + CH] = dbb
            du = du + _dot_nt(dab, w13_ref[:, c0:c0 + CH]) + _dot_nt(dbb, w13_ref[:, DFF + c0:DFF + c0 + CH])
        _, xhat, r = _norm_mod(h_ref[...], vec)
        dhn, dg, dsh, dsc = _norm_mod_bwd(du, xhat, r, vec)
        dh_ref[...] = dho_v + dhn
        _accumulate(ps_ref, pl.program_id(0) == 0, [dg, dsh, dsc])

    tok = lambda i: (i, 0)
    return pl.pallas_call(
        body, name="ffn_bwd", grid=(S // TS,),
        out_shape=[jax.ShapeDtypeStruct((S, D), F32), jax.ShapeDtypeStruct((S, DFF), BF),
                   jax.ShapeDtypeStruct((S, DFF), BF), jax.ShapeDtypeStruct((S, DFF), BF),
                   jax.ShapeDtypeStruct((S, D), BF), jax.ShapeDtypeStruct((8, D), F32)],
        in_specs=[_rows((TS, D), tok), _rows((TS, D), tok), _rows((TS, DFF), tok), _rows((TS, DFF), tok),
                  _rows((8, D), lambda i: (0, 0)), VMEM_FULL, VMEM_FULL],
        out_specs=[_rows((TS, D), tok), _rows((TS, DFF), tok), _rows((TS, DFF), tok), _rows((TS, DFF), tok),
                   _rows((TS, D), tok), _rows((8, D), lambda i: (0, 0))],
        compiler_params=_params(1),
    )(dho, h, a, b, vec, w13, w2)


def _wgrad(a, b, tm, tn, name, gate=None):
    S, M = a.shape
    N = b.shape[1]
    n_s = S // TW

    def body(*refs):
        if gate is None:
            a_ref, b_ref, o_ref, acc_ref = refs
        else:
            a_ref, b_ref, w_ref, sc_ref, o_ref, gs_ref, acc_ref = refs
        s = pl.program_id(2)

        @pl.when(s == 0)
        def _():
            acc_ref[...] = jnp.zeros((tm, tn), F32)

        acc_ref[...] += _dot_tn(a_ref[...], b_ref[...])

        @pl.when(s == n_s - 1)
        def _():
            acc = acc_ref[...]
            if gate is None:
                o_ref[...] = acc.astype(BF)
            else:
                o_ref[...] = (acc * sc_ref[0:1, :]).astype(BF)
                gs_ref[...] = jnp.broadcast_to(_sum0(acc * w_ref[...].astype(F32)), (8, tn))

    in_specs = [_rows((TW, tm), lambda m, n, s: (s, m)), _rows((TW, tn), lambda m, n, s: (s, n))]
    out_shape = [jax.ShapeDtypeStruct((M, N), BF)]
    out_specs = [_rows((tm, tn), lambda m, n, s: (m, n))]
    args = [a, b]
    if gate is not None:
        in_specs += [_rows((tm, tn), lambda m, n, s: (m, n)), _rows((8, tn), lambda m, n, s: (0, n))]
        out_shape.append(jax.ShapeDtypeStruct((M // tm, 8, N), F32))
        out_specs.append(_rows((None, 8, tn), lambda m, n, s: (m, 0, n)))
        args += list(gate)
    res = pl.pallas_call(
        body, name=name, grid=(M // tm, N // tn, n_s), out_shape=out_shape, in_specs=in_specs,
        out_specs=out_specs, scratch_shapes=[pltpu.VMEM((tm, tn), F32)], compiler_params=_params(3),
    )(*args)
    return res[0] if gate is None else (res[0], res[1])


def _conv_in_fwd(h, vec, w1, b1):
    S = h.shape[0]

    def body(h_ref, vec_ref, w_ref, b1_ref, hn_ref, pre_ref):
        u, _, _ = _norm_mod(h_ref[...], vec_ref[...])
        ub = u.astype(BF)
        hn_ref[...] = ub
        pre_ref[...] = _dot(ub, w_ref[...]) + b1_ref[0:1, :]

    tok = lambda i: (i, 0)
    return pl.pallas_call(
        body, name="conv_in_fwd", grid=(S // TS,),
        out_shape=[jax.ShapeDtypeStruct((S, D), BF), jax.ShapeDtypeStruct((S, 2 * D), F32)],
        in_specs=[_rows((TS, D), tok), _rows((8, D), lambda i: (0, 0)), VMEM_FULL, _rows((8, 2 * D), lambda i: (0, 0))],
        out_specs=[_rows((TS, D), tok), _rows((TS, 2 * D), tok)], compiler_params=_params(1),
    )(h, vec, w1, b1)


def _glu(pre):
    return pre[:, :D] * _sigmoid(pre[:, D:])


def _layernorm(u2):
    mu = _mean1(u2)
    xc = u2 - mu
    rstd = lax.rsqrt(_mean1(xc * xc) + EPS)
    return xc * rstd, rstd


def _conv_out_fwd(pre, h, cw, w2):
    S = h.shape[0]
    hb = TS // HALO

    def body(pre_ref, ph_ref, h_ref, cw_ref, w2_ref, u2_ref, z_ref, ho_ref, win_ref):
        i = pl.program_id(0)
        cwv = cw_ref[...]
        win_ref[0:HALO, :] = jnp.where(i > 0, _glu(ph_ref[...]), 0.0)
        win_ref[HALO:HALO + TS, :] = _glu(pre_ref[...])
        u2 = jnp.broadcast_to(cwv[31:32], (TS, D))
        for j in range(CONV_W):
            off = HALO - (CONV_W - 1) + j
            u2 = u2 + cwv[j:j + 1] * win_ref[off:off + TS, :]
        u2_ref[...] = u2
        xh, _ = _layernorm(u2)
        un = xh * cwv[32:33] + cwv[33:34]
        zb = (un * _sigmoid(un)).astype(BF)
        z_ref[...] = zb
        y = _dot(zb, w2_ref[...]) + cwv[34:35]
        ho_ref[...] = h_ref[...] + cwv[35:36] * y

    tok = lambda i: (i, 0)
    return pl.pallas_call(
        body, name="conv_out_fwd", grid=(S // TS,),
        out_shape=[jax.ShapeDtypeStruct((S, D), F32), jax.ShapeDtypeStruct((S, D), BF), jax.ShapeDtypeStruct((S, D), F32)],
        in_specs=[_rows((TS, 2 * D), tok), _rows((HALO, 2 * D), lambda i: (jnp.maximum(i * hb - 1, 0), 0)),
                  _rows((TS, D), tok), _rows((40, D), lambda i: (0, 0)), VMEM_FULL],
        out_specs=[_rows((TS, D), tok), _rows((TS, D), tok), _rows((TS, D), tok)],
        scratch_shapes=[pltpu.VMEM((TS + HALO, D), F32)], compiler_params=_params(1),
    )(pre, pre, h, cw, w2)


def _conv_out_bwd(dho, u2, cw, w2):
    S = dho.shape[0]

    def body(dho_ref, u2_ref, cw_ref, w2_ref, du2_ref, dhb_ref, ps_ref):
        dho_v, cwv = dho_ref[...], cw_ref[...]
        dhb_ref[...] = dho_v.astype(BF)
        dz = _dot_nt((cwv[35:36] * dho_v).astype(BF), w2_ref[...])
        xh, rstd = _layernorm(u2_ref[...])
        un = xh * cwv[32:33] + cwv[33:34]
        sig = _sigmoid(un)
        dun = dz * (sig * (1.0 + un * (1.0 - sig)))
        dxh = dun * cwv[32:33]
        du2_ref[...] = rstd * (dxh - _mean1(dxh) - xh * _mean1(dxh * xh))
        _accumulate(ps_ref, pl.program_id(0) == 0, [_sum0(dun * xh), _sum0(dun), _sum0(dho_v)])

    tok = lambda i: (i, 0)
    return pl.pallas_call(
        body, name="conv_out_bwd", grid=(S // TS,),
        out_shape=[jax.ShapeDtypeStruct((S, D), F32), jax.ShapeDtypeStruct((S, D), BF), jax.ShapeDtypeStruct((8, D), F32)],
        in_specs=[_rows((TS, D), tok), _rows((TS, D), tok), _rows((40, D), lambda i: (0, 0)), VMEM_FULL],
        out_specs=[_rows((TS, D), tok), _rows((TS, D), tok), _rows((8, D), lambda i: (0, 0))],
        compiler_params=_params(1),
    )(dho, u2, cw, w2)


def _conv_in_bwd(du2, pre, h, dho, vec, cw, w1):
    S = h.shape[0]
    n_t = S // TS
    hb = TS // HALO

    def body(du2_ref, dh2h_ref, pre_ref, ph_ref, h_ref, dho_ref, vec_ref, cw_ref, w1_ref,
             dh_ref, dpre_ref, ps_ref, pw_ref, pb_ref, winu_ref, wind_ref):
        i = pl.program_id(0)
        vec, cwv = vec_ref[...], cw_ref[...]
        pre = pre_ref[...]
        av, sg = pre[:, :D], _sigmoid(pre[:, D:])
        winu_ref[0:HALO, :] = jnp.where(i > 0, _glu(ph_ref[...]), 0.0)
        winu_ref[HALO:HALO + TS, :] = av * sg
        du2v = du2_ref[...]
        wind_ref[0:TS, :] = du2v
        wind_ref[TS:TS + HALO, :] = jnp.where(i < n_t - 1, dh2h_ref[...], 0.0)

        @pl.when(i == 0)
        def _():
            pw_ref[...] = jnp.zeros((32, D), F32)

        du1 = jnp.zeros((TS, D), F32)
        for j in range(CONV_W):
            off = HALO - (CONV_W - 1) + j
            pw_ref[j:j + 1, :] += _sum0(du2v * winu_ref[off:off + TS, :])
            du1 = du1 + cwv[j:j + 1] * wind_ref[CONV_W - 1 - j:CONV_W - 1 - j + TS, :]
        pw_ref[31:32, :] += _sum0(du2v)
        da = du1 * sg
        dg = du1 * av * sg * (1.0 - sg)
        dab, dgb = da.astype(BF), dg.astype(BF)
        dpre_ref[:, :D] = dab
        dpre_ref[:, D:] = dgb

        @pl.when(i == 0)
        def _():
            pb_ref[...] = jnp.zeros((8, 2 * D), F32)

        pb_ref[0:1, :D] += _sum0(da)
        pb_ref[0:1, D:] += _sum0(dg)
        du = _dot_nt(dab, w1_ref[:, :D]) + _dot_nt(dgb, w1_ref[:, D:])
        _, xhat, r = _norm_mod(h_ref[...], vec)
        dhn, dgn, dsh, dsc = _norm_mod_bwd(du, xhat, r, vec)
        dh_ref[...] = dho_ref[...] + dhn
        _accumulate(ps_ref, i == 0, [dgn, dsh, dsc])

    tok = lambda i: (i, 0)
    fixed = lambda i: (0, 0)
    return pl.pallas_call(
        body, name="conv_in_bwd", grid=(n_t,),
        out_shape=[jax.ShapeDtypeStruct((S, D), F32), jax.ShapeDtypeStruct((S, 2 * D), BF),
                   jax.ShapeDtypeStruct((8, D), F32), jax.ShapeDtypeStruct((32, D), F32),
                   jax.ShapeDtypeStruct((8, 2 * D), F32)],
        in_specs=[_rows((TS, D), tok), _rows((HALO, D), lambda i: (jnp.minimum((i + 1) * hb, S // HALO - 1), 0)),
                  _rows((TS, 2 * D), tok), _rows((HALO, 2 * D), lambda i: (jnp.maximum(i * hb - 1, 0), 0)),
                  _rows((TS, D), tok), _rows((TS, D), tok), _rows((8, D), fixed), _rows((40, D), fixed), VMEM_FULL],
        out_specs=[_rows((TS, D), tok), _rows((TS, 2 * D), tok), _rows((8, D), fixed), _rows((32, D), fixed),
                   _rows((8, 2 * D), fixed)],
        scratch_shapes=[pltpu.VMEM((TS + HALO, D), F32), pltpu.VMEM((TS + HALO, D), F32)],
        compiler_params=_params(1),
    )(du2, du2, pre, pre, h, dho, vec, cw, w1)


def _lane():
    return lax.broadcasted_iota(jnp.int32, (TS, HP), 1)


def _kv_fwd(h, vec, wkva, g2, wkvb, rope):
    S = h.shape[0]

    def body(h_ref, vec_ref, wa_ref, g2_ref, wb_ref, rope_ref, hn_ref, ckv_ref, ckn_ref, k_ref, v_ref):
        u, _, _ = _norm_mod(h_ref[...], vec_ref[...])
        ub = u.astype(BF)
        hn_ref[...] = ub
        kva = _dot(ub, wa_ref[...])
        ckv = kva[:, :KV_LORA]
        ckv_ref[...] = ckv
        r2 = lax.rsqrt(_mean1(ckv * ckv) + EPS)
        cknb = ((ckv * r2) * g2_ref[0:1, :]).astype(BF)
        ckn_ref[...] = cknb
        kvb = _dot(cknb, wb_ref[...])
        kpe = _rope(kva[:, KV_LORA:KVA_P], rope_ref[...])
        lane = _lane()
        for hd in range(NH):
            blk = kvb[:, hd * HP:(hd + 1) * HP]
            k_ref[:, hd * HP:(hd + 1) * HP] = jnp.where(lane < 64, blk, kpe).astype(BF)
            v_ref[:, hd * HP:(hd + 1) * HP] = jnp.where(lane >= 64, blk, 0.0).astype(BF)

    tok = lambda i: (i, 0)
    fixed = lambda i: (0, 0)
    return pl.pallas_call(
        body, name="kv_fwd", grid=(S // TS,),
        out_shape=[jax.ShapeDtypeStruct((S, D), BF), jax.ShapeDtypeStruct((S, KV_LORA), F32),
                   jax.ShapeDtypeStruct((S, KV_LORA), BF), jax.ShapeDtypeStruct((S, NH * HP), BF),
                   jax.ShapeDtypeStruct((S, NH * HP), BF)],
        in_specs=[_rows((TS, D), tok), _rows((8, D), fixed), VMEM_FULL, _rows((8, KV_LORA), fixed), VMEM_FULL,
                  _rows((TS, 3 * HP), tok)],
        out_specs=[_rows((TS, D), tok), _rows((TS, KV_LORA), tok), _rows((TS, KV_LORA), tok),
                   _rows((TS, NH * HP), tok), _rows((TS, NH * HP), tok)],
        compiler_params=_params(1),
    )(h, vec, wkva, g2, wkvb, rope)


def _kv_bwd(dk, dv, h, ckv, dho, vec, wkva, g2, wkvb, rope):
    S = h.shape[0]

    def body(dk_ref, dv_ref, h_ref, ckv_ref, dho_ref, vec_ref, wa_ref, g2_ref, wb_ref, rope_ref,
             dh_ref, dkva_ref, dkvb_ref, ps_ref, ps2_ref):
        i = pl.program_id(0)
        vec = vec_ref[...]
        lane = _lane()
        dkpe = jnp.zeros((TS, HP), F32)
        for hd in range(NH):
            dkh = dk_ref[:, hd * HP:(hd + 1) * HP]
            dvh = dv_ref[:, hd * HP:(hd + 1) * HP]
            dkvb_ref[:, hd * HP:(hd + 1) * HP] = jnp.where(lane < 64, dkh, dvh).astype(BF)
            dkpe = dkpe + jnp.where(lane >= 64, dkh, 0.0)
        dkpe = _rope_t(dkpe, rope_ref[...])
        dckn = _dot_nt(dkvb_ref[...], wb_ref[...])
        ckv = ckv_ref[...]
        r2 = lax.rsqrt(_mean1(ckv * ckv) + EPS)
        dckv, dg2 = _rms_bwd(dckn, ckv, r2, g2_ref[0:1, :])
        dkva_ref[:, :KV_LORA] = dckv.astype(BF)
        dkva_ref[:, KV_LORA:KVA_P] = dkpe.astype(BF)
        du = _dot_nt(dkva_ref[...], wa_ref[...])
        _, xhat, r = _norm_mod(h_ref[...], vec)
        dhn, dgn, dsh, dsc = _norm_mod_bwd(du, xhat, r, vec)
        dh_ref[...] = dho_ref[...] + dhn
        _accumulate(ps_ref, i == 0, [dgn, dsh, dsc])
        _accumulate(ps2_ref, i == 0, [dg2])

    tok = lambda i: (i, 0)
    fixed = lambda i: (0, 0)
    return pl.pallas_call(
        body, name="kv_bwd", grid=(S // TS,),
        out_shape=[jax.ShapeDtypeStruct((S, D), F32), jax.ShapeDtypeStruct((S, KVA_P), BF),
                   jax.ShapeDtypeStruct((S, NH * HP), BF), jax.ShapeDtypeStruct((8, D), F32),
                   jax.ShapeDtypeStruct((8, KV_LORA), F32)],
        in_specs=[_rows((TS, NH * HP), tok), _rows((TS, NH * HP), tok), _rows((TS, D), tok), _rows((TS, KV_LORA), tok),
                  _rows((TS, D), tok), _rows((8, D), fixed), VMEM_FULL, _rows((8, KV_LORA), fixed), VMEM_FULL,
                  _rows((TS, 3 * HP), tok)],
        out_specs=[_rows((TS, D), tok), _rows((TS, KVA_P), tok), _rows((TS, NH * HP), tok), _rows((8, D), fixed),
                   _rows((8, KV_LORA), fixed)],
        compiler_params=_params(1),
    )(dk, dv, h, ckv, dho, vec, wkva, g2, wkvb, rope)


def _q_fwd(h, vec, wqa, g2, wqb, rope):
    S = h.shape[0]

    def body(h_ref, vec_ref, wa_ref, g2_ref, wb_ref, rope_ref, hn_ref, qa_ref, qan_ref, q_ref):
        u, _, _ = _norm_mod(h_ref[...], vec_ref[...])
        ub = u.astype(BF)
        hn_ref[...] = ub
        qa = _dot(ub, wa_ref[...])
        qa_ref[...] = qa
        r2 = lax.rsqrt(_mean1(qa * qa) + EPS)
        qanb = ((qa * r2) * g2_ref[0:1, :]).astype(BF)
        qan_ref[...] = qanb
        q = _dot(qanb, wb_ref[...])
        tab = rope_ref[...]
        for hd in range(NH):
            q_ref[:, hd * HP:(hd + 1) * HP] = _rope(q[:, hd * HP:(hd + 1) * HP], tab).astype(BF)

    tok = lambda i: (i, 0)
    fixed = lambda i: (0, 0)
    return pl.pallas_call(
        body, name="q_fwd", grid=(S // TS,),
        out_shape=[jax.ShapeDtypeStruct((S, D), BF), jax.ShapeDtypeStruct((S, Q_LORA), F32),
                   jax.ShapeDtypeStruct((S, Q_LORA), BF), jax.ShapeDtypeStruct((S, NH * HP), BF)],
        in_specs=[_rows((TS, D), tok), _rows((8, D), fixed), VMEM_FULL, _rows((8, Q_LORA), fixed), VMEM_FULL,
                  _rows((TS, 3 * HP), tok)],
        out_specs=[_rows((TS, D), tok), _rows((TS, Q_LORA), tok), _rows((TS, Q_LORA), tok), _rows((TS, NH * HP), tok)],
        compiler_params=_params(1),
    )(h, vec, wqa, g2, wqb, rope)


def _q_bwd(dq, h, qa, dho, vec, wqa, g2, wqb, rope):
    S = h.shape[0]

    def body(dq_ref, h_ref, qa_ref, dho_ref, vec_ref, wa_ref, g2_ref, wb_ref, rope_ref,
             dh_ref, dqb_ref, dqa_ref, ps_ref, ps2_ref):
        i = pl.program_id(0)
        vec, tab = vec_ref[...], rope_ref[...]
        for hd in range(NH):
            dqb_ref[:, hd * HP:(hd + 1) * HP] = _rope_t(dq_ref[:, hd * HP:(hd + 1) * HP], tab).astype(BF)
        dqan = _dot_nt(dqb_ref[...], wb_ref[...])
        qa = qa_ref[...]
        r2 = lax.rsqrt(_mean1(qa * qa) + EPS)
        dqa, dg2 = _rms_bwd(dqan, qa, r2, g2_ref[0:1, :])
        dqab = dqa.astype(BF)
        dqa_ref[...] = dqab
        du = _dot_nt(dqab, wa_ref[...])
        _, xhat, r = _norm_mod(h_ref[...], vec)
        dhn, dgn, dsh, dsc = _norm_mod_bwd(du, xhat, r, vec)
        dh_ref[...] = dho_ref[...] + dhn
        _accumulate(ps_ref, i == 0, [dgn, dsh, dsc])
        _accumulate(ps2_ref, i == 0, [dg2])

    tok = lambda i: (i, 0)
    fixed = lambda i: (0, 0)
    return pl.pallas_call(
        body, name="q_bwd", grid=(S // TS,),
        out_shape=[jax.ShapeDtypeStruct((S, D), F32), jax.ShapeDtypeStruct((S, NH * HP), BF),
                   jax.ShapeDtypeStruct((S, Q_LORA), BF), jax.ShapeDtypeStruct((8, D), F32),
                   jax.ShapeDtypeStruct((8, Q_LORA), F32)],
        in_specs=[_rows((TS, NH * HP), tok), _rows((TS, D), tok), _rows((TS, Q_LORA), tok), _rows((TS, D), tok),
                  _rows((8, D), fixed), VMEM_FULL, _rows((8, Q_LORA), fixed), VMEM_FULL, _rows((TS, 3 * HP), tok)],
        out_specs=[_rows((TS, D), tok), _rows((TS, NH * HP), tok), _rows((TS, Q_LORA), tok), _rows((8, D), fixed),
                   _rows((8, Q_LORA), fixed)],
        compiler_params=_params(1),
    )(dq, h, qa, dho, vec, wqa, g2, wqb, rope)


def _attn_fwd(q, k, v):
    S = q.shape[0]
    nq = S // TA
    groups = 2
    rg = TA // groups

    def softmax_pv(scores, vt, state, masks):
        out = []
        for g in range(groups):
            m, l, acc = state[g]
            s = scores[g] if masks is None else jnp.where(masks[g], scores[g], NEG)
            m_new = jnp.maximum(m, jnp.max(s, axis=1, keepdims=True))
            p = jnp.exp2((s - m_new) * EXP2_SCALE)
            alpha = jnp.exp2((m - m_new) * EXP2_SCALE)
            out.append((m_new, alpha * l + jnp.sum(p, axis=1, keepdims=True), alpha * acc + _dot(p.astype(BF), vt)))
        return tuple(out)

    def body(q_ref, k_ref, v_ref, o_ref, lse_ref):
        qi = pl.program_id(1)
        qs = [q_ref[g * rg:(g + 1) * rg, :] for g in range(groups)]

        def keys(j):
            return pl.ds(pl.multiple_of(j * TA, TA), TA)

        def scores_of(j):
            kt = k_ref[keys(j), :]
            return tuple(_dot_nt(qs[g], kt) for g in range(groups))

        state = tuple((jnp.full((rg, 1), NEG, F32), jnp.zeros((rg, 1), F32), jnp.zeros((rg, HP), F32))
                      for _ in range(groups))

        def step(kj, state):
            return softmax_pv(scores_of(kj), v_ref[keys(kj), :], state, None)

        state = lax.fori_loop(0, qi, step, state)
        row = lax.broadcasted_iota(jnp.int32, (rg, TA), 0)
        col = lax.broadcasted_iota(jnp.int32, (rg, TA), 1)
        final = softmax_pv(scores_of(qi), v_ref[keys(qi), :], state, [col <= row + g * rg for g in range(groups)])
        for g in range(groups):
            m, l, acc = final[g]
            o_ref[g * rg:(g + 1) * rg, :] = (acc / l).astype(BF)
            lse_ref[g * rg:(g + 1) * rg, :] = m * EXP2_SCALE + jnp.log(l) * LOG2E

    return pl.pallas_call(
        body, name="attn_fwd", grid=(NH, nq),
        out_shape=[jax.ShapeDtypeStruct((S, NH * HP), BF), jax.ShapeDtypeStruct((NH, S, 1), F32)],
        in_specs=[_rows((TA, HP), lambda h, i: (i, h)), _rows((S, HP), lambda h, i: (0, h)),
                  _rows((S, HP), lambda h, i: (0, h))],
        out_specs=[_rows((TA, HP), lambda h, i: (i, h)), _rows((None, TA, 1), lambda h, i: (h, i, 0))],
        compiler_params=_params(2),
    )(q, k, v)


def _attn_bwd(q, k, v, do, lse, delta):
    S = q.shape[0]
    nq = S // TA
    groups = 2
    rg = TA // groups

    def body(q_ref, do_ref, lse_ref, dl_ref, k_ref, v_ref, dq_ref, dk_ref, dv_ref):
        kj = pl.program_id(1)

        @pl.when(kj == 0)
        def _():
            dq_ref[...] = jnp.zeros((S, HP), F32)

        kt, vt = k_ref[...], v_ref[...]
        row = lax.broadcasted_iota(jnp.int32, (rg, TA), 0)
        col = lax.broadcasted_iota(jnp.int32, (rg, TA), 1)

        def tile(qi, dk, dv, diagonal):
            rows = [pl.ds(pl.multiple_of(qi * TA + g * rg, rg), rg) for g in range(groups)]
            qg = [q_ref[r, :] for r in rows]
            dog = [do_ref[r, :] for r in rows]
            scores = [_dot_nt(qg[g], kt) for g in range(groups)]
            dps = [_dot_nt(dog[g], vt) for g in range(groups)]
            for g in range(groups):
                p = jnp.exp2(scores[g] * EXP2_SCALE - lse_ref[rows[g], :])
                if diagonal:
                    p = jnp.where(col <= row + g * rg, p, 0.0)
                ds = p * (dps[g] - dl_ref[rows[g], :])
                pb, dsb = p.astype(BF), ds.astype(BF)
                dv = dv + _dot_tn(pb, dog[g])
                dk = dk + _dot_tn(dsb, qg[g])
                dq_ref[rows[g], :] += _dot(dsb, kt) * SM_SCALE
            return dk, dv

        zero = jnp.zeros((TA, HP), F32)
        dk, dv = tile(kj, zero, zero, True)
        dk, dv = lax.fori_loop(kj + 1, nq, lambda qi, cr: tile(qi, cr[0], cr[1], False), (dk, dv))
        dk_ref[...] = dk * SM_SCALE
        dv_ref[...] = dv

    head = lambda h, j: (0, h)
    col1 = lambda h, j: (h, 0, 0)
    return pl.pallas_call(
        body, name="attn_bwd", grid=(NH, nq), out_shape=[jax.ShapeDtypeStruct((S, NH * HP), F32)] * 3,
        in_specs=[_rows((S, HP), head), _rows((S, HP), head), _rows((None, S, 1), col1), _rows((None, S, 1), col1),
                  _rows((TA, HP), lambda h, j: (j, h)), _rows((TA, HP), lambda h, j: (j, h))],
        out_specs=[_rows((S, HP), head), _rows((TA, HP), lambda h, j: (j, h)), _rows((TA, HP), lambda h, j: (j, h))],
        compiler_params=_params(2),
    )(q, do, lse, delta, k, v)


def _attn_out_fwd(o, h, wo, vec):
    S = h.shape[0]

    def body(o_ref, h_ref, wo_ref, vec_ref, ho_ref):
        ho_ref[...] = h_ref[...] + vec_ref[3:4, :] * _dot(o_ref[...], wo_ref[...])

    tok = lambda i: (i, 0)
    return pl.pallas_call(
        body, name="attn_out_fwd", grid=(S // TS,), out_shape=jax.ShapeDtypeStruct((S, D), F32),
        in_specs=[_rows((TS, NH * HP), tok), _rows((TS, D), tok), VMEM_FULL, _rows((8, D), lambda i: (0, 0))],
        out_specs=_rows((TS, D), tok), compiler_params=_params(1),
    )(o, h, wo, vec)


def _attn_out_bwd(dho, o, wo, vec):
    S = dho.shape[0]

    def body(dho_ref, o_ref, wo_ref, vec_ref, do_ref, dl_ref, dhb_ref):
        dho_v = dho_ref[...]
        dhb_ref[...] = dho_v.astype(BF)
        do = _dot_nt((vec_ref[3:4, :] * dho_v).astype(BF), wo_ref[...])
        do_ref[...] = do.astype(BF)
        prod = do * o_ref[...].astype(F32)
        for hd in range(NH):
            dl_ref[hd] = jnp.sum(prod[:, hd * HP:(hd + 1) * HP], axis=1, keepdims=True)

    tok = lambda i: (i, 0)
    return pl.pallas_call(
        body, name="attn_out_bwd", grid=(S // TS,),
        out_shape=[jax.ShapeDtypeStruct((S, NH * HP), BF), jax.ShapeDtypeStruct((NH, S, 1), F32),
                   jax.ShapeDtypeStruct((S, D), BF)],
        in_specs=[_rows((TS, D), tok), _rows((TS, NH * HP), tok), VMEM_FULL, _rows((8, D), lambda i: (0, 0))],
        out_specs=[_rows((TS, NH * HP), tok), _rows((NH, TS, 1), lambda i: (0, i, 0)), _rows((TS, D), tok)],
        compiler_params=_params(1),
    )(dho, o, wo, vec)


def _final(h, target, fg):
    S = h.shape[0]

    def body(h_ref, t_ref, g_ref, dh_ref, ps_ref):
        hv, g = h_ref[...], g_ref[0:1, :]
        r = lax.rsqrt(_mean1(hv * hv) + EPS)
        xhat = hv * r
        err = xhat * g - t_ref[...]
        loss = 0.5 * jnp.sum(_mean1(err * err), axis=0, keepdims=True)
        dy = err * (1.0 / D)
        dxhat = dy * g
        dh_ref[...] = r * (dxhat - xhat * _mean1(dxhat * xhat))
        _accumulate(ps_ref, pl.program_id(0) == 0, [_sum0(dy * xhat), jnp.broadcast_to(loss, (1, D))])

    tok = lambda i: (i, 0)
    return pl.pallas_call(
        body, name="final_loss", grid=(S // TS,),
        out_shape=[jax.ShapeDtypeStruct((S, D), F32), jax.ShapeDtypeStruct((8, D), F32)],
        in_specs=[_rows((TS, D), tok), _rows((TS, D), tok), _rows((8, D), lambda i: (0, 0))],
        out_specs=[_rows((TS, D), tok), _rows((8, D), lambda i: (0, 0))], compiler_params=_params(1),
    )(h, target, fg)


def _row_tile(r):
    if r <= 256:
        return r
    for t in range(256, 7, -8):
        if r % t == 0:
            return t
    return r


def _adamw(parts, w, m, v):
    P, R, C = parts.shape
    tr = _row_tile(R)

    def body(p_ref, w_ref, m_ref, v_ref, g_ref, d_ref, mo_ref, vo_ref):
        g = p_ref[0].astype(F32)
        for k in range(1, P):
            g = g + p_ref[k].astype(F32)
        g_ref[...] = g
        m2 = B1 * m_ref[...] + (1.0 - B1) * g
        v2 = B2 * v_ref[...] + (1.0 - B2) * (g * g)
        mo_ref[...] = m2
        vo_ref[...] = v2
        m_hat = m2 / (1.0 - B1 ** STEP)
        v_hat = v2 / (1.0 - B2 ** STEP)
        d_ref[...] = -LR * (m_hat / (jnp.sqrt(v_hat) + EPS_ADAM) + WD * w_ref[...])

    blk = _rows((tr, C), lambda i: (i, 0))
    return pl.pallas_call(
        body, name="adamw", grid=(R // tr,), out_shape=[jax.ShapeDtypeStruct((R, C), F32)] * 4,
        in_specs=[_rows((P, tr, C), lambda i: (0, i, 0)), blk, blk, blk], out_specs=[blk] * 4,
        compiler_params=_params(1),
    )(parts, w, m, v)


_WEIGHTS = ['ada_w', 'ada_b', 'norm_g', 'ffn_w13', 'ffn_w2', 'conv_w_pw1', 'conv_b_pw1', 'conv_w_dw', 'conv_b_dw',
            'conv_ln_g', 'conv_ln_b', 'conv_w_pw2', 'conv_b_pw2', 'kv_ada_w', 'kv_ada_b', 'kv_norm_g', 'w_kv_a',
            'kv_a_norm_g', 'w_kv_b', 'w_q_a', 'q_a_norm_g', 'w_q_b', 'w_o', 'final_norm_g']


def _vec(rows):
    rows = [r.reshape(1, -1).astype(F32) for r in rows]
    return jnp.concatenate(rows + [jnp.zeros((8 - len(rows), rows[0].shape[1]), F32)], axis=0)


def kernel(x, c, positions, ada_w, ada_b, norm_g, ffn_w13, ffn_w2, conv_w_pw1, conv_b_pw1, conv_w_dw, conv_b_dw, conv_ln_g, conv_ln_b, conv_w_pw2, conv_b_pw2, kv_ada_w, kv_ada_b, kv_norm_g, w_kv_a, kv_a_norm_g, w_kv_b, w_q_a, q_a_norm_g, w_q_b, w_o, final_norm_g, loss_target, m_ada_w, m_ada_b, m_norm_g, m_ffn_w13, m_ffn_w2, m_conv_w_pw1, m_conv_b_pw1, m_conv_w_dw, m_conv_b_dw, m_conv_ln_g, m_conv_ln_b, m_conv_w_pw2, m_conv_b_pw2, m_kv_ada_w, m_kv_ada_b, m_kv_norm_g, m_w_kv_a, m_kv_a_norm_g, m_w_kv_b, m_w_q_a, m_q_a_norm_g, m_w_q_b, m_w_o, m_final_norm_g, v_ada_w, v_ada_b, v_norm_g, v_ffn_w13, v_ffn_w2, v_conv_w_pw1, v_conv_b_pw1, v_conv_w_dw, v_conv_b_dw, v_conv_ln_g, v_conv_ln_b, v_conv_w_pw2, v_conv_b_pw2, v_kv_ada_w, v_kv_ada_b, v_kv_norm_g, v_w_kv_a, v_kv_a_norm_g, v_w_kv_b, v_w_q_a, v_q_a_norm_g, v_w_q_b, v_w_o, v_final_norm_g):
    given = dict(locals())
    S = x.shape[1]
    me = 4 * lax.axis_index("x") + 2 * lax.axis_index("y") + lax.axis_index("c")

    small = jnp.concatenate([
        conv_w_dw[0], conv_b_dw, conv_ln_g, conv_ln_b, conv_b_pw2,
        norm_g.reshape(6, 128), conv_b_pw1.reshape(2, 128),
        c.reshape(8, 128), jnp.zeros((5, 128), F32)], axis=0)
    big = [ffn_w13, ffn_w2, conv_w_pw1, conv_w_pw2, w_kv_a, w_kv_b, w_q_a, w_q_b, w_o]
    got = _exchange([(small, "gather")] + [(w.astype(BF), "gather") for w in big], "gather_weights")
    sm = got[0]
    chan = lambda lo, hi: jnp.moveaxis(sm[:, lo:hi, :], 0, 1).reshape(hi - lo, D)
    w_dw_f, b_dw_f, ln_g_f, ln_b_f, b_pw2_f = chan(0, 31), chan(31, 32), chan(32, 33), chan(33, 34), chan(34, 35)
    norm_f = chan(35, 41).reshape(2, 3, D)
    b_pw1_f = sm[:, 41:43, :].reshape(1, 2 * D)
    c_all = sm[:, 43:51, :].reshape(N_DEV, D)
    w13_f = jnp.transpose(got[1], (1, 2, 3, 0, 4)).reshape(2, 2, D, 2 * DFF)
    w2_f = jnp.transpose(got[2], (1, 2, 0, 3, 4)).reshape(2, 2, DFF, D)
    pw1_f = jnp.transpose(got[3][:, 0], (1, 0, 2)).reshape(D, 2 * D)
    pw2_f = got[4].reshape(D, D)
    wkva = got[5].reshape(D, KV_LORA + ROPE)
    wkva_f = jnp.concatenate([wkva[:, :KV_LORA], jnp.zeros((D, 64), BF), wkva[:, KV_LORA:], jnp.zeros((D, 32), BF)], axis=1)
    wkvb_f = jnp.transpose(got[6], (1, 0, 2)).reshape(KV_LORA, NH * HP)
    wqa_f = got[7].reshape(D, Q_LORA)
    wqb = jnp.transpose(got[8][:, 0], (1, 0, 2)).reshape(Q_LORA, NH, 96)
    wqb_f = jnp.pad(wqb, ((0, 0), (0, 0), (0, HP - 96))).reshape(Q_LORA, NH * HP)
    wo_f = jnp.pad(got[9].reshape(NH, 64, D), ((0, 0), (64, 0), (0, 0))).reshape(NH * HP, D)

    n_ada = ada_w.shape[2]
    n_kva = kv_ada_w.shape[1]
    modp = _mod_fwd(c_all, ada_w[0], ada_w[1], kv_ada_w)
    (modr,) = _exchange([(modp.reshape(N_DEV, 1, 2 * n_ada + n_kva), "scatter")], "scatter_mod")
    modr = modr[:, 0, :]
    mod = jnp.transpose(modr[:, :2 * n_ada].reshape(N_DEV, 2, n_ada), (1, 0, 2)).reshape(2, 9 * D) + ada_b
    mod = mod.reshape(2, 9, D)
    kvmod = (modr[:, 2 * n_ada:].reshape(2 * D) + kv_ada_b).reshape(2, D)

    def sub_vec(l, idx):
        return _vec([norm_f[l, idx], mod[l, 3 * idx], mod[l, 3 * idx + 1], mod[l, 3 * idx + 2]])

    vec_kv = _vec([kv_norm_g, kvmod[0], kvmod[1]])
    cw = jnp.concatenate([w_dw_f, b_dw_f, ln_g_f, ln_b_f, b_pw2_f, mod[0, 5].reshape(1, D), jnp.zeros((4, D), F32)], axis=0)
    b1v = _vec([b_pw1_f])
    g_kva = _vec([kv_a_norm_g])
    g_qa = _vec([q_a_norm_g[0]])
    fgv = _vec([final_norm_g])

    inv_freq = 10000.0 ** (-jnp.arange(0, ROPE, 2, dtype=F32) / ROPE)
    ang = positions[0].astype(F32)[:, None] * inv_freq
    cs, sn = jnp.cos(ang), jnp.sin(ang)
    z16, z32, z64 = jnp.zeros((S, 16), F32), jnp.zeros((S, 32), F32), jnp.zeros((S, 64), F32)
    rope = jnp.concatenate([jnp.ones((S, 64), F32), cs, cs, z32,
                            z64, z16, sn, z32,
                            z64, -sn, z16, z32], axis=1)

    h0 = x[0]
    h1, u00, a00, b00 = _ffn_fwd(h0, sub_vec(0, 0), w13_f[0, 0], w2_f[0, 0])
    hn_c, pre = _conv_in_fwd(h1, sub_vec(0, 1), pw1_f, b1v)
    u2, z_c, h2 = _conv_out_fwd(pre, h1, cw, pw2_f)
    h3, u01, a01, b01 = _ffn_fwd(h2, sub_vec(0, 2), w13_f[0, 1], w2_f[0, 1])
    hn_kv, ckv, ckn, k_all, v_all = _kv_fwd(h3, vec_kv, wkva_f, g_kva, wkvb_f, rope)
    h4, u10, a10, b10 = _ffn_fwd(h3, sub_vec(1, 0), w13_f[1, 0], w2_f[1, 0])
    hn_q, qa, qan, q_all = _q_fwd(h4, sub_vec(1, 1), wqa_f, g_qa, wqb_f, rope)
    o_all, lse = _attn_fwd(q_all, k_all, v_all)
    h5 = _attn_out_fwd(o_all, h4, wo_f, sub_vec(1, 1))
    h6, u11, a11, b11 = _ffn_fwd(h5, sub_vec(1, 2), w13_f[1, 1], w2_f[1, 1])

    dh6, ps_fin = _final(h6, loss_target[0], fgv)
    loss = lax.psum(ps_fin[1, 0], ("x", "y", "c"))

    def ffn_back(dho, h_in, u, a, b, l, i):
        vec = sub_vec(l, 2 * i)
        dh, da, db, t, dhb, ps = _ffn_bwd(dho, h_in, a, b, vec, w13_f[l, i], w2_f[l, i])
        dwa = _wgrad(u, da, D, CH, "wgrad_w13")
        dwb = _wgrad(u, db, D, CH, "wgrad_w13")
        dw2, gs = _wgrad(t, dhb, CH, D, "wgrad_w2", gate=(w2_f[l, i], _vec([0.5 * vec[3]])))
        dgate = 0.5 * jnp.sum(gs[:, 0, :], axis=0)
        return dh, jnp.concatenate([dwa, dwb], axis=1), dw2, ps, dgate

    dh5, dw13_11, dw2_11, ps11, dg11 = ffn_back(dh6, h5, u11, a11, b11, 1, 1)
    vec_m1 = sub_vec(1, 1)
    do_all, delta, dhb5 = _attn_out_bwd(dh5, o_all, wo_f, vec_m1)
    dwo_p, gs_o = _wgrad(o_all, dhb5, D, D, "wgrad_wo", gate=(wo_f, _vec([vec_m1[3]])))
    dgm1 = jnp.sum(gs_o[:, 0, :], axis=0)
    dq_all, dk_all, dv_all = _attn_bwd(q_all, k_all, v_all, do_all, lse, delta)
    dh4, dqb, dqab, ps_q, ps_q2 = _q_bwd(dq_all, h4, qa, dh5, vec_m1, wqa_f, g_qa, wqb_f, rope)
    dwqb_p = _wgrad(qan, dqb, Q_LORA, D, "wgrad_wqb")
    dwqa = _wgrad(hn_q, dqab, D, Q_LORA, "wgrad_wqa")
    dh3a, dw13_10, dw2_10, ps10, dg10 = ffn_back(dh4, h3, u10, a10, b10, 1, 0)
    dh3, dkva, dkvb, ps_kv, ps_kv2 = _kv_bwd(dk_all, dv_all, h3, ckv, dh3a, vec_kv, wkva_f, g_kva, wkvb_f, rope)
    dwkva_p = _wgrad(hn_kv, dkva, D, KVA_P, "wgrad_wkva")
    dwkvb = _wgrad(ckn, dkvb, KV_LORA, D, "wgrad_wkvb")
    dh2, dw13_01, dw2_01, ps01, dg01 = ffn_back(dh3, h2, u01, a01, b01, 0, 1)
    vec_m0 = sub_vec(0, 1)
    du2, dhb2, ps_co = _conv_out_bwd(dh2, u2, cw, pw2_f)
    dpw2, gs_c = _wgrad(z_c, dhb2, D, D, "wgrad_pw2", gate=(pw2_f, _vec([vec_m0[3]])))
    dgm0 = jnp.sum(gs_c[:, 0, :], axis=0) + b_pw2_f[0] * ps_co[2]
    dh1, dpre, ps_ci, ps_dw, ps_b1 = _conv_in_bwd(du2, pre, h1, dh2, vec_m0, cw, pw1_f)
    dpw1 = _wgrad(hn_c, dpre, D, D, "wgrad_pw1")
    dh0, dw13_00, dw2_00, ps00, dg00 = ffn_back(dh1, h0, u00, a00, b00, 0, 0)

    dw13 = jnp.stack([jnp.stack([dw13_00, dw13_01]), jnp.stack([dw13_10, dw13_11])])
    dw2 = jnp.stack([jnp.stack([dw2_00, dw2_01]), jnp.stack([dw2_10, dw2_11])])
    send = [
        jnp.transpose(dw13.reshape(2, 2, D, N_DEV, 704), (3, 0, 1, 2, 4)),
        jnp.transpose(dw2.reshape(2, 2, N_DEV, 352, D), (2, 0, 1, 3, 4)),
        jnp.transpose(dpw1.reshape(D, N_DEV, 256), (1, 0, 2)),
        dpw2.reshape(N_DEV, 128, D),
        jnp.concatenate([dwkva_p[:, :KV_LORA], dwkva_p[:, KV_LORA + 64:KV_LORA + 96]], axis=1).reshape(N_DEV, 128, KV_LORA + ROPE),
        jnp.transpose(dwkvb.reshape(KV_LORA, N_DEV, 256), (1, 0, 2)),
        dwqa.reshape(N_DEV, 128, Q_LORA),
        jnp.transpose(dwqb_p.reshape(Q_LORA, NH, HP)[:, :, :96].reshape(Q_LORA, N_DEV, 192), (1, 0, 2)),
        dwo_p.reshape(NH, HP, D)[:, 64:, :].reshape(N_DEV, 128, D),
    ]
    dmod = jnp.stack([
        jnp.stack([ps00[1], ps00[2], dg00, ps_ci[1], ps_ci[2], dgm0, ps01[1], ps01[2], dg01]),
        jnp.stack([ps10[1], ps10[2], dg10, ps_q[1], ps_q[2], dgm1, ps11[1], ps11[2], dg11])])
    dnorm = jnp.stack([ps00[0], ps_ci[0], ps01[0], ps10[0], ps_q[0], ps11[0]])
    pieces = [dnorm, ps_b1[0], ps_dw[0:31], ps_dw[31], ps_co[0], ps_co[1], vec_m0[3] * ps_co[2],
              ps_kv[0], ps_kv2[0], ps_q2[0], ps_fin[0], dmod, ps_kv[1], ps_kv[2]]
    sizes = [int(np.prod(p.shape)) for p in pieces]
    offs = np.concatenate([[0], np.cumsum(sizes)]).astype(int)
    flat = jnp.concatenate([p.reshape(-1) for p in pieces]).reshape(-1, 128)
    got2 = _exchange([(flat, "gather")] + [(s, "scatter") for s in send], "exchange_grads")
    part = got2[0].reshape(N_DEV, -1)

    def piece(i, rows, cols):
        return part[:, offs[i]:offs[i + 1]].reshape(N_DEV, rows, cols)

    def mine(i, rows):
        return lax.dynamic_slice_in_dim(piece(i, rows, D), me * 128, 128, axis=2)

    dmod_all = piece(11, 2, 9 * D)
    c_t = jnp.transpose(c_all)
    g_ada = jnp.stack([_mod_wgrad(c_t, lax.dynamic_slice_in_dim(dmod_all[:, l], me * n_ada, n_ada, axis=1))
                       for l in range(2)])
    dkvmod_all = jnp.concatenate([piece(12, 1, D)[:, 0], piece(13, 1, D)[:, 0]], axis=1)
    g_kvada = _mod_wgrad(c_t, lax.dynamic_slice_in_dim(dkvmod_all, me * n_kva, n_kva, axis=1))

    parts = {
        'ada_w': g_ada.reshape(1, 2 * D, n_ada),
        'ada_b': dmod_all,
        'norm_g': mine(0, 6),
        'ffn_w13': got2[1].reshape(N_DEV, 4 * D, 704),
        'ffn_w2': got2[2].reshape(N_DEV, 4 * 352, D),
        'conv_w_pw1': got2[3],
        'conv_b_pw1': lax.dynamic_slice_in_dim(piece(1, 1, 2 * D), me * 256, 256, axis=2),
        'conv_w_dw': mine(2, 31),
        'conv_b_dw': mine(3, 1),
        'conv_ln_g': mine(4, 1),
        'conv_ln_b': mine(5, 1),
        'conv_w_pw2': got2[4],
        'conv_b_pw2': mine(6, 1),
        'kv_ada_w': g_kvada.reshape(1, D, n_kva),
        'kv_ada_b': dkvmod_all.reshape(N_DEV, 1, 2 * D),
        'kv_norm_g': piece(7, 1, D),
        'w_kv_a': got2[5],
        'kv_a_norm_g': piece(8, 1, KV_LORA),
        'w_kv_b': got2[6],
        'w_q_a': got2[7],
        'q_a_norm_g': piece(9, 1, Q_LORA),
        'w_q_b': got2[8],
        'w_o': got2[9],
        'final_norm_g': piece(10, 1, D),
    }
    grads, deltas, new_m, new_v = [], [], [], []
    for name in _WEIGHTS:
        w = given[name]
        p = parts[name]
        shape2 = p.shape[1:]
        g, dlt, m2, v2 = _adamw(p, w.reshape(shape2), given['m_' + name].reshape(shape2), given['v_' + name].reshape(shape2))
        grads.append(g.reshape(w.shape))
        deltas.append(dlt.reshape(w.shape))
        new_m.append(m2.reshape(w.shape))
        new_v.append(v2.reshape(w.shape))
    return (loss, dh0.reshape(1, S, D), *grads, *deltas, *new_m, *new_v)
```

```python
import functools

import numpy as np
import jax
import jax.numpy as jnp
from jax import lax
from jax.experimental import pallas as pl
from jax.experimental.pallas import tpu as pltpu

F32, BF = jnp.float32, jnp.bfloat16

D = 1024
DFF = 2816
CH = 1408
NH = 16
HP = 128
KV_LORA, Q_LORA, ROPE = 256, 512, 32
KVA_P = 384
CONV_W = 31
HALO = 32
EPS = 1e-6
SM_SCALE = float((64 + 32) ** -0.5)
LOG2E = 1.4426950408889634
EXP2_SCALE = SM_SCALE * LOG2E
NEG = -1e30
N_DEV = 8
MESH = pl.DeviceIdType.MESH

TS = 256
TA = 512
TA_FWD = 1024
ATT_ROWS = 256
TW = 512
VMEM_LIMIT = 56 * 1024 * 1024

LR, B1, B2, EPS_ADAM, WD, STEP = 0.001, 0.9, 0.999, 1e-08, 0.01, 10

VMEM_FULL = pl.BlockSpec(memory_space=pltpu.VMEM)
HBM_FULL = pl.BlockSpec(memory_space=pltpu.HBM)


def _params(n_grid):
    return pltpu.CompilerParams(dimension_semantics=("arbitrary",) * n_grid, vmem_limit_bytes=VMEM_LIMIT)


def _dot(a, b):
    return jnp.dot(a, b, preferred_element_type=F32)


def _dot_nt(a, b):
    return lax.dot_general(a, b, (((1,), (1,)), ((), ())), preferred_element_type=F32)


def _dot_tn(a, b):
    return lax.dot_general(a, b, (((0,), (0,)), ((), ())), preferred_element_type=F32)


def _sum0(x):
    return jnp.sum(x, axis=0, keepdims=True)


def _mean1(x):
    return jnp.mean(x, axis=-1, keepdims=True)


def _sigmoid(x):
    return jax.nn.sigmoid(x)


def _rows(shape, imap):
    return pl.BlockSpec(shape, imap)


def _norm_mod(h, vec):
    r = lax.rsqrt(_mean1(h * h) + EPS)
    xhat = h * r
    u = (xhat * vec[0:1]) * (1.0 + vec[2:3]) + vec[1:2]
    return u, xhat, r


def _norm_mod_bwd(du, xhat, r, vec):
    g = vec[0:1]
    dxn = du * (1.0 + vec[2:3])
    dsh = _sum0(du)
    dsc = _sum0(du * (xhat * g))
    dg = _sum0(dxn * xhat)
    dxhat = dxn * g
    dh = r * (dxhat - xhat * _mean1(dxhat * xhat))
    return dh, dg, dsh, dsc


def _rms_bwd(dy, x, r, g):
    xhat = x * r
    dg = _sum0(dy * xhat)
    dxhat = dy * g
    return r * (dxhat - xhat * _mean1(dxhat * xhat)), dg


def _accumulate(ref, first, rows):
    @pl.when(first)
    def _():
        ref[...] = jnp.zeros(ref.shape, ref.dtype)

    for i, row in enumerate(rows):
        ref[i:i + 1, :] += row


def _rope(x, tab):
    return x * tab[:, 0:HP] + pltpu.roll(x, 16, 1) * tab[:, HP:2 * HP] + pltpu.roll(x, HP - 16, 1) * tab[:, 2 * HP:3 * HP]


def _rope_t(dy, tab):
    return (dy * tab[:, 0:HP] + pltpu.roll(dy * tab[:, HP:2 * HP], HP - 16, 1)
            + pltpu.roll(dy * tab[:, 2 * HP:3 * HP], 16, 1))


def _exchange(items, name):
    n = len(items)

    def body(*refs):
        ins, outs = refs[:n], refs[n:2 * n]
        send_sems, recv_sems, local_sems = refs[2 * n:]
        x, y, c = lax.axis_index("x"), lax.axis_index("y"), lax.axis_index("c")
        me = 4 * x + 2 * y + c

        def source(j, dev):
            return ins[j] if items[j][1] == "gather" else ins[j].at[dev]

        own = [pltpu.make_async_copy(source(j, me), outs[j].at[me], local_sems.at[j]) for j in range(n)]
        for cp in own:
            cp.start()
        remote = []
        for d in range(1, N_DEV):
            px = 1 - x if d & 4 else x
            py = 1 - y if d & 2 else y
            pc = 1 - c if d & 1 else c
            peer = 4 * px + 2 * py + pc
            for j in range(n):
                pltpu.make_async_remote_copy(
                    src_ref=source(j, peer), dst_ref=outs[j].at[me], send_sem=send_sems.at[j, d - 1],
                    recv_sem=recv_sems.at[j, d - 1], device_id=(px, py, pc), device_id_type=MESH).start()
                remote.append(pltpu.make_async_remote_copy(
                    src_ref=source(j, peer), dst_ref=outs[j].at[peer], send_sem=send_sems.at[j, d - 1],
                    recv_sem=recv_sems.at[j, d - 1], device_id=(px, py, pc), device_id_type=MESH))
        for cp in remote:
            cp.wait_send()
            cp.wait_recv()
        for cp in own:
            cp.wait()

    out_shape = []
    for arr, mode in items:
        shp = (N_DEV,) + tuple(arr.shape) if mode == "gather" else tuple(arr.shape)
        out_shape.append(jax.ShapeDtypeStruct(shp, arr.dtype))
    return pl.pallas_call(
        body, name=name, out_shape=out_shape,
        in_specs=[HBM_FULL] * n, out_specs=[HBM_FULL] * n,
        scratch_shapes=[pltpu.SemaphoreType.DMA((n, N_DEV - 1)), pltpu.SemaphoreType.DMA((n, N_DEV - 1)),
                        pltpu.SemaphoreType.DMA((n,))],
        compiler_params=pltpu.CompilerParams(has_side_effects=True),
    )(*[a for a, _ in items])


def _gather_two_level(arrs, name):
    n = len(arrs)

    def body(*refs):
        ins, outs = refs[:n], refs[n:2 * n]
        send_sems, recv_sems, local_sems = refs[2 * n:]
        x, y, c = lax.axis_index("x"), lax.axis_index("y"), lax.axis_index("c")
        sibling = (x, y, 1 - c)
        chips = [(1 - x, y), (x, 1 - y), (1 - x, 1 - y)]

        def slot(j, px, py, pc):
            return outs[j].at[4 * px + 2 * py + pc]

        def copy(j, k, block, to, src=None):
            return pltpu.make_async_remote_copy(
                src_ref=slot(j, *block) if src is None else src, dst_ref=slot(j, *block),
                send_sem=send_sems.at[j, k], recv_sem=recv_sems.at[j, k], device_id=to, device_id_type=MESH)

        own = [pltpu.make_async_copy(ins[j], slot(j, x, y, c), local_sems.at[j]) for j in range(n)]
        for cp in own:
            cp.start()
        sent = []
        for j in range(n):
            sent.append(copy(j, 0, (x, y, c), sibling, src=ins[j]))
            sent += [copy(j, 1 + i, (x, y, c), (*chip, c), src=ins[j]) for i, chip in enumerate(chips)]
        for cp in sent:
            cp.start()
        for i, chip in enumerate(chips):
            for j in range(n):
                copy(j, 1 + i, (*chip, c), (x, y, c)).wait_recv()
                passed = copy(j, 4 + i, (*chip, c), sibling)
                passed.start()
                sent.append(passed)
        for j in range(n):
            copy(j, 0, (x, y, 1 - c), (x, y, c)).wait_recv()
            for i, chip in enumerate(chips):
                copy(j, 4 + i, (*chip, 1 - c), (x, y, c)).wait_recv()
        for cp in sent:
            cp.wait_send()
        for cp in own:
            cp.wait()

    return pl.pallas_call(
        body, name=name, out_shape=[jax.ShapeDtypeStruct((N_DEV,) + tuple(a.shape), a.dtype) for a in arrs],
        in_specs=[HBM_FULL] * n, out_specs=[HBM_FULL] * n,
        scratch_shapes=[pltpu.SemaphoreType.DMA((n, N_DEV - 1)), pltpu.SemaphoreType.DMA((n, N_DEV - 1)),
                        pltpu.SemaphoreType.DMA((n,))],
        compiler_params=pltpu.CompilerParams(has_side_effects=True),
    )(*arrs)


def _pair_swap(arrs, name):
    n = len(arrs)

    def body(*refs):
        ins, outs = refs[:n], refs[n:2 * n]
        send_sems, recv_sems = refs[2 * n:]
        x, y, c = lax.axis_index("x"), lax.axis_index("y"), lax.axis_index("c")
        copies = [pltpu.make_async_remote_copy(
            src_ref=ins[j].at[1 - c], dst_ref=outs[j], send_sem=send_sems.at[j], recv_sem=recv_sems.at[j],
            device_id=(x, y, 1 - c), device_id_type=MESH) for j in range(n)]
        for cp in copies:
            cp.start()
        for cp in copies:
            cp.wait_send()
            cp.wait_recv()

    return pl.pallas_call(
        body, name=name, out_shape=[jax.ShapeDtypeStruct(tuple(a.shape[1:]), a.dtype) for a in arrs],
        in_specs=[HBM_FULL] * n, out_specs=[HBM_FULL] * n,
        scratch_shapes=[pltpu.SemaphoreType.DMA((n,)), pltpu.SemaphoreType.DMA((n,))],
        compiler_params=pltpu.CompilerParams(has_side_effects=True),
    )(*arrs)


def _chip_scatter(arrs, name):
    n = len(arrs)

    def body(*refs):
        ins, outs = refs[:n], refs[n:2 * n]
        send_sems, recv_sems, local_sems = refs[2 * n:]
        x, y, c = lax.axis_index("x"), lax.axis_index("y"), lax.axis_index("c")
        here = 2 * x + y
        own = [pltpu.make_async_copy(ins[j].at[here], outs[j].at[here], local_sems.at[j]) for j in range(n)]
        for cp in own:
            cp.start()
        waits = []
        for i, (px, py) in enumerate([(1 - x, y), (x, 1 - y), (1 - x, 1 - y)]):
            there = 2 * px + py
            for j in range(n):
                pltpu.make_async_remote_copy(
                    src_ref=ins[j].at[there], dst_ref=outs[j].at[here], send_sem=send_sems.at[j, i],
                    recv_sem=recv_sems.at[j, i], device_id=(px, py, c), device_id_type=MESH).start()
                waits.append(pltpu.make_async_remote_copy(
                    src_ref=ins[j].at[there], dst_ref=outs[j].at[there], send_sem=send_sems.at[j, i],
                    recv_sem=recv_sems.at[j, i], device_id=(px, py, c), device_id_type=MESH))
        for cp in waits:
            cp.wait_send()
            cp.wait_recv()
        for cp in own:
            cp.wait()

    return pl.pallas_call(
        body, name=name, out_shape=[jax.ShapeDtypeStruct(tuple(a.shape), a.dtype) for a in arrs],
        in_specs=[HBM_FULL] * n, out_specs=[HBM_FULL] * n,
        scratch_shapes=[pltpu.SemaphoreType.DMA((n, 3)), pltpu.SemaphoreType.DMA((n, 3)), pltpu.SemaphoreType.DMA((n,))],
        compiler_params=pltpu.CompilerParams(has_side_effects=True),
    )(*arrs)


def _pair_add(core, a, b):
    _, R, C = a.shape
    tr = _row_tile(R)

    def body(core_ref, a_ref, b_ref, o_ref):
        o_ref[...] = (a_ref[...].astype(F32) + b_ref[...].astype(F32)).astype(BF)

    blk = _rows((tr, C), lambda i, core: (i, 0))
    grid_spec = pltpu.PrefetchScalarGridSpec(
        num_scalar_prefetch=1, grid=(R // tr,),
        in_specs=[_rows((None, tr, C), lambda i, core: (core[0], i, 0)), blk], out_specs=blk)
    return pl.pallas_call(
        body, name="pair_add", grid_spec=grid_spec, out_shape=jax.ShapeDtypeStruct((R, C), BF),
        compiler_params=_params(1),
    )(core, a, b)


def _mod_fwd(c_all, w0, w1, wkv):
    n0, n1, n2 = w0.shape[1], w1.shape[1], wkv.shape[1]

    def body(c_ref, w0_ref, w1_ref, w2_ref, o_ref):
        cc = c_ref[...]
        s = cc * _sigmoid(cc)
        o_ref[:, 0:n0] = _dot(s, w0_ref[...])
        o_ref[:, n0:n0 + n1] = _dot(s, w1_ref[...])
        o_ref[:, n0 + n1:n0 + n1 + n2] = _dot(s, w2_ref[...])

    return pl.pallas_call(
        body, name="mod_fwd", out_shape=jax.ShapeDtypeStruct((N_DEV, n0 + n1 + n2), F32),
        in_specs=[VMEM_FULL] * 4, out_specs=VMEM_FULL,
        compiler_params=pltpu.CompilerParams(vmem_limit_bytes=VMEM_LIMIT),
    )(c_all, w0, w1, wkv)


def _mod_wgrad(c_t, dm):
    C = dm.shape[1]
    tr = 256

    def body(ct_ref, dm_ref, o_ref):
        ct = ct_ref[...]
        s = ct * _sigmoid(ct)
        dmv = dm_ref[...]
        lane = lax.broadcasted_iota(jnp.int32, (tr, N_DEV), 1)
        acc = jnp.zeros((tr, C), F32)
        for r in range(N_DEV):
            col = jnp.sum(jnp.where(lane == r, s, 0.0), axis=1, keepdims=True)
            acc = acc + col * dmv[r:r + 1, :]
        o_ref[...] = acc

    return pl.pallas_call(
        body, name="mod_wgrad", grid=(D // tr,), out_shape=jax.ShapeDtypeStruct((D, C), F32),
        in_specs=[_rows((tr, N_DEV), lambda i: (i, 0)), _rows((N_DEV, C), lambda i: (0, 0))],
        out_specs=_rows((tr, C), lambda i: (i, 0)), compiler_params=_params(1),
    )(c_t, dm)


def _ffn_fwd(h, vec, w13, w2):
    S = h.shape[0]

    def body(h_ref, vec_ref, w13_ref, w2_ref, ho_ref, u_ref, a_ref, b_ref):
        hv, vec = h_ref[...], vec_ref[...]
        u, _, _ = _norm_mod(hv, vec)
        ub = u.astype(BF)
        u_ref[...] = ub
        y = jnp.zeros((TS, D), F32)
        for k in range(DFF // CH):
            c0 = k * CH
            a = _dot(ub, w13_ref[:, c0:c0 + CH])
            b = _dot(ub, w13_ref[:, DFF + c0:DFF + c0 + CH])
            a_ref[:, c0:c0 + CH] = a.astype(BF)
            b_ref[:, c0:c0 + CH] = b.astype(BF)
            t = (a * _sigmoid(a)) * b
            y = y + _dot(t.astype(BF), w2_ref[c0:c0 + CH, :])
        ho_ref[...] = hv + (0.5 * vec[3:4]) * y

    tok = lambda i: (i, 0)
    return pl.pallas_call(
        body, name="ffn_fwd", grid=(S // TS,),
        out_shape=[jax.ShapeDtypeStruct((S, D), F32), jax.ShapeDtypeStruct((S, D), BF),
                   jax.ShapeDtypeStruct((S, DFF), BF), jax.ShapeDtypeStruct((S, DFF), BF)],
        in_specs=[_rows((TS, D), tok), _rows((8, D), lambda i: (0, 0)), VMEM_FULL, VMEM_FULL],
        out_specs=[_rows((TS, D), tok), _rows((TS, D), tok), _rows((TS, DFF), tok), _rows((TS, DFF), tok)],
        compiler_params=_params(1),
    )(h, vec, w13, w2)


def _ffn_bwd(dho, h, a, b, vec, w13, w2):
    S = h.shape[0]

    def body(dho_ref, h_ref, a_ref, b_ref, vec_ref, w13_ref, w2_ref,
             dh_ref, da_ref, db_ref, t_ref, dhb_ref, ps_ref):
        dho_v, vec = dho_ref[...], vec_ref[...]
        dhb_ref[...] = dho_v.astype(BF)
        dyb = ((0.5 * vec[3:4]) * dho_v).astype(BF)
        du = jnp.zeros((TS, D), F32)
        for k in range(DFF // CH):
            c0 = k * CH
            av = a_ref[:, c0:c0 + CH].astype(F32)
            bv = b_ref[:, c0:c0 + CH].astype(F32)
            dt = _dot_nt(dyb, w2_ref[c0:c0 + CH, :])
            sig = _sigmoid(av)
            sl = av * sig
            t_ref[:, c0:c0 + CH] = (sl * bv).astype(BF)
            dab = (dt * bv * (sig * (1.0 + av * (1.0 - sig)))).astype(BF)
            dbb = (dt * sl).astype(BF)
            da_ref[:, c0:c0 + CH] = dab
            db_ref[:, c0:c0 + CH] = dbb
            du = du + _dot_nt(dab, w13_ref[:, c0:c0 + CH]) + _dot_nt(dbb, w13_ref[:, DFF + c0:DFF + c0 + CH])
        _, xhat, r = _norm_mod(h_ref[...], vec)
        dhn, dg, dsh, dsc = _norm_mod_bwd(du, xhat, r, vec)
        dh_ref[...] = dho_v + dhn
        _accumulate(ps_ref, pl.program_id(0) == 0, [dg, dsh, dsc])

    tok = lambda i: (i, 0)
    return pl.pallas_call(
        body, name="ffn_bwd", grid=(S // TS,),
        out_shape=[jax.ShapeDtypeStruct((S, D), F32), jax.ShapeDtypeStruct((S, DFF), BF),
                   jax.ShapeDtypeStruct((S, DFF), BF), jax.ShapeDtypeStruct((S, DFF), BF),
                   jax.ShapeDtypeStruct((S, D), BF), jax.ShapeDtypeStruct((8, D), F32)],
        in_specs=[_rows((TS, D), tok), _rows((TS, D), tok), _rows((TS, DFF), tok), _rows((TS, DFF), tok),
                  _rows((8, D), lambda i: (0, 0)), VMEM_FULL, VMEM_FULL],
        out_specs=[_rows((TS, D), tok), _rows((TS, DFF), tok), _rows((TS, DFF), tok), _rows((TS, DFF), tok),
                   _rows((TS, D), tok), _rows((8, D), lambda i: (0, 0))],
        compiler_params=_params(1),
    )(dho, h, a, b, vec, w13, w2)


def _wgrad(a, b, tm, tn, name, gate=None):
    S, M = a.shape
    N = b.shape[1]
    n_s = S // TW

    def body(*refs):
        if gate is None:
            a_ref, b_ref, o_ref, acc_ref = refs
        else:
            a_ref, b_ref, w_ref, sc_ref, o_ref, gs_ref, acc_ref = refs
        s = pl.program_id(2)

        @pl.when(s == 0)
        def _():
            acc_ref[...] = jnp.zeros((tm, tn), F32)

        acc_ref[...] += _dot_tn(a_ref[...], b_ref[...])

        @pl.when(s == n_s - 1)
        def _():
            acc = acc_ref[...]
            if gate is None:
                o_ref[...] = acc.astype(BF)
            else:
                o_ref[...] = (acc * sc_ref[0:1, :]).astype(BF)
                gs_ref[...] = jnp.broadcast_to(_sum0(acc * w_ref[...].astype(F32)), (8, tn))

    in_specs = [_rows((TW, tm), lambda m, n, s: (s, m)), _rows((TW, tn), lambda m, n, s: (s, n))]
    out_shape = [jax.ShapeDtypeStruct((M, N), BF)]
    out_specs = [_rows((tm, tn), lambda m, n, s: (m, n))]
    args = [a, b]
    if gate is not None:
        in_specs += [_rows((tm, tn), lambda m, n, s: (m, n)), _rows((8, tn), lambda m, n, s: (0, n))]
        out_shape.append(jax.ShapeDtypeStruct((M // tm, 8, N), F32))
        out_specs.append(_rows((None, 8, tn), lambda m, n, s: (m, 0, n)))
        args += list(gate)
    res = pl.pallas_call(
        body, name=name, grid=(M // tm, N // tn, n_s), out_shape=out_shape, in_specs=in_specs,
        out_specs=out_specs, scratch_shapes=[pltpu.VMEM((tm, tn), F32)], compiler_params=_params(3),
    )(*args)
    return res[0] if gate is None else (res[0], res[1])


def _conv_in_fwd(h, vec, w1, b1):
    S = h.shape[0]

    def body(h_ref, vec_ref, w_ref, b1_ref, hn_ref, pre_ref):
        u, _, _ = _norm_mod(h_ref[...], vec_ref[...])
        ub = u.astype(BF)
        hn_ref[...] = ub
        pre_ref[...] = _dot(ub, w_ref[...]) + b1_ref[0:1, :]

    tok = lambda i: (i, 0)
    return pl.pallas_call(
        body, name="conv_in_fwd", grid=(S // TS,),
        out_shape=[jax.ShapeDtypeStruct((S, D), BF), jax.ShapeDtypeStruct((S, 2 * D), F32)],
        in_specs=[_rows((TS, D), tok), _rows((8, D), lambda i: (0, 0)), VMEM_FULL, _rows((8, 2 * D), lambda i: (0, 0))],
        out_specs=[_rows((TS, D), tok), _rows((TS, 2 * D), tok)], compiler_params=_params(1),
    )(h, vec, w1, b1)


def _glu(pre):
    return pre[:, :D] * _sigmoid(pre[:, D:])


def _tap_groups(offset):
    groups = {}
    for j in range(CONV_W):
        off = offset(j)
        groups.setdefault(off % 8, []).append((off - off % 8, j))
    return [(phase, sorted(taps)) for phase, taps in sorted(groups.items())]


def _shift_window(dst_ref, win_ref, phase, rows):
    dst_ref[0:rows, :] = win_ref[phase:phase + rows, :]


def _layernorm(u2):
    mu = _mean1(u2)
    xc = u2 - mu
    rstd = lax.rsqrt(_mean1(xc * xc) + EPS)
    return xc * rstd, rstd


def _conv_out_fwd(pre, h, cw, w2):
    S = h.shape[0]
    hb = TS // HALO

    def body(pre_ref, ph_ref, h_ref, cw_ref, w2_ref, u2_ref, z_ref, ho_ref, win_ref, sh_ref):
        i = pl.program_id(0)
        cwv = cw_ref[...]
        win_ref[0:HALO, :] = jnp.where(i > 0, _glu(ph_ref[...]), 0.0)
        win_ref[HALO:HALO + TS, :] = _glu(pre_ref[...])
        u2 = jnp.broadcast_to(cwv[31:32], (TS, D))
        for phase, taps in _tap_groups(lambda j: HALO - (CONV_W - 1) + j):
            _shift_window(sh_ref, win_ref, phase, taps[-1][0] + TS)
            for lo, j in taps:
                u2 = u2 + cwv[j:j + 1] * sh_ref[lo:lo + TS, :]
        u2_ref[...] = u2
        xh, _ = _layernorm(u2)
        un = xh * cwv[32:33] + cwv[33:34]
        zb = (un * _sigmoid(un)).astype(BF)
        z_ref[...] = zb
        y = _dot(zb, w2_ref[...]) + cwv[34:35]
        ho_ref[...] = h_ref[...] + cwv[35:36] * y

    tok = lambda i: (i, 0)
    return pl.pallas_call(
        body, name="conv_out_fwd", grid=(S // TS,),
        out_shape=[jax.ShapeDtypeStruct((S, D), F32), jax.ShapeDtypeStruct((S, D), BF), jax.ShapeDtypeStruct((S, D), F32)],
        in_specs=[_rows((TS, 2 * D), tok), _rows((HALO, 2 * D), lambda i: (jnp.maximum(i * hb - 1, 0), 0)),
                  _rows((TS, D), tok), _rows((40, D), lambda i: (0, 0)), VMEM_FULL],
        out_specs=[_rows((TS, D), tok), _rows((TS, D), tok), _rows((TS, D), tok)],
        scratch_shapes=[pltpu.VMEM((TS + HALO, D), F32)] * 2, compiler_params=_params(1),
    )(pre, pre, h, cw, w2)


def _conv_out_bwd(dho, u2, cw, w2):
    S = dho.shape[0]

    def body(dho_ref, u2_ref, cw_ref, w2_ref, du2_ref, dhb_ref, ps_ref):
        dho_v, cwv = dho_ref[...], cw_ref[...]
        dhb_ref[...] = dho_v.astype(BF)
        dz = _dot_nt((cwv[35:36] * dho_v).astype(BF), w2_ref[...])
        xh, rstd = _layernorm(u2_ref[...])
        un = xh * cwv[32:33] + cwv[33:34]
        sig = _sigmoid(un)
        dun = dz * (sig * (1.0 + un * (1.0 - sig)))
        dxh = dun * cwv[32:33]
        du2_ref[...] = rstd * (dxh - _mean1(dxh) - xh * _mean1(dxh * xh))
        _accumulate(ps_ref, pl.program_id(0) == 0, [_sum0(dun * xh), _sum0(dun), _sum0(dho_v)])

    tok = lambda i: (i, 0)
    return pl.pallas_call(
        body, name="conv_out_bwd", grid=(S // TS,),
        out_shape=[jax.ShapeDtypeStruct((S, D), F32), jax.ShapeDtypeStruct((S, D), BF), jax.ShapeDtypeStruct((8, D), F32)],
        in_specs=[_rows((TS, D), tok), _rows((TS, D), tok), _rows((40, D), lambda i: (0, 0)), VMEM_FULL],
        out_specs=[_rows((TS, D), tok), _rows((TS, D), tok), _rows((8, D), lambda i: (0, 0))],
        compiler_params=_params(1),
    )(dho, u2, cw, w2)


def _conv_in_bwd(du2, pre, h, dho, vec, cw, w1):
    S = h.shape[0]
    n_t = S // TS
    hb = TS // HALO

    def body(du2_ref, dh2h_ref, pre_ref, ph_ref, h_ref, dho_ref, vec_ref, cw_ref, w1_ref,
             dh_ref, dpre_ref, ps_ref, pw_ref, pb_ref, winu_ref, wind_ref, sh_ref):
        i = pl.program_id(0)
        vec, cwv = vec_ref[...], cw_ref[...]
        pre = pre_ref[...]
        av, sg = pre[:, :D], _sigmoid(pre[:, D:])
        winu_ref[0:HALO, :] = jnp.where(i > 0, _glu(ph_ref[...]), 0.0)
        winu_ref[HALO:HALO + TS, :] = av * sg
        du2v = du2_ref[...]
        wind_ref[0:TS, :] = du2v
        wind_ref[TS:TS + HALO, :] = jnp.where(i < n_t - 1, dh2h_ref[...], 0.0)

        @pl.when(i == 0)
        def _():
            pw_ref[...] = jnp.zeros((32, D), F32)

        for phase, taps in _tap_groups(lambda j: HALO - (CONV_W - 1) + j):
            _shift_window(sh_ref, winu_ref, phase, taps[-1][0] + TS)
            for lo, j in taps:
                pw_ref[j:j + 1, :] += _sum0(du2v * sh_ref[lo:lo + TS, :])
        pw_ref[31:32, :] += _sum0(du2v)
        du1 = jnp.zeros((TS, D), F32)
        for phase, taps in _tap_groups(lambda j: CONV_W - 1 - j):
            _shift_window(sh_ref, wind_ref, phase, taps[-1][0] + TS)
            for lo, j in taps:
                du1 = du1 + cwv[j:j + 1] * sh_ref[lo:lo + TS, :]
        da = du1 * sg
        dg = du1 * av * sg * (1.0 - sg)
        dab, dgb = da.astype(BF), dg.astype(BF)
        dpre_ref[:, :D] = dab
        dpre_ref[:, D:] = dgb

        @pl.when(i == 0)
        def _():
            pb_ref[...] = jnp.zeros((8, 2 * D), F32)

        pb_ref[0:1, :D] += _sum0(da)
        pb_ref[0:1, D:] += _sum0(dg)
        du = _dot_nt(dab, w1_ref[:, :D]) + _dot_nt(dgb, w1_ref[:, D:])
        _, xhat, r = _norm_mod(h_ref[...], vec)
        dhn, dgn, dsh, dsc = _norm_mod_bwd(du, xhat, r, vec)
        dh_ref[...] = dho_ref[...] + dhn
        _accumulate(ps_ref, i == 0, [dgn, dsh, dsc])

    tok = lambda i: (i, 0)
    fixed = lambda i: (0, 0)
    return pl.pallas_call(
        body, name="conv_in_bwd", grid=(n_t,),
        out_shape=[jax.ShapeDtypeStruct((S, D), F32), jax.ShapeDtypeStruct((S, 2 * D), BF),
                   jax.ShapeDtypeStruct((8, D), F32), jax.ShapeDtypeStruct((32, D), F32),
                   jax.ShapeDtypeStruct((8, 2 * D), F32)],
        in_specs=[_rows((TS, D), tok), _rows((HALO, D), lambda i: (jnp.minimum((i + 1) * hb, S // HALO - 1), 0)),
                  _rows((TS, 2 * D), tok), _rows((HALO, 2 * D), lambda i: (jnp.maximum(i * hb - 1, 0), 0)),
                  _rows((TS, D), tok), _rows((TS, D), tok), _rows((8, D), fixed), _rows((40, D), fixed), VMEM_FULL],
        out_specs=[_rows((TS, D), tok), _rows((TS, 2 * D), tok), _rows((8, D), fixed), _rows((32, D), fixed),
                   _rows((8, 2 * D), fixed)],
        scratch_shapes=[pltpu.VMEM((TS + HALO, D), F32)] * 3,
        compiler_params=_params(1),
    )(du2, du2, pre, pre, h, dho, vec, cw, w1)


def _lane():
    return lax.broadcasted_iota(jnp.int32, (TS, HP), 1)


def _kv_fwd(h, vec, wkva, g2, wkvb, rope):
    S = h.shape[0]

    def body(h_ref, vec_ref, wa_ref, g2_ref, wb_ref, rope_ref, hn_ref, ckv_ref, ckn_ref, k_ref, v_ref):
        u, _, _ = _norm_mod(h_ref[...], vec_ref[...])
        ub = u.astype(BF)
        hn_ref[...] = ub
        kva = _dot(ub, wa_ref[...])
        ckv = kva[:, :KV_LORA]
        ckv_ref[...] = ckv
        r2 = lax.rsqrt(_mean1(ckv * ckv) + EPS)
        cknb = ((ckv * r2) * g2_ref[0:1, :]).astype(BF)
        ckn_ref[...] = cknb
        kvb = _dot(cknb, wb_ref[...])
        kpe = _rope(kva[:, KV_LORA:KVA_P], rope_ref[...])
        lane = _lane()
        for hd in range(NH):
            blk = kvb[:, hd * HP:(hd + 1) * HP]
            k_ref[:, hd * HP:(hd + 1) * HP] = jnp.where(lane < 64, blk, kpe).astype(BF)
            v_ref[:, hd * HP:(hd + 1) * HP] = jnp.where(lane >= 64, blk, 0.0).astype(BF)

    tok = lambda i: (i, 0)
    fixed = lambda i: (0, 0)
    return pl.pallas_call(
        body, name="kv_fwd", grid=(S // TS,),
        out_shape=[jax.ShapeDtypeStruct((S, D), BF), jax.ShapeDtypeStruct((S, KV_LORA), F32),
                   jax.ShapeDtypeStruct((S, KV_LORA), BF), jax.ShapeDtypeStruct((S, NH * HP), BF),
                   jax.ShapeDtypeStruct((S, NH * HP), BF)],
        in_specs=[_rows((TS, D), tok), _rows((8, D), fixed), VMEM_FULL, _rows((8, KV_LORA), fixed), VMEM_FULL,
                  _rows((TS, 3 * HP), tok)],
        out_specs=[_rows((TS, D), tok), _rows((TS, KV_LORA), tok), _rows((TS, KV_LORA), tok),
                   _rows((TS, NH * HP), tok), _rows((TS, NH * HP), tok)],
        compiler_params=_params(1),
    )(h, vec, wkva, g2, wkvb, rope)


def _kv_bwd(dk, dv, h, ckv, dho, vec, wkva, g2, wkvb, rope):
    S = h.shape[0]

    def body(dk_ref, dv_ref, h_ref, ckv_ref, dho_ref, vec_ref, wa_ref, g2_ref, wb_ref, rope_ref,
             dh_ref, dkva_ref, dkvb_ref, ps_ref, ps2_ref):
        i = pl.program_id(0)
        vec = vec_ref[...]
        lane = _lane()
        dkpe = jnp.zeros((TS, HP), F32)
        for hd in range(NH):
            dkh = dk_ref[:, hd * HP:(hd + 1) * HP]
            dvh = dv_ref[:, hd * HP:(hd + 1) * HP]
            dkvb_ref[:, hd * HP:(hd + 1) * HP] = jnp.where(lane < 64, dkh, dvh).astype(BF)
            dkpe = dkpe + jnp.where(lane >= 64, dkh, 0.0)
        dkpe = _rope_t(dkpe, rope_ref[...])
        dckn = _dot_nt(dkvb_ref[...], wb_ref[...])
        ckv = ckv_ref[...]
        r2 = lax.rsqrt(_mean1(ckv * ckv) + EPS)
        dckv, dg2 = _rms_bwd(dckn, ckv, r2, g2_ref[0:1, :])
        dkva_ref[:, :KV_LORA] = dckv.astype(BF)
        dkva_ref[:, KV_LORA:KVA_P] = dkpe.astype(BF)
        du = _dot_nt(dkva_ref[...], wa_ref[...])
        _, xhat, r = _norm_mod(h_ref[...], vec)
        dhn, dgn, dsh, dsc = _norm_mod_bwd(du, xhat, r, vec)
        dh_ref[...] = dho_ref[...] + dhn
        _accumulate(ps_ref, i == 0, [dgn, dsh, dsc])
        _accumulate(ps2_ref, i == 0, [dg2])

    tok = lambda i: (i, 0)
    fixed = lambda i: (0, 0)
    return pl.pallas_call(
        body, name="kv_bwd", grid=(S // TS,),
        out_shape=[jax.ShapeDtypeStruct((S, D), F32), jax.ShapeDtypeStruct((S, KVA_P), BF),
                   jax.ShapeDtypeStruct((S, NH * HP), BF), jax.ShapeDtypeStruct((8, D), F32),
                   jax.ShapeDtypeStruct((8, KV_LORA), F32)],
        in_specs=[_rows((TS, NH * HP), tok), _rows((TS, NH * HP), tok), _rows((TS, D), tok), _rows((TS, KV_LORA), tok),
                  _rows((TS, D), tok), _rows((8, D), fixed), VMEM_FULL, _rows((8, KV_LORA), fixed), VMEM_FULL,
                  _rows((TS, 3 * HP), tok)],
        out_specs=[_rows((TS, D), tok), _rows((TS, KVA_P), tok), _rows((TS, NH * HP), tok), _rows((8, D), fixed),
                   _rows((8, KV_LORA), fixed)],
        compiler_params=_params(1),
    )(dk, dv, h, ckv, dho, vec, wkva, g2, wkvb, rope)


def _q_fwd(h, vec, wqa, g2, wqb, rope):
    S = h.shape[0]

    def body(h_ref, vec_ref, wa_ref, g2_ref, wb_ref, rope_ref, hn_ref, qa_ref, qan_ref, q_ref):
        u, _, _ = _norm_mod(h_ref[...], vec_ref[...])
        ub = u.astype(BF)
        hn_ref[...] = ub
        qa = _dot(ub, wa_ref[...])
        qa_ref[...] = qa
        r2 = lax.rsqrt(_mean1(qa * qa) + EPS)
        qanb = ((qa * r2) * g2_ref[0:1, :]).astype(BF)
        qan_ref[...] = qanb
        q = _dot(qanb, wb_ref[...])
        tab = rope_ref[...]
        for hd in range(NH):
            q_ref[:, hd * HP:(hd + 1) * HP] = _rope(q[:, hd * HP:(hd + 1) * HP], tab).astype(BF)

    tok = lambda i: (i, 0)
    fixed = lambda i: (0, 0)
    return pl.pallas_call(
        body, name="q_fwd", grid=(S // TS,),
        out_shape=[jax.ShapeDtypeStruct((S, D), BF), jax.ShapeDtypeStruct((S, Q_LORA), F32),
                   jax.ShapeDtypeStruct((S, Q_LORA), BF), jax.ShapeDtypeStruct((S, NH * HP), BF)],
        in_specs=[_rows((TS, D), tok), _rows((8, D), fixed), VMEM_FULL, _rows((8, Q_LORA), fixed), VMEM_FULL,
                  _rows((TS, 3 * HP), tok)],
        out_specs=[_rows((TS, D), tok), _rows((TS, Q_LORA), tok), _rows((TS, Q_LORA), tok), _rows((TS, NH * HP), tok)],
        compiler_params=_params(1),
    )(h, vec, wqa, g2, wqb, rope)


def _q_bwd(dq, h, qa, dho, vec, wqa, g2, wqb, rope):
    S = h.shape[0]

    def body(dq_ref, h_ref, qa_ref, dho_ref, vec_ref, wa_ref, g2_ref, wb_ref, rope_ref,
             dh_ref, dqb_ref, dqa_ref, ps_ref, ps2_ref):
        i = pl.program_id(0)
        vec, tab = vec_ref[...], rope_ref[...]
        for hd in range(NH):
            dqb_ref[:, hd * HP:(hd + 1) * HP] = _rope_t(dq_ref[:, hd * HP:(hd + 1) * HP], tab).astype(BF)
        dqan = _dot_nt(dqb_ref[...], wb_ref[...])
        qa = qa_ref[...]
        r2 = lax.rsqrt(_mean1(qa * qa) + EPS)
        dqa, dg2 = _rms_bwd(dqan, qa, r2, g2_ref[0:1, :])
        dqab = dqa.astype(BF)
        dqa_ref[...] = dqab
        du = _dot_nt(dqab, wa_ref[...])
        _, xhat, r = _norm_mod(h_ref[...], vec)
        dhn, dgn, dsh, dsc = _norm_mod_bwd(du, xhat, r, vec)
        dh_ref[...] = dho_ref[...] + dhn
        _accumulate(ps_ref, i == 0, [dgn, dsh, dsc])
        _accumulate(ps2_ref, i == 0, [dg2])

    tok = lambda i: (i, 0)
    fixed = lambda i: (0, 0)
    return pl.pallas_call(
        body, name="q_bwd", grid=(S // TS,),
        out_shape=[jax.ShapeDtypeStruct((S, D), F32), jax.ShapeDtypeStruct((S, NH * HP), BF),
                   jax.ShapeDtypeStruct((S, Q_LORA), BF), jax.ShapeDtypeStruct((8, D), F32),
                   jax.ShapeDtypeStruct((8, Q_LORA), F32)],
        in_specs=[_rows((TS, NH * HP), tok), _rows((TS, D), tok), _rows((TS, Q_LORA), tok), _rows((TS, D), tok),
                  _rows((8, D), fixed), VMEM_FULL, _rows((8, Q_LORA), fixed), VMEM_FULL, _rows((TS, 3 * HP), tok)],
        out_specs=[_rows((TS, D), tok), _rows((TS, NH * HP), tok), _rows((TS, Q_LORA), tok), _rows((8, D), fixed),
                   _rows((8, Q_LORA), fixed)],
        compiler_params=_params(1),
    )(dq, h, qa, dho, vec, wqa, g2, wqb, rope)


def _attn_fwd(q, k, v):
    S = q.shape[0]
    TA = TA_FWD
    nq = S // TA
    rg = min(TA, ATT_ROWS)
    groups = TA // rg

    def softmax_pv(scores, vt, state, masks):
        out = []
        for g in range(groups):
            m, l, acc = state[g]
            s = scores[g] if masks is None else jnp.where(masks[g], scores[g], NEG)
            m_new = jnp.maximum(m, jnp.max(s, axis=1, keepdims=True))
            p = jnp.exp2((s - m_new) * EXP2_SCALE)
            alpha = jnp.exp2((m - m_new) * EXP2_SCALE)
            out.append((m_new, alpha * l + jnp.sum(p, axis=1, keepdims=True), alpha * acc + _dot(p.astype(BF), vt)))
        return tuple(out)

    def body(q_ref, k_ref, v_ref, o_ref, lse_ref):
        qi = pl.program_id(1)
        qs = [q_ref[g * rg:(g + 1) * rg, :] for g in range(groups)]

        def keys(j):
            return pl.ds(pl.multiple_of(j * TA, TA), TA)

        def scores_of(j):
            kt = k_ref[keys(j), :]
            return tuple(_dot_nt(qs[g], kt) for g in range(groups))

        state = tuple((jnp.full((rg, 1), NEG, F32), jnp.zeros((rg, 1), F32), jnp.zeros((rg, HP), F32))
                      for _ in range(groups))

        def step(kj, state):
            return softmax_pv(scores_of(kj), v_ref[keys(kj), :], state, None)

        state = lax.fori_loop(0, qi, step, state)
        row = lax.broadcasted_iota(jnp.int32, (rg, TA), 0)
        col = lax.broadcasted_iota(jnp.int32, (rg, TA), 1)
        final = softmax_pv(scores_of(qi), v_ref[keys(qi), :], state, [col <= row + g * rg for g in range(groups)])
        for g in range(groups):
            m, l, acc = final[g]
            o_ref[g * rg:(g + 1) * rg, :] = (acc / l).astype(BF)
            lse_ref[g * rg:(g + 1) * rg, :] = m * EXP2_SCALE + jnp.log(l) * LOG2E

    return pl.pallas_call(
        body, name="attn_fwd", grid=(NH, nq),
        out_shape=[jax.ShapeDtypeStruct((S, NH * HP), BF), jax.ShapeDtypeStruct((NH, S, 1), F32)],
        in_specs=[_rows((TA, HP), lambda h, i: (i, h)), _rows((S, HP), lambda h, i: (0, h)),
                  _rows((S, HP), lambda h, i: (0, h))],
        out_specs=[_rows((TA, HP), lambda h, i: (i, h)), _rows((None, TA, 1), lambda h, i: (h, i, 0))],
        compiler_params=_params(2),
    )(q, k, v)


def _attn_bwd(q, k, v, do, lse, delta):
    S = q.shape[0]
    nq = S // TA
    rg = min(TA, ATT_ROWS)
    groups = TA // rg

    def body(q_ref, do_ref, lse_ref, dl_ref, k_ref, v_ref, dq_ref, dk_ref, dv_ref):
        kj = pl.program_id(1)

        @pl.when(kj == 0)
        def _():
            dq_ref[...] = jnp.zeros((S, HP), F32)

        kt, vt = k_ref[...], v_ref[...]
        row = lax.broadcasted_iota(jnp.int32, (rg, TA), 0)
        col = lax.broadcasted_iota(jnp.int32, (rg, TA), 1)

        def tile(qi, dk, dv, diagonal):
            rows = [pl.ds(pl.multiple_of(qi * TA + g * rg, rg), rg) for g in range(groups)]
            qg = [q_ref[r, :] for r in rows]
            dog = [do_ref[r, :] for r in rows]
            scores = [_dot_nt(qg[g], kt) for g in range(groups)]
            dps = [_dot_nt(dog[g], vt) for g in range(groups)]
            for g in range(groups):
                p = jnp.exp2(scores[g] * EXP2_SCALE - lse_ref[rows[g], :])
                if diagonal:
                    p = jnp.where(col <= row + g * rg, p, 0.0)
                ds = p * (dps[g] - dl_ref[rows[g], :])
                pb, dsb = p.astype(BF), ds.astype(BF)
                dv = dv + _dot_tn(pb, dog[g])
                dk = dk + _dot_tn(dsb, qg[g])
                dq_ref[rows[g], :] += _dot(dsb, kt) * SM_SCALE
            return dk, dv

        zero = jnp.zeros((TA, HP), F32)
        dk, dv = tile(kj, zero, zero, True)
        dk, dv = lax.fori_loop(kj + 1, nq, lambda qi, cr: tile(qi, cr[0], cr[1], False), (dk, dv))
        dk_ref[...] = dk * SM_SCALE
        dv_ref[...] = dv

    head = lambda h, j: (0, h)
    col1 = lambda h, j: (h, 0, 0)
    return pl.pallas_call(
        body, name="attn_bwd", grid=(NH, nq), out_shape=[jax.ShapeDtypeStruct((S, NH * HP), F32)] * 3,
        in_specs=[_rows((S, HP), head), _rows((S, HP), head), _rows((None, S, 1), col1), _rows((None, S, 1), col1),
                  _rows((TA, HP), lambda h, j: (j, h)), _rows((TA, HP), lambda h, j: (j, h))],
        out_specs=[_rows((S, HP), head), _rows((TA, HP), lambda h, j: (j, h)), _rows((TA, HP), lambda h, j: (j, h))],
        compiler_params=_params(2),
    )(q, do, lse, delta, k, v)


def _attn_out_fwd(o, h, wo, vec):
    S = h.shape[0]

    def body(o_ref, h_ref, wo_ref, vec_ref, ho_ref):
        ho_ref[...] = h_ref[...] + vec_ref[3:4, :] * _dot(o_ref[...], wo_ref[...])

    tok = lambda i: (i, 0)
    return pl.pallas_call(
        body, name="attn_out_fwd", grid=(S // TS,), out_shape=jax.ShapeDtypeStruct((S, D), F32),
        in_specs=[_rows((TS, NH * HP), tok), _rows((TS, D), tok), VMEM_FULL, _rows((8, D), lambda i: (0, 0))],
        out_specs=_rows((TS, D), tok), compiler_params=_params(1),
    )(o, h, wo, vec)


def _attn_out_bwd(dho, o, wo, vec):
    S = dho.shape[0]

    def body(dho_ref, o_ref, wo_ref, vec_ref, do_ref, dl_ref, dhb_ref):
        dho_v = dho_ref[...]
        dhb_ref[...] = dho_v.astype(BF)
        do = _dot_nt((vec_ref[3:4, :] * dho_v).astype(BF), wo_ref[...])
        do_ref[...] = do.astype(BF)
        prod = do * o_ref[...].astype(F32)
        for hd in range(NH):
            dl_ref[hd] = jnp.sum(prod[:, hd * HP:(hd + 1) * HP], axis=1, keepdims=True)

    tok = lambda i: (i, 0)
    return pl.pallas_call(
        body, name="attn_out_bwd", grid=(S // TS,),
        out_shape=[jax.ShapeDtypeStruct((S, NH * HP), BF), jax.ShapeDtypeStruct((NH, S, 1), F32),
                   jax.ShapeDtypeStruct((S, D), BF)],
        in_specs=[_rows((TS, D), tok), _rows((TS, NH * HP), tok), VMEM_FULL, _rows((8, D), lambda i: (0, 0))],
        out_specs=[_rows((TS, NH * HP), tok), _rows((NH, TS, 1), lambda i: (0, i, 0)), _rows((TS, D), tok)],
        compiler_params=_params(1),
    )(dho, o, wo, vec)


def _final(h, target, fg):
    S = h.shape[0]

    def body(h_ref, t_ref, g_ref, dh_ref, ps_ref):
        hv, g = h_ref[...], g_ref[0:1, :]
        r = lax.rsqrt(_mean1(hv * hv) + EPS)
        xhat = hv * r
        err = xhat * g - t_ref[...]
        loss = 0.5 * jnp.sum(_mean1(err * err), axis=0, keepdims=True)
        dy = err * (1.0 / D)
        dxhat = dy * g
        dh_ref[...] = r * (dxhat - xhat * _mean1(dxhat * xhat))
        _accumulate(ps_ref, pl.program_id(0) == 0, [_sum0(dy * xhat), jnp.broadcast_to(loss, (1, D))])

    tok = lambda i: (i, 0)
    return pl.pallas_call(
        body, name="final_loss", grid=(S // TS,),
        out_shape=[jax.ShapeDtypeStruct((S, D), F32), jax.ShapeDtypeStruct((8, D), F32)],
        in_specs=[_rows((TS, D), tok), _rows((TS, D), tok), _rows((8, D), lambda i: (0, 0))],
        out_specs=[_rows((TS, D), tok), _rows((8, D), lambda i: (0, 0))], compiler_params=_params(1),
    )(h, target, fg)


def _row_tile(r):
    if r <= 256:
        return r
    for t in range(256, 7, -8):
        if r % t == 0:
            return t
    return r


def _adamw(parts, w, m, v):
    P, R, C = parts.shape
    tr = _row_tile(R)

    def body(p_ref, w_ref, m_ref, v_ref, g_ref, d_ref, mo_ref, vo_ref):
        g = p_ref[0].astype(F32)
        for k in range(1, P):
            g = g + p_ref[k].astype(F32)
        g_ref[...] = g
        m2 = B1 * m_ref[...] + (1.0 - B1) * g
        v2 = B2 * v_ref[...] + (1.0 - B2) * (g * g)
        mo_ref[...] = m2
        vo_ref[...] = v2
        m_hat = m2 / (1.0 - B1 ** STEP)
        v_hat = v2 / (1.0 - B2 ** STEP)
        d_ref[...] = -LR * (m_hat / (jnp.sqrt(v_hat) + EPS_ADAM) + WD * w_ref[...])

    blk = _rows((tr, C), lambda i: (i, 0))
    return pl.pallas_call(
        body, name="adamw", grid=(R // tr,), out_shape=[jax.ShapeDtypeStruct((R, C), F32)] * 4,
        in_specs=[_rows((P, tr, C), lambda i: (0, i, 0)), blk, blk, blk], out_specs=[blk] * 4,
        compiler_params=_params(1),
    )(parts, w, m, v)


_WEIGHTS = ['ada_w', 'ada_b', 'norm_g', 'ffn_w13', 'ffn_w2', 'conv_w_pw1', 'conv_b_pw1', 'conv_w_dw', 'conv_b_dw',
            'conv_ln_g', 'conv_ln_b', 'conv_w_pw2', 'conv_b_pw2', 'kv_ada_w', 'kv_ada_b', 'kv_norm_g', 'w_kv_a',
            'kv_a_norm_g', 'w_kv_b', 'w_q_a', 'q_a_norm_g', 'w_q_b', 'w_o', 'final_norm_g']


def _vec(rows):
    rows = [r.reshape(1, -1).astype(F32) for r in rows]
    return jnp.concatenate(rows + [jnp.zeros((8 - len(rows), rows[0].shape[1]), F32)], axis=0)


def kernel(x, c, positions, ada_w, ada_b, norm_g, ffn_w13, ffn_w2, conv_w_pw1, conv_b_pw1, conv_w_dw, conv_b_dw, conv_ln_g, conv_ln_b, conv_w_pw2, conv_b_pw2, kv_ada_w, kv_ada_b, kv_norm_g, w_kv_a, kv_a_norm_g, w_kv_b, w_q_a, q_a_norm_g, w_q_b, w_o, final_norm_g, loss_target, m_ada_w, m_ada_b, m_norm_g, m_ffn_w13, m_ffn_w2, m_conv_w_pw1, m_conv_b_pw1, m_conv_w_dw, m_conv_b_dw, m_conv_ln_g, m_conv_ln_b, m_conv_w_pw2, m_conv_b_pw2, m_kv_ada_w, m_kv_ada_b, m_kv_norm_g, m_w_kv_a, m_kv_a_norm_g, m_w_kv_b, m_w_q_a, m_q_a_norm_g, m_w_q_b, m_w_o, m_final_norm_g, v_ada_w, v_ada_b, v_norm_g, v_ffn_w13, v_ffn_w2, v_conv_w_pw1, v_conv_b_pw1, v_conv_w_dw, v_conv_b_dw, v_conv_ln_g, v_conv_ln_b, v_conv_w_pw2, v_conv_b_pw2, v_kv_ada_w, v_kv_ada_b, v_kv_norm_g, v_w_kv_a, v_kv_a_norm_g, v_w_kv_b, v_w_q_a, v_q_a_norm_g, v_w_q_b, v_w_o, v_final_norm_g):
    given = dict(locals())
    S = x.shape[1]
    me = 4 * lax.axis_index("x") + 2 * lax.axis_index("y") + lax.axis_index("c")

    small = jnp.concatenate([
        conv_w_dw[0], conv_b_dw, conv_ln_g, conv_ln_b, conv_b_pw2,
        norm_g.reshape(6, 128), conv_b_pw1.reshape(2, 128),
        c.reshape(8, 128), jnp.zeros((5, 128), F32)], axis=0)
    big = [ffn_w13, ffn_w2, conv_w_pw1, conv_w_pw2, w_kv_a, w_kv_b, w_q_a, w_q_b, w_o]
    got = _gather_two_level([small] + [w.astype(BF) for w in big], "gather_weights")
    sm = got[0]
    chan = lambda lo, hi: jnp.moveaxis(sm[:, lo:hi, :], 0, 1).reshape(hi - lo, D)
    w_dw_f, b_dw_f, ln_g_f, ln_b_f, b_pw2_f = chan(0, 31), chan(31, 32), chan(32, 33), chan(33, 34), chan(34, 35)
    norm_f = chan(35, 41).reshape(2, 3, D)
    b_pw1_f = sm[:, 41:43, :].reshape(1, 2 * D)
    c_all = sm[:, 43:51, :].reshape(N_DEV, D)
    w13_f = jnp.transpose(got[1], (1, 2, 3, 0, 4)).reshape(2, 2, D, 2 * DFF)
    w2_f = jnp.transpose(got[2], (1, 2, 0, 3, 4)).reshape(2, 2, DFF, D)
    pw1_f = jnp.transpose(got[3][:, 0], (1, 0, 2)).reshape(D, 2 * D)
    pw2_f = got[4].reshape(D, D)
    wkva = got[5].reshape(D, KV_LORA + ROPE)
    wkva_f = jnp.concatenate([wkva[:, :KV_LORA], jnp.zeros((D, 64), BF), wkva[:, KV_LORA:], jnp.zeros((D, 32), BF)], axis=1)
    wkvb_f = jnp.transpose(got[6], (1, 0, 2)).reshape(KV_LORA, NH * HP)
    wqa_f = got[7].reshape(D, Q_LORA)
    wqb = jnp.transpose(got[8][:, 0], (1, 0, 2)).reshape(Q_LORA, NH, 96)
    wqb_f = jnp.pad(wqb, ((0, 0), (0, 0), (0, HP - 96))).reshape(Q_LORA, NH * HP)
    wo_f = jnp.pad(got[9].reshape(NH, 64, D), ((0, 0), (64, 0), (0, 0))).reshape(NH * HP, D)

    n_ada = ada_w.shape[2]
    n_kva = kv_ada_w.shape[1]
    modp = _mod_fwd(c_all, ada_w[0], ada_w[1], kv_ada_w)
    (modr,) = _exchange([(modp.reshape(N_DEV, 1, 2 * n_ada + n_kva), "scatter")], "scatter_mod")
    modr = modr[:, 0, :]
    mod = jnp.transpose(modr[:, :2 * n_ada].reshape(N_DEV, 2, n_ada), (1, 0, 2)).reshape(2, 9 * D) + ada_b
    mod = mod.reshape(2, 9, D)
    kvmod = (modr[:, 2 * n_ada:].reshape(2 * D) + kv_ada_b).reshape(2, D)

    def sub_vec(l, idx):
        return _vec([norm_f[l, idx], mod[l, 3 * idx], mod[l, 3 * idx + 1], mod[l, 3 * idx + 2]])

    vec_kv = _vec([kv_norm_g, kvmod[0], kvmod[1]])
    cw = jnp.concatenate([w_dw_f, b_dw_f, ln_g_f, ln_b_f, b_pw2_f, mod[0, 5].reshape(1, D), jnp.zeros((4, D), F32)], axis=0)
    b1v = _vec([b_pw1_f])
    g_kva = _vec([kv_a_norm_g])
    g_qa = _vec([q_a_norm_g[0]])
    fgv = _vec([final_norm_g])

    inv_freq = 10000.0 ** (-jnp.arange(0, ROPE, 2, dtype=F32) / ROPE)
    ang = positions[0].astype(F32)[:, None] * inv_freq
    cs, sn = jnp.cos(ang), jnp.sin(ang)
    z16, z32, z64 = jnp.zeros((S, 16), F32), jnp.zeros((S, 32), F32), jnp.zeros((S, 64), F32)
    rope = jnp.concatenate([jnp.ones((S, 64), F32), cs, cs, z32,
                            z64, z16, sn, z32,
                            z64, -sn, z16, z32], axis=1)

    h0 = x[0]
    h1, u00, a00, b00 = _ffn_fwd(h0, sub_vec(0, 0), w13_f[0, 0], w2_f[0, 0])
    hn_c, pre = _conv_in_fwd(h1, sub_vec(0, 1), pw1_f, b1v)
    u2, z_c, h2 = _conv_out_fwd(pre, h1, cw, pw2_f)
    h3, u01, a01, b01 = _ffn_fwd(h2, sub_vec(0, 2), w13_f[0, 1], w2_f[0, 1])
    hn_kv, ckv, ckn, k_all, v_all = _kv_fwd(h3, vec_kv, wkva_f, g_kva, wkvb_f, rope)
    h4, u10, a10, b10 = _ffn_fwd(h3, sub_vec(1, 0), w13_f[1, 0], w2_f[1, 0])
    hn_q, qa, qan, q_all = _q_fwd(h4, sub_vec(1, 1), wqa_f, g_qa, wqb_f, rope)
    o_all, lse = _attn_fwd(q_all, k_all, v_all)
    h5 = _attn_out_fwd(o_all, h4, wo_f, sub_vec(1, 1))
    h6, u11, a11, b11 = _ffn_fwd(h5, sub_vec(1, 2), w13_f[1, 1], w2_f[1, 1])

    dh6, ps_fin = _final(h6, loss_target[0], fgv)
    loss = lax.psum(ps_fin[1, 0], ("x", "y", "c"))

    def ffn_back(dho, h_in, u, a, b, l, i):
        vec = sub_vec(l, 2 * i)
        dh, da, db, t, dhb, ps = _ffn_bwd(dho, h_in, a, b, vec, w13_f[l, i], w2_f[l, i])
        dwa = _wgrad(u, da, D, CH, "wgrad_w13")
        dwb = _wgrad(u, db, D, CH, "wgrad_w13")
        dw2, gs = _wgrad(t, dhb, CH, D, "wgrad_w2", gate=(w2_f[l, i], _vec([0.5 * vec[3]])))
        dgate = 0.5 * jnp.sum(gs[:, 0, :], axis=0)
        return dh, jnp.concatenate([dwa, dwb], axis=1), dw2, ps, dgate

    dh5, dw13_11, dw2_11, ps11, dg11 = ffn_back(dh6, h5, u11, a11, b11, 1, 1)
    vec_m1 = sub_vec(1, 1)
    do_all, delta, dhb5 = _attn_out_bwd(dh5, o_all, wo_f, vec_m1)
    dwo_p, gs_o = _wgrad(o_all, dhb5, D, D, "wgrad_wo", gate=(wo_f, _vec([vec_m1[3]])))
    dgm1 = jnp.sum(gs_o[:, 0, :], axis=0)
    dq_all, dk_all, dv_all = _attn_bwd(q_all, k_all, v_all, do_all, lse, delta)
    dh4, dqb, dqab, ps_q, ps_q2 = _q_bwd(dq_all, h4, qa, dh5, vec_m1, wqa_f, g_qa, wqb_f, rope)
    dwqb_p = _wgrad(qan, dqb, Q_LORA, D, "wgrad_wqb")
    dwqa = _wgrad(hn_q, dqab, D, Q_LORA, "wgrad_wqa")
    dh3a, dw13_10, dw2_10, ps10, dg10 = ffn_back(dh4, h3, u10, a10, b10, 1, 0)
    dh3, dkva, dkvb, ps_kv, ps_kv2 = _kv_bwd(dk_all, dv_all, h3, ckv, dh3a, vec_kv, wkva_f, g_kva, wkvb_f, rope)
    dwkva_p = _wgrad(hn_kv, dkva, D, KVA_P, "wgrad_wkva")
    dwkvb = _wgrad(ckn, dkvb, KV_LORA, D, "wgrad_wkvb")
    dh2, dw13_01, dw2_01, ps01, dg01 = ffn_back(dh3, h2, u01, a01, b01, 0, 1)
    vec_m0 = sub_vec(0, 1)
    du2, dhb2, ps_co = _conv_out_bwd(dh2, u2, cw, pw2_f)
    dpw2, gs_c = _wgrad(z_c, dhb2, D, D, "wgrad_pw2", gate=(pw2_f, _vec([vec_m0[3]])))
    dgm0 = jnp.sum(gs_c[:, 0, :], axis=0) + b_pw2_f[0] * ps_co[2]
    dh1, dpre, ps_ci, ps_dw, ps_b1 = _conv_in_bwd(du2, pre, h1, dh2, vec_m0, cw, pw1_f)
    dpw1 = _wgrad(hn_c, dpre, D, D, "wgrad_pw1")
    dh0, dw13_00, dw2_00, ps00, dg00 = ffn_back(dh1, h0, u00, a00, b00, 0, 0)

    dw13 = jnp.stack([jnp.stack([dw13_00, dw13_01]), jnp.stack([dw13_10, dw13_11])])
    dw2 = jnp.stack([jnp.stack([dw2_00, dw2_01]), jnp.stack([dw2_10, dw2_11])])
    send = [
        jnp.transpose(dw13.reshape(2, 2, D, N_DEV, 704), (3, 0, 1, 2, 4)),
        jnp.transpose(dw2.reshape(2, 2, N_DEV, 352, D), (2, 0, 1, 3, 4)),
        jnp.transpose(dpw1.reshape(D, N_DEV, 256), (1, 0, 2)),
        dpw2.reshape(N_DEV, 128, D),
        jnp.concatenate([dwkva_p[:, :KV_LORA], dwkva_p[:, KV_LORA + 64:KV_LORA + 96]], axis=1).reshape(N_DEV, 128, KV_LORA + ROPE),
        jnp.transpose(dwkvb.reshape(KV_LORA, N_DEV, 256), (1, 0, 2)),
        dwqa.reshape(N_DEV, 128, Q_LORA),
        jnp.transpose(dwqb_p.reshape(Q_LORA, NH, HP)[:, :, :96].reshape(Q_LORA, N_DEV, 192), (1, 0, 2)),
        dwo_p.reshape(NH, HP, D)[:, 64:, :].reshape(N_DEV, 128, D),
    ]
    dmod = jnp.stack([
        jnp.stack([ps00[1], ps00[2], dg00, ps_ci[1], ps_ci[2], dgm0, ps01[1], ps01[2], dg01]),
        jnp.stack([ps10[1], ps10[2], dg10, ps_q[1], ps_q[2], dgm1, ps11[1], ps11[2], dg11])])
    dnorm = jnp.stack([ps00[0], ps_ci[0], ps01[0], ps10[0], ps_q[0], ps11[0]])
    pieces = [dnorm, ps_b1[0], ps_dw[0:31], ps_dw[31], ps_co[0], ps_co[1], vec_m0[3] * ps_co[2],
              ps_kv[0], ps_kv2[0], ps_q2[0], ps_fin[0], dmod, ps_kv[1], ps_kv[2]]
    sizes = [int(np.prod(p.shape)) for p in pieces]
    offs = np.concatenate([[0], np.cumsum(sizes)]).astype(int)
    flat = jnp.concatenate([p.reshape(-1) for p in pieces]).reshape(-1, 128)
    core = lax.axis_index("c").reshape(1).astype(jnp.int32)
    by_core = [s.reshape((4, 2) + s.shape[1:]).swapaxes(0, 1) for s in send]
    from_sibling = _pair_swap(by_core, "pair_swap_grads")
    pair_sums = [_pair_add(core, a.reshape(2, -1, a.shape[-1]), b.reshape(-1, b.shape[-1])).reshape(b.shape)
                 for a, b in zip(by_core, from_sibling)]
    got2 = _exchange([(flat, "gather")], "gather_partials") + _chip_scatter(pair_sums, "chip_scatter_grads")
    part = got2[0].reshape(N_DEV, -1)

    def piece(i, rows, cols):
        return part[:, offs[i]:offs[i + 1]].reshape(N_DEV, rows, cols)

    def mine(i, rows):
        return lax.dynamic_slice_in_dim(piece(i, rows, D), me * 128, 128, axis=2)

    dmod_all = piece(11, 2, 9 * D)
    c_t = jnp.transpose(c_all)
    g_ada = jnp.stack([_mod_wgrad(c_t, lax.dynamic_slice_in_dim(dmod_all[:, l], me * n_ada, n_ada, axis=1))
                       for l in range(2)])
    dkvmod_all = jnp.concatenate([piece(12, 1, D)[:, 0], piece(13, 1, D)[:, 0]], axis=1)
    g_kvada = _mod_wgrad(c_t, lax.dynamic_slice_in_dim(dkvmod_all, me * n_kva, n_kva, axis=1))

    parts = {
        'ada_w': g_ada.reshape(1, 2 * D, n_ada),
        'ada_b': dmod_all,
        'norm_g': mine(0, 6),
        'ffn_w13': got2[1].reshape(4, 4 * D, 704),
        'ffn_w2': got2[2].reshape(4, 4 * 352, D),
        'conv_w_pw1': got2[3],
        'conv_b_pw1': lax.dynamic_slice_in_dim(piece(1, 1, 2 * D), me * 256, 256, axis=2),
        'conv_w_dw': mine(2, 31),
        'conv_b_dw': mine(3, 1),
        'conv_ln_g': mine(4, 1),
        'conv_ln_b': mine(5, 1),
        'conv_w_pw2': got2[4],
        'conv_b_pw2': mine(6, 1),
        'kv_ada_w': g_kvada.reshape(1, D, n_kva),
        'kv_ada_b': dkvmod_all.reshape(N_DEV, 1, 2 * D),
        'kv_norm_g': piece(7, 1, D),
        'w_kv_a': got2[5],
        'kv_a_norm_g': piece(8, 1, KV_LORA),
        'w_kv_b': got2[6],
        'w_q_a': got2[7],
        'q_a_norm_g': piece(9, 1, Q_LORA),
        'w_q_b': got2[8],
        'w_o': got2[9],
        'final_norm_g': piece(10, 1, D),
    }
    grads, deltas, new_m, new_v = [], [], [], []
    for name in _WEIGHTS:
        w = given[name]
        p = parts[name]
        shape2 = p.shape[1:]
        g, dlt, m2, v2 = _adamw(p, w.reshape(shape2), given['m_' + name].reshape(shape2), given['v_' + name].reshape(shape2))
        grads.append(g.reshape(w.shape))
        deltas.append(dlt.reshape(w.shape))
        new_m.append(m2.reshape(w.shape))
        new_v.append(v2.reshape(w.shape))
    return (loss, dh0.reshape(1, S, D), *grads, *deltas, *new_m, *new_v)
```

```python
import functools

import numpy as np
import jax
import jax.numpy as jnp
from jax import lax
from jax.experimental import pallas as pl
from jax.experimental.pallas import tpu as pltpu

F32, BF = jnp.float32, jnp.bfloat16

D = 1024
DFF = 2816
CH = 1408
NH = 16
HP = 128
KV_LORA, Q_LORA, ROPE = 256, 512, 32
KVA_P = 384
CONV_W = 31
HALO = 32
EPS = 1e-6
SM_SCALE = float((64 + 32) ** -0.5)
LOG2E = 1.4426950408889634
EXP2_SCALE = SM_SCALE * LOG2E
NEG = -1e30
N_DEV = 8
MESH = pl.DeviceIdType.MESH

TS = 256
TA = 1024
TA_FWD = 1024
ATT_ROWS = 256
TW = 2048
VMEM_LIMIT = 56 * 1024 * 1024

LR, B1, B2, EPS_ADAM, WD, STEP = 0.001, 0.9, 0.999, 1e-08, 0.01, 10

VMEM_FULL = pl.BlockSpec(memory_space=pltpu.VMEM)
HBM_FULL = pl.BlockSpec(memory_space=pltpu.HBM)


def _params(n_grid):
    return pltpu.CompilerParams(dimension_semantics=("arbitrary",) * n_grid, vmem_limit_bytes=VMEM_LIMIT)


def _dot(a, b):
    return jnp.dot(a, b, preferred_element_type=F32)


def _dot_nt(a, b):
    return lax.dot_general(a, b, (((1,), (1,)), ((), ())), preferred_element_type=F32)


def _dot_tn(a, b):
    return lax.dot_general(a, b, (((0,), (0,)), ((), ())), preferred_element_type=F32)


def _sum0(x):
    return jnp.sum(x, axis=0, keepdims=True)


def _mean1(x):
    return jnp.mean(x, axis=-1, keepdims=True)


def _sigmoid(x):
    return jax.nn.sigmoid(x)


def _rows(shape, imap):
    return pl.BlockSpec(shape, imap)


def _norm_mod(h, vec):
    r = lax.rsqrt(_mean1(h * h) + EPS)
    xhat = h * r
    u = (xhat * vec[0:1]) * (1.0 + vec[2:3]) + vec[1:2]
    return u, xhat, r


def _norm_mod_bwd(du, xhat, r, vec):
    g = vec[0:1]
    dxn = du * (1.0 + vec[2:3])
    dsh = _sum0(du)
    dsc = _sum0(du * (xhat * g))
    dg = _sum0(dxn * xhat)
    dxhat = dxn * g
    dh = r * (dxhat - xhat * _mean1(dxhat * xhat))
    return dh, dg, dsh, dsc


def _rms_bwd(dy, x, r, g):
    xhat = x * r
    dg = _sum0(dy * xhat)
    dxhat = dy * g
    return r * (dxhat - xhat * _mean1(dxhat * xhat)), dg


def _accumulate(ref, first, rows):
    @pl.when(first)
    def _():
        ref[...] = jnp.zeros(ref.shape, ref.dtype)

    for i, row in enumerate(rows):
        ref[i:i + 1, :] += row


def _rope(x, tab):
    return x * tab[:, 0:HP] + pltpu.roll(x, 16, 1) * tab[:, HP:2 * HP] + pltpu.roll(x, HP - 16, 1) * tab[:, 2 * HP:3 * HP]


def _rope_t(dy, tab):
    return (dy * tab[:, 0:HP] + pltpu.roll(dy * tab[:, HP:2 * HP], HP - 16, 1)
            + pltpu.roll(dy * tab[:, 2 * HP:3 * HP], 16, 1))


def _exchange(items, name):
    n = len(items)

    def body(*refs):
        ins, outs = refs[:n], refs[n:2 * n]
        send_sems, recv_sems, local_sems = refs[2 * n:]
        x, y, c = lax.axis_index("x"), lax.axis_index("y"), lax.axis_index("c")
        me = 4 * x + 2 * y + c

        def source(j, dev):
            return ins[j] if items[j][1] == "gather" else ins[j].at[dev]

        own = [pltpu.make_async_copy(source(j, me), outs[j].at[me], local_sems.at[j]) for j in range(n)]
        for cp in own:
            cp.start()
        remote = []
        for d in range(1, N_DEV):
            px = 1 - x if d & 4 else x
            py = 1 - y if d & 2 else y
            pc = 1 - c if d & 1 else c
            peer = 4 * px + 2 * py + pc
            for j in range(n):
                pltpu.make_async_remote_copy(
                    src_ref=source(j, peer), dst_ref=outs[j].at[me], send_sem=send_sems.at[j, d - 1],
                    recv_sem=recv_sems.at[j, d - 1], device_id=(px, py, pc), device_id_type=MESH).start()
                remote.append(pltpu.make_async_remote_copy(
                    src_ref=source(j, peer), dst_ref=outs[j].at[peer], send_sem=send_sems.at[j, d - 1],
                    recv_sem=recv_sems.at[j, d - 1], device_id=(px, py, pc), device_id_type=MESH))
        for cp in remote:
            cp.wait_send()
            cp.wait_recv()
        for cp in own:
            cp.wait()

    out_shape = []
    for arr, mode in items:
        shp = (N_DEV,) + tuple(arr.shape) if mode == "gather" else tuple(arr.shape)
        out_shape.append(jax.ShapeDtypeStruct(shp, arr.dtype))
    return pl.pallas_call(
        body, name=name, out_shape=out_shape,
        in_specs=[HBM_FULL] * n, out_specs=[HBM_FULL] * n,
        scratch_shapes=[pltpu.SemaphoreType.DMA((n, N_DEV - 1)), pltpu.SemaphoreType.DMA((n, N_DEV - 1)),
                        pltpu.SemaphoreType.DMA((n,))],
        compiler_params=pltpu.CompilerParams(has_side_effects=True),
    )(*[a for a, _ in items])


def _gather_two_level(arrs, name):
    n = len(arrs)

    def body(*refs):
        ins, outs = refs[:n], refs[n:2 * n]
        send_sems, recv_sems, local_sems = refs[2 * n:]
        x, y, c = lax.axis_index("x"), lax.axis_index("y"), lax.axis_index("c")
        sibling = (x, y, 1 - c)
        chips = [(1 - x, y), (x, 1 - y), (1 - x, 1 - y)]

        def slot(j, px, py, pc):
            return outs[j].at[4 * px + 2 * py + pc]

        def copy(j, k, block, to, src=None):
            return pltpu.make_async_remote_copy(
                src_ref=slot(j, *block) if src is None else src, dst_ref=slot(j, *block),
                send_sem=send_sems.at[j, k], recv_sem=recv_sems.at[j, k], device_id=to, device_id_type=MESH)

        own = [pltpu.make_async_copy(ins[j], slot(j, x, y, c), local_sems.at[j]) for j in range(n)]
        for cp in own:
            cp.start()
        sent = []
        for j in range(n):
            sent.append(copy(j, 0, (x, y, c), sibling, src=ins[j]))
            sent += [copy(j, 1 + i, (x, y, c), (*chip, c), src=ins[j]) for i, chip in enumerate(chips)]
        for cp in sent:
            cp.start()
        for i, chip in enumerate(chips):
            for j in range(n):
                copy(j, 1 + i, (*chip, c), (x, y, c)).wait_recv()
                passed = copy(j, 4 + i, (*chip, c), sibling)
                passed.start()
                sent.append(passed)
        for j in range(n):
            copy(j, 0, (x, y, 1 - c), (x, y, c)).wait_recv()
            for i, chip in enumerate(chips):
                copy(j, 4 + i, (*chip, 1 - c), (x, y, c)).wait_recv()
        for cp in sent:
            cp.wait_send()
        for cp in own:
            cp.wait()

    return pl.pallas_call(
        body, name=name, out_shape=[jax.ShapeDtypeStruct((N_DEV,) + tuple(a.shape), a.dtype) for a in arrs],
        in_specs=[HBM_FULL] * n, out_specs=[HBM_FULL] * n,
        scratch_shapes=[pltpu.SemaphoreType.DMA((n, N_DEV - 1)), pltpu.SemaphoreType.DMA((n, N_DEV - 1)),
                        pltpu.SemaphoreType.DMA((n,))],
        compiler_params=pltpu.CompilerParams(has_side_effects=True),
    )(*arrs)


def _pair_swap(arrs, name):
    n = len(arrs)

    def body(*refs):
        ins, outs = refs[:n], refs[n:2 * n]
        send_sems, recv_sems = refs[2 * n:]
        x, y, c = lax.axis_index("x"), lax.axis_index("y"), lax.axis_index("c")
        copies = [pltpu.make_async_remote_copy(
            src_ref=ins[j].at[1 - c], dst_ref=outs[j], send_sem=send_sems.at[j], recv_sem=recv_sems.at[j],
            device_id=(x, y, 1 - c), device_id_type=MESH) for j in range(n)]
        for cp in copies:
            cp.start()
        for cp in copies:
            cp.wait_send()
            cp.wait_recv()

    return pl.pallas_call(
        body, name=name, out_shape=[jax.ShapeDtypeStruct(tuple(a.shape[1:]), a.dtype) for a in arrs],
        in_specs=[HBM_FULL] * n, out_specs=[HBM_FULL] * n,
        scratch_shapes=[pltpu.SemaphoreType.DMA((n,)), pltpu.SemaphoreType.DMA((n,))],
        compiler_params=pltpu.CompilerParams(has_side_effects=True),
    )(*arrs)


def _chip_scatter(arrs, name):
    n = len(arrs)

    def body(*refs):
        ins, outs = refs[:n], refs[n:2 * n]
        send_sems, recv_sems, local_sems = refs[2 * n:]
        x, y, c = lax.axis_index("x"), lax.axis_index("y"), lax.axis_index("c")
        here = 2 * x + y
        own = [pltpu.make_async_copy(ins[j].at[here], outs[j].at[here], local_sems.at[j]) for j in range(n)]
        for cp in own:
            cp.start()
        waits = []
        for i, (px, py) in enumerate([(1 - x, y), (x, 1 - y), (1 - x, 1 - y)]):
            there = 2 * px + py
            for j in range(n):
                pltpu.make_async_remote_copy(
                    src_ref=ins[j].at[there], dst_ref=outs[j].at[here], send_sem=send_sems.at[j, i],
                    recv_sem=recv_sems.at[j, i], device_id=(px, py, c), device_id_type=MESH).start()
                waits.append(pltpu.make_async_remote_copy(
                    src_ref=ins[j].at[there], dst_ref=outs[j].at[there], send_sem=send_sems.at[j, i],
                    recv_sem=recv_sems.at[j, i], device_id=(px, py, c), device_id_type=MESH))
        for cp in waits:
            cp.wait_send()
            cp.wait_recv()
        for cp in own:
            cp.wait()

    return pl.pallas_call(
        body, name=name, out_shape=[jax.ShapeDtypeStruct(tuple(a.shape), a.dtype) for a in arrs],
        in_specs=[HBM_FULL] * n, out_specs=[HBM_FULL] * n,
        scratch_shapes=[pltpu.SemaphoreType.DMA((n, 3)), pltpu.SemaphoreType.DMA((n, 3)), pltpu.SemaphoreType.DMA((n,))],
        compiler_params=pltpu.CompilerParams(has_side_effects=True),
    )(*arrs)


def _pair_add(core, a, b):
    _, R, C = a.shape
    tr = _row_tile(R)

    def body(core_ref, a_ref, b_ref, o_ref):
        o_ref[...] = (a_ref[...].astype(F32) + b_ref[...].astype(F32)).astype(BF)

    blk = _rows((tr, C), lambda i, core: (i, 0))
    grid_spec = pltpu.PrefetchScalarGridSpec(
        num_scalar_prefetch=1, grid=(R // tr,),
        in_specs=[_rows((None, tr, C), lambda i, core: (core[0], i, 0)), blk], out_specs=blk)
    return pl.pallas_call(
        body, name="pair_add", grid_spec=grid_spec, out_shape=jax.ShapeDtypeStruct((R, C), BF),
        compiler_params=_params(1),
    )(core, a, b)


def _mod_fwd(c_all, w0, w1, wkv):
    n0, n1, n2 = w0.shape[1], w1.shape[1], wkv.shape[1]

    def body(c_ref, w0_ref, w1_ref, w2_ref, o_ref):
        cc = c_ref[...]
        s = cc * _sigmoid(cc)
        o_ref[:, 0:n0] = _dot(s, w0_ref[...])
        o_ref[:, n0:n0 + n1] = _dot(s, w1_ref[...])
        o_ref[:, n0 + n1:n0 + n1 + n2] = _dot(s, w2_ref[...])

    return pl.pallas_call(
        body, name="mod_fwd", out_shape=jax.ShapeDtypeStruct((N_DEV, n0 + n1 + n2), F32),
        in_specs=[VMEM_FULL] * 4, out_specs=VMEM_FULL,
        compiler_params=pltpu.CompilerParams(vmem_limit_bytes=VMEM_LIMIT),
    )(c_all, w0, w1, wkv)


def _mod_wgrad(c_t, dm):
    C = dm.shape[1]
    tr = 256

    def body(ct_ref, dm_ref, o_ref):
        ct = ct_ref[...]
        s = ct * _sigmoid(ct)
        dmv = dm_ref[...]
        lane = lax.broadcasted_iota(jnp.int32, (tr, N_DEV), 1)
        acc = jnp.zeros((tr, C), F32)
        for r in range(N_DEV):
            col = jnp.sum(jnp.where(lane == r, s, 0.0), axis=1, keepdims=True)
            acc = acc + col * dmv[r:r + 1, :]
        o_ref[...] = acc

    return pl.pallas_call(
        body, name="mod_wgrad", grid=(D // tr,), out_shape=jax.ShapeDtypeStruct((D, C), F32),
        in_specs=[_rows((tr, N_DEV), lambda i: (i, 0)), _rows((N_DEV, C), lambda i: (0, 0))],
        out_specs=_rows((tr, C), lambda i: (i, 0)), compiler_params=_params(1),
    )(c_t, dm)


def _ffn_fwd(h, vec, w13, w2):
    S = h.shape[0]

    def body(h_ref, vec_ref, w13_ref, w2_ref, ho_ref, u_ref, a_ref, b_ref):
        hv, vec = h_ref[...], vec_ref[...]
        u, _, _ = _norm_mod(hv, vec)
        ub = u.astype(BF)
        u_ref[...] = ub
        y = jnp.zeros((TS, D), F32)
        for k in range(DFF // CH):
            c0 = k * CH
            a = _dot(ub, w13_ref[:, c0:c0 + CH])
            b = _dot(ub, w13_ref[:, DFF + c0:DFF + c0 + CH])
            a_ref[:, c0:c0 + CH] = a.astype(BF)
            b_ref[:, c0:c0 + CH] = b.astype(BF)
            t = (a * _sigmoid(a)) * b
            y = y + _dot(t.astype(BF), w2_ref[c0:c0 + CH, :])
        ho_ref[...] = hv + (0.5 * vec[3:4]) * y

    tok = lambda i: (i, 0)
    return pl.pallas_call(
        body, name="ffn_fwd", grid=(S // TS,),
        out_shape=[jax.ShapeDtypeStruct((S, D), F32), jax.ShapeDtypeStruct((S, D), BF),
                   jax.ShapeDtypeStruct((S, DFF), BF), jax.ShapeDtypeStruct((S, DFF), BF)],
        in_specs=[_rows((TS, D), tok), _rows((8, D), lambda i: (0, 0)), VMEM_FULL, VMEM_FULL],
        out_specs=[_rows((TS, D), tok), _rows((TS, D), tok), _rows((TS, DFF), tok), _rows((TS, DFF), tok)],
        compiler_params=_params(1),
    )(h, vec, w13, w2)


def _ffn_bwd(dho, h, a, b, vec, w13, w2):
    S = h.shape[0]

    def body(dho_ref, h_ref, a_ref, b_ref, vec_ref, w13_ref, w2_ref,
             dh_ref, da_ref, db_ref, t_ref, dhb_ref, ps_ref):
        dho_v, vec = dho_ref[...], vec_ref[...]
        dhb_ref[...] = dho_v.astype(BF)
        dyb = ((0.5 * vec[3:4]) * dho_v).astype(BF)
        du = jnp.zeros((TS, D), F32)
        for k in range(DFF // CH):
            c0 = k * CH
            av = a_ref[:, c0:c0 + CH].astype(F32)
            bv = b_ref[:, c0:c0 + CH].astype(F32)
            dt = _dot_nt(dyb, w2_ref[c0:c0 + CH, :])
            sig = _sigmoid(av)
            sl = av * sig
            t_ref[:, c0:c0 + CH] = (sl * bv).astype(BF)
            dab = (dt * bv * (sig * (1.0 + av * (1.0 - sig)))).astype(BF)
            dbb = (dt * sl).astype(BF)
            da_ref[:, c0:c0 + CH] = dab
            db_ref[:, c0:c0 + CH] = dbb
            du = du + _dot_nt(dab, w13_ref[:, c0:c0 + CH]) + _dot_nt(dbb, w13_ref[:, DFF + c0:DFF + c0 + CH])
        _, xhat, r = _norm_mod(h_ref[...], vec)
        dhn, dg, dsh, dsc = _norm_mod_bwd(du, xhat, r, vec)
        dh_ref[...] = dho_v + dhn
        _accumulate(ps_ref, pl.program_id(0) == 0, [dg, dsh, dsc])

    tok = lambda i: (i, 0)
    return pl.pallas_call(
        body, name="ffn_bwd", grid=(S // TS,),
        out_shape=[jax.ShapeDtypeStruct((S, D), F32), jax.ShapeDtypeStruct((S, DFF), BF),
                   jax.ShapeDtypeStruct((S, DFF), BF), jax.ShapeDtypeStruct((S, DFF), BF),
                   jax.ShapeDtypeStruct((S, D), BF), jax.ShapeDtypeStruct((8, D), F32)],
        in_specs=[_rows((TS, D), tok), _rows((TS, D), tok), _rows((TS, DFF), tok), _rows((TS, DFF), tok),
                  _rows((8, D), lambda i: (0, 0)), VMEM_FULL, VMEM_FULL],
        out_specs=[_rows((TS, D), tok), _rows((TS, DFF), tok), _rows((TS, DFF), tok), _rows((TS, DFF), tok),
                   _rows((TS, D), tok), _rows((8, D), lambda i: (0, 0))],
        compiler_params=_params(1),
    )(dho, h, a, b, vec, w13, w2)


def _wgrad(a, b, tm, tn, name, gate=None):
    S, M = a.shape
    N = b.shape[1]
    n_s = S // TW

    def body(*refs):
        if gate is None:
            a_ref, b_ref, o_ref, acc_ref = refs
        else:
            a_ref, b_ref, w_ref, sc_ref, o_ref, gs_ref, acc_ref = refs
        s = pl.program_id(2)

        @pl.when(s == 0)
        def _():
            acc_ref[...] = jnp.zeros((tm, tn), F32)

        acc_ref[...] += _dot_tn(a_ref[...], b_ref[...])

        @pl.when(s == n_s - 1)
        def _():
            acc = acc_ref[...]
            if gate is None:
                o_ref[...] = acc.astype(BF)
            else:
                o_ref[...] = (acc * sc_ref[0:1, :]).astype(BF)
                gs_ref[...] = jnp.broadcast_to(_sum0(acc * w_ref[...].astype(F32)), (8, tn))

    in_specs = [_rows((TW, tm), lambda m, n, s: (s, m)), _rows((TW, tn), lambda m, n, s: (s, n))]
    out_shape = [jax.ShapeDtypeStruct((M, N), BF)]
    out_specs = [_rows((tm, tn), lambda m, n, s: (m, n))]
    args = [a, b]
    if gate is not None:
        in_specs += [_rows((tm, tn), lambda m, n, s: (m, n)), _rows((8, tn), lambda m, n, s: (0, n))]
        out_shape.append(jax.ShapeDtypeStruct((M // tm, 8, N), F32))
        out_specs.append(_rows((None, 8, tn), lambda m, n, s: (m, 0, n)))
        args += list(gate)
    res = pl.pallas_call(
        body, name=name, grid=(M // tm, N // tn, n_s), out_shape=out_shape, in_specs=in_specs,
        out_specs=out_specs, scratch_shapes=[pltpu.VMEM((tm, tn), F32)], compiler_params=_params(3),
    )(*args)
    return res[0] if gate is None else (res[0], res[1])


def _conv_in_fwd(h, vec, w1, b1):
    S = h.shape[0]

    def body(h_ref, vec_ref, w_ref, b1_ref, hn_ref, pre_ref):
        u, _, _ = _norm_mod(h_ref[...], vec_ref[...])
        ub = u.astype(BF)
        hn_ref[...] = ub
        pre_ref[...] = _dot(ub, w_ref[...]) + b1_ref[0:1, :]

    tok = lambda i: (i, 0)
    return pl.pallas_call(
        body, name="conv_in_fwd", grid=(S // TS,),
        out_shape=[jax.ShapeDtypeStruct((S, D), BF), jax.ShapeDtypeStruct((S, 2 * D), F32)],
        in_specs=[_rows((TS, D), tok), _rows((8, D), lambda i: (0, 0)), VMEM_FULL, _rows((8, 2 * D), lambda i: (0, 0))],
        out_specs=[_rows((TS, D), tok), _rows((TS, 2 * D), tok)], compiler_params=_params(1),
    )(h, vec, w1, b1)


def _glu(pre):
    return pre[:, :D] * _sigmoid(pre[:, D:])


def _tap_groups(offset):
    groups = {}
    for j in range(CONV_W):
        off = offset(j)
        groups.setdefault(off % 8, []).append((off - off % 8, j))
    return [(phase, sorted(taps)) for phase, taps in sorted(groups.items())]


def _shift_window(dst_ref, win_ref, phase, rows):
    dst_ref[0:rows, :] = win_ref[phase:phase + rows, :]


def _layernorm(u2):
    mu = _mean1(u2)
    xc = u2 - mu
    rstd = lax.rsqrt(_mean1(xc * xc) + EPS)
    return xc * rstd, rstd


def _conv_out_fwd(pre, h, cw, w2):
    S = h.shape[0]
    hb = TS // HALO

    def body(pre_ref, ph_ref, h_ref, cw_ref, w2_ref, u2_ref, z_ref, ho_ref, win_ref, sh_ref):
        i = pl.program_id(0)
        cwv = cw_ref[...]
        win_ref[0:HALO, :] = jnp.where(i > 0, _glu(ph_ref[...]), 0.0)
        win_ref[HALO:HALO + TS, :] = _glu(pre_ref[...])
        u2 = jnp.broadcast_to(cwv[31:32], (TS, D))
        for phase, taps in _tap_groups(lambda j: HALO - (CONV_W - 1) + j):
            _shift_window(sh_ref, win_ref, phase, taps[-1][0] + TS)
            for lo, j in taps:
                u2 = u2 + cwv[j:j + 1] * sh_ref[lo:lo + TS, :]
        u2_ref[...] = u2
        xh, _ = _layernorm(u2)
        un = xh * cwv[32:33] + cwv[33:34]
        zb = (un * _sigmoid(un)).astype(BF)
        z_ref[...] = zb
        y = _dot(zb, w2_ref[...]) + cwv[34:35]
        ho_ref[...] = h_ref[...] + cwv[35:36] * y

    tok = lambda i: (i, 0)
    return pl.pallas_call(
        body, name="conv_out_fwd", grid=(S // TS,),
        out_shape=[jax.ShapeDtypeStruct((S, D), F32), jax.ShapeDtypeStruct((S, D), BF), jax.ShapeDtypeStruct((S, D), F32)],
        in_specs=[_rows((TS, 2 * D), tok), _rows((HALO, 2 * D), lambda i: (jnp.maximum(i * hb - 1, 0), 0)),
                  _rows((TS, D), tok), _rows((40, D), lambda i: (0, 0)), VMEM_FULL],
        out_specs=[_rows((TS, D), tok), _rows((TS, D), tok), _rows((TS, D), tok)],
        scratch_shapes=[pltpu.VMEM((TS + HALO, D), F32)] * 2, compiler_params=_params(1),
    )(pre, pre, h, cw, w2)


def _conv_out_bwd(dho, u2, cw, w2):
    S = dho.shape[0]

    def body(dho_ref, u2_ref, cw_ref, w2_ref, du2_ref, dhb_ref, ps_ref):
        dho_v, cwv = dho_ref[...], cw_ref[...]
        dhb_ref[...] = dho_v.astype(BF)
        dz = _dot_nt((cwv[35:36] * dho_v).astype(BF), w2_ref[...])
        xh, rstd = _layernorm(u2_ref[...])
        un = xh * cwv[32:33] + cwv[33:34]
        sig = _sigmoid(un)
        dun = dz * (sig * (1.0 + un * (1.0 - sig)))
        dxh = dun * cwv[32:33]
        du2_ref[...] = rstd * (dxh - _mean1(dxh) - xh * _mean1(dxh * xh))
        _accumulate(ps_ref, pl.program_id(0) == 0, [_sum0(dun * xh), _sum0(dun), _sum0(dho_v)])

    tok = lambda i: (i, 0)
    return pl.pallas_call(
        body, name="conv_out_bwd", grid=(S // TS,),
        out_shape=[jax.ShapeDtypeStruct((S, D), F32), jax.ShapeDtypeStruct((S, D), BF), jax.ShapeDtypeStruct((8, D), F32)],
        in_specs=[_rows((TS, D), tok), _rows((TS, D), tok), _rows((40, D), lambda i: (0, 0)), VMEM_FULL],
        out_specs=[_rows((TS, D), tok), _rows((TS, D), tok), _rows((8, D), lambda i: (0, 0))],
        compiler_params=_params(1),
    )(dho, u2, cw, w2)


def _conv_in_bwd(du2, pre, h, dho, vec, cw, w1):
    S = h.shape[0]
    n_t = S // TS
    hb = TS // HALO

    def body(du2_ref, dh2h_ref, pre_ref, ph_ref, h_ref, dho_ref, vec_ref, cw_ref, w1_ref,
             dh_ref, dpre_ref, ps_ref, pw_ref, pb_ref, winu_ref, wind_ref, sh_ref):
        i = pl.program_id(0)
        vec, cwv = vec_ref[...], cw_ref[...]
        pre = pre_ref[...]
        av, sg = pre[:, :D], _sigmoid(pre[:, D:])
        winu_ref[0:HALO, :] = jnp.where(i > 0, _glu(ph_ref[...]), 0.0)
        winu_ref[HALO:HALO + TS, :] = av * sg
        du2v = du2_ref[...]
        wind_ref[0:TS, :] = du2v
        wind_ref[TS:TS + HALO, :] = jnp.where(i < n_t - 1, dh2h_ref[...], 0.0)

        @pl.when(i == 0)
        def _():
            pw_ref[...] = jnp.zeros((32, D), F32)

        for phase, taps in _tap_groups(lambda j: HALO - (CONV_W - 1) + j):
            _shift_window(sh_ref, winu_ref, phase, taps[-1][0] + TS)
            for lo, j in taps:
                pw_ref[j:j + 1, :] += _sum0(du2v * sh_ref[lo:lo + TS, :])
        pw_ref[31:32, :] += _sum0(du2v)
        du1 = jnp.zeros((TS, D), F32)
        for phase, taps in _tap_groups(lambda j: CONV_W - 1 - j):
            _shift_window(sh_ref, wind_ref, phase, taps[-1][0] + TS)
            for lo, j in taps:
                du1 = du1 + cwv[j:j + 1] * sh_ref[lo:lo + TS, :]
        da = du1 * sg
        dg = du1 * av * sg * (1.0 - sg)
        dab, dgb = da.astype(BF), dg.astype(BF)
        dpre_ref[:, :D] = dab
        dpre_ref[:, D:] = dgb

        @pl.when(i == 0)
        def _():
            pb_ref[...] = jnp.zeros((8, 2 * D), F32)

        pb_ref[0:1, :D] += _sum0(da)
        pb_ref[0:1, D:] += _sum0(dg)
        du = _dot_nt(dab, w1_ref[:, :D]) + _dot_nt(dgb, w1_ref[:, D:])
        _, xhat, r = _norm_mod(h_ref[...], vec)
        dhn, dgn, dsh, dsc = _norm_mod_bwd(du, xhat, r, vec)
        dh_ref[...] = dho_ref[...] + dhn
        _accumulate(ps_ref, i == 0, [dgn, dsh, dsc])

    tok = lambda i: (i, 0)
    fixed = lambda i: (0, 0)
    return pl.pallas_call(
        body, name="conv_in_bwd", grid=(n_t,),
        out_shape=[jax.ShapeDtypeStruct((S, D), F32), jax.ShapeDtypeStruct((S, 2 * D), BF),
                   jax.ShapeDtypeStruct((8, D), F32), jax.ShapeDtypeStruct((32, D), F32),
                   jax.ShapeDtypeStruct((8, 2 * D), F32)],
        in_specs=[_rows((TS, D), tok), _rows((HALO, D), lambda i: (jnp.minimum((i + 1) * hb, S // HALO - 1), 0)),
                  _rows((TS, 2 * D), tok), _rows((HALO, 2 * D), lambda i: (jnp.maximum(i * hb - 1, 0), 0)),
                  _rows((TS, D), tok), _rows((TS, D), tok), _rows((8, D), fixed), _rows((40, D), fixed), VMEM_FULL],
        out_specs=[_rows((TS, D), tok), _rows((TS, 2 * D), tok), _rows((8, D), fixed), _rows((32, D), fixed),
                   _rows((8, 2 * D), fixed)],
        scratch_shapes=[pltpu.VMEM((TS + HALO, D), F32)] * 3,
        compiler_params=_params(1),
    )(du2, du2, pre, pre, h, dho, vec, cw, w1)


def _lane():
    return lax.broadcasted_iota(jnp.int32, (TS, HP), 1)


def _kv_fwd(h, vec, wkva, g2, wkvb, rope):
    S = h.shape[0]

    def body(h_ref, vec_ref, wa_ref, g2_ref, wb_ref, rope_ref, hn_ref, ckv_ref, ckn_ref, k_ref, v_ref):
        u, _, _ = _norm_mod(h_ref[...], vec_ref[...])
        ub = u.astype(BF)
        hn_ref[...] = ub
        kva = _dot(ub, wa_ref[...])
        ckv = kva[:, :KV_LORA]
        ckv_ref[...] = ckv
        r2 = lax.rsqrt(_mean1(ckv * ckv) + EPS)
        cknb = ((ckv * r2) * g2_ref[0:1, :]).astype(BF)
        ckn_ref[...] = cknb
        kvb = _dot(cknb, wb_ref[...])
        kpe = _rope(kva[:, KV_LORA:KVA_P], rope_ref[...])
        lane = _lane()
        for hd in range(NH):
            blk = kvb[:, hd * HP:(hd + 1) * HP]
            k_ref[:, hd * HP:(hd + 1) * HP] = jnp.where(lane < 64, blk, kpe).astype(BF)
            v_ref[:, hd * HP:(hd + 1) * HP] = jnp.where(lane >= 64, blk, 0.0).astype(BF)

    tok = lambda i: (i, 0)
    fixed = lambda i: (0, 0)
    return pl.pallas_call(
        body, name="kv_fwd", grid=(S // TS,),
        out_shape=[jax.ShapeDtypeStruct((S, D), BF), jax.ShapeDtypeStruct((S, KV_LORA), F32),
                   jax.ShapeDtypeStruct((S, KV_LORA), BF), jax.ShapeDtypeStruct((S, NH * HP), BF),
                   jax.ShapeDtypeStruct((S, NH * HP), BF)],
        in_specs=[_rows((TS, D), tok), _rows((8, D), fixed), VMEM_FULL, _rows((8, KV_LORA), fixed), VMEM_FULL,
                  _rows((TS, 3 * HP), tok)],
        out_specs=[_rows((TS, D), tok), _rows((TS, KV_LORA), tok), _rows((TS, KV_LORA), tok),
                   _rows((TS, NH * HP), tok), _rows((TS, NH * HP), tok)],
        compiler_params=_params(1),
    )(h, vec, wkva, g2, wkvb, rope)


def _kv_bwd(dk, dv, h, ckv, dho, vec, wkva, g2, wkvb, rope):
    S = h.shape[0]

    def body(dk_ref, dv_ref, h_ref, ckv_ref, dho_ref, vec_ref, wa_ref, g2_ref, wb_ref, rope_ref,
             dh_ref, dkva_ref, dkvb_ref, ps_ref, ps2_ref):
        i = pl.program_id(0)
        vec = vec_ref[...]
        lane = _lane()
        dkpe = jnp.zeros((TS, HP), F32)
        for hd in range(NH):
            dkh = dk_ref[:, hd * HP:(hd + 1) * HP]
            dvh = dv_ref[:, hd * HP:(hd + 1) * HP]
            dkvb_ref[:, hd * HP:(hd + 1) * HP] = jnp.where(lane < 64, dkh, dvh).astype(BF)
            dkpe = dkpe + jnp.where(lane >= 64, dkh, 0.0)
        dkpe = _rope_t(dkpe, rope_ref[...])
        dckn = _dot_nt(dkvb_ref[...], wb_ref[...])
        ckv = ckv_ref[...]
        r2 = lax.rsqrt(_mean1(ckv * ckv) + EPS)
        dckv, dg2 = _rms_bwd(dckn, ckv, r2, g2_ref[0:1, :])
        dkva_ref[:, :KV_LORA] = dckv.astype(BF)
        dkva_ref[:, KV_LORA:KVA_P] = dkpe.astype(BF)
        du = _dot_nt(dkva_ref[...], wa_ref[...])
        _, xhat, r = _norm_mod(h_ref[...], vec)
        dhn, dgn, dsh, dsc = _norm_mod_bwd(du, xhat, r, vec)
        dh_ref[...] = dho_ref[...] + dhn
        _accumulate(ps_ref, i == 0, [dgn, dsh, dsc])
        _accumulate(ps2_ref, i == 0, [dg2])

    tok = lambda i: (i, 0)
    fixed = lambda i: (0, 0)
    return pl.pallas_call(
        body, name="kv_bwd", grid=(S // TS,),
        out_shape=[jax.ShapeDtypeStruct((S, D), F32), jax.ShapeDtypeStruct((S, KVA_P), BF),
                   jax.ShapeDtypeStruct((S, NH * HP), BF), jax.ShapeDtypeStruct((8, D), F32),
                   jax.ShapeDtypeStruct((8, KV_LORA), F32)],
        in_specs=[_rows((TS, NH * HP), tok), _rows((TS, NH * HP), tok), _rows((TS, D), tok), _rows((TS, KV_LORA), tok),
                  _rows((TS, D), tok), _rows((8, D), fixed), VMEM_FULL, _rows((8, KV_LORA), fixed), VMEM_FULL,
                  _rows((TS, 3 * HP), tok)],
        out_specs=[_rows((TS, D), tok), _rows((TS, KVA_P), tok), _rows((TS, NH * HP), tok), _rows((8, D), fixed),
                   _rows((8, KV_LORA), fixed)],
        compiler_params=_params(1),
    )(dk, dv, h, ckv, dho, vec, wkva, g2, wkvb, rope)


def _q_fwd(h, vec, wqa, g2, wqb, rope):
    S = h.shape[0]

    def body(h_ref, vec_ref, wa_ref, g2_ref, wb_ref, rope_ref, hn_ref, qa_ref, qan_ref, q_ref):
        u, _, _ = _norm_mod(h_ref[...], vec_ref[...])
        ub = u.astype(BF)
        hn_ref[...] = ub
        qa = _dot(ub, wa_ref[...])
        qa_ref[...] = qa
        r2 = lax.rsqrt(_mean1(qa * qa) + EPS)
        qanb = ((qa * r2) * g2_ref[0:1, :]).astype(BF)
        qan_ref[...] = qanb
        q = _dot(qanb, wb_ref[...])
        tab = rope_ref[...]
        for hd in range(NH):
            q_ref[:, hd * HP:(hd + 1) * HP] = _rope(q[:, hd * HP:(hd + 1) * HP], tab).astype(BF)

    tok = lambda i: (i, 0)
    fixed = lambda i: (0, 0)
    return pl.pallas_call(
        body, name="q_fwd", grid=(S // TS,),
        out_shape=[jax.ShapeDtypeStruct((S, D), BF), jax.ShapeDtypeStruct((S, Q_LORA), F32),
                   jax.ShapeDtypeStruct((S, Q_LORA), BF), jax.ShapeDtypeStruct((S, NH * HP), BF)],
        in_specs=[_rows((TS, D), tok), _rows((8, D), fixed), VMEM_FULL, _rows((8, Q_LORA), fixed), VMEM_FULL,
                  _rows((TS, 3 * HP), tok)],
        out_specs=[_rows((TS, D), tok), _rows((TS, Q_LORA), tok), _rows((TS, Q_LORA), tok), _rows((TS, NH * HP), tok)],
        compiler_params=_params(1),
    )(h, vec, wqa, g2, wqb, rope)


def _q_bwd(dq, h, qa, dho, vec, wqa, g2, wqb, rope):
    S = h.shape[0]

    def body(dq_ref, h_ref, qa_ref, dho_ref, vec_ref, wa_ref, g2_ref, wb_ref, rope_ref,
             dh_ref, dqb_ref, dqa_ref, ps_ref, ps2_ref):
        i = pl.program_id(0)
        vec, tab = vec_ref[...], rope_ref[...]
        for hd in range(NH):
            dqb_ref[:, hd * HP:(hd + 1) * HP] = _rope_t(dq_ref[:, hd * HP:(hd + 1) * HP], tab).astype(BF)
        dqan = _dot_nt(dqb_ref[...], wb_ref[...])
        qa = qa_ref[...]
        r2 = lax.rsqrt(_mean1(qa * qa) + EPS)
        dqa, dg2 = _rms_bwd(dqan, qa, r2, g2_ref[0:1, :])
        dqab = dqa.astype(BF)
        dqa_ref[...] = dqab
        du = _dot_nt(dqab, wa_ref[...])
        _, xhat, r = _norm_mod(h_ref[...], vec)
        dhn, dgn, dsh, dsc = _norm_mod_bwd(du, xhat, r, vec)
        dh_ref[...] = dho_ref[...] + dhn
        _accumulate(ps_ref, i == 0, [dgn, dsh, dsc])
        _accumulate(ps2_ref, i == 0, [dg2])

    tok = lambda i: (i, 0)
    fixed = lambda i: (0, 0)
    return pl.pallas_call(
        body, name="q_bwd", grid=(S // TS,),
        out_shape=[jax.ShapeDtypeStruct((S, D), F32), jax.ShapeDtypeStruct((S, NH * HP), BF),
                   jax.ShapeDtypeStruct((S, Q_LORA), BF), jax.ShapeDtypeStruct((8, D), F32),
                   jax.ShapeDtypeStruct((8, Q_LORA), F32)],
        in_specs=[_rows((TS, NH * HP), tok), _rows((TS, D), tok), _rows((TS, Q_LORA), tok), _rows((TS, D), tok),
                  _rows((8, D), fixed), VMEM_FULL, _rows((8, Q_LORA), fixed), VMEM_FULL, _rows((TS, 3 * HP), tok)],
        out_specs=[_rows((TS, D), tok), _rows((TS, NH * HP), tok), _rows((TS, Q_LORA), tok), _rows((8, D), fixed),
                   _rows((8, Q_LORA), fixed)],
        compiler_params=_params(1),
    )(dq, h, qa, dho, vec, wqa, g2, wqb, rope)


def _attn_fwd(q, k, v):
    S = q.shape[0]
    TA = TA_FWD
    nq = S // TA
    rg = min(TA, ATT_ROWS)
    groups = TA // rg

    def softmax_pv(scores, vt, state, masks):
        out = []
        for g in range(groups):
            m, l, acc = state[g]
            s = scores[g] if masks is None else jnp.where(masks[g], scores[g], NEG)
            m_new = jnp.maximum(m, jnp.max(s, axis=1, keepdims=True))
            p = jnp.exp2((s - m_new) * EXP2_SCALE)
            alpha = jnp.exp2((m - m_new) * EXP2_SCALE)
            vg = vt[g] if isinstance(vt, list) else vt
            out.append((m_new, alpha * l + jnp.sum(p, axis=1, keepdims=True), alpha * acc + _dot(p.astype(BF), vg)))
        return tuple(out)

    def body(q_ref, k_ref, v_ref, o_ref, lse_ref):
        qi = pl.program_id(1)
        qs = [q_ref[g * rg:(g + 1) * rg, :] for g in range(groups)]

        def keys(j):
            return pl.ds(pl.multiple_of(j * TA, TA), TA)

        def scores_of(j):
            kt = k_ref[keys(j), :]
            return tuple(_dot_nt(qs[g], kt) for g in range(groups))

        state = tuple((jnp.full((rg, 1), NEG, F32), jnp.zeros((rg, 1), F32), jnp.zeros((rg, HP), F32))
                      for _ in range(groups))

        def step(kj, state):
            return softmax_pv(scores_of(kj), v_ref[keys(kj), :], state, None)

        state = lax.fori_loop(0, qi, step, state)
        kd, vd = k_ref[keys(qi), :], v_ref[keys(qi), :]
        ends = [(g + 1) * rg for g in range(groups)]
        diag_scores = tuple(_dot_nt(qs[g], kd[:ends[g]]) for g in range(groups))
        masks = [lax.broadcasted_iota(jnp.int32, (rg, ends[g]), 1)
                 <= lax.broadcasted_iota(jnp.int32, (rg, ends[g]), 0) + g * rg for g in range(groups)]
        final = softmax_pv(diag_scores, [vd[:ends[g]] for g in range(groups)], state, masks)
        for g in range(groups):
            m, l, acc = final[g]
            o_ref[g * rg:(g + 1) * rg, :] = (acc / l).astype(BF)
            lse_ref[g * rg:(g + 1) * rg, :] = m * EXP2_SCALE + jnp.log(l) * LOG2E

    return pl.pallas_call(
        body, name="attn_fwd", grid=(NH, nq),
        out_shape=[jax.ShapeDtypeStruct((S, NH * HP), BF), jax.ShapeDtypeStruct((NH, S, 1), F32)],
        in_specs=[_rows((TA, HP), lambda h, i: (i, h)), _rows((S, HP), lambda h, i: (0, h)),
                  _rows((S, HP), lambda h, i: (0, h))],
        out_specs=[_rows((TA, HP), lambda h, i: (i, h)), _rows((None, TA, 1), lambda h, i: (h, i, 0))],
        compiler_params=_params(2),
    )(q, k, v)


def _attn_bwd(q, k, v, do, lse, delta):
    S = q.shape[0]
    nq = S // TA
    rg = min(TA, ATT_ROWS)
    groups = TA // rg

    def body(q_ref, do_ref, lse_ref, dl_ref, k_ref, v_ref, dq_ref, dk_ref, dv_ref, dka_ref, dva_ref):
        kj = pl.program_id(1)

        @pl.when(kj == 0)
        def _():
            dq_ref[...] = jnp.zeros((S, HP), F32)

        dka_ref[...] = jnp.zeros((TA, HP), F32)
        dva_ref[...] = jnp.zeros((TA, HP), F32)
        kt, vt = k_ref[...], v_ref[...]

        def tile(qi, diagonal):
            rows = [pl.ds(pl.multiple_of(qi * TA + g * rg, rg), rg) for g in range(groups)]
            ends = [(g + 1) * rg if diagonal else TA for g in range(groups)]
            qg = [q_ref[r, :] for r in rows]
            dog = [do_ref[r, :] for r in rows]
            scores = [_dot_nt(qg[g], kt[:ends[g]]) for g in range(groups)]
            dps = [_dot_nt(dog[g], vt[:ends[g]]) for g in range(groups)]
            for g in range(groups):
                p = jnp.exp2(scores[g] * EXP2_SCALE - lse_ref[rows[g], :])
                if diagonal:
                    col = lax.broadcasted_iota(jnp.int32, (rg, ends[g]), 1)
                    row = lax.broadcasted_iota(jnp.int32, (rg, ends[g]), 0)
                    p = jnp.where(col <= row + g * rg, p, 0.0)
                ds = p * (dps[g] - dl_ref[rows[g], :])
                pb, dsb = p.astype(BF), ds.astype(BF)
                dva_ref[0:ends[g], :] += _dot_tn(pb, dog[g])
                dka_ref[0:ends[g], :] += _dot_tn(dsb, qg[g])
                dq_ref[rows[g], :] += _dot(dsb, kt[:ends[g]]) * SM_SCALE

        tile(kj, True)

        def step(qi, carry):
            tile(qi, False)
            return carry

        lax.fori_loop(kj + 1, nq, step, 0)
        dk_ref[...] = dka_ref[...] * SM_SCALE
        dv_ref[...] = dva_ref[...]

    head = lambda h, j: (0, h)
    col1 = lambda h, j: (h, 0, 0)
    return pl.pallas_call(
        body, name="attn_bwd", grid=(NH, nq), out_shape=[jax.ShapeDtypeStruct((S, NH * HP), F32)] * 3,
        in_specs=[_rows((S, HP), head), _rows((S, HP), head), _rows((None, S, 1), col1), _rows((None, S, 1), col1),
                  _rows((TA, HP), lambda h, j: (j, h)), _rows((TA, HP), lambda h, j: (j, h))],
        out_specs=[_rows((S, HP), head), _rows((TA, HP), lambda h, j: (j, h)), _rows((TA, HP), lambda h, j: (j, h))],
        scratch_shapes=[pltpu.VMEM((TA, HP), F32), pltpu.VMEM((TA, HP), F32)], compiler_params=_params(2),
    )(q, do, lse, delta, k, v)


def _attn_out_fwd(o, h, wo, vec):
    S = h.shape[0]

    def body(o_ref, h_ref, wo_ref, vec_ref, ho_ref):
        ho_ref[...] = h_ref[...] + vec_ref[3:4, :] * _dot(o_ref[...], wo_ref[...])

    tok = lambda i: (i, 0)
    return pl.pallas_call(
        body, name="attn_out_fwd", grid=(S // TS,), out_shape=jax.ShapeDtypeStruct((S, D), F32),
        in_specs=[_rows((TS, NH * HP), tok), _rows((TS, D), tok), VMEM_FULL, _rows((8, D), lambda i: (0, 0))],
        out_specs=_rows((TS, D), tok), compiler_params=_params(1),
    )(o, h, wo, vec)


def _attn_out_bwd(dho, o, wo, vec):
    S = dho.shape[0]

    def body(dho_ref, o_ref, wo_ref, vec_ref, do_ref, dl_ref, dhb_ref):
        dho_v = dho_ref[...]
        dhb_ref[...] = dho_v.astype(BF)
        do = _dot_nt((vec_ref[3:4, :] * dho_v).astype(BF), wo_ref[...])
        do_ref[...] = do.astype(BF)
        prod = do * o_ref[...].astype(F32)
        for hd in range(NH):
            dl_ref[hd] = jnp.sum(prod[:, hd * HP:(hd + 1) * HP], axis=1, keepdims=True)

    tok = lambda i: (i, 0)
    return pl.pallas_call(
        body, name="attn_out_bwd", grid=(S // TS,),
        out_shape=[jax.ShapeDtypeStruct((S, NH * HP), BF), jax.ShapeDtypeStruct((NH, S, 1), F32),
                   jax.ShapeDtypeStruct((S, D), BF)],
        in_specs=[_rows((TS, D), tok), _rows((TS, NH * HP), tok), VMEM_FULL, _rows((8, D), lambda i: (0, 0))],
        out_specs=[_rows((TS, NH * HP), tok), _rows((NH, TS, 1), lambda i: (0, i, 0)), _rows((TS, D), tok)],
        compiler_params=_params(1),
    )(dho, o, wo, vec)


def _final(h, target, fg):
    S = h.shape[0]

    def body(h_ref, t_ref, g_ref, dh_ref, ps_ref):
        hv, g = h_ref[...], g_ref[0:1, :]
        r = lax.rsqrt(_mean1(hv * hv) + EPS)
        xhat = hv * r
        err = xhat * g - t_ref[...]
        loss = 0.5 * jnp.sum(_mean1(err * err), axis=0, keepdims=True)
        dy = err * (1.0 / D)
        dxhat = dy * g
        dh_ref[...] = r * (dxhat - xhat * _mean1(dxhat * xhat))
        _accumulate(ps_ref, pl.program_id(0) == 0, [_sum0(dy * xhat), jnp.broadcast_to(loss, (1, D))])

    tok = lambda i: (i, 0)
    return pl.pallas_call(
        body, name="final_loss", grid=(S // TS,),
        out_shape=[jax.ShapeDtypeStruct((S, D), F32), jax.ShapeDtypeStruct((8, D), F32)],
        in_specs=[_rows((TS, D), tok), _rows((TS, D), tok), _rows((8, D), lambda i: (0, 0))],
        out_specs=[_rows((TS, D), tok), _rows((8, D), lambda i: (0, 0))], compiler_params=_params(1),
    )(h, target, fg)


def _row_tile(r):
    if r <= 256:
        return r
    for t in range(256, 7, -8):
        if r % t == 0:
            return t
    return r


def _adamw(parts, w, m, v):
    P, R, C = parts.shape
    tr = _row_tile(R)

    def body(p_ref, w_ref, m_ref, v_ref, g_ref, d_ref, mo_ref, vo_ref):
        g = p_ref[0].astype(F32)
        for k in range(1, P):
            g = g + p_ref[k].astype(F32)
        g_ref[...] = g
        m2 = B1 * m_ref[...] + (1.0 - B1) * g
        v2 = B2 * v_ref[...] + (1.0 - B2) * (g * g)
        mo_ref[...] = m2
        vo_ref[...] = v2
        m_hat = m2 / (1.0 - B1 ** STEP)
        v_hat = v2 / (1.0 - B2 ** STEP)
        d_ref[...] = -LR * (m_hat / (jnp.sqrt(v_hat) + EPS_ADAM) + WD * w_ref[...])

    blk = _rows((tr, C), lambda i: (i, 0))
    return pl.pallas_call(
        body, name="adamw", grid=(R // tr,), out_shape=[jax.ShapeDtypeStruct((R, C), F32)] * 4,
        in_specs=[_rows((P, tr, C), lambda i: (0, i, 0)), blk, blk, blk], out_specs=[blk] * 4,
        compiler_params=_params(1),
    )(parts, w, m, v)


_WEIGHTS = ['ada_w', 'ada_b', 'norm_g', 'ffn_w13', 'ffn_w2', 'conv_w_pw1', 'conv_b_pw1', 'conv_w_dw', 'conv_b_dw',
            'conv_ln_g', 'conv_ln_b', 'conv_w_pw2', 'conv_b_pw2', 'kv_ada_w', 'kv_ada_b', 'kv_norm_g', 'w_kv_a',
            'kv_a_norm_g', 'w_kv_b', 'w_q_a', 'q_a_norm_g', 'w_q_b', 'w_o', 'final_norm_g']


def _vec(rows):
    rows = [r.reshape(1, -1).astype(F32) for r in rows]
    return jnp.concatenate(rows + [jnp.zeros((8 - len(rows), rows[0].shape[1]), F32)], axis=0)


def kernel(x, c, positions, ada_w, ada_b, norm_g, ffn_w13, ffn_w2, conv_w_pw1, conv_b_pw1, conv_w_dw, conv_b_dw, conv_ln_g, conv_ln_b, conv_w_pw2, conv_b_pw2, kv_ada_w, kv_ada_b, kv_norm_g, w_kv_a, kv_a_norm_g, w_kv_b, w_q_a, q_a_norm_g, w_q_b, w_o, final_norm_g, loss_target, m_ada_w, m_ada_b, m_norm_g, m_ffn_w13, m_ffn_w2, m_conv_w_pw1, m_conv_b_pw1, m_conv_w_dw, m_conv_b_dw, m_conv_ln_g, m_conv_ln_b, m_conv_w_pw2, m_conv_b_pw2, m_kv_ada_w, m_kv_ada_b, m_kv_norm_g, m_w_kv_a, m_kv_a_norm_g, m_w_kv_b, m_w_q_a, m_q_a_norm_g, m_w_q_b, m_w_o, m_final_norm_g, v_ada_w, v_ada_b, v_norm_g, v_ffn_w13, v_ffn_w2, v_conv_w_pw1, v_conv_b_pw1, v_conv_w_dw, v_conv_b_dw, v_conv_ln_g, v_conv_ln_b, v_conv_w_pw2, v_conv_b_pw2, v_kv_ada_w, v_kv_ada_b, v_kv_norm_g, v_w_kv_a, v_kv_a_norm_g, v_w_kv_b, v_w_q_a, v_q_a_norm_g, v_w_q_b, v_w_o, v_final_norm_g):
    given = dict(locals())
    S = x.shape[1]
    me = 4 * lax.axis_index("x") + 2 * lax.axis_index("y") + lax.axis_index("c")

    small = jnp.concatenate([
        conv_w_dw[0], conv_b_dw, conv_ln_g, conv_ln_b, conv_b_pw2,
        norm_g.reshape(6, 128), conv_b_pw1.reshape(2, 128),
        c.reshape(8, 128), jnp.zeros((5, 128), F32)], axis=0)
    big = [ffn_w13, ffn_w2, conv_w_pw1, conv_w_pw2, w_kv_a, w_kv_b, w_q_a, w_q_b, w_o]
    got = _gather_two_level([small] + [w.astype(BF) for w in big], "gather_weights")
    sm = got[0]
    chan = lambda lo, hi: jnp.moveaxis(sm[:, lo:hi, :], 0, 1).reshape(hi - lo, D)
    w_dw_f, b_dw_f, ln_g_f, ln_b_f, b_pw2_f = chan(0, 31), chan(31, 32), chan(32, 33), chan(33, 34), chan(34, 35)
    norm_f = chan(35, 41).reshape(2, 3, D)
    b_pw1_f = sm[:, 41:43, :].reshape(1, 2 * D)
    c_all = sm[:, 43:51, :].reshape(N_DEV, D)
    w13_f = jnp.transpose(got[1], (1, 2, 3, 0, 4)).reshape(2, 2, D, 2 * DFF)
    w2_f = jnp.transpose(got[2], (1, 2, 0, 3, 4)).reshape(2, 2, DFF, D)
    pw1_f = jnp.transpose(got[3][:, 0], (1, 0, 2)).reshape(D, 2 * D)
    pw2_f = got[4].reshape(D, D)
    wkva = got[5].reshape(D, KV_LORA + ROPE)
    wkva_f = jnp.concatenate([wkva[:, :KV_LORA], jnp.zeros((D, 64), BF), wkva[:, KV_LORA:], jnp.zeros((D, 32), BF)], axis=1)
    wkvb_f = jnp.transpose(got[6], (1, 0, 2)).reshape(KV_LORA, NH * HP)
    wqa_f = got[7].reshape(D, Q_LORA)
    wqb = jnp.transpose(got[8][:, 0], (1, 0, 2)).reshape(Q_LORA, NH, 96)
    wqb_f = jnp.pad(wqb, ((0, 0), (0, 0), (0, HP - 96))).reshape(Q_LORA, NH * HP)
    wo_f = jnp.pad(got[9].reshape(NH, 64, D), ((0, 0), (64, 0), (0, 0))).reshape(NH * HP, D)

    n_ada = ada_w.shape[2]
    n_kva = kv_ada_w.shape[1]
    modp = _mod_fwd(c_all, ada_w[0], ada_w[1], kv_ada_w)
    (modr,) = _exchange([(modp.reshape(N_DEV, 1, 2 * n_ada + n_kva), "scatter")], "scatter_mod")
    modr = modr[:, 0, :]
    mod = jnp.transpose(modr[:, :2 * n_ada].reshape(N_DEV, 2, n_ada), (1, 0, 2)).reshape(2, 9 * D) + ada_b
    mod = mod.reshape(2, 9, D)
    kvmod = (modr[:, 2 * n_ada:].reshape(2 * D) + kv_ada_b).reshape(2, D)

    def sub_vec(l, idx):
        return _vec([norm_f[l, idx], mod[l, 3 * idx], mod[l, 3 * idx + 1], mod[l, 3 * idx + 2]])

    vec_kv = _vec([kv_norm_g, kvmod[0], kvmod[1]])
    cw = jnp.concatenate([w_dw_f, b_dw_f, ln_g_f, ln_b_f, b_pw2_f, mod[0, 5].reshape(1, D), jnp.zeros((4, D), F32)], axis=0)
    b1v = _vec([b_pw1_f])
    g_kva = _vec([kv_a_norm_g])
    g_qa = _vec([q_a_norm_g[0]])
    fgv = _vec([final_norm_g])

    inv_freq = 10000.0 ** (-jnp.arange(0, ROPE, 2, dtype=F32) / ROPE)
    ang = positions[0].astype(F32)[:, None] * inv_freq
    cs, sn = jnp.cos(ang), jnp.sin(ang)
    z16, z32, z64 = jnp.zeros((S, 16), F32), jnp.zeros((S, 32), F32), jnp.zeros((S, 64), F32)
    rope = jnp.concatenate([jnp.ones((S, 64), F32), cs, cs, z32,
                            z64, z16, sn, z32,
                            z64, -sn, z16, z32], axis=1)

    h0 = x[0]
    h1, u00, a00, b00 = _ffn_fwd(h0, sub_vec(0, 0), w13_f[0, 0], w2_f[0, 0])
    hn_c, pre = _conv_in_fwd(h1, sub_vec(0, 1), pw1_f, b1v)
    u2, z_c, h2 = _conv_out_fwd(pre, h1, cw, pw2_f)
    h3, u01, a01, b01 = _ffn_fwd(h2, sub_vec(0, 2), w13_f[0, 1], w2_f[0, 1])
    hn_kv, ckv, ckn, k_all, v_all = _kv_fwd(h3, vec_kv, wkva_f, g_kva, wkvb_f, rope)
    h4, u10, a10, b10 = _ffn_fwd(h3, sub_vec(1, 0), w13_f[1, 0], w2_f[1, 0])
    hn_q, qa, qan, q_all = _q_fwd(h4, sub_vec(1, 1), wqa_f, g_qa, wqb_f, rope)
    o_all, lse = _attn_fwd(q_all, k_all, v_all)
    h5 = _attn_out_fwd(o_all, h4, wo_f, sub_vec(1, 1))
    h6, u11, a11, b11 = _ffn_fwd(h5, sub_vec(1, 2), w13_f[1, 1], w2_f[1, 1])

    dh6, ps_fin = _final(h6, loss_target[0], fgv)
    loss = lax.psum(ps_fin[1, 0], ("x", "y", "c"))

    def ffn_back(dho, h_in, u, a, b, l, i):
        vec = sub_vec(l, 2 * i)
        dh, da, db, t, dhb, ps = _ffn_bwd(dho, h_in, a, b, vec, w13_f[l, i], w2_f[l, i])
        dwa = _wgrad(u, da, D, CH, "wgrad_w13")
        dwb = _wgrad(u, db, D, CH, "wgrad_w13")
        dw2, gs = _wgrad(t, dhb, CH, D, "wgrad_w2", gate=(w2_f[l, i], _vec([0.5 * vec[3]])))
        dgate = 0.5 * jnp.sum(gs[:, 0, :], axis=0)
        return dh, jnp.concatenate([dwa, dwb], axis=1), dw2, ps, dgate

    dh5, dw13_11, dw2_11, ps11, dg11 = ffn_back(dh6, h5, u11, a11, b11, 1, 1)
    vec_m1 = sub_vec(1, 1)
    do_all, delta, dhb5 = _attn_out_bwd(dh5, o_all, wo_f, vec_m1)
    dwo_p, gs_o = _wgrad(o_all, dhb5, D, D, "wgrad_wo", gate=(wo_f, _vec([vec_m1[3]])))
    dgm1 = jnp.sum(gs_o[:, 0, :], axis=0)
    dq_all, dk_all, dv_all = _attn_bwd(q_all, k_all, v_all, do_all, lse, delta)
    dh4, dqb, dqab, ps_q, ps_q2 = _q_bwd(dq_all, h4, qa, dh5, vec_m1, wqa_f, g_qa, wqb_f, rope)
    dwqb_p = _wgrad(qan, dqb, Q_LORA, D, "wgrad_wqb")
    dwqa = _wgrad(hn_q, dqab, D, Q_LORA, "wgrad_wqa")
    dh3a, dw13_10, dw2_10, ps10, dg10 = ffn_back(dh4, h3, u10, a10, b10, 1, 0)
    dh3, dkva, dkvb, ps_kv, ps_kv2 = _kv_bwd(dk_all, dv_all, h3, ckv, dh3a, vec_kv, wkva_f, g_kva, wkvb_f, rope)
    dwkva_p = _wgrad(hn_kv, dkva, D, KVA_P, "wgrad_wkva")
    dwkvb = _wgrad(ckn, dkvb, KV_LORA, D, "wgrad_wkvb")
    dh2, dw13_01, dw2_01, ps01, dg01 = ffn_back(dh3, h2, u01, a01, b01, 0, 1)
    vec_m0 = sub_vec(0, 1)
    du2, dhb2, ps_co = _conv_out_bwd(dh2, u2, cw, pw2_f)
    dpw2, gs_c = _wgrad(z_c, dhb2, D, D, "wgrad_pw2", gate=(pw2_f, _vec([vec_m0[3]])))
    dgm0 = jnp.sum(gs_c[:, 0, :], axis=0) + b_pw2_f[0] * ps_co[2]
    dh1, dpre, ps_ci, ps_dw, ps_b1 = _conv_in_bwd(du2, pre, h1, dh2, vec_m0, cw, pw1_f)
    dpw1 = _wgrad(hn_c, dpre, D, D, "wgrad_pw1")
    dh0, dw13_00, dw2_00, ps00, dg00 = ffn_back(dh1, h0, u00, a00, b00, 0, 0)

    dw13 = jnp.stack([jnp.stack([dw13_00, dw13_01]), jnp.stack([dw13_10, dw13_11])])
    dw2 = jnp.stack([jnp.stack([dw2_00, dw2_01]), jnp.stack([dw2_10, dw2_11])])
    send = [
        jnp.transpose(dw13.reshape(2, 2, D, N_DEV, 704), (3, 0, 1, 2, 4)),
        jnp.transpose(dw2.reshape(2, 2, N_DEV, 352, D), (2, 0, 1, 3, 4)),
        jnp.transpose(dpw1.reshape(D, N_DEV, 256), (1, 0, 2)),
        dpw2.reshape(N_DEV, 128, D),
        jnp.concatenate([dwkva_p[:, :KV_LORA], dwkva_p[:, KV_LORA + 64:KV_LORA + 96]], axis=1).reshape(N_DEV, 128, KV_LORA + ROPE),
        jnp.transpose(dwkvb.reshape(KV_LORA, N_DEV, 256), (1, 0, 2)),
        dwqa.reshape(N_DEV, 128, Q_LORA),
        jnp.transpose(dwqb_p.reshape(Q_LORA, NH, HP)[:, :, :96].reshape(Q_LORA, N_DEV, 192), (1, 0, 2)),
        dwo_p.reshape(NH, HP, D)[:, 64:, :].reshape(N_DEV, 128, D),
    ]
    dmod = jnp.stack([
        jnp.stack([ps00[1], ps00[2], dg00, ps_ci[1], ps_ci[2], dgm0, ps01[1], ps01[2], dg01]),
        jnp.stack([ps10[1], ps10[2], dg10, ps_q[1], ps_q[2], dgm1, ps11[1], ps11[2], dg11])])
    dnorm = jnp.stack([ps00[0], ps_ci[0], ps01[0], ps10[0], ps_q[0], ps11[0]])
    pieces = [dnorm, ps_b1[0], ps_dw[0:31], ps_dw[31], ps_co[0], ps_co[1], vec_m0[3] * ps_co[2],
              ps_kv[0], ps_kv2[0], ps_q2[0], ps_fin[0], dmod, ps_kv[1], ps_kv[2]]
    sizes = [int(np.prod(p.shape)) for p in pieces]
    offs = np.concatenate([[0], np.cumsum(sizes)]).astype(int)
    flat = jnp.concatenate([p.reshape(-1) for p in pieces]).reshape(-1, 128)
    core = lax.axis_index("c").reshape(1).astype(jnp.int32)
    by_core = [s.reshape((4, 2) + s.shape[1:]).swapaxes(0, 1) for s in send]
    from_sibling = _pair_swap(by_core, "pair_swap_grads")
    pair_sums = [_pair_add(core, a.reshape(2, -1, a.shape[-1]), b.reshape(-1, b.shape[-1])).reshape(b.shape)
                 for a, b in zip(by_core, from_sibling)]
    got2 = _exchange([(flat, "gather")], "gather_partials") + _chip_scatter(pair_sums, "chip_scatter_grads")
    part = got2[0].reshape(N_DEV, -1)

    def piece(i, rows, cols):
        return part[:, offs[i]:offs[i + 1]].reshape(N_DEV, rows, cols)

    def mine(i, rows):
        return lax.dynamic_slice_in_dim(piece(i, rows, D), me * 128, 128, axis=2)

    dmod_all = piece(11, 2, 9 * D)
    c_t = jnp.transpose(c_all)
    g_ada = jnp.stack([_mod_wgrad(c_t, lax.dynamic_slice_in_dim(dmod_all[:, l], me * n_ada, n_ada, axis=1))
                       for l in range(2)])
    dkvmod_all = jnp.concatenate([piece(12, 1, D)[:, 0], piece(13, 1, D)[:, 0]], axis=1)
    g_kvada = _mod_wgrad(c_t, lax.dynamic_slice_in_dim(dkvmod_all, me * n_kva, n_kva, axis=1))

    parts = {
        'ada_w': g_ada.reshape(1, 2 * D, n_ada),
        'ada_b': dmod_all,
        'norm_g': mine(0, 6),
        'ffn_w13': got2[1].reshape(4, 4 * D, 704),
        'ffn_w2': got2[2].reshape(4, 4 * 352, D),
        'conv_w_pw1': got2[3],
        'conv_b_pw1': lax.dynamic_slice_in_dim(piece(1, 1, 2 * D), me * 256, 256, axis=2),
        'conv_w_dw': mine(2, 31),
        'conv_b_dw': mine(3, 1),
        'conv_ln_g': mine(4, 1),
        'conv_ln_b': mine(5, 1),
        'conv_w_pw2': got2[4],
        'conv_b_pw2': mine(6, 1),
        'kv_ada_w': g_kvada.reshape(1, D, n_kva),
        'kv_ada_b': dkvmod_all.reshape(N_DEV, 1, 2 * D),
        'kv_norm_g': piece(7, 1, D),
        'w_kv_a': got2[5],
        'kv_a_norm_g': piece(8, 1, KV_LORA),
        'w_kv_b': got2[6],
        'w_q_a': got2[7],
        'q_a_norm_g': piece(9, 1, Q_LORA),
        'w_q_b': got2[8],
        'w_o': got2[9],
        'final_norm_g': piece(10, 1, D),
    }
    grads, deltas, new_m, new_v = [], [], [], []
    for name in _WEIGHTS:
        w = given[name]
        p = parts[name]
        shape2 = p.shape[1:]
        g, dlt, m2, v2 = _adamw(p, w.reshape(shape2), given['m_' + name].reshape(shape2), given['v_' + name].reshape(shape2))
        grads.append(g.reshape(w.shape))
        deltas.append(dlt.reshape(w.shape))
        new_m.append(m2.reshape(w.shape))
        new_v.append(v2.reshape(w.shape))
    return (loss, dh0.reshape(1, S, D), *grads, *deltas, *new_m, *new_v)
```

```python
import functools

import numpy as np
import jax
import jax.numpy as jnp
from jax import lax
from jax.experimental import pallas as pl
from jax.experimental.pallas import tpu as pltpu

F32, BF = jnp.float32, jnp.bfloat16

D = 1024
DFF = 2816
CH = 1408
NH = 16
HP = 128
KV_LORA, Q_LORA, ROPE = 256, 512, 32
KVA_P = 384
CONV_W = 31
HALO = 32
EPS = 1e-6
SM_SCALE = float((64 + 32) ** -0.5)
LOG2E = 1.4426950408889634
EXP2_SCALE = SM_SCALE * LOG2E
NEG = -1e30
N_DEV = 8
MESH = pl.DeviceIdType.MESH

TS = 256
TA = 1024
TA_FWD = 1024
ATT_ROWS = 256
TW = 2048
VMEM_LIMIT = 56 * 1024 * 1024

LR, B1, B2, EPS_ADAM, WD, STEP = 0.001, 0.9, 0.999, 1e-08, 0.01, 10

VMEM_FULL = pl.BlockSpec(memory_space=pltpu.VMEM)
HBM_FULL = pl.BlockSpec(memory_space=pltpu.HBM)


def _params(n_grid):
    return pltpu.CompilerParams(dimension_semantics=("arbitrary",) * n_grid, vmem_limit_bytes=VMEM_LIMIT)


def _dot(a, b):
    return jnp.dot(a, b, preferred_element_type=F32)


def _dot_nt(a, b):
    return lax.dot_general(a, b, (((1,), (1,)), ((), ())), preferred_element_type=F32)


def _dot_tn(a, b):
    return lax.dot_general(a, b, (((0,), (0,)), ((), ())), preferred_element_type=F32)


def _sum0(x):
    return jnp.sum(x, axis=0, keepdims=True)


def _mean1(x):
    return jnp.mean(x, axis=-1, keepdims=True)


def _sigmoid(x):
    return jax.nn.sigmoid(x)


def _rows(shape, imap):
    return pl.BlockSpec(shape, imap)


def _norm_mod(h, vec):
    r = lax.rsqrt(_mean1(h * h) + EPS)
    xhat = h * r
    u = (xhat * vec[0:1]) * (1.0 + vec[2:3]) + vec[1:2]
    return u, xhat, r


def _norm_mod_bwd(du, xhat, r, vec):
    g = vec[0:1]
    dxn = du * (1.0 + vec[2:3])
    dsh = _sum0(du)
    dsc = _sum0(du * (xhat * g))
    dg = _sum0(dxn * xhat)
    dxhat = dxn * g
    dh = r * (dxhat - xhat * _mean1(dxhat * xhat))
    return dh, dg, dsh, dsc


def _rms_bwd(dy, x, r, g):
    xhat = x * r
    dg = _sum0(dy * xhat)
    dxhat = dy * g
    return r * (dxhat - xhat * _mean1(dxhat * xhat)), dg


def _accumulate(ref, first, rows):
    @pl.when(first)
    def _():
        ref[...] = jnp.zeros(ref.shape, ref.dtype)

    for i, row in enumerate(rows):
        ref[i:i + 1, :] += row


def _rope(x, tab):
    return x * tab[:, 0:HP] + pltpu.roll(x, 16, 1) * tab[:, HP:2 * HP] + pltpu.roll(x, HP - 16, 1) * tab[:, 2 * HP:3 * HP]


def _rope_t(dy, tab):
    return (dy * tab[:, 0:HP] + pltpu.roll(dy * tab[:, HP:2 * HP], HP - 16, 1)
            + pltpu.roll(dy * tab[:, 2 * HP:3 * HP], 16, 1))


def _exchange(items, name):
    n = len(items)

    def body(*refs):
        ins, outs = refs[:n], refs[n:2 * n]
        send_sems, recv_sems, local_sems = refs[2 * n:]
        x, y, c = lax.axis_index("x"), lax.axis_index("y"), lax.axis_index("c")
        me = 4 * x + 2 * y + c

        def source(j, dev):
            return ins[j] if items[j][1] == "gather" else ins[j].at[dev]

        own = [pltpu.make_async_copy(source(j, me), outs[j].at[me], local_sems.at[j]) for j in range(n)]
        for cp in own:
            cp.start()
        remote = []
        for d in range(1, N_DEV):
            px = 1 - x if d & 4 else x
            py = 1 - y if d & 2 else y
            pc = 1 - c if d & 1 else c
            peer = 4 * px + 2 * py + pc
            for j in range(n):
                pltpu.make_async_remote_copy(
                    src_ref=source(j, peer), dst_ref=outs[j].at[me], send_sem=send_sems.at[j, d - 1],
                    recv_sem=recv_sems.at[j, d - 1], device_id=(px, py, pc), device_id_type=MESH).start()
                remote.append(pltpu.make_async_remote_copy(
                    src_ref=source(j, peer), dst_ref=outs[j].at[peer], send_sem=send_sems.at[j, d - 1],
                    recv_sem=recv_sems.at[j, d - 1], device_id=(px, py, pc), device_id_type=MESH))
        for cp in remote:
            cp.wait_send()
            cp.wait_recv()
        for cp in own:
            cp.wait()

    out_shape = []
    for arr, mode in items:
        shp = (N_DEV,) + tuple(arr.shape) if mode == "gather" else tuple(arr.shape)
        out_shape.append(jax.ShapeDtypeStruct(shp, arr.dtype))
    return pl.pallas_call(
        body, name=name, out_shape=out_shape,
        in_specs=[HBM_FULL] * n, out_specs=[HBM_FULL] * n,
        scratch_shapes=[pltpu.SemaphoreType.DMA((n, N_DEV - 1)), pltpu.SemaphoreType.DMA((n, N_DEV - 1)),
                        pltpu.SemaphoreType.DMA((n,))],
        compiler_params=pltpu.CompilerParams(has_side_effects=True),
    )(*[a for a, _ in items])


def _gather_two_level(arrs, name):
    n = len(arrs)

    def body(*refs):
        ins, outs = refs[:n], refs[n:2 * n]
        send_sems, recv_sems, local_sems = refs[2 * n:]
        x, y, c = lax.axis_index("x"), lax.axis_index("y"), lax.axis_index("c")
        sibling = (x, y, 1 - c)
        chips = [(1 - x, y), (x, 1 - y), (1 - x, 1 - y)]

        def slot(j, px, py, pc):
            return outs[j].at[4 * px + 2 * py + pc]

        def copy(j, k, block, to, src=None):
            return pltpu.make_async_remote_copy(
                src_ref=slot(j, *block) if src is None else src, dst_ref=slot(j, *block),
                send_sem=send_sems.at[j, k], recv_sem=recv_sems.at[j, k], device_id=to, device_id_type=MESH)

        own = [pltpu.make_async_copy(ins[j], slot(j, x, y, c), local_sems.at[j]) for j in range(n)]
        for cp in own:
            cp.start()
        sent = []
        for j in range(n):
            sent.append(copy(j, 0, (x, y, c), sibling, src=ins[j]))
            sent += [copy(j, 1 + i, (x, y, c), (*chip, c), src=ins[j]) for i, chip in enumerate(chips)]
        for cp in sent:
            cp.start()
        for i, chip in enumerate(chips):
            for j in range(n):
                copy(j, 1 + i, (*chip, c), (x, y, c)).wait_recv()
                passed = copy(j, 4 + i, (*chip, c), sibling)
                passed.start()
                sent.append(passed)
        for j in range(n):
            copy(j, 0, (x, y, 1 - c), (x, y, c)).wait_recv()
            for i, chip in enumerate(chips):
                copy(j, 4 + i, (*chip, 1 - c), (x, y, c)).wait_recv()
        for cp in sent:
            cp.wait_send()
        for cp in own:
            cp.wait()

    return pl.pallas_call(
        body, name=name, out_shape=[jax.ShapeDtypeStruct((N_DEV,) + tuple(a.shape), a.dtype) for a in arrs],
        in_specs=[HBM_FULL] * n, out_specs=[HBM_FULL] * n,
        scratch_shapes=[pltpu.SemaphoreType.DMA((n, N_DEV - 1)), pltpu.SemaphoreType.DMA((n, N_DEV - 1)),
                        pltpu.SemaphoreType.DMA((n,))],
        compiler_params=pltpu.CompilerParams(has_side_effects=True),
    )(*arrs)


def _pair_swap(arrs, name):
    n = len(arrs)

    def body(*refs):
        ins, outs = refs[:n], refs[n:2 * n]
        send_sems, recv_sems = refs[2 * n:]
        x, y, c = lax.axis_index("x"), lax.axis_index("y"), lax.axis_index("c")
        copies = [pltpu.make_async_remote_copy(
            src_ref=ins[j].at[1 - c], dst_ref=outs[j], send_sem=send_sems.at[j], recv_sem=recv_sems.at[j],
            device_id=(x, y, 1 - c), device_id_type=MESH) for j in range(n)]
        for cp in copies:
            cp.start()
        for cp in copies:
            cp.wait_send()
            cp.wait_recv()

    return pl.pallas_call(
        body, name=name, out_shape=[jax.ShapeDtypeStruct(tuple(a.shape[1:]), a.dtype) for a in arrs],
        in_specs=[HBM_FULL] * n, out_specs=[HBM_FULL] * n,
        scratch_shapes=[pltpu.SemaphoreType.DMA((n,)), pltpu.SemaphoreType.DMA((n,))],
        compiler_params=pltpu.CompilerParams(has_side_effects=True),
    )(*arrs)


def _chip_comm(arrs, scatter):
    n = len(arrs)
    out_shapes = [jax.ShapeDtypeStruct(tuple(a.shape) if scatter else (4,) + tuple(a.shape), a.dtype) for a in arrs]
    scratch = [pltpu.SemaphoreType.DMA((n, 3)), pltpu.SemaphoreType.DMA((n, 3)), pltpu.SemaphoreType.DMA((n,))]

    def copies(ins, outs, send_sems, recv_sems, local_sems, arriving):
        x, y, c = lax.axis_index("x"), lax.axis_index("y"), lax.axis_index("c")
        here = 2 * x + y

        def source(j, chip):
            return ins[j].at[chip] if scatter else ins[j]

        own = [pltpu.make_async_copy(source(j, here), outs[j].at[here], local_sems.at[j]) for j in range(n)]
        remote = []
        for i, (px, py) in enumerate([(1 - x, y), (x, 1 - y), (1 - x, 1 - y)]):
            there = 2 * px + py
            for j in range(n):
                remote.append(pltpu.make_async_remote_copy(
                    src_ref=source(j, there), dst_ref=outs[j].at[there if arriving else here],
                    send_sem=send_sems.at[j, i], recv_sem=recv_sems.at[j, i], device_id=(px, py, c),
                    device_id_type=MESH))
        return own, remote

    def start(*refs):
        own, remote = copies(*refs, arriving=False)
        for cp in own + remote:
            cp.start()

    def finish(*refs):
        own, remote = copies(*refs, arriving=True)
        for cp in remote:
            cp.wait_send()
            cp.wait_recv()
        for cp in own:
            cp.wait()

    return out_shapes, scratch, start, finish


def _chip_scatter(arrs, name):
    n = len(arrs)
    out_shapes, scratch, start, finish = _chip_comm(arrs, True)

    def body(*refs):
        start(refs[:n], refs[n:2 * n], *refs[2 * n:])
        finish(refs[:n], refs[n:2 * n], *refs[2 * n:])

    return pl.pallas_call(
        body, name=name, out_shape=out_shapes, in_specs=[HBM_FULL] * n, out_specs=[HBM_FULL] * n,
        scratch_shapes=scratch, compiler_params=pltpu.CompilerParams(has_side_effects=True),
    )(*arrs)


def _hosted_call(body, args, comm, *, name, n_steps, in_specs, out_specs, out_shape, scratch_shapes=()):
    if comm is None:
        res = pl.pallas_call(
            body, name=name, grid=(n_steps,), out_shape=list(out_shape), in_specs=list(in_specs),
            out_specs=list(out_specs), scratch_shapes=list(scratch_shapes), compiler_params=_params(1))(*args)
        return res, []
    arrs, scatter = comm
    c_shapes, c_scratch, start, finish = _chip_comm(arrs, scatter)
    n_in, n_out, n_sc, k = len(in_specs), len(out_specs), len(scratch_shapes), len(arrs)

    def hosting(*refs):
        ins, cin = refs[:n_in], refs[n_in:n_in + k]
        outs, cout = refs[n_in + k:n_in + k + n_out], refs[n_in + k + n_out:n_in + 2 * k + n_out]
        scratch = refs[n_in + 2 * k + n_out:n_in + 2 * k + n_out + n_sc]
        sems = refs[n_in + 2 * k + n_out + n_sc:]
        step = pl.program_id(0)

        @pl.when(step == 0)
        def _():
            start(cin, cout, *sems)

        body(*ins, *outs, *scratch)

        @pl.when(step == n_steps - 1)
        def _():
            finish(cin, cout, *sems)

    res = pl.pallas_call(
        hosting, name=name + "_hosting", grid=(n_steps,), out_shape=list(out_shape) + c_shapes,
        in_specs=list(in_specs) + [HBM_FULL] * k, out_specs=list(out_specs) + [HBM_FULL] * k,
        scratch_shapes=list(scratch_shapes) + c_scratch,
        compiler_params=pltpu.CompilerParams(dimension_semantics=("arbitrary",), vmem_limit_bytes=VMEM_LIMIT,
                                             has_side_effects=True))(*args, *arrs)
    return res[:n_out], res[n_out:]


def _sibling_merge(arrs, name):
    n = len(arrs)

    def body(*refs):
        ins, outs = refs[:n], refs[n:2 * n]
        send_sems, recv_sems, local_sems = refs[2 * n:]
        x, y, c = lax.axis_index("x"), lax.axis_index("y"), lax.axis_index("c")
        own = [pltpu.make_async_copy(ins[j], outs[j].at[pl.ds(0, 4), c], local_sems.at[j]) for j in range(n)]
        for cp in own:
            cp.start()
        waits = []
        for j in range(n):
            pltpu.make_async_remote_copy(
                src_ref=ins[j], dst_ref=outs[j].at[pl.ds(0, 4), c], send_sem=send_sems.at[j], recv_sem=recv_sems.at[j],
                device_id=(x, y, 1 - c), device_id_type=MESH).start()
            waits.append(pltpu.make_async_remote_copy(
                src_ref=ins[j], dst_ref=outs[j].at[pl.ds(0, 4), 1 - c], send_sem=send_sems.at[j],
                recv_sem=recv_sems.at[j], device_id=(x, y, 1 - c), device_id_type=MESH))
        for cp in waits:
            cp.wait_send()
            cp.wait_recv()
        for cp in own:
            cp.wait()

    return pl.pallas_call(
        body, name=name, out_shape=[jax.ShapeDtypeStruct((4, 2) + tuple(a.shape[1:]), a.dtype) for a in arrs],
        in_specs=[HBM_FULL] * n, out_specs=[HBM_FULL] * n,
        scratch_shapes=[pltpu.SemaphoreType.DMA((n,)), pltpu.SemaphoreType.DMA((n,)), pltpu.SemaphoreType.DMA((n,))],
        compiler_params=pltpu.CompilerParams(has_side_effects=True),
    )(*arrs)


def _pair_add(core, a, b):
    _, R, C = a.shape
    tr = _row_tile(R)

    def body(core_ref, a_ref, b_ref, o_ref):
        o_ref[...] = (a_ref[...].astype(F32) + b_ref[...].astype(F32)).astype(BF)

    blk = _rows((tr, C), lambda i, core: (i, 0))
    grid_spec = pltpu.PrefetchScalarGridSpec(
        num_scalar_prefetch=1, grid=(R // tr,),
        in_specs=[_rows((None, tr, C), lambda i, core: (core[0], i, 0)), blk], out_specs=blk)
    return pl.pallas_call(
        body, name="pair_add", grid_spec=grid_spec, out_shape=jax.ShapeDtypeStruct((R, C), BF),
        compiler_params=_params(1),
    )(core, a, b)


def _mod_fwd(c_all, w0, w1, wkv):
    n0, n1, n2 = w0.shape[1], w1.shape[1], wkv.shape[1]

    def body(c_ref, w0_ref, w1_ref, w2_ref, o_ref):
        cc = c_ref[...]
        s = cc * _sigmoid(cc)
        o_ref[:, 0:n0] = _dot(s, w0_ref[...])
        o_ref[:, n0:n0 + n1] = _dot(s, w1_ref[...])
        o_ref[:, n0 + n1:n0 + n1 + n2] = _dot(s, w2_ref[...])

    return pl.pallas_call(
        body, name="mod_fwd", out_shape=jax.ShapeDtypeStruct((N_DEV, n0 + n1 + n2), F32),
        in_specs=[VMEM_FULL] * 4, out_specs=VMEM_FULL,
        compiler_params=pltpu.CompilerParams(vmem_limit_bytes=VMEM_LIMIT),
    )(c_all, w0, w1, wkv)


def _mod_wgrad(c_t, dm):
    C = dm.shape[1]
    tr = 256

    def body(ct_ref, dm_ref, o_ref):
        ct = ct_ref[...]
        s = ct * _sigmoid(ct)
        dmv = dm_ref[...]
        lane = lax.broadcasted_iota(jnp.int32, (tr, N_DEV), 1)
        acc = jnp.zeros((tr, C), F32)
        for r in range(N_DEV):
            col = jnp.sum(jnp.where(lane == r, s, 0.0), axis=1, keepdims=True)
            acc = acc + col * dmv[r:r + 1, :]
        o_ref[...] = acc

    return pl.pallas_call(
        body, name="mod_wgrad", grid=(D // tr,), out_shape=jax.ShapeDtypeStruct((D, C), F32),
        in_specs=[_rows((tr, N_DEV), lambda i: (i, 0)), _rows((N_DEV, C), lambda i: (0, 0))],
        out_specs=_rows((tr, C), lambda i: (i, 0)), compiler_params=_params(1),
    )(c_t, dm)


def _ffn_fwd(h, vec, w13, w2, comm=None):
    S = h.shape[0]

    def body(h_ref, vec_ref, w13_ref, w2_ref, ho_ref, u_ref, a_ref, b_ref):
        hv, vec = h_ref[...], vec_ref[...]
        u, _, _ = _norm_mod(hv, vec)
        ub = u.astype(BF)
        u_ref[...] = ub
        y = jnp.zeros((TS, D), F32)
        for k in range(DFF // CH):
            c0 = k * CH
            a = _dot(ub, w13_ref[:, c0:c0 + CH])
            b = _dot(ub, w13_ref[:, DFF + c0:DFF + c0 + CH])
            a_ref[:, c0:c0 + CH] = a.astype(BF)
            b_ref[:, c0:c0 + CH] = b.astype(BF)
            t = (a * _sigmoid(a)) * b
            y = y + _dot(t.astype(BF), w2_ref[c0:c0 + CH, :])
        ho_ref[...] = hv + (0.5 * vec[3:4]) * y

    tok = lambda i: (i, 0)
    return _hosted_call(
        body, (h, vec, w13, w2), comm, name="ffn_fwd", n_steps=S // TS,
        out_shape=[jax.ShapeDtypeStruct((S, D), F32), jax.ShapeDtypeStruct((S, D), BF),
                   jax.ShapeDtypeStruct((S, DFF), BF), jax.ShapeDtypeStruct((S, DFF), BF)],
        in_specs=[_rows((TS, D), tok), _rows((8, D), lambda i: (0, 0)), VMEM_FULL, VMEM_FULL],
        out_specs=[_rows((TS, D), tok), _rows((TS, D), tok), _rows((TS, DFF), tok), _rows((TS, DFF), tok)])


def _ffn_bwd(dho, h, a, b, vec, w13, w2, comm=None):
    S = h.shape[0]

    def body(dho_ref, h_ref, a_ref, b_ref, vec_ref, w13_ref, w2_ref,
             dh_ref, da_ref, db_ref, t_ref, dhb_ref, ps_ref):
        dho_v, vec = dho_ref[...], vec_ref[...]
        dhb_ref[...] = dho_v.astype(BF)
        dyb = ((0.5 * vec[3:4]) * dho_v).astype(BF)
        du = jnp.zeros((TS, D), F32)
        for k in range(DFF // CH):
            c0 = k * CH
            av = a_ref[:, c0:c0 + CH].astype(F32)
            bv = b_ref[:, c0:c0 + CH].astype(F32)
            dt = _dot_nt(dyb, w2_ref[c0:c0 + CH, :])
            sig = _sigmoid(av)
            sl = av * sig
            t_ref[:, c0:c0 + CH] = (sl * bv).astype(BF)
            dab = (dt * bv * (sig * (1.0 + av * (1.0 - sig)))).astype(BF)
            dbb = (dt * sl).astype(BF)
            da_ref[:, c0:c0 + CH] = dab
            db_ref[:, c0:c0 + CH] = dbb
            du = du + _dot_nt(dab, w13_ref[:, c0:c0 + CH]) + _dot_nt(dbb, w13_ref[:, DFF + c0:DFF + c0 + CH])
        _, xhat, r = _norm_mod(h_ref[...], vec)
        dhn, dg, dsh, dsc = _norm_mod_bwd(du, xhat, r, vec)
        dh_ref[...] = dho_v + dhn
        _accumulate(ps_ref, pl.program_id(0) == 0, [dg, dsh, dsc])

    tok = lambda i: (i, 0)
    return _hosted_call(
        body, (dho, h, a, b, vec, w13, w2), comm, name="ffn_bwd", n_steps=S // TS,
        out_shape=[jax.ShapeDtypeStruct((S, D), F32), jax.ShapeDtypeStruct((S, DFF), BF),
                   jax.ShapeDtypeStruct((S, DFF), BF), jax.ShapeDtypeStruct((S, DFF), BF),
                   jax.ShapeDtypeStruct((S, D), BF), jax.ShapeDtypeStruct((8, D), F32)],
        in_specs=[_rows((TS, D), tok), _rows((TS, D), tok), _rows((TS, DFF), tok), _rows((TS, DFF), tok),
                  _rows((8, D), lambda i: (0, 0)), VMEM_FULL, VMEM_FULL],
        out_specs=[_rows((TS, D), tok), _rows((TS, DFF), tok), _rows((TS, DFF), tok), _rows((TS, DFF), tok),
                   _rows((TS, D), tok), _rows((8, D), lambda i: (0, 0))])


def _wgrad(a, b, tm, tn, name, gate=None):
    S, M = a.shape
    N = b.shape[1]
    n_s = S // TW

    def body(*refs):
        if gate is None:
            a_ref, b_ref, o_ref, acc_ref = refs
        else:
            a_ref, b_ref, w_ref, sc_ref, o_ref, gs_ref, acc_ref = refs
        s = pl.program_id(2)

        @pl.when(s == 0)
        def _():
            acc_ref[...] = jnp.zeros((tm, tn), F32)

        acc_ref[...] += _dot_tn(a_ref[...], b_ref[...])

        @pl.when(s == n_s - 1)
        def _():
            acc = acc_ref[...]
            if gate is None:
                o_ref[...] = acc.astype(BF)
            else:
                o_ref[...] = (acc * sc_ref[0:1, :]).astype(BF)
                gs_ref[...] = jnp.broadcast_to(_sum0(acc * w_ref[...].astype(F32)), (8, tn))

    in_specs = [_rows((TW, tm), lambda m, n, s: (s, m)), _rows((TW, tn), lambda m, n, s: (s, n))]
    out_shape = [jax.ShapeDtypeStruct((M, N), BF)]
    out_specs = [_rows((tm, tn), lambda m, n, s: (m, n))]
    args = [a, b]
    if gate is not None:
        in_specs += [_rows((tm, tn), lambda m, n, s: (m, n)), _rows((8, tn), lambda m, n, s: (0, n))]
        out_shape.append(jax.ShapeDtypeStruct((M // tm, 8, N), F32))
        out_specs.append(_rows((None, 8, tn), lambda m, n, s: (m, 0, n)))
        args += list(gate)
    res = pl.pallas_call(
        body, name=name, grid=(M // tm, N // tn, n_s), out_shape=out_shape, in_specs=in_specs,
        out_specs=out_specs, scratch_shapes=[pltpu.VMEM((tm, tn), F32)], compiler_params=_params(3),
    )(*args)
    return res[0] if gate is None else (res[0], res[1])


def _conv_in_fwd(h, vec, w1, b1):
    S = h.shape[0]

    def body(h_ref, vec_ref, w_ref, b1_ref, hn_ref, pre_ref):
        u, _, _ = _norm_mod(h_ref[...], vec_ref[...])
        ub = u.astype(BF)
        hn_ref[...] = ub
        pre_ref[...] = _dot(ub, w_ref[...]) + b1_ref[0:1, :]

    tok = lambda i: (i, 0)
    return pl.pallas_call(
        body, name="conv_in_fwd", grid=(S // TS,),
        out_shape=[jax.ShapeDtypeStruct((S, D), BF), jax.ShapeDtypeStruct((S, 2 * D), F32)],
        in_specs=[_rows((TS, D), tok), _rows((8, D), lambda i: (0, 0)), VMEM_FULL, _rows((8, 2 * D), lambda i: (0, 0))],
        out_specs=[_rows((TS, D), tok), _rows((TS, 2 * D), tok)], compiler_params=_params(1),
    )(h, vec, w1, b1)


def _glu(pre):
    return pre[:, :D] * _sigmoid(pre[:, D:])


def _tap_groups(offset):
    groups = {}
    for j in range(CONV_W):
        off = offset(j)
        groups.setdefault(off % 8, []).append((off - off % 8, j))
    return [(phase, sorted(taps)) for phase, taps in sorted(groups.items())]


def _shift_window(dst_ref, win_ref, phase, rows):
    dst_ref[0:rows, :] = win_ref[phase:phase + rows, :]


def _layernorm(u2):
    mu = _mean1(u2)
    xc = u2 - mu
    rstd = lax.rsqrt(_mean1(xc * xc) + EPS)
    return xc * rstd, rstd


def _conv_out_fwd(pre, h, cw, w2, comm=None):
    S = h.shape[0]
    hb = TS // HALO

    def body(pre_ref, ph_ref, h_ref, cw_ref, w2_ref, u2_ref, z_ref, ho_ref, win_ref, sh_ref):
        i = pl.program_id(0)
        cwv = cw_ref[...]
        win_ref[0:HALO, :] = jnp.where(i > 0, _glu(ph_ref[...]), 0.0)
        win_ref[HALO:HALO + TS, :] = _glu(pre_ref[...])
        u2 = jnp.broadcast_to(cwv[31:32], (TS, D))
        for phase, taps in _tap_groups(lambda j: HALO - (CONV_W - 1) + j):
            _shift_window(sh_ref, win_ref, phase, taps[-1][0] + TS)
            for lo, j in taps:
                u2 = u2 + cwv[j:j + 1] * sh_ref[lo:lo + TS, :]
        u2_ref[...] = u2
        xh, _ = _layernorm(u2)
        un = xh * cwv[32:33] + cwv[33:34]
        zb = (un * _sigmoid(un)).astype(BF)
        z_ref[...] = zb
        y = _dot(zb, w2_ref[...]) + cwv[34:35]
        ho_ref[...] = h_ref[...] + cwv[35:36] * y

    tok = lambda i: (i, 0)
    return _hosted_call(
        body, (pre, pre, h, cw, w2), comm, name="conv_out_fwd", n_steps=S // TS,
        out_shape=[jax.ShapeDtypeStruct((S, D), F32), jax.ShapeDtypeStruct((S, D), BF), jax.ShapeDtypeStruct((S, D), F32)],
        in_specs=[_rows((TS, 2 * D), tok), _rows((HALO, 2 * D), lambda i: (jnp.maximum(i * hb - 1, 0), 0)),
                  _rows((TS, D), tok), _rows((40, D), lambda i: (0, 0)), VMEM_FULL],
        out_specs=[_rows((TS, D), tok), _rows((TS, D), tok), _rows((TS, D), tok)],
        scratch_shapes=[pltpu.VMEM((TS + HALO, D), F32)] * 2)


def _conv_out_bwd(dho, u2, cw, w2):
    S = dho.shape[0]

    def body(dho_ref, u2_ref, cw_ref, w2_ref, du2_ref, dhb_ref, ps_ref):
        dho_v, cwv = dho_ref[...], cw_ref[...]
        dhb_ref[...] = dho_v.astype(BF)
        dz = _dot_nt((cwv[35:36] * dho_v).astype(BF), w2_ref[...])
        xh, rstd = _layernorm(u2_ref[...])
        un = xh * cwv[32:33] + cwv[33:34]
        sig = _sigmoid(un)
        dun = dz * (sig * (1.0 + un * (1.0 - sig)))
        dxh = dun * cwv[32:33]
        du2_ref[...] = rstd * (dxh - _mean1(dxh) - xh * _mean1(dxh * xh))
        _accumulate(ps_ref, pl.program_id(0) == 0, [_sum0(dun * xh), _sum0(dun), _sum0(dho_v)])

    tok = lambda i: (i, 0)
    return pl.pallas_call(
        body, name="conv_out_bwd", grid=(S // TS,),
        out_shape=[jax.ShapeDtypeStruct((S, D), F32), jax.ShapeDtypeStruct((S, D), BF), jax.ShapeDtypeStruct((8, D), F32)],
        in_specs=[_rows((TS, D), tok), _rows((TS, D), tok), _rows((40, D), lambda i: (0, 0)), VMEM_FULL],
        out_specs=[_rows((TS, D), tok), _rows((TS, D), tok), _rows((8, D), lambda i: (0, 0))],
        compiler_params=_params(1),
    )(dho, u2, cw, w2)


def _conv_in_bwd(du2, pre, h, dho, vec, cw, w1):
    S = h.shape[0]
    n_t = S // TS
    hb = TS // HALO

    def body(du2_ref, dh2h_ref, pre_ref, ph_ref, h_ref, dho_ref, vec_ref, cw_ref, w1_ref,
             dh_ref, dpre_ref, ps_ref, pw_ref, pb_ref, winu_ref, wind_ref, sh_ref):
        i = pl.program_id(0)
        vec, cwv = vec_ref[...], cw_ref[...]
        pre = pre_ref[...]
        av, sg = pre[:, :D], _sigmoid(pre[:, D:])
        winu_ref[0:HALO, :] = jnp.where(i > 0, _glu(ph_ref[...]), 0.0)
        winu_ref[HALO:HALO + TS, :] = av * sg
        du2v = du2_ref[...]
        wind_ref[0:TS, :] = du2v
        wind_ref[TS:TS + HALO, :] = jnp.where(i < n_t - 1, dh2h_ref[...], 0.0)

        @pl.when(i == 0)
        def _():
            pw_ref[...] = jnp.zeros((32, D), F32)

        for phase, taps in _tap_groups(lambda j: HALO - (CONV_W - 1) + j):
            _shift_window(sh_ref, winu_ref, phase, taps[-1][0] + TS)
            for lo, j in taps:
                pw_ref[j:j + 1, :] += _sum0(du2v * sh_ref[lo:lo + TS, :])
        pw_ref[31:32, :] += _sum0(du2v)
        du1 = jnp.zeros((TS, D), F32)
        for phase, taps in _tap_groups(lambda j: CONV_W - 1 - j):
            _shift_window(sh_ref, wind_ref, phase, taps[-1][0] + TS)
            for lo, j in taps:
                du1 = du1 + cwv[j:j + 1] * sh_ref[lo:lo + TS, :]
        da = du1 * sg
        dg = du1 * av * sg * (1.0 - sg)
        dab, dgb = da.astype(BF), dg.astype(BF)
        dpre_ref[:, :D] = dab
        dpre_ref[:, D:] = dgb

        @pl.when(i == 0)
        def _():
            pb_ref[...] = jnp.zeros((8, 2 * D), F32)

        pb_ref[0:1, :D] += _sum0(da)
        pb_ref[0:1, D:] += _sum0(dg)
        du = _dot_nt(dab, w1_ref[:, :D]) + _dot_nt(dgb, w1_ref[:, D:])
        _, xhat, r = _norm_mod(h_ref[...], vec)
        dhn, dgn, dsh, dsc = _norm_mod_bwd(du, xhat, r, vec)
        dh_ref[...] = dho_ref[...] + dhn
        _accumulate(ps_ref, i == 0, [dgn, dsh, dsc])

    tok = lambda i: (i, 0)
    fixed = lambda i: (0, 0)
    return pl.pallas_call(
        body, name="conv_in_bwd", grid=(n_t,),
        out_shape=[jax.ShapeDtypeStruct((S, D), F32), jax.ShapeDtypeStruct((S, 2 * D), BF),
                   jax.ShapeDtypeStruct((8, D), F32), jax.ShapeDtypeStruct((32, D), F32),
                   jax.ShapeDtypeStruct((8, 2 * D), F32)],
        in_specs=[_rows((TS, D), tok), _rows((HALO, D), lambda i: (jnp.minimum((i + 1) * hb, S // HALO - 1), 0)),
                  _rows((TS, 2 * D), tok), _rows((HALO, 2 * D), lambda i: (jnp.maximum(i * hb - 1, 0), 0)),
                  _rows((TS, D), tok), _rows((TS, D), tok), _rows((8, D), fixed), _rows((40, D), fixed), VMEM_FULL],
        out_specs=[_rows((TS, D), tok), _rows((TS, 2 * D), tok), _rows((8, D), fixed), _rows((32, D), fixed),
                   _rows((8, 2 * D), fixed)],
        scratch_shapes=[pltpu.VMEM((TS + HALO, D), F32)] * 3,
        compiler_params=_params(1),
    )(du2, du2, pre, pre, h, dho, vec, cw, w1)


def _lane():
    return lax.broadcasted_iota(jnp.int32, (TS, HP), 1)


def _kv_fwd(h, vec, wkva, g2, wkvb, rope):
    S = h.shape[0]

    def body(h_ref, vec_ref, wa_ref, g2_ref, wb_ref, rope_ref, hn_ref, ckv_ref, ckn_ref, k_ref, v_ref):
        u, _, _ = _norm_mod(h_ref[...], vec_ref[...])
        ub = u.astype(BF)
        hn_ref[...] = ub
        kva = _dot(ub, wa_ref[...])
        ckv = kva[:, :KV_LORA]
        ckv_ref[...] = ckv
        r2 = lax.rsqrt(_mean1(ckv * ckv) + EPS)
        cknb = ((ckv * r2) * g2_ref[0:1, :]).astype(BF)
        ckn_ref[...] = cknb
        kvb = _dot(cknb, wb_ref[...])
        kpe = _rope(kva[:, KV_LORA:KVA_P], rope_ref[...])
        lane = _lane()
        for hd in range(NH):
            blk = kvb[:, hd * HP:(hd + 1) * HP]
            k_ref[:, hd * HP:(hd + 1) * HP] = jnp.where(lane < 64, blk, kpe).astype(BF)
            v_ref[:, hd * HP:(hd + 1) * HP] = jnp.where(lane >= 64, blk, 0.0).astype(BF)

    tok = lambda i: (i, 0)
    fixed = lambda i: (0, 0)
    return pl.pallas_call(
        body, name="kv_fwd", grid=(S // TS,),
        out_shape=[jax.ShapeDtypeStruct((S, D), BF), jax.ShapeDtypeStruct((S, KV_LORA), F32),
                   jax.ShapeDtypeStruct((S, KV_LORA), BF), jax.ShapeDtypeStruct((S, NH * HP), BF),
                   jax.ShapeDtypeStruct((S, NH * HP), BF)],
        in_specs=[_rows((TS, D), tok), _rows((8, D), fixed), VMEM_FULL, _rows((8, KV_LORA), fixed), VMEM_FULL,
                  _rows((TS, 3 * HP), tok)],
        out_specs=[_rows((TS, D), tok), _rows((TS, KV_LORA), tok), _rows((TS, KV_LORA), tok),
                   _rows((TS, NH * HP), tok), _rows((TS, NH * HP), tok)],
        compiler_params=_params(1),
    )(h, vec, wkva, g2, wkvb, rope)


def _kv_bwd(dk, dv, h, ckv, dho, vec, wkva, g2, wkvb, rope):
    S = h.shape[0]

    def body(dk_ref, dv_ref, h_ref, ckv_ref, dho_ref, vec_ref, wa_ref, g2_ref, wb_ref, rope_ref,
             dh_ref, dkva_ref, dkvb_ref, ps_ref, ps2_ref):
        i = pl.program_id(0)
        vec = vec_ref[...]
        lane = _lane()
        dkpe = jnp.zeros((TS, HP), F32)
        for hd in range(NH):
            dkh = dk_ref[:, hd * HP:(hd + 1) * HP]
            dvh = dv_ref[:, hd * HP:(hd + 1) * HP]
            dkvb_ref[:, hd * HP:(hd + 1) * HP] = jnp.where(lane < 64, dkh, dvh).astype(BF)
            dkpe = dkpe + jnp.where(lane >= 64, dkh, 0.0)
        dkpe = _rope_t(dkpe, rope_ref[...])
        dckn = _dot_nt(dkvb_ref[...], wb_ref[...])
        ckv = ckv_ref[...]
        r2 = lax.rsqrt(_mean1(ckv * ckv) + EPS)
        dckv, dg2 = _rms_bwd(dckn, ckv, r2, g2_ref[0:1, :])
        dkva_ref[:, :KV_LORA] = dckv.astype(BF)
        dkva_ref[:, KV_LORA:KVA_P] = dkpe.astype(BF)
        du = _dot_nt(dkva_ref[...], wa_ref[...])
        _, xhat, r = _norm_mod(h_ref[...], vec)
        dhn, dgn, dsh, dsc = _norm_mod_bwd(du, xhat, r, vec)
        dh_ref[...] = dho_ref[...] + dhn
        _accumulate(ps_ref, i == 0, [dgn, dsh, dsc])
        _accumulate(ps2_ref, i == 0, [dg2])

    tok = lambda i: (i, 0)
    fixed = lambda i: (0, 0)
    return pl.pallas_call(
        body, name="kv_bwd", grid=(S // TS,),
        out_shape=[jax.ShapeDtypeStruct((S, D), F32), jax.ShapeDtypeStruct((S, KVA_P), BF),
                   jax.ShapeDtypeStruct((S, NH * HP), BF), jax.ShapeDtypeStruct((8, D), F32),
                   jax.ShapeDtypeStruct((8, KV_LORA), F32)],
        in_specs=[_rows((TS, NH * HP), tok), _rows((TS, NH * HP), tok), _rows((TS, D), tok), _rows((TS, KV_LORA), tok),
                  _rows((TS, D), tok), _rows((8, D), fixed), VMEM_FULL, _rows((8, KV_LORA), fixed), VMEM_FULL,
                  _rows((TS, 3 * HP), tok)],
        out_specs=[_rows((TS, D), tok), _rows((TS, KVA_P), tok), _rows((TS, NH * HP), tok), _rows((8, D), fixed),
                   _rows((8, KV_LORA), fixed)],
        compiler_params=_params(1),
    )(dk, dv, h, ckv, dho, vec, wkva, g2, wkvb, rope)


def _q_fwd(h, vec, wqa, g2, wqb, rope):
    S = h.shape[0]

    def body(h_ref, vec_ref, wa_ref, g2_ref, wb_ref, rope_ref, hn_ref, qa_ref, qan_ref, q_ref):
        u, _, _ = _norm_mod(h_ref[...], vec_ref[...])
        ub = u.astype(BF)
        hn_ref[...] = ub
        qa = _dot(ub, wa_ref[...])
        qa_ref[...] = qa
        r2 = lax.rsqrt(_mean1(qa * qa) + EPS)
        qanb = ((qa * r2) * g2_ref[0:1, :]).astype(BF)
        qan_ref[...] = qanb
        q = _dot(qanb, wb_ref[...])
        tab = rope_ref[...]
        for hd in range(NH):
            q_ref[:, hd * HP:(hd + 1) * HP] = _rope(q[:, hd * HP:(hd + 1) * HP], tab).astype(BF)

    tok = lambda i: (i, 0)
    fixed = lambda i: (0, 0)
    return pl.pallas_call(
        body, name="q_fwd", grid=(S // TS,),
        out_shape=[jax.ShapeDtypeStruct((S, D), BF), jax.ShapeDtypeStruct((S, Q_LORA), F32),
                   jax.ShapeDtypeStruct((S, Q_LORA), BF), jax.ShapeDtypeStruct((S, NH * HP), BF)],
        in_specs=[_rows((TS, D), tok), _rows((8, D), fixed), VMEM_FULL, _rows((8, Q_LORA), fixed), VMEM_FULL,
                  _rows((TS, 3 * HP), tok)],
        out_specs=[_rows((TS, D), tok), _rows((TS, Q_LORA), tok), _rows((TS, Q_LORA), tok), _rows((TS, NH * HP), tok)],
        compiler_params=_params(1),
    )(h, vec, wqa, g2, wqb, rope)


def _q_bwd(dq, h, qa, dho, vec, wqa, g2, wqb, rope):
    S = h.shape[0]

    def body(dq_ref, h_ref, qa_ref, dho_ref, vec_ref, wa_ref, g2_ref, wb_ref, rope_ref,
             dh_ref, dqb_ref, dqa_ref, ps_ref, ps2_ref):
        i = pl.program_id(0)
        vec, tab = vec_ref[...], rope_ref[...]
        for hd in range(NH):
            dqb_ref[:, hd * HP:(hd + 1) * HP] = _rope_t(dq_ref[:, hd * HP:(hd + 1) * HP], tab).astype(BF)
        dqan = _dot_nt(dqb_ref[...], wb_ref[...])
        qa = qa_ref[...]
        r2 = lax.rsqrt(_mean1(qa * qa) + EPS)
        dqa, dg2 = _rms_bwd(dqan, qa, r2, g2_ref[0:1, :])
        dqab = dqa.astype(BF)
        dqa_ref[...] = dqab
        du = _dot_nt(dqab, wa_ref[...])
        _, xhat, r = _norm_mod(h_ref[...], vec)
        dhn, dgn, dsh, dsc = _norm_mod_bwd(du, xhat, r, vec)
        dh_ref[...] = dho_ref[...] + dhn
        _accumulate(ps_ref, i == 0, [dgn, dsh, dsc])
        _accumulate(ps2_ref, i == 0, [dg2])

    tok = lambda i: (i, 0)
    fixed = lambda i: (0, 0)
    return pl.pallas_call(
        body, name="q_bwd", grid=(S // TS,),
        out_shape=[jax.ShapeDtypeStruct((S, D), F32), jax.ShapeDtypeStruct((S, NH * HP), BF),
                   jax.ShapeDtypeStruct((S, Q_LORA), BF), jax.ShapeDtypeStruct((8, D), F32),
                   jax.ShapeDtypeStruct((8, Q_LORA), F32)],
        in_specs=[_rows((TS, NH * HP), tok), _rows((TS, D), tok), _rows((TS, Q_LORA), tok), _rows((TS, D), tok),
                  _rows((8, D), fixed), VMEM_FULL, _rows((8, Q_LORA), fixed), VMEM_FULL, _rows((TS, 3 * HP), tok)],
        out_specs=[_rows((TS, D), tok), _rows((TS, NH * HP), tok), _rows((TS, Q_LORA), tok), _rows((8, D), fixed),
                   _rows((8, Q_LORA), fixed)],
        compiler_params=_params(1),
    )(dq, h, qa, dho, vec, wqa, g2, wqb, rope)


def _attn_fwd(q, k, v):
    S = q.shape[0]
    TA = TA_FWD
    nq = S // TA
    rg = min(TA, ATT_ROWS)
    groups = TA // rg

    def softmax_pv(scores, vt, state, masks):
        out = []
        for g in range(groups):
            m, l, acc = state[g]
            s = scores[g] if masks is None else jnp.where(masks[g], scores[g], NEG)
            m_new = jnp.maximum(m, jnp.max(s, axis=1, keepdims=True))
            p = jnp.exp2((s - m_new) * EXP2_SCALE)
            alpha = jnp.exp2((m - m_new) * EXP2_SCALE)
            vg = vt[g] if isinstance(vt, list) else vt
            out.append((m_new, alpha * l + jnp.sum(p, axis=1, keepdims=True), alpha * acc + _dot(p.astype(BF), vg)))
        return tuple(out)

    def body(q_ref, k_ref, v_ref, o_ref, lse_ref):
        qi = pl.program_id(1)
        qs = [q_ref[g * rg:(g + 1) * rg, :] for g in range(groups)]

        def keys(j):
            return pl.ds(pl.multiple_of(j * TA, TA), TA)

        def scores_of(j):
            kt = k_ref[keys(j), :]
            return tuple(_dot_nt(qs[g], kt) for g in range(groups))

        state = tuple((jnp.full((rg, 1), NEG, F32), jnp.zeros((rg, 1), F32), jnp.zeros((rg, HP), F32))
                      for _ in range(groups))

        def step(kj, state):
            return softmax_pv(scores_of(kj), v_ref[keys(kj), :], state, None)

        state = lax.fori_loop(0, qi, step, state)
        kd, vd = k_ref[keys(qi), :], v_ref[keys(qi), :]
        ends = [(g + 1) * rg for g in range(groups)]
        diag_scores = tuple(_dot_nt(qs[g], kd[:ends[g]]) for g in range(groups))
        masks = [lax.broadcasted_iota(jnp.int32, (rg, ends[g]), 1)
                 <= lax.broadcasted_iota(jnp.int32, (rg, ends[g]), 0) + g * rg for g in range(groups)]
        final = softmax_pv(diag_scores, [vd[:ends[g]] for g in range(groups)], state, masks)
        for g in range(groups):
            m, l, acc = final[g]
            o_ref[g * rg:(g + 1) * rg, :] = (acc / l).astype(BF)
            lse_ref[g * rg:(g + 1) * rg, :] = m * EXP2_SCALE + jnp.log(l) * LOG2E

    return pl.pallas_call(
        body, name="attn_fwd", grid=(NH, nq),
        out_shape=[jax.ShapeDtypeStruct((S, NH * HP), BF), jax.ShapeDtypeStruct((NH, S, 1), F32)],
        in_specs=[_rows((TA, HP), lambda h, i: (i, h)), _rows((S, HP), lambda h, i: (0, h)),
                  _rows((S, HP), lambda h, i: (0, h))],
        out_specs=[_rows((TA, HP), lambda h, i: (i, h)), _rows((None, TA, 1), lambda h, i: (h, i, 0))],
        compiler_params=_params(2),
    )(q, k, v)


def _attn_bwd(q, k, v, do, lse, delta):
    S = q.shape[0]
    nq = S // TA
    rg = min(TA, ATT_ROWS)
    groups = TA // rg

    def body(q_ref, do_ref, lse_ref, dl_ref, k_ref, v_ref, dq_ref, dk_ref, dv_ref, dka_ref, dva_ref):
        kj = pl.program_id(1)

        @pl.when(kj == 0)
        def _():
            dq_ref[...] = jnp.zeros((S, HP), F32)

        dka_ref[...] = jnp.zeros((TA, HP), F32)
        dva_ref[...] = jnp.zeros((TA, HP), F32)
        kt, vt = k_ref[...], v_ref[...]

        def tile(qi, diagonal):
            rows = [pl.ds(pl.multiple_of(qi * TA + g * rg, rg), rg) for g in range(groups)]
            ends = [(g + 1) * rg if diagonal else TA for g in range(groups)]
            qg = [q_ref[r, :] for r in rows]
            dog = [do_ref[r, :] for r in rows]
            scores = [_dot_nt(qg[g], kt[:ends[g]]) for g in range(groups)]
            dps = [_dot_nt(dog[g], vt[:ends[g]]) for g in range(groups)]
            for g in range(groups):
                p = jnp.exp2(scores[g] * EXP2_SCALE - lse_ref[rows[g], :])
                if diagonal:
                    col = lax.broadcasted_iota(jnp.int32, (rg, ends[g]), 1)
                    row = lax.broadcasted_iota(jnp.int32, (rg, ends[g]), 0)
                    p = jnp.where(col <= row + g * rg, p, 0.0)
                ds = p * (dps[g] - dl_ref[rows[g], :])
                pb, dsb = p.astype(BF), ds.astype(BF)
                dva_ref[0:ends[g], :] += _dot_tn(pb, dog[g])
                dka_ref[0:ends[g], :] += _dot_tn(dsb, qg[g])
                dq_ref[rows[g], :] += _dot(dsb, kt[:ends[g]]) * SM_SCALE

        tile(kj, True)

        def step(qi, carry):
            tile(qi, False)
            return carry

        lax.fori_loop(kj + 1, nq, step, 0)
        dk_ref[...] = dka_ref[...] * SM_SCALE
        dv_ref[...] = dva_ref[...]

    head = lambda h, j: (0, h)
    col1 = lambda h, j: (h, 0, 0)
    return pl.pallas_call(
        body, name="attn_bwd", grid=(NH, nq), out_shape=[jax.ShapeDtypeStruct((S, NH * HP), F32)] * 3,
        in_specs=[_rows((S, HP), head), _rows((S, HP), head), _rows((None, S, 1), col1), _rows((None, S, 1), col1),
                  _rows((TA, HP), lambda h, j: (j, h)), _rows((TA, HP), lambda h, j: (j, h))],
        out_specs=[_rows((S, HP), head), _rows((TA, HP), lambda h, j: (j, h)), _rows((TA, HP), lambda h, j: (j, h))],
        scratch_shapes=[pltpu.VMEM((TA, HP), F32), pltpu.VMEM((TA, HP), F32)], compiler_params=_params(2),
    )(q, do, lse, delta, k, v)


def _attn_out_fwd(o, h, wo, vec):
    S = h.shape[0]

    def body(o_ref, h_ref, wo_ref, vec_ref, ho_ref):
        ho_ref[...] = h_ref[...] + vec_ref[3:4, :] * _dot(o_ref[...], wo_ref[...])

    tok = lambda i: (i, 0)
    return pl.pallas_call(
        body, name="attn_out_fwd", grid=(S // TS,), out_shape=jax.ShapeDtypeStruct((S, D), F32),
        in_specs=[_rows((TS, NH * HP), tok), _rows((TS, D), tok), VMEM_FULL, _rows((8, D), lambda i: (0, 0))],
        out_specs=_rows((TS, D), tok), compiler_params=_params(1),
    )(o, h, wo, vec)


def _attn_out_bwd(dho, o, wo, vec):
    S = dho.shape[0]

    def body(dho_ref, o_ref, wo_ref, vec_ref, do_ref, dl_ref, dhb_ref):
        dho_v = dho_ref[...]
        dhb_ref[...] = dho_v.astype(BF)
        do = _dot_nt((vec_ref[3:4, :] * dho_v).astype(BF), wo_ref[...])
        do_ref[...] = do.astype(BF)
        prod = do * o_ref[...].astype(F32)
        for hd in range(NH):
            dl_ref[hd] = jnp.sum(prod[:, hd * HP:(hd + 1) * HP], axis=1, keepdims=True)

    tok = lambda i: (i, 0)
    return pl.pallas_call(
        body, name="attn_out_bwd", grid=(S // TS,),
        out_shape=[jax.ShapeDtypeStruct((S, NH * HP), BF), jax.ShapeDtypeStruct((NH, S, 1), F32),
                   jax.ShapeDtypeStruct((S, D), BF)],
        in_specs=[_rows((TS, D), tok), _rows((TS, NH * HP), tok), VMEM_FULL, _rows((8, D), lambda i: (0, 0))],
        out_specs=[_rows((TS, NH * HP), tok), _rows((NH, TS, 1), lambda i: (0, i, 0)), _rows((TS, D), tok)],
        compiler_params=_params(1),
    )(dho, o, wo, vec)


def _final(h, target, fg):
    S = h.shape[0]

    def body(h_ref, t_ref, g_ref, dh_ref, ps_ref):
        hv, g = h_ref[...], g_ref[0:1, :]
        r = lax.rsqrt(_mean1(hv * hv) + EPS)
        xhat = hv * r
        err = xhat * g - t_ref[...]
        loss = 0.5 * jnp.sum(_mean1(err * err), axis=0, keepdims=True)
        dy = err * (1.0 / D)
        dxhat = dy * g
        dh_ref[...] = r * (dxhat - xhat * _mean1(dxhat * xhat))
        _accumulate(ps_ref, pl.program_id(0) == 0, [_sum0(dy * xhat), jnp.broadcast_to(loss, (1, D))])

    tok = lambda i: (i, 0)
    return pl.pallas_call(
        body, name="final_loss", grid=(S // TS,),
        out_shape=[jax.ShapeDtypeStruct((S, D), F32), jax.ShapeDtypeStruct((8, D), F32)],
        in_specs=[_rows((TS, D), tok), _rows((TS, D), tok), _rows((8, D), lambda i: (0, 0))],
        out_specs=[_rows((TS, D), tok), _rows((8, D), lambda i: (0, 0))], compiler_params=_params(1),
    )(h, target, fg)


def _row_tile(r):
    if r <= 256:
        return r
    for t in range(256, 7, -8):
        if r % t == 0:
            return t
    return r


def _adamw(parts, w, m, v):
    P, R, C = parts.shape
    tr = _row_tile(R)

    def body(p_ref, w_ref, m_ref, v_ref, g_ref, d_ref, mo_ref, vo_ref):
        g = p_ref[0].astype(F32)
        for k in range(1, P):
            g = g + p_ref[k].astype(F32)
        g_ref[...] = g
        m2 = B1 * m_ref[...] + (1.0 - B1) * g
        v2 = B2 * v_ref[...] + (1.0 - B2) * (g * g)
        mo_ref[...] = m2
        vo_ref[...] = v2
        m_hat = m2 / (1.0 - B1 ** STEP)
        v_hat = v2 / (1.0 - B2 ** STEP)
        d_ref[...] = -LR * (m_hat / (jnp.sqrt(v_hat) + EPS_ADAM) + WD * w_ref[...])

    blk = _rows((tr, C), lambda i: (i, 0))
    return pl.pallas_call(
        body, name="adamw", grid=(R // tr,), out_shape=[jax.ShapeDtypeStruct((R, C), F32)] * 4,
        in_specs=[_rows((P, tr, C), lambda i: (0, i, 0)), blk, blk, blk], out_specs=[blk] * 4,
        compiler_params=_params(1),
    )(parts, w, m, v)


_WEIGHTS = ['ada_w', 'ada_b', 'norm_g', 'ffn_w13', 'ffn_w2', 'conv_w_pw1', 'conv_b_pw1', 'conv_w_dw', 'conv_b_dw',
            'conv_ln_g', 'conv_ln_b', 'conv_w_pw2', 'conv_b_pw2', 'kv_ada_w', 'kv_ada_b', 'kv_norm_g', 'w_kv_a',
            'kv_a_norm_g', 'w_kv_b', 'w_q_a', 'q_a_norm_g', 'w_q_b', 'w_o', 'final_norm_g']


def _vec(rows):
    rows = [r.reshape(1, -1).astype(F32) for r in rows]
    return jnp.concatenate(rows + [jnp.zeros((8 - len(rows), rows[0].shape[1]), F32)], axis=0)


def kernel(x, c, positions, ada_w, ada_b, norm_g, ffn_w13, ffn_w2, conv_w_pw1, conv_b_pw1, conv_w_dw, conv_b_dw, conv_ln_g, conv_ln_b, conv_w_pw2, conv_b_pw2, kv_ada_w, kv_ada_b, kv_norm_g, w_kv_a, kv_a_norm_g, w_kv_b, w_q_a, q_a_norm_g, w_q_b, w_o, final_norm_g, loss_target, m_ada_w, m_ada_b, m_norm_g, m_ffn_w13, m_ffn_w2, m_conv_w_pw1, m_conv_b_pw1, m_conv_w_dw, m_conv_b_dw, m_conv_ln_g, m_conv_ln_b, m_conv_w_pw2, m_conv_b_pw2, m_kv_ada_w, m_kv_ada_b, m_kv_norm_g, m_w_kv_a, m_kv_a_norm_g, m_w_kv_b, m_w_q_a, m_q_a_norm_g, m_w_q_b, m_w_o, m_final_norm_g, v_ada_w, v_ada_b, v_norm_g, v_ffn_w13, v_ffn_w2, v_conv_w_pw1, v_conv_b_pw1, v_conv_w_dw, v_conv_b_dw, v_conv_ln_g, v_conv_ln_b, v_conv_w_pw2, v_conv_b_pw2, v_kv_ada_w, v_kv_ada_b, v_kv_norm_g, v_w_kv_a, v_kv_a_norm_g, v_w_kv_b, v_w_q_a, v_q_a_norm_g, v_w_q_b, v_w_o, v_final_norm_g):
    given = dict(locals())
    S = x.shape[1]
    me = 4 * lax.axis_index("x") + 2 * lax.axis_index("y") + lax.axis_index("c")

    small = jnp.concatenate([
        conv_w_dw[0], conv_b_dw, conv_ln_g, conv_ln_b, conv_b_pw2,
        norm_g.reshape(6, 128), conv_b_pw1.reshape(2, 128),
        c.reshape(8, 128), jnp.zeros((5, 128), F32)], axis=0)
    bf = lambda w: w.astype(BF)
    full_w13 = lambda g: jnp.transpose(g.reshape(N_DEV, D, 704), (1, 0, 2)).reshape(D, 2 * DFF)
    full_w2 = lambda g: g.reshape(DFF, D)
    got = _gather_two_level([small, bf(ffn_w13[0, 0]), bf(ffn_w2[0, 0])], "gather_first")
    w13_00, w2_00 = full_w13(got[1]), full_w2(got[2])
    sm = got[0]
    chan = lambda lo, hi: jnp.moveaxis(sm[:, lo:hi, :], 0, 1).reshape(hi - lo, D)
    w_dw_f, b_dw_f, ln_g_f, ln_b_f, b_pw2_f = chan(0, 31), chan(31, 32), chan(32, 33), chan(33, 34), chan(34, 35)
    norm_f = chan(35, 41).reshape(2, 3, D)
    b_pw1_f = sm[:, 41:43, :].reshape(1, 2 * D)
    c_all = sm[:, 43:51, :].reshape(N_DEV, D)

    n_ada = ada_w.shape[2]
    n_kva = kv_ada_w.shape[1]
    modp = _mod_fwd(c_all, ada_w[0], ada_w[1], kv_ada_w)
    (modr,) = _exchange([(modp.reshape(N_DEV, 1, 2 * n_ada + n_kva), "scatter")], "scatter_mod")
    modr = modr[:, 0, :]
    mod = jnp.transpose(modr[:, :2 * n_ada].reshape(N_DEV, 2, n_ada), (1, 0, 2)).reshape(2, 9 * D) + ada_b
    mod = mod.reshape(2, 9, D)
    kvmod = (modr[:, 2 * n_ada:].reshape(2 * D) + kv_ada_b).reshape(2, D)

    def sub_vec(l, idx):
        return _vec([norm_f[l, idx], mod[l, 3 * idx], mod[l, 3 * idx + 1], mod[l, 3 * idx + 2]])

    vec_kv = _vec([kv_norm_g, kvmod[0], kvmod[1]])
    cw = jnp.concatenate([w_dw_f, b_dw_f, ln_g_f, ln_b_f, b_pw2_f, mod[0, 5].reshape(1, D), jnp.zeros((4, D), F32)], axis=0)
    b1v = _vec([b_pw1_f])
    g_kva = _vec([kv_a_norm_g])
    g_qa = _vec([q_a_norm_g[0]])
    fgv = _vec([final_norm_g])

    inv_freq = 10000.0 ** (-jnp.arange(0, ROPE, 2, dtype=F32) / ROPE)
    ang = positions[0].astype(F32)[:, None] * inv_freq
    cs, sn = jnp.cos(ang), jnp.sin(ang)
    z16, z32, z64 = jnp.zeros((S, 16), F32), jnp.zeros((S, 32), F32), jnp.zeros((S, 64), F32)
    rope = jnp.concatenate([jnp.ones((S, 64), F32), cs, cs, z32,
                            z64, z16, sn, z32,
                            z64, -sn, z16, z32], axis=1)

    def merged(blocks, name):
        return [m.reshape((N_DEV,) + m.shape[2:]) for m in _sibling_merge(blocks, name)]

    h0 = x[0]
    group1 = [bf(conv_w_pw1[0]), bf(conv_w_pw2[0]), bf(ffn_w13[0, 1]), bf(ffn_w2[0, 1])]
    (h1, u00, a00, b00), blocks1 = _ffn_fwd(h0, sub_vec(0, 0), w13_00, w2_00, comm=(group1, False))
    g_pw1, g_pw2, g_w13, g_w2 = merged(blocks1, "merge_group1")
    pw1_f = jnp.transpose(g_pw1, (1, 0, 2)).reshape(D, 2 * D)
    pw2_f = g_pw2.reshape(D, D)
    w13_01, w2_01 = full_w13(g_w13), full_w2(g_w2)
    hn_c, pre = _conv_in_fwd(h1, sub_vec(0, 1), pw1_f, b1v)
    group2 = [bf(w_kv_a), bf(w_kv_b), bf(ffn_w13[1, 0]), bf(ffn_w2[1, 0])]
    (u2, z_c, h2), blocks2 = _conv_out_fwd(pre, h1, cw, pw2_f, comm=(group2, False))
    g_kva_w, g_kvb_w, g_w13, g_w2 = merged(blocks2, "merge_group2")
    wkva = g_kva_w.reshape(D, KV_LORA + ROPE)
    wkva_f = jnp.concatenate([wkva[:, :KV_LORA], jnp.zeros((D, 64), BF), wkva[:, KV_LORA:], jnp.zeros((D, 32), BF)], axis=1)
    wkvb_f = jnp.transpose(g_kvb_w, (1, 0, 2)).reshape(KV_LORA, NH * HP)
    w13_10, w2_10 = full_w13(g_w13), full_w2(g_w2)
    group3 = [bf(w_q_a[0]), bf(w_q_b[0]), bf(w_o[0]), bf(ffn_w13[1, 1]), bf(ffn_w2[1, 1])]
    (h3, u01, a01, b01), blocks3 = _ffn_fwd(h2, sub_vec(0, 2), w13_01, w2_01, comm=(group3, False))
    g_qa_w, g_qb_w, g_wo, g_w13, g_w2 = merged(blocks3, "merge_group3")
    wqa_f = g_qa_w.reshape(D, Q_LORA)
    wqb = jnp.transpose(g_qb_w, (1, 0, 2)).reshape(Q_LORA, NH, 96)
    wqb_f = jnp.pad(wqb, ((0, 0), (0, 0), (0, HP - 96))).reshape(Q_LORA, NH * HP)
    wo_f = jnp.pad(g_wo.reshape(NH, 64, D), ((0, 0), (64, 0), (0, 0))).reshape(NH * HP, D)
    w13_11, w2_11 = full_w13(g_w13), full_w2(g_w2)
    hn_kv, ckv, ckn, k_all, v_all = _kv_fwd(h3, vec_kv, wkva_f, g_kva, wkvb_f, rope)
    (h4, u10, a10, b10), _ = _ffn_fwd(h3, sub_vec(1, 0), w13_10, w2_10)
    hn_q, qa, qan, q_all = _q_fwd(h4, sub_vec(1, 1), wqa_f, g_qa, wqb_f, rope)
    o_all, lse = _attn_fwd(q_all, k_all, v_all)
    h5 = _attn_out_fwd(o_all, h4, wo_f, sub_vec(1, 1))
    (h6, u11, a11, b11), _ = _ffn_fwd(h5, sub_vec(1, 2), w13_11, w2_11)

    dh6, ps_fin = _final(h6, loss_target[0], fgv)
    loss = lax.psum(ps_fin[1, 0], ("x", "y", "c"))

    core = lax.axis_index("c").reshape(1).astype(jnp.int32)

    def pair_sums(sends, name):
        by_core = [s.reshape((4, 2) + s.shape[1:]).swapaxes(0, 1) for s in sends]
        from_sibling = _pair_swap(by_core, name)
        return [_pair_add(core, a.reshape(2, -1, a.shape[-1]), b.reshape(-1, b.shape[-1])).reshape(b.shape)
                for a, b in zip(by_core, from_sibling)]

    def ffn_back(dho, h_in, u, a, b, l, i, w13, w2, comm=None):
        vec = sub_vec(l, 2 * i)
        (dh, da, db, t, dhb, ps), reduced = _ffn_bwd(dho, h_in, a, b, vec, w13, w2, comm=comm)
        dwa = _wgrad(u, da, D, CH, "wgrad_w13")
        dwb = _wgrad(u, db, D, CH, "wgrad_w13")
        dw2, gs = _wgrad(t, dhb, CH, D, "wgrad_w2", gate=(w2, _vec([0.5 * vec[3]])))
        dgate = 0.5 * jnp.sum(gs[:, 0, :], axis=0)
        send13 = jnp.transpose(jnp.concatenate([dwa, dwb], axis=1).reshape(D, N_DEV, 704), (1, 0, 2))
        return dh, send13, dw2.reshape(N_DEV, 352, D), ps, dgate, reduced

    dh5, s13_11, s2_11, ps11, dg11, _ = ffn_back(dh6, h5, u11, a11, b11, 1, 1, w13_11, w2_11)
    vec_m1 = sub_vec(1, 1)
    do_all, delta, dhb5 = _attn_out_bwd(dh5, o_all, wo_f, vec_m1)
    dwo_p, gs_o = _wgrad(o_all, dhb5, D, D, "wgrad_wo", gate=(wo_f, _vec([vec_m1[3]])))
    dgm1 = jnp.sum(gs_o[:, 0, :], axis=0)
    dq_all, dk_all, dv_all = _attn_bwd(q_all, k_all, v_all, do_all, lse, delta)
    dh4, dqb, dqab, ps_q, ps_q2 = _q_bwd(dq_all, h4, qa, dh5, vec_m1, wqa_f, g_qa, wqb_f, rope)
    dwqb_p = _wgrad(qan, dqb, Q_LORA, D, "wgrad_wqb")
    dwqa = _wgrad(hn_q, dqab, D, Q_LORA, "wgrad_wqa")
    dh3a, s13_10, s2_10, ps10, dg10, _ = ffn_back(dh4, h3, u10, a10, b10, 1, 0, w13_10, w2_10)
    sums_a = pair_sums([
        s13_11, s2_11, s13_10, s2_10, dwqa.reshape(N_DEV, 128, Q_LORA),
        jnp.transpose(dwqb_p.reshape(Q_LORA, NH, HP)[:, :, :96].reshape(Q_LORA, N_DEV, 192), (1, 0, 2)),
        dwo_p.reshape(NH, HP, D)[:, 64:, :].reshape(N_DEV, 128, D)], "pair_swap_a")
    dh3, dkva, dkvb, ps_kv, ps_kv2 = _kv_bwd(dk_all, dv_all, h3, ckv, dh3a, vec_kv, wkva_f, g_kva, wkvb_f, rope)
    dwkva_p = _wgrad(hn_kv, dkva, D, KVA_P, "wgrad_wkva")
    dwkvb = _wgrad(ckn, dkvb, KV_LORA, D, "wgrad_wkvb")
    dh2, s13_01, s2_01, ps01, dg01, red_a = ffn_back(dh3, h2, u01, a01, b01, 0, 1, w13_01, w2_01, comm=(sums_a, True))
    vec_m0 = sub_vec(0, 1)
    du2, dhb2, ps_co = _conv_out_bwd(dh2, u2, cw, pw2_f)
    dpw2, gs_c = _wgrad(z_c, dhb2, D, D, "wgrad_pw2", gate=(pw2_f, _vec([vec_m0[3]])))
    dgm0 = jnp.sum(gs_c[:, 0, :], axis=0) + b_pw2_f[0] * ps_co[2]
    dh1, dpre, ps_ci, ps_dw, ps_b1 = _conv_in_bwd(du2, pre, h1, dh2, vec_m0, cw, pw1_f)
    dpw1 = _wgrad(hn_c, dpre, D, D, "wgrad_pw1")
    sums_b = pair_sums([
        s13_01, s2_01, jnp.transpose(dpw1.reshape(D, N_DEV, 256), (1, 0, 2)), dpw2.reshape(N_DEV, 128, D),
        jnp.concatenate([dwkva_p[:, :KV_LORA], dwkva_p[:, KV_LORA + 64:KV_LORA + 96]], axis=1).reshape(N_DEV, 128, KV_LORA + ROPE),
        jnp.transpose(dwkvb.reshape(KV_LORA, N_DEV, 256), (1, 0, 2))], "pair_swap_b")
    dh0, s13_00, s2_00, ps00, dg00, red_b = ffn_back(dh1, h0, u00, a00, b00, 0, 0, w13_00, w2_00, comm=(sums_b, True))
    red_c = _chip_scatter(pair_sums([s13_00, s2_00], "pair_swap_c"), "chip_scatter_last")
    r13_11, r2_11, r13_10, r2_10, r_wqa, r_wqb, r_wo = red_a
    r13_01, r2_01, r_pw1, r_pw2, r_wkva, r_wkvb = red_b
    r13_00, r2_00 = red_c

    dmod = jnp.stack([
        jnp.stack([ps00[1], ps00[2], dg00, ps_ci[1], ps_ci[2], dgm0, ps01[1], ps01[2], dg01]),
        jnp.stack([ps10[1], ps10[2], dg10, ps_q[1], ps_q[2], dgm1, ps11[1], ps11[2], dg11])])
    dnorm = jnp.stack([ps00[0], ps_ci[0], ps01[0], ps10[0], ps_q[0], ps11[0]])
    pieces = [dnorm, ps_b1[0], ps_dw[0:31], ps_dw[31], ps_co[0], ps_co[1], vec_m0[3] * ps_co[2],
              ps_kv[0], ps_kv2[0], ps_q2[0], ps_fin[0], dmod, ps_kv[1], ps_kv[2]]
    sizes = [int(np.prod(p.shape)) for p in pieces]
    offs = np.concatenate([[0], np.cumsum(sizes)]).astype(int)
    flat = jnp.concatenate([p.reshape(-1) for p in pieces]).reshape(-1, 128)
    (part,) = _exchange([(flat, "gather")], "gather_partials")
    part = part.reshape(N_DEV, -1)

    def piece(i, rows, cols):
        return part[:, offs[i]:offs[i + 1]].reshape(N_DEV, rows, cols)

    def mine(i, rows):
        return lax.dynamic_slice_in_dim(piece(i, rows, D), me * 128, 128, axis=2)

    dmod_all = piece(11, 2, 9 * D)
    c_t = jnp.transpose(c_all)
    g_ada = jnp.stack([_mod_wgrad(c_t, lax.dynamic_slice_in_dim(dmod_all[:, l], me * n_ada, n_ada, axis=1))
                       for l in range(2)])
    dkvmod_all = jnp.concatenate([piece(12, 1, D)[:, 0], piece(13, 1, D)[:, 0]], axis=1)
    g_kvada = _mod_wgrad(c_t, lax.dynamic_slice_in_dim(dkvmod_all, me * n_kva, n_kva, axis=1))

    parts = {
        'ada_w': g_ada.reshape(1, 2 * D, n_ada),
        'ada_b': dmod_all,
        'norm_g': mine(0, 6),
        'ffn_w13': jnp.stack([r13_00, r13_01, r13_10, r13_11], axis=1).reshape(4, 4 * D, 704),
        'ffn_w2': jnp.stack([r2_00, r2_01, r2_10, r2_11], axis=1).reshape(4, 4 * 352, D),
        'conv_w_pw1': r_pw1,
        'conv_b_pw1': lax.dynamic_slice_in_dim(piece(1, 1, 2 * D), me * 256, 256, axis=2),
        'conv_w_dw': mine(2, 31),
        'conv_b_dw': mine(3, 1),
        'conv_ln_g': mine(4, 1),
        'conv_ln_b': mine(5, 1),
        'conv_w_pw2': r_pw2,
        'conv_b_pw2': mine(6, 1),
        'kv_ada_w': g_kvada.reshape(1, D, n_kva),
        'kv_ada_b': dkvmod_all.reshape(N_DEV, 1, 2 * D),
        'kv_norm_g': piece(7, 1, D),
        'w_kv_a': r_wkva,
        'kv_a_norm_g': piece(8, 1, KV_LORA),
        'w_kv_b': r_wkvb,
        'w_q_a': r_wqa,
        'q_a_norm_g': piece(9, 1, Q_LORA),
        'w_q_b': r_wqb,
        'w_o': r_wo,
        'final_norm_g': piece(10, 1, D),
    }
    grads, deltas, new_m, new_v = [], [], [], []
    for name in _WEIGHTS:
        w = given[name]
        p = parts[name]
        shape2 = p.shape[1:]
        g, dlt, m2, v2 = _adamw(p, w.reshape(shape2), given['m_' + name].reshape(shape2), given['v_' + name].reshape(shape2))
        grads.append(g.reshape(w.shape))
        deltas.append(dlt.reshape(w.shape))
        new_m.append(m2.reshape(w.shape))
        new_v.append(v2.reshape(w.shape))
    return (loss, dh0.reshape(1, S, D), *grads, *deltas, *new_m, *new_v)
```

```python
import functools

import numpy as np
import jax
import jax.numpy as jnp
from jax import lax
from jax.experimental import pallas as pl
from jax.experimental.pallas import tpu as pltpu

F32, BF = jnp.float32, jnp.bfloat16

D = 1024
DFF = 2816
CH = 1408
NH = 16
HP = 128
KV_LORA, Q_LORA, ROPE = 256, 512, 32
KVA_P = 384
CONV_W = 31
HALO = 32
EPS = 1e-6
SM_SCALE = float((64 + 32) ** -0.5)
LOG2E = 1.4426950408889634
EXP2_SCALE = SM_SCALE * LOG2E
NEG = -1e30
N_DEV = 8
MESH = pl.DeviceIdType.MESH

TS = 256
TA = 1024
TA_FWD = 1024
ATT_ROWS = 256
TW = 2048
VMEM_LIMIT = 56 * 1024 * 1024

LR, B1, B2, EPS_ADAM, WD, STEP = 0.001, 0.9, 0.999, 1e-08, 0.01, 10

VMEM_FULL = pl.BlockSpec(memory_space=pltpu.VMEM)
HBM_FULL = pl.BlockSpec(memory_space=pltpu.HBM)


def _params(n_grid):
    return pltpu.CompilerParams(dimension_semantics=("arbitrary",) * n_grid, vmem_limit_bytes=VMEM_LIMIT)


def _dot(a, b):
    return jnp.dot(a, b, preferred_element_type=F32)


def _dot_nt(a, b):
    return lax.dot_general(a, b, (((1,), (1,)), ((), ())), preferred_element_type=F32)


def _dot_tn(a, b):
    return lax.dot_general(a, b, (((0,), (0,)), ((), ())), preferred_element_type=F32)


def _sum0(x):
    return jnp.sum(x, axis=0, keepdims=True)


def _mean1(x):
    return jnp.mean(x, axis=-1, keepdims=True)


def _sigmoid(x):
    return jax.nn.sigmoid(x)


def _rows(shape, imap):
    return pl.BlockSpec(shape, imap)


def _norm_mod(h, vec):
    r = lax.rsqrt(_mean1(h * h) + EPS)
    xhat = h * r
    u = (xhat * vec[0:1]) * (1.0 + vec[2:3]) + vec[1:2]
    return u, xhat, r


def _norm_mod_bwd(du, xhat, r, vec):
    g = vec[0:1]
    dxn = du * (1.0 + vec[2:3])
    dsh = _sum0(du)
    dsc = _sum0(du * (xhat * g))
    dg = _sum0(dxn * xhat)
    dxhat = dxn * g
    dh = r * (dxhat - xhat * _mean1(dxhat * xhat))
    return dh, dg, dsh, dsc


def _rms_bwd(dy, x, r, g):
    xhat = x * r
    dg = _sum0(dy * xhat)
    dxhat = dy * g
    return r * (dxhat - xhat * _mean1(dxhat * xhat)), dg


def _accumulate(ref, first, rows):
    @pl.when(first)
    def _():
        ref[...] = jnp.zeros(ref.shape, ref.dtype)

    for i, row in enumerate(rows):
        ref[i:i + 1, :] += row


def _rope(x, tab):
    return x * tab[:, 0:HP] + pltpu.roll(x, 16, 1) * tab[:, HP:2 * HP] + pltpu.roll(x, HP - 16, 1) * tab[:, 2 * HP:3 * HP]


def _rope_t(dy, tab):
    return (dy * tab[:, 0:HP] + pltpu.roll(dy * tab[:, HP:2 * HP], HP - 16, 1)
            + pltpu.roll(dy * tab[:, 2 * HP:3 * HP], 16, 1))


def _exchange(items, name):
    n = len(items)

    def body(*refs):
        ins, outs = refs[:n], refs[n:2 * n]
        send_sems, recv_sems, local_sems = refs[2 * n:]
        x, y, c = lax.axis_index("x"), lax.axis_index("y"), lax.axis_index("c")
        me = 4 * x + 2 * y + c

        def source(j, dev):
            return ins[j] if items[j][1] == "gather" else ins[j].at[dev]

        own = [pltpu.make_async_copy(source(j, me), outs[j].at[me], local_sems.at[j]) for j in range(n)]
        for cp in own:
            cp.start()
        remote = []
        for d in range(1, N_DEV):
            px = 1 - x if d & 4 else x
            py = 1 - y if d & 2 else y
            pc = 1 - c if d & 1 else c
            peer = 4 * px + 2 * py + pc
            for j in range(n):
                pltpu.make_async_remote_copy(
                    src_ref=source(j, peer), dst_ref=outs[j].at[me], send_sem=send_sems.at[j, d - 1],
                    recv_sem=recv_sems.at[j, d - 1], device_id=(px, py, pc), device_id_type=MESH).start()
                remote.append(pltpu.make_async_remote_copy(
                    src_ref=source(j, peer), dst_ref=outs[j].at[peer], send_sem=send_sems.at[j, d - 1],
                    recv_sem=recv_sems.at[j, d - 1], device_id=(px, py, pc), device_id_type=MESH))
        for cp in remote:
            cp.wait_send()
            cp.wait_recv()
        for cp in own:
            cp.wait()

    out_shape = []
    for arr, mode in items:
        shp = (N_DEV,) + tuple(arr.shape) if mode == "gather" else tuple(arr.shape)
        out_shape.append(jax.ShapeDtypeStruct(shp, arr.dtype))
    return pl.pallas_call(
        body, name=name, out_shape=out_shape,
        in_specs=[HBM_FULL] * n, out_specs=[HBM_FULL] * n,
        scratch_shapes=[pltpu.SemaphoreType.DMA((n, N_DEV - 1)), pltpu.SemaphoreType.DMA((n, N_DEV - 1)),
                        pltpu.SemaphoreType.DMA((n,))],
        compiler_params=pltpu.CompilerParams(has_side_effects=True),
    )(*[a for a, _ in items])


def _gather_two_level(arrs, name):
    n = len(arrs)

    def body(*refs):
        ins, outs = refs[:n], refs[n:2 * n]
        send_sems, recv_sems, local_sems = refs[2 * n:]
        x, y, c = lax.axis_index("x"), lax.axis_index("y"), lax.axis_index("c")
        sibling = (x, y, 1 - c)
        chips = [(1 - x, y), (x, 1 - y), (1 - x, 1 - y)]

        def slot(j, px, py, pc):
            return outs[j].at[4 * px + 2 * py + pc]

        def copy(j, k, block, to, src=None):
            return pltpu.make_async_remote_copy(
                src_ref=slot(j, *block) if src is None else src, dst_ref=slot(j, *block),
                send_sem=send_sems.at[j, k], recv_sem=recv_sems.at[j, k], device_id=to, device_id_type=MESH)

        own = [pltpu.make_async_copy(ins[j], slot(j, x, y, c), local_sems.at[j]) for j in range(n)]
        for cp in own:
            cp.start()
        sent = []
        for j in range(n):
            sent.append(copy(j, 0, (x, y, c), sibling, src=ins[j]))
            sent += [copy(j, 1 + i, (x, y, c), (*chip, c), src=ins[j]) for i, chip in enumerate(chips)]
        for cp in sent:
            cp.start()
        for i, chip in enumerate(chips):
            for j in range(n):
                copy(j, 1 + i, (*chip, c), (x, y, c)).wait_recv()
                passed = copy(j, 4 + i, (*chip, c), sibling)
                passed.start()
                sent.append(passed)
        for j in range(n):
            copy(j, 0, (x, y, 1 - c), (x, y, c)).wait_recv()
            for i, chip in enumerate(chips):
                copy(j, 4 + i, (*chip, 1 - c), (x, y, c)).wait_recv()
        for cp in sent:
            cp.wait_send()
        for cp in own:
            cp.wait()

    return pl.pallas_call(
        body, name=name, out_shape=[jax.ShapeDtypeStruct((N_DEV,) + tuple(a.shape), a.dtype) for a in arrs],
        in_specs=[HBM_FULL] * n, out_specs=[HBM_FULL] * n,
        scratch_shapes=[pltpu.SemaphoreType.DMA((n, N_DEV - 1)), pltpu.SemaphoreType.DMA((n, N_DEV - 1)),
                        pltpu.SemaphoreType.DMA((n,))],
        compiler_params=pltpu.CompilerParams(has_side_effects=True),
    )(*arrs)


def _pair_swap(arrs, name):
    n = len(arrs)

    def body(*refs):
        ins, outs = refs[:n], refs[n:2 * n]
        send_sems, recv_sems = refs[2 * n:]
        x, y, c = lax.axis_index("x"), lax.axis_index("y"), lax.axis_index("c")
        copies = [pltpu.make_async_remote_copy(
            src_ref=ins[j].at[1 - c], dst_ref=outs[j], send_sem=send_sems.at[j], recv_sem=recv_sems.at[j],
            device_id=(x, y, 1 - c), device_id_type=MESH) for j in range(n)]
        for cp in copies:
            cp.start()
        for cp in copies:
            cp.wait_send()
            cp.wait_recv()

    return pl.pallas_call(
        body, name=name, out_shape=[jax.ShapeDtypeStruct(tuple(a.shape[1:]), a.dtype) for a in arrs],
        in_specs=[HBM_FULL] * n, out_specs=[HBM_FULL] * n,
        scratch_shapes=[pltpu.SemaphoreType.DMA((n,)), pltpu.SemaphoreType.DMA((n,))],
        compiler_params=pltpu.CompilerParams(has_side_effects=True),
    )(*arrs)


def _chip_comm(arrs, scatter):
    n = len(arrs)
    out_shapes = [jax.ShapeDtypeStruct(tuple(a.shape) if scatter else (4,) + tuple(a.shape), a.dtype) for a in arrs]
    scratch = [pltpu.SemaphoreType.DMA((n, 3)), pltpu.SemaphoreType.DMA((n, 3)), pltpu.SemaphoreType.DMA((n,))]

    def copies(ins, outs, send_sems, recv_sems, local_sems, arriving):
        x, y, c = lax.axis_index("x"), lax.axis_index("y"), lax.axis_index("c")
        here = 2 * x + y

        def source(j, chip):
            return ins[j].at[chip] if scatter else ins[j]

        own = [pltpu.make_async_copy(source(j, here), outs[j].at[here], local_sems.at[j]) for j in range(n)]
        remote = []
        for i, (px, py) in enumerate([(1 - x, y), (x, 1 - y), (1 - x, 1 - y)]):
            there = 2 * px + py
            for j in range(n):
                remote.append(pltpu.make_async_remote_copy(
                    src_ref=source(j, there), dst_ref=outs[j].at[there if arriving else here],
                    send_sem=send_sems.at[j, i], recv_sem=recv_sems.at[j, i], device_id=(px, py, c),
                    device_id_type=MESH))
        return own, remote

    def start(*refs):
        own, remote = copies(*refs, arriving=False)
        for cp in own + remote:
            cp.start()

    def finish(*refs):
        own, remote = copies(*refs, arriving=True)
        for cp in remote:
            cp.wait_send()
            cp.wait_recv()
        for cp in own:
            cp.wait()

    return out_shapes, scratch, start, finish


def _chip_scatter(arrs, name):
    n = len(arrs)
    out_shapes, scratch, start, finish = _chip_comm(arrs, True)

    def body(*refs):
        start(refs[:n], refs[n:2 * n], *refs[2 * n:])
        finish(refs[:n], refs[n:2 * n], *refs[2 * n:])

    return pl.pallas_call(
        body, name=name, out_shape=out_shapes, in_specs=[HBM_FULL] * n, out_specs=[HBM_FULL] * n,
        scratch_shapes=scratch, compiler_params=pltpu.CompilerParams(has_side_effects=True),
    )(*arrs)


def _hosted_call(body, args, comm, *, name, n_steps, in_specs, out_specs, out_shape, scratch_shapes=()):
    if comm is None:
        res = pl.pallas_call(
            body, name=name, grid=(n_steps,), out_shape=list(out_shape), in_specs=list(in_specs),
            out_specs=list(out_specs), scratch_shapes=list(scratch_shapes), compiler_params=_params(1))(*args)
        return res, []
    arrs, scatter = comm
    c_shapes, c_scratch, start, finish = _chip_comm(arrs, scatter)
    n_in, n_out, n_sc, k = len(in_specs), len(out_specs), len(scratch_shapes), len(arrs)

    def hosting(*refs):
        ins, cin = refs[:n_in], refs[n_in:n_in + k]
        outs, cout = refs[n_in + k:n_in + k + n_out], refs[n_in + k + n_out:n_in + 2 * k + n_out]
        scratch = refs[n_in + 2 * k + n_out:n_in + 2 * k + n_out + n_sc]
        sems = refs[n_in + 2 * k + n_out + n_sc:]
        step = pl.program_id(0)

        @pl.when(step == 0)
        def _():
            start(cin, cout, *sems)

        body(*ins, *outs, *scratch)

        @pl.when(step == n_steps - 1)
        def _():
            finish(cin, cout, *sems)

    res = pl.pallas_call(
        hosting, name=name + "_hosting", grid=(n_steps,), out_shape=list(out_shape) + c_shapes,
        in_specs=list(in_specs) + [HBM_FULL] * k, out_specs=list(out_specs) + [HBM_FULL] * k,
        scratch_shapes=list(scratch_shapes) + c_scratch,
        compiler_params=pltpu.CompilerParams(dimension_semantics=("arbitrary",), vmem_limit_bytes=VMEM_LIMIT,
                                             has_side_effects=True))(*args, *arrs)
    return res[:n_out], res[n_out:]


def _sibling_merge(arrs, name):
    n = len(arrs)

    def body(*refs):
        ins, outs = refs[:n], refs[n:2 * n]
        send_sems, recv_sems, local_sems = refs[2 * n:]
        x, y, c = lax.axis_index("x"), lax.axis_index("y"), lax.axis_index("c")
        own = [pltpu.make_async_copy(ins[j], outs[j].at[c], local_sems.at[j]) for j in range(n)]
        for cp in own:
            cp.start()
        waits = []
        for j in range(n):
            pltpu.make_async_remote_copy(
                src_ref=ins[j], dst_ref=outs[j].at[c], send_sem=send_sems.at[j], recv_sem=recv_sems.at[j],
                device_id=(x, y, 1 - c), device_id_type=MESH).start()
            waits.append(pltpu.make_async_remote_copy(
                src_ref=ins[j], dst_ref=outs[j].at[1 - c], send_sem=send_sems.at[j],
                recv_sem=recv_sems.at[j], device_id=(x, y, 1 - c), device_id_type=MESH))
        for cp in waits:
            cp.wait_send()
            cp.wait_recv()
        for cp in own:
            cp.wait()

    return pl.pallas_call(
        body, name=name, out_shape=[jax.ShapeDtypeStruct((2, 4) + tuple(a.shape[1:]), a.dtype) for a in arrs],
        in_specs=[HBM_FULL] * n, out_specs=[HBM_FULL] * n,
        scratch_shapes=[pltpu.SemaphoreType.DMA((n,)), pltpu.SemaphoreType.DMA((n,)), pltpu.SemaphoreType.DMA((n,))],
        compiler_params=pltpu.CompilerParams(has_side_effects=True),
    )(*arrs)


def _pair_add(core, a, b):
    _, R, C = a.shape
    tr = _row_tile(R)

    def body(core_ref, a_ref, b_ref, o_ref):
        o_ref[...] = (a_ref[...].astype(F32) + b_ref[...].astype(F32)).astype(BF)

    blk = _rows((tr, C), lambda i, core: (i, 0))
    grid_spec = pltpu.PrefetchScalarGridSpec(
        num_scalar_prefetch=1, grid=(R // tr,),
        in_specs=[_rows((None, tr, C), lambda i, core: (core[0], i, 0)), blk], out_specs=blk)
    return pl.pallas_call(
        body, name="pair_add", grid_spec=grid_spec, out_shape=jax.ShapeDtypeStruct((R, C), BF),
        compiler_params=_params(1),
    )(core, a, b)


def _mod_fwd(c_all, w0, w1, wkv):
    n0, n1, n2 = w0.shape[1], w1.shape[1], wkv.shape[1]

    def body(c_ref, w0_ref, w1_ref, w2_ref, o_ref):
        cc = c_ref[...]
        s = cc * _sigmoid(cc)
        o_ref[:, 0:n0] = _dot(s, w0_ref[...])
        o_ref[:, n0:n0 + n1] = _dot(s, w1_ref[...])
        o_ref[:, n0 + n1:n0 + n1 + n2] = _dot(s, w2_ref[...])

    return pl.pallas_call(
        body, name="mod_fwd", out_shape=jax.ShapeDtypeStruct((N_DEV, n0 + n1 + n2), F32),
        in_specs=[VMEM_FULL] * 4, out_specs=VMEM_FULL,
        compiler_params=pltpu.CompilerParams(vmem_limit_bytes=VMEM_LIMIT),
    )(c_all, w0, w1, wkv)


def _mod_wgrad(c_t, dm):
    C = dm.shape[1]
    tr = 256

    def body(ct_ref, dm_ref, o_ref):
        ct = ct_ref[...]
        s = ct * _sigmoid(ct)
        dmv = dm_ref[...]
        lane = lax.broadcasted_iota(jnp.int32, (tr, N_DEV), 1)
        acc = jnp.zeros((tr, C), F32)
        for r in range(N_DEV):
            col = jnp.sum(jnp.where(lane == r, s, 0.0), axis=1, keepdims=True)
            acc = acc + col * dmv[r:r + 1, :]
        o_ref[...] = acc

    return pl.pallas_call(
        body, name="mod_wgrad", grid=(D // tr,), out_shape=jax.ShapeDtypeStruct((D, C), F32),
        in_specs=[_rows((tr, N_DEV), lambda i: (i, 0)), _rows((N_DEV, C), lambda i: (0, 0))],
        out_specs=_rows((tr, C), lambda i: (i, 0)), compiler_params=_params(1),
    )(c_t, dm)


def _ffn_fwd(h, vec, w13, w2, comm=None):
    S = h.shape[0]

    def body(h_ref, vec_ref, w13_ref, w2_ref, ho_ref, u_ref, a_ref, b_ref):
        hv, vec = h_ref[...], vec_ref[...]
        u, _, _ = _norm_mod(hv, vec)
        ub = u.astype(BF)
        u_ref[...] = ub
        y = jnp.zeros((TS, D), F32)
        for k in range(DFF // CH):
            c0 = k * CH
            a = _dot(ub, w13_ref[:, c0:c0 + CH])
            b = _dot(ub, w13_ref[:, DFF + c0:DFF + c0 + CH])
            a_ref[:, c0:c0 + CH] = a.astype(BF)
            b_ref[:, c0:c0 + CH] = b.astype(BF)
            t = (a * _sigmoid(a)) * b
            y = y + _dot(t.astype(BF), w2_ref[c0:c0 + CH, :])
        ho_ref[...] = hv + (0.5 * vec[3:4]) * y

    tok = lambda i: (i, 0)
    return _hosted_call(
        body, (h, vec, w13, w2), comm, name="ffn_fwd", n_steps=S // TS,
        out_shape=[jax.ShapeDtypeStruct((S, D), F32), jax.ShapeDtypeStruct((S, D), BF),
                   jax.ShapeDtypeStruct((S, DFF), BF), jax.ShapeDtypeStruct((S, DFF), BF)],
        in_specs=[_rows((TS, D), tok), _rows((8, D), lambda i: (0, 0)), VMEM_FULL, VMEM_FULL],
        out_specs=[_rows((TS, D), tok), _rows((TS, D), tok), _rows((TS, DFF), tok), _rows((TS, DFF), tok)])


def _ffn_bwd(dho, h, a, b, vec, w13, w2, comm=None):
    S = h.shape[0]

    def body(dho_ref, h_ref, a_ref, b_ref, vec_ref, w13_ref, w2_ref,
             dh_ref, da_ref, db_ref, t_ref, dhb_ref, ps_ref):
        dho_v, vec = dho_ref[...], vec_ref[...]
        dhb_ref[...] = dho_v.astype(BF)
        dyb = ((0.5 * vec[3:4]) * dho_v).astype(BF)
        du = jnp.zeros((TS, D), F32)
        for k in range(DFF // CH):
            c0 = k * CH
            av = a_ref[:, c0:c0 + CH].astype(F32)
            bv = b_ref[:, c0:c0 + CH].astype(F32)
            dt = _dot_nt(dyb, w2_ref[c0:c0 + CH, :])
            sig = _sigmoid(av)
            sl = av * sig
            t_ref[:, c0:c0 + CH] = (sl * bv).astype(BF)
            dab = (dt * bv * (sig * (1.0 + av * (1.0 - sig)))).astype(BF)
            dbb = (dt * sl).astype(BF)
            da_ref[:, c0:c0 + CH] = dab
            db_ref[:, c0:c0 + CH] = dbb
            du = du + _dot_nt(dab, w13_ref[:, c0:c0 + CH]) + _dot_nt(dbb, w13_ref[:, DFF + c0:DFF + c0 + CH])
        _, xhat, r = _norm_mod(h_ref[...], vec)
        dhn, dg, dsh, dsc = _norm_mod_bwd(du, xhat, r, vec)
        dh_ref[...] = dho_v + dhn
        _accumulate(ps_ref, pl.program_id(0) == 0, [dg, dsh, dsc])

    tok = lambda i: (i, 0)
    return _hosted_call(
        body, (dho, h, a, b, vec, w13, w2), comm, name="ffn_bwd", n_steps=S // TS,
        out_shape=[jax.ShapeDtypeStruct((S, D), F32), jax.ShapeDtypeStruct((S, DFF), BF),
                   jax.ShapeDtypeStruct((S, DFF), BF), jax.ShapeDtypeStruct((S, DFF), BF),
                   jax.ShapeDtypeStruct((S, D), BF), jax.ShapeDtypeStruct((8, D), F32)],
        in_specs=[_rows((TS, D), tok), _rows((TS, D), tok), _rows((TS, DFF), tok), _rows((TS, DFF), tok),
                  _rows((8, D), lambda i: (0, 0)), VMEM_FULL, VMEM_FULL],
        out_specs=[_rows((TS, D), tok), _rows((TS, DFF), tok), _rows((TS, DFF), tok), _rows((TS, DFF), tok),
                   _rows((TS, D), tok), _rows((8, D), lambda i: (0, 0))])


def _wgrad(a, b, tm, tn, name, gate=None):
    S, M = a.shape
    N = b.shape[1]
    n_s = S // TW

    def body(*refs):
        if gate is None:
            a_ref, b_ref, o_ref, acc_ref = refs
        else:
            a_ref, b_ref, w_ref, sc_ref, o_ref, gs_ref, acc_ref = refs
        s = pl.program_id(2)

        @pl.when(s == 0)
        def _():
            acc_ref[...] = jnp.zeros((tm, tn), F32)

        acc_ref[...] += _dot_tn(a_ref[...], b_ref[...])

        @pl.when(s == n_s - 1)
        def _():
            acc = acc_ref[...]
            if gate is None:
                o_ref[...] = acc.astype(BF)
            else:
                o_ref[...] = (acc * sc_ref[0:1, :]).astype(BF)
                gs_ref[...] = jnp.broadcast_to(_sum0(acc * w_ref[...].astype(F32)), (8, tn))

    in_specs = [_rows((TW, tm), lambda m, n, s: (s, m)), _rows((TW, tn), lambda m, n, s: (s, n))]
    out_shape = [jax.ShapeDtypeStruct((M, N), BF)]
    out_specs = [_rows((tm, tn), lambda m, n, s: (m, n))]
    args = [a, b]
    if gate is not None:
        in_specs += [_rows((tm, tn), lambda m, n, s: (m, n)), _rows((8, tn), lambda m, n, s: (0, n))]
        out_shape.append(jax.ShapeDtypeStruct((M // tm, 8, N), F32))
        out_specs.append(_rows((None, 8, tn), lambda m, n, s: (m, 0, n)))
        args += list(gate)
    res = pl.pallas_call(
        body, name=name, grid=(M // tm, N // tn, n_s), out_shape=out_shape, in_specs=in_specs,
        out_specs=out_specs, scratch_shapes=[pltpu.VMEM((tm, tn), F32)], compiler_params=_params(3),
    )(*args)
    return res[0] if gate is None else (res[0], res[1])


def _conv_in_fwd(h, vec, w1, b1):
    S = h.shape[0]

    def body(h_ref, vec_ref, w_ref, b1_ref, hn_ref, pre_ref):
        u, _, _ = _norm_mod(h_ref[...], vec_ref[...])
        ub = u.astype(BF)
        hn_ref[...] = ub
        pre_ref[...] = _dot(ub, w_ref[...]) + b1_ref[0:1, :]

    tok = lambda i: (i, 0)
    return pl.pallas_call(
        body, name="conv_in_fwd", grid=(S // TS,),
        out_shape=[jax.ShapeDtypeStruct((S, D), BF), jax.ShapeDtypeStruct((S, 2 * D), F32)],
        in_specs=[_rows((TS, D), tok), _rows((8, D), lambda i: (0, 0)), VMEM_FULL, _rows((8, 2 * D), lambda i: (0, 0))],
        out_specs=[_rows((TS, D), tok), _rows((TS, 2 * D), tok)], compiler_params=_params(1),
    )(h, vec, w1, b1)


def _glu(pre):
    return pre[:, :D] * _sigmoid(pre[:, D:])


def _tap_groups(offset):
    groups = {}
    for j in range(CONV_W):
        off = offset(j)
        groups.setdefault(off % 8, []).append((off - off % 8, j))
    return [(phase, sorted(taps)) for phase, taps in sorted(groups.items())]


def _shift_window(dst_ref, win_ref, phase, rows):
    dst_ref[0:rows, :] = win_ref[phase:phase + rows, :]


def _layernorm(u2):
    mu = _mean1(u2)
    xc = u2 - mu
    rstd = lax.rsqrt(_mean1(xc * xc) + EPS)
    return xc * rstd, rstd


def _conv_out_fwd(pre, h, cw, w2, comm=None):
    S = h.shape[0]
    hb = TS // HALO

    def body(pre_ref, ph_ref, h_ref, cw_ref, w2_ref, u2_ref, z_ref, ho_ref, win_ref, sh_ref):
        i = pl.program_id(0)
        cwv = cw_ref[...]
        win_ref[0:HALO, :] = jnp.where(i > 0, _glu(ph_ref[...]), 0.0)
        win_ref[HALO:HALO + TS, :] = _glu(pre_ref[...])
        u2 = jnp.broadcast_to(cwv[31:32], (TS, D))
        for phase, taps in _tap_groups(lambda j: HALO - (CONV_W - 1) + j):
            _shift_window(sh_ref, win_ref, phase, taps[-1][0] + TS)
            for lo, j in taps:
                u2 = u2 + cwv[j:j + 1] * sh_ref[lo:lo + TS, :]
        u2_ref[...] = u2
        xh, _ = _layernorm(u2)
        un = xh * cwv[32:33] + cwv[33:34]
        zb = (un * _sigmoid(un)).astype(BF)
        z_ref[...] = zb
        y = _dot(zb, w2_ref[...]) + cwv[34:35]
        ho_ref[...] = h_ref[...] + cwv[35:36] * y

    tok = lambda i: (i, 0)
    return _hosted_call(
        body, (pre, pre, h, cw, w2), comm, name="conv_out_fwd", n_steps=S // TS,
        out_shape=[jax.ShapeDtypeStruct((S, D), F32), jax.ShapeDtypeStruct((S, D), BF), jax.ShapeDtypeStruct((S, D), F32)],
        in_specs=[_rows((TS, 2 * D), tok), _rows((HALO, 2 * D), lambda i: (jnp.maximum(i * hb - 1, 0), 0)),
                  _rows((TS, D), tok), _rows((40, D), lambda i: (0, 0)), VMEM_FULL],
        out_specs=[_rows((TS, D), tok), _rows((TS, D), tok), _rows((TS, D), tok)],
        scratch_shapes=[pltpu.VMEM((TS + HALO, D), F32)] * 2)


def _conv_out_bwd(dho, u2, cw, w2):
    S = dho.shape[0]

    def body(dho_ref, u2_ref, cw_ref, w2_ref, du2_ref, dhb_ref, ps_ref):
        dho_v, cwv = dho_ref[...], cw_ref[...]
        dhb_ref[...] = dho_v.astype(BF)
        dz = _dot_nt((cwv[35:36] * dho_v).astype(BF), w2_ref[...])
        xh, rstd = _layernorm(u2_ref[...])
        un = xh * cwv[32:33] + cwv[33:34]
        sig = _sigmoid(un)
        dun = dz * (sig * (1.0 + un * (1.0 - sig)))
        dxh = dun * cwv[32:33]
        du2_ref[...] = rstd * (dxh - _mean1(dxh) - xh * _mean1(dxh * xh))
        _accumulate(ps_ref, pl.program_id(0) == 0, [_sum0(dun * xh), _sum0(dun), _sum0(dho_v)])

    tok = lambda i: (i, 0)
    return pl.pallas_call(
        body, name="conv_out_bwd", grid=(S // TS,),
        out_shape=[jax.ShapeDtypeStruct((S, D), F32), jax.ShapeDtypeStruct((S, D), BF), jax.ShapeDtypeStruct((8, D), F32)],
        in_specs=[_rows((TS, D), tok), _rows((TS, D), tok), _rows((40, D), lambda i: (0, 0)), VMEM_FULL],
        out_specs=[_rows((TS, D), tok), _rows((TS, D), tok), _rows((8, D), lambda i: (0, 0))],
        compiler_params=_params(1),
    )(dho, u2, cw, w2)


def _conv_in_bwd(du2, pre, h, dho, vec, cw, w1):
    S = h.shape[0]
    n_t = S // TS
    hb = TS // HALO

    def body(du2_ref, dh2h_ref, pre_ref, ph_ref, h_ref, dho_ref, vec_ref, cw_ref, w1_ref,
             dh_ref, dpre_ref, ps_ref, pw_ref, pb_ref, winu_ref, wind_ref, sh_ref):
        i = pl.program_id(0)
        vec, cwv = vec_ref[...], cw_ref[...]
        pre = pre_ref[...]
        av, sg = pre[:, :D], _sigmoid(pre[:, D:])
        winu_ref[0:HALO, :] = jnp.where(i > 0, _glu(ph_ref[...]), 0.0)
        winu_ref[HALO:HALO + TS, :] = av * sg
        du2v = du2_ref[...]
        wind_ref[0:TS, :] = du2v
        wind_ref[TS:TS + HALO, :] = jnp.where(i < n_t - 1, dh2h_ref[...], 0.0)

        @pl.when(i == 0)
        def _():
            pw_ref[...] = jnp.zeros((32, D), F32)

        for phase, taps in _tap_groups(lambda j: HALO - (CONV_W - 1) + j):
            _shift_window(sh_ref, winu_ref, phase, taps[-1][0] + TS)
            for lo, j in taps:
                pw_ref[j:j + 1, :] += _sum0(du2v * sh_ref[lo:lo + TS, :])
        pw_ref[31:32, :] += _sum0(du2v)
        du1 = jnp.zeros((TS, D), F32)
        for phase, taps in _tap_groups(lambda j: CONV_W - 1 - j):
            _shift_window(sh_ref, wind_ref, phase, taps[-1][0] + TS)
            for lo, j in taps:
                du1 = du1 + cwv[j:j + 1] * sh_ref[lo:lo + TS, :]
        da = du1 * sg
        dg = du1 * av * sg * (1.0 - sg)
        dab, dgb = da.astype(BF), dg.astype(BF)
        dpre_ref[:, :D] = dab
        dpre_ref[:, D:] = dgb

        @pl.when(i == 0)
        def _():
            pb_ref[...] = jnp.zeros((8, 2 * D), F32)

        pb_ref[0:1, :D] += _sum0(da)
        pb_ref[0:1, D:] += _sum0(dg)
        du = _dot_nt(dab, w1_ref[:, :D]) + _dot_nt(dgb, w1_ref[:, D:])
        _, xhat, r = _norm_mod(h_ref[...], vec)
        dhn, dgn, dsh, dsc = _norm_mod_bwd(du, xhat, r, vec)
        dh_ref[...] = dho_ref[...] + dhn
        _accumulate(ps_ref, i == 0, [dgn, dsh, dsc])

    tok = lambda i: (i, 0)
    fixed = lambda i: (0, 0)
    return pl.pallas_call(
        body, name="conv_in_bwd", grid=(n_t,),
        out_shape=[jax.ShapeDtypeStruct((S, D), F32), jax.ShapeDtypeStruct((S, 2 * D), BF),
                   jax.ShapeDtypeStruct((8, D), F32), jax.ShapeDtypeStruct((32, D), F32),
                   jax.ShapeDtypeStruct((8, 2 * D), F32)],
        in_specs=[_rows((TS, D), tok), _rows((HALO, D), lambda i: (jnp.minimum((i + 1) * hb, S // HALO - 1), 0)),
                  _rows((TS, 2 * D), tok), _rows((HALO, 2 * D), lambda i: (jnp.maximum(i * hb - 1, 0), 0)),
                  _rows((TS, D), tok), _rows((TS, D), tok), _rows((8, D), fixed), _rows((40, D), fixed), VMEM_FULL],
        out_specs=[_rows((TS, D), tok), _rows((TS, 2 * D), tok), _rows((8, D), fixed), _rows((32, D), fixed),
                   _rows((8, 2 * D), fixed)],
        scratch_shapes=[pltpu.VMEM((TS + HALO, D), F32)] * 3,
        compiler_params=_params(1),
    )(du2, du2, pre, pre, h, dho, vec, cw, w1)


def _lane():
    return lax.broadcasted_iota(jnp.int32, (TS, HP), 1)


def _kv_fwd(h, vec, wkva, g2, wkvb, rope):
    S = h.shape[0]

    def body(h_ref, vec_ref, wa_ref, g2_ref, wb_ref, rope_ref, hn_ref, ckv_ref, ckn_ref, k_ref, v_ref):
        u, _, _ = _norm_mod(h_ref[...], vec_ref[...])
        ub = u.astype(BF)
        hn_ref[...] = ub
        kva = _dot(ub, wa_ref[...])
        ckv = kva[:, :KV_LORA]
        ckv_ref[...] = ckv
        r2 = lax.rsqrt(_mean1(ckv * ckv) + EPS)
        cknb = ((ckv * r2) * g2_ref[0:1, :]).astype(BF)
        ckn_ref[...] = cknb
        kvb = _dot(cknb, wb_ref[...])
        kpe = _rope(kva[:, KV_LORA:KVA_P], rope_ref[...])
        lane = _lane()
        for hd in range(NH):
            blk = kvb[:, hd * HP:(hd + 1) * HP]
            k_ref[:, hd * HP:(hd + 1) * HP] = jnp.where(lane < 64, blk, kpe).astype(BF)
            v_ref[:, hd * HP:(hd + 1) * HP] = jnp.where(lane >= 64, blk, 0.0).astype(BF)

    tok = lambda i: (i, 0)
    fixed = lambda i: (0, 0)
    return pl.pallas_call(
        body, name="kv_fwd", grid=(S // TS,),
        out_shape=[jax.ShapeDtypeStruct((S, D), BF), jax.ShapeDtypeStruct((S, KV_LORA), F32),
                   jax.ShapeDtypeStruct((S, KV_LORA), BF), jax.ShapeDtypeStruct((S, NH * HP), BF),
                   jax.ShapeDtypeStruct((S, NH * HP), BF)],
        in_specs=[_rows((TS, D), tok), _rows((8, D), fixed), VMEM_FULL, _rows((8, KV_LORA), fixed), VMEM_FULL,
                  _rows((TS, 3 * HP), tok)],
        out_specs=[_rows((TS, D), tok), _rows((TS, KV_LORA), tok), _rows((TS, KV_LORA), tok),
                   _rows((TS, NH * HP), tok), _rows((TS, NH * HP), tok)],
        compiler_params=_params(1),
    )(h, vec, wkva, g2, wkvb, rope)


def _kv_bwd(dk, dv, h, ckv, dho, vec, wkva, g2, wkvb, rope):
    S = h.shape[0]

    def body(dk_ref, dv_ref, h_ref, ckv_ref, dho_ref, vec_ref, wa_ref, g2_ref, wb_ref, rope_ref,
             dh_ref, dkva_ref, dkvb_ref, ps_ref, ps2_ref):
        i = pl.program_id(0)
        vec = vec_ref[...]
        lane = _lane()
        dkpe = jnp.zeros((TS, HP), F32)
        for hd in range(NH):
            dkh = dk_ref[:, hd * HP:(hd + 1) * HP]
            dvh = dv_ref[:, hd * HP:(hd + 1) * HP]
            dkvb_ref[:, hd * HP:(hd + 1) * HP] = jnp.where(lane < 64, dkh, dvh).astype(BF)
            dkpe = dkpe + jnp.where(lane >= 64, dkh, 0.0)
        dkpe = _rope_t(dkpe, rope_ref[...])
        dckn = _dot_nt(dkvb_ref[...], wb_ref[...])
        ckv = ckv_ref[...]
        r2 = lax.rsqrt(_mean1(ckv * ckv) + EPS)
        dckv, dg2 = _rms_bwd(dckn, ckv, r2, g2_ref[0:1, :])
        dkva_ref[:, :KV_LORA] = dckv.astype(BF)
        dkva_ref[:, KV_LORA:KVA_P] = dkpe.astype(BF)
        du = _dot_nt(dkva_ref[...], wa_ref[...])
        _, xhat, r = _norm_mod(h_ref[...], vec)
        dhn, dgn, dsh, dsc = _norm_mod_bwd(du, xhat, r, vec)
        dh_ref[...] = dho_ref[...] + dhn
        _accumulate(ps_ref, i == 0, [dgn, dsh, dsc])
        _accumulate(ps2_ref, i == 0, [dg2])

    tok = lambda i: (i, 0)
    fixed = lambda i: (0, 0)
    return pl.pallas_call(
        body, name="kv_bwd", grid=(S // TS,),
        out_shape=[jax.ShapeDtypeStruct((S, D), F32), jax.ShapeDtypeStruct((S, KVA_P), BF),
                   jax.ShapeDtypeStruct((S, NH * HP), BF), jax.ShapeDtypeStruct((8, D), F32),
                   jax.ShapeDtypeStruct((8, KV_LORA), F32)],
        in_specs=[_rows((TS, NH * HP), tok), _rows((TS, NH * HP), tok), _rows((TS, D), tok), _rows((TS, KV_LORA), tok),
                  _rows((TS, D), tok), _rows((8, D), fixed), VMEM_FULL, _rows((8, KV_LORA), fixed), VMEM_FULL,
                  _rows((TS, 3 * HP), tok)],
        out_specs=[_rows((TS, D), tok), _rows((TS, KVA_P), tok), _rows((TS, NH * HP), tok), _rows((8, D), fixed),
                   _rows((8, KV_LORA), fixed)],
        compiler_params=_params(1),
    )(dk, dv, h, ckv, dho, vec, wkva, g2, wkvb, rope)


def _q_fwd(h, vec, wqa, g2, wqb, rope):
    S = h.shape[0]

    def body(h_ref, vec_ref, wa_ref, g2_ref, wb_ref, rope_ref, hn_ref, qa_ref, qan_ref, q_ref):
        u, _, _ = _norm_mod(h_ref[...], vec_ref[...])
        ub = u.astype(BF)
        hn_ref[...] = ub
        qa = _dot(ub, wa_ref[...])
        qa_ref[...] = qa
        r2 = lax.rsqrt(_mean1(qa * qa) + EPS)
        qanb = ((qa * r2) * g2_ref[0:1, :]).astype(BF)
        qan_ref[...] = qanb
        q = _dot(qanb, wb_ref[...])
        tab = rope_ref[...]
        for hd in range(NH):
            q_ref[:, hd * HP:(hd + 1) * HP] = _rope(q[:, hd * HP:(hd + 1) * HP], tab).astype(BF)

    tok = lambda i: (i, 0)
    fixed = lambda i: (0, 0)
    return pl.pallas_call(
        body, name="q_fwd", grid=(S // TS,),
        out_shape=[jax.ShapeDtypeStruct((S, D), BF), jax.ShapeDtypeStruct((S, Q_LORA), F32),
                   jax.ShapeDtypeStruct((S, Q_LORA), BF), jax.ShapeDtypeStruct((S, NH * HP), BF)],
        in_specs=[_rows((TS, D), tok), _rows((8, D), fixed), VMEM_FULL, _rows((8, Q_LORA), fixed), VMEM_FULL,
                  _rows((TS, 3 * HP), tok)],
        out_specs=[_rows((TS, D), tok), _rows((TS, Q_LORA), tok), _rows((TS, Q_LORA), tok), _rows((TS, NH * HP), tok)],
        compiler_params=_params(1),
    )(h, vec, wqa, g2, wqb, rope)


def _q_bwd(dq, h, qa, dho, vec, wqa, g2, wqb, rope):
    S = h.shape[0]

    def body(dq_ref, h_ref, qa_ref, dho_ref, vec_ref, wa_ref, g2_ref, wb_ref, rope_ref,
             dh_ref, dqb_ref, dqa_ref, ps_ref, ps2_ref):
        i = pl.program_id(0)
        vec, tab = vec_ref[...], rope_ref[...]
        for hd in range(NH):
            dqb_ref[:, hd * HP:(hd + 1) * HP] = _rope_t(dq_ref[:, hd * HP:(hd + 1) * HP], tab).astype(BF)
        dqan = _dot_nt(dqb_ref[...], wb_ref[...])
        qa = qa_ref[...]
        r2 = lax.rsqrt(_mean1(qa * qa) + EPS)
        dqa, dg2 = _rms_bwd(dqan, qa, r2, g2_ref[0:1, :])
        dqab = dqa.astype(BF)
        dqa_ref[...] = dqab
        du = _dot_nt(dqab, wa_ref[...])
        _, xhat, r = _norm_mod(h_ref[...], vec)
        dhn, dgn, dsh, dsc = _norm_mod_bwd(du, xhat, r, vec)
        dh_ref[...] = dho_ref[...] + dhn
        _accumulate(ps_ref, i == 0, [dgn, dsh, dsc])
        _accumulate(ps2_ref, i == 0, [dg2])

    tok = lambda i: (i, 0)
    fixed = lambda i: (0, 0)
    return pl.pallas_call(
        body, name="q_bwd", grid=(S // TS,),
        out_shape=[jax.ShapeDtypeStruct((S, D), F32), jax.ShapeDtypeStruct((S, NH * HP), BF),
                   jax.ShapeDtypeStruct((S, Q_LORA), BF), jax.ShapeDtypeStruct((8, D), F32),
                   jax.ShapeDtypeStruct((8, Q_LORA), F32)],
        in_specs=[_rows((TS, NH * HP), tok), _rows((TS, D), tok), _rows((TS, Q_LORA), tok), _rows((TS, D), tok),
                  _rows((8, D), fixed), VMEM_FULL, _rows((8, Q_LORA), fixed), VMEM_FULL, _rows((TS, 3 * HP), tok)],
        out_specs=[_rows((TS, D), tok), _rows((TS, NH * HP), tok), _rows((TS, Q_LORA), tok), _rows((8, D), fixed),
                   _rows((8, Q_LORA), fixed)],
        compiler_params=_params(1),
    )(dq, h, qa, dho, vec, wqa, g2, wqb, rope)


def _attn_fwd(q, k, v):
    S = q.shape[0]
    TA = TA_FWD
    nq = S // TA
    rg = min(TA, ATT_ROWS)
    groups = TA // rg

    def softmax_pv(scores, vt, state, masks):
        out = []
        for g in range(groups):
            m, l, acc = state[g]
            s = scores[g] if masks is None else jnp.where(masks[g], scores[g], NEG)
            m_new = jnp.maximum(m, jnp.max(s, axis=1, keepdims=True))
            p = jnp.exp2((s - m_new) * EXP2_SCALE)
            alpha = jnp.exp2((m - m_new) * EXP2_SCALE)
            vg = vt[g] if isinstance(vt, list) else vt
            out.append((m_new, alpha * l + jnp.sum(p, axis=1, keepdims=True), alpha * acc + _dot(p.astype(BF), vg)))
        return tuple(out)

    def body(q_ref, k_ref, v_ref, o_ref, lse_ref):
        qi = pl.program_id(1)
        qs = [q_ref[g * rg:(g + 1) * rg, :] for g in range(groups)]

        def keys(j):
            return pl.ds(pl.multiple_of(j * TA, TA), TA)

        def scores_of(j):
            kt = k_ref[keys(j), :]
            return tuple(_dot_nt(qs[g], kt) for g in range(groups))

        state = tuple((jnp.full((rg, 1), NEG, F32), jnp.zeros((rg, 1), F32), jnp.zeros((rg, HP), F32))
                      for _ in range(groups))

        def step(kj, state):
            return softmax_pv(scores_of(kj), v_ref[keys(kj), :], state, None)

        state = lax.fori_loop(0, qi, step, state)
        kd, vd = k_ref[keys(qi), :], v_ref[keys(qi), :]
        ends = [(g + 1) * rg for g in range(groups)]
        diag_scores = tuple(_dot_nt(qs[g], kd[:ends[g]]) for g in range(groups))
        masks = [lax.broadcasted_iota(jnp.int32, (rg, ends[g]), 1)
                 <= lax.broadcasted_iota(jnp.int32, (rg, ends[g]), 0) + g * rg for g in range(groups)]
        final = softmax_pv(diag_scores, [vd[:ends[g]] for g in range(groups)], state, masks)
        for g in range(groups):
            m, l, acc = final[g]
            o_ref[g * rg:(g + 1) * rg, :] = (acc / l).astype(BF)
            lse_ref[g * rg:(g + 1) * rg, :] = m * EXP2_SCALE + jnp.log(l) * LOG2E

    return pl.pallas_call(
        body, name="attn_fwd", grid=(NH, nq),
        out_shape=[jax.ShapeDtypeStruct((S, NH * HP), BF), jax.ShapeDtypeStruct((NH, S, 1), F32)],
        in_specs=[_rows((TA, HP), lambda h, i: (i, h)), _rows((S, HP), lambda h, i: (0, h)),
                  _rows((S, HP), lambda h, i: (0, h))],
        out_specs=[_rows((TA, HP), lambda h, i: (i, h)), _rows((None, TA, 1), lambda h, i: (h, i, 0))],
        compiler_params=_params(2),
    )(q, k, v)


def _attn_bwd(q, k, v, do, lse, delta):
    S = q.shape[0]
    nq = S // TA
    rg = min(TA, ATT_ROWS)
    groups = TA // rg

    def body(q_ref, do_ref, lse_ref, dl_ref, k_ref, v_ref, dq_ref, dk_ref, dv_ref, dka_ref, dva_ref):
        kj = pl.program_id(1)

        @pl.when(kj == 0)
        def _():
            dq_ref[...] = jnp.zeros((S, HP), F32)

        dka_ref[...] = jnp.zeros((TA, HP), F32)
        dva_ref[...] = jnp.zeros((TA, HP), F32)
        kt, vt = k_ref[...], v_ref[...]

        def tile(qi, diagonal):
            rows = [pl.ds(pl.multiple_of(qi * TA + g * rg, rg), rg) for g in range(groups)]
            ends = [(g + 1) * rg if diagonal else TA for g in range(groups)]
            qg = [q_ref[r, :] for r in rows]
            dog = [do_ref[r, :] for r in rows]
            scores = [_dot_nt(qg[g], kt[:ends[g]]) for g in range(groups)]
            dps = [_dot_nt(dog[g], vt[:ends[g]]) for g in range(groups)]
            for g in range(groups):
                p = jnp.exp2(scores[g] * EXP2_SCALE - lse_ref[rows[g], :])
                if diagonal:
                    col = lax.broadcasted_iota(jnp.int32, (rg, ends[g]), 1)
                    row = lax.broadcasted_iota(jnp.int32, (rg, ends[g]), 0)
                    p = jnp.where(col <= row + g * rg, p, 0.0)
                ds = p * (dps[g] - dl_ref[rows[g], :])
                pb, dsb = p.astype(BF), ds.astype(BF)
                dva_ref[0:ends[g], :] += _dot_tn(pb, dog[g])
                dka_ref[0:ends[g], :] += _dot_tn(dsb, qg[g])
                dq_ref[rows[g], :] += _dot(dsb, kt[:ends[g]]) * SM_SCALE

        tile(kj, True)

        def step(qi, carry):
            tile(qi, False)
            return carry

        lax.fori_loop(kj + 1, nq, step, 0)
        dk_ref[...] = dka_ref[...] * SM_SCALE
        dv_ref[...] = dva_ref[...]

    head = lambda h, j: (0, h)
    col1 = lambda h, j: (h, 0, 0)
    return pl.pallas_call(
        body, name="attn_bwd", grid=(NH, nq), out_shape=[jax.ShapeDtypeStruct((S, NH * HP), F32)] * 3,
        in_specs=[_rows((S, HP), head), _rows((S, HP), head), _rows((None, S, 1), col1), _rows((None, S, 1), col1),
                  _rows((TA, HP), lambda h, j: (j, h)), _rows((TA, HP), lambda h, j: (j, h))],
        out_specs=[_rows((S, HP), head), _rows((TA, HP), lambda h, j: (j, h)), _rows((TA, HP), lambda h, j: (j, h))],
        scratch_shapes=[pltpu.VMEM((TA, HP), F32), pltpu.VMEM((TA, HP), F32)], compiler_params=_params(2),
    )(q, do, lse, delta, k, v)


def _attn_out_fwd(o, h, wo, vec):
    S = h.shape[0]

    def body(o_ref, h_ref, wo_ref, vec_ref, ho_ref):
        ho_ref[...] = h_ref[...] + vec_ref[3:4, :] * _dot(o_ref[...], wo_ref[...])

    tok = lambda i: (i, 0)
    return pl.pallas_call(
        body, name="attn_out_fwd", grid=(S // TS,), out_shape=jax.ShapeDtypeStruct((S, D), F32),
        in_specs=[_rows((TS, NH * HP), tok), _rows((TS, D), tok), VMEM_FULL, _rows((8, D), lambda i: (0, 0))],
        out_specs=_rows((TS, D), tok), compiler_params=_params(1),
    )(o, h, wo, vec)


def _attn_out_bwd(dho, o, wo, vec):
    S = dho.shape[0]

    def body(dho_ref, o_ref, wo_ref, vec_ref, do_ref, dl_ref, dhb_ref):
        dho_v = dho_ref[...]
        dhb_ref[...] = dho_v.astype(BF)
        do = _dot_nt((vec_ref[3:4, :] * dho_v).astype(BF), wo_ref[...])
        do_ref[...] = do.astype(BF)
        prod = do * o_ref[...].astype(F32)
        for hd in range(NH):
            dl_ref[hd] = jnp.sum(prod[:, hd * HP:(hd + 1) * HP], axis=1, keepdims=True)

    tok = lambda i: (i, 0)
    return pl.pallas_call(
        body, name="attn_out_bwd", grid=(S // TS,),
        out_shape=[jax.ShapeDtypeStruct((S, NH * HP), BF), jax.ShapeDtypeStruct((NH, S, 1), F32),
                   jax.ShapeDtypeStruct((S, D), BF)],
        in_specs=[_rows((TS, D), tok), _rows((TS, NH * HP), tok), VMEM_FULL, _rows((8, D), lambda i: (0, 0))],
        out_specs=[_rows((TS, NH * HP), tok), _rows((NH, TS, 1), lambda i: (0, i, 0)), _rows((TS, D), tok)],
        compiler_params=_params(1),
    )(dho, o, wo, vec)


def _final(h, target, fg):
    S = h.shape[0]

    def body(h_ref, t_ref, g_ref, dh_ref, ps_ref):
        hv, g = h_ref[...], g_ref[0:1, :]
        r = lax.rsqrt(_mean1(hv * hv) + EPS)
        xhat = hv * r
        err = xhat * g - t_ref[...]
        loss = 0.5 * jnp.sum(_mean1(err * err), axis=0, keepdims=True)
        dy = err * (1.0 / D)
        dxhat = dy * g
        dh_ref[...] = r * (dxhat - xhat * _mean1(dxhat * xhat))
        _accumulate(ps_ref, pl.program_id(0) == 0, [_sum0(dy * xhat), jnp.broadcast_to(loss, (1, D))])

    tok = lambda i: (i, 0)
    return pl.pallas_call(
        body, name="final_loss", grid=(S // TS,),
        out_shape=[jax.ShapeDtypeStruct((S, D), F32), jax.ShapeDtypeStruct((8, D), F32)],
        in_specs=[_rows((TS, D), tok), _rows((TS, D), tok), _rows((8, D), lambda i: (0, 0))],
        out_specs=[_rows((TS, D), tok), _rows((8, D), lambda i: (0, 0))], compiler_params=_params(1),
    )(h, target, fg)


def _row_tile(r):
    if r <= 256:
        return r
    for t in range(256, 7, -8):
        if r % t == 0:
            return t
    return r


def _adamw(parts, w, m, v):
    P, R, C = parts.shape
    tr = _row_tile(R)

    def body(p_ref, w_ref, m_ref, v_ref, g_ref, d_ref, mo_ref, vo_ref):
        g = p_ref[0].astype(F32)
        for k in range(1, P):
            g = g + p_ref[k].astype(F32)
        g_ref[...] = g
        m2 = B1 * m_ref[...] + (1.0 - B1) * g
        v2 = B2 * v_ref[...] + (1.0 - B2) * (g * g)
        mo_ref[...] = m2
        vo_ref[...] = v2
        m_hat = m2 / (1.0 - B1 ** STEP)
        v_hat = v2 / (1.0 - B2 ** STEP)
        d_ref[...] = -LR * (m_hat / (jnp.sqrt(v_hat) + EPS_ADAM) + WD * w_ref[...])

    blk = _rows((tr, C), lambda i: (i, 0))
    return pl.pallas_call(
        body, name="adamw", grid=(R // tr,), out_shape=[jax.ShapeDtypeStruct((R, C), F32)] * 4,
        in_specs=[_rows((P, tr, C), lambda i: (0, i, 0)), blk, blk, blk], out_specs=[blk] * 4,
        compiler_params=_params(1),
    )(parts, w, m, v)


_WEIGHTS = ['ada_w', 'ada_b', 'norm_g', 'ffn_w13', 'ffn_w2', 'conv_w_pw1', 'conv_b_pw1', 'conv_w_dw', 'conv_b_dw',
            'conv_ln_g', 'conv_ln_b', 'conv_w_pw2', 'conv_b_pw2', 'kv_ada_w', 'kv_ada_b', 'kv_norm_g', 'w_kv_a',
            'kv_a_norm_g', 'w_kv_b', 'w_q_a', 'q_a_norm_g', 'w_q_b', 'w_o', 'final_norm_g']


def _vec(rows):
    rows = [r.reshape(1, -1).astype(F32) for r in rows]
    return jnp.concatenate(rows + [jnp.zeros((8 - len(rows), rows[0].shape[1]), F32)], axis=0)


def kernel(x, c, positions, ada_w, ada_b, norm_g, ffn_w13, ffn_w2, conv_w_pw1, conv_b_pw1, conv_w_dw, conv_b_dw, conv_ln_g, conv_ln_b, conv_w_pw2, conv_b_pw2, kv_ada_w, kv_ada_b, kv_norm_g, w_kv_a, kv_a_norm_g, w_kv_b, w_q_a, q_a_norm_g, w_q_b, w_o, final_norm_g, loss_target, m_ada_w, m_ada_b, m_norm_g, m_ffn_w13, m_ffn_w2, m_conv_w_pw1, m_conv_b_pw1, m_conv_w_dw, m_conv_b_dw, m_conv_ln_g, m_conv_ln_b, m_conv_w_pw2, m_conv_b_pw2, m_kv_ada_w, m_kv_ada_b, m_kv_norm_g, m_w_kv_a, m_kv_a_norm_g, m_w_kv_b, m_w_q_a, m_q_a_norm_g, m_w_q_b, m_w_o, m_final_norm_g, v_ada_w, v_ada_b, v_norm_g, v_ffn_w13, v_ffn_w2, v_conv_w_pw1, v_conv_b_pw1, v_conv_w_dw, v_conv_b_dw, v_conv_ln_g, v_conv_ln_b, v_conv_w_pw2, v_conv_b_pw2, v_kv_ada_w, v_kv_ada_b, v_kv_norm_g, v_w_kv_a, v_kv_a_norm_g, v_w_kv_b, v_w_q_a, v_q_a_norm_g, v_w_q_b, v_w_o, v_final_norm_g):
    given = dict(locals())
    S = x.shape[1]
    me = 4 * lax.axis_index("x") + 2 * lax.axis_index("y") + lax.axis_index("c")

    small = jnp.concatenate([
        conv_w_dw[0], conv_b_dw, conv_ln_g, conv_ln_b, conv_b_pw2,
        norm_g.reshape(6, 128), conv_b_pw1.reshape(2, 128),
        c.reshape(8, 128), jnp.zeros((5, 128), F32)], axis=0)
    bf = lambda w: w.astype(BF)
    full_w13 = lambda g: jnp.transpose(g.reshape(N_DEV, D, 704), (1, 0, 2)).reshape(D, 2 * DFF)
    full_w2 = lambda g: g.reshape(DFF, D)
    got = _gather_two_level([small, bf(ffn_w13[0, 0]), bf(ffn_w2[0, 0])], "gather_first")
    w13_00, w2_00 = full_w13(got[1]), full_w2(got[2])
    sm = got[0]
    chan = lambda lo, hi: jnp.moveaxis(sm[:, lo:hi, :], 0, 1).reshape(hi - lo, D)
    w_dw_f, b_dw_f, ln_g_f, ln_b_f, b_pw2_f = chan(0, 31), chan(31, 32), chan(32, 33), chan(33, 34), chan(34, 35)
    norm_f = chan(35, 41).reshape(2, 3, D)
    b_pw1_f = sm[:, 41:43, :].reshape(1, 2 * D)
    c_all = sm[:, 43:51, :].reshape(N_DEV, D)

    n_ada = ada_w.shape[2]
    n_kva = kv_ada_w.shape[1]
    modp = _mod_fwd(c_all, ada_w[0], ada_w[1], kv_ada_w)
    (modr,) = _exchange([(modp.reshape(N_DEV, 1, 2 * n_ada + n_kva), "scatter")], "scatter_mod")
    modr = modr[:, 0, :]
    mod = jnp.transpose(modr[:, :2 * n_ada].reshape(N_DEV, 2, n_ada), (1, 0, 2)).reshape(2, 9 * D) + ada_b
    mod = mod.reshape(2, 9, D)
    kvmod = (modr[:, 2 * n_ada:].reshape(2 * D) + kv_ada_b).reshape(2, D)

    def sub_vec(l, idx):
        return _vec([norm_f[l, idx], mod[l, 3 * idx], mod[l, 3 * idx + 1], mod[l, 3 * idx + 2]])

    vec_kv = _vec([kv_norm_g, kvmod[0], kvmod[1]])
    cw = jnp.concatenate([w_dw_f, b_dw_f, ln_g_f, ln_b_f, b_pw2_f, mod[0, 5].reshape(1, D), jnp.zeros((4, D), F32)], axis=0)
    b1v = _vec([b_pw1_f])
    g_kva = _vec([kv_a_norm_g])
    g_qa = _vec([q_a_norm_g[0]])
    fgv = _vec([final_norm_g])

    inv_freq = 10000.0 ** (-jnp.arange(0, ROPE, 2, dtype=F32) / ROPE)
    ang = positions[0].astype(F32)[:, None] * inv_freq
    cs, sn = jnp.cos(ang), jnp.sin(ang)
    z16, z32, z64 = jnp.zeros((S, 16), F32), jnp.zeros((S, 32), F32), jnp.zeros((S, 64), F32)
    rope = jnp.concatenate([jnp.ones((S, 64), F32), cs, cs, z32,
                            z64, z16, sn, z32,
                            z64, -sn, z16, z32], axis=1)

    def merged(blocks, name):
        return [jnp.swapaxes(m, 0, 1).reshape((N_DEV,) + m.shape[2:]) for m in _sibling_merge(blocks, name)]

    h0 = x[0]
    group1 = [bf(conv_w_pw1[0]), bf(conv_w_pw2[0]), bf(ffn_w13[0, 1]), bf(ffn_w2[0, 1])]
    (h1, u00, a00, b00), blocks1 = _ffn_fwd(h0, sub_vec(0, 0), w13_00, w2_00, comm=(group1, False))
    g_pw1, g_pw2, g_w13, g_w2 = merged(blocks1, "merge_group1")
    pw1_f = jnp.transpose(g_pw1, (1, 0, 2)).reshape(D, 2 * D)
    pw2_f = g_pw2.reshape(D, D)
    w13_01, w2_01 = full_w13(g_w13), full_w2(g_w2)
    hn_c, pre = _conv_in_fwd(h1, sub_vec(0, 1), pw1_f, b1v)
    group2 = [bf(w_kv_a), bf(w_kv_b), bf(ffn_w13[1, 0]), bf(ffn_w2[1, 0])]
    (u2, z_c, h2), blocks2 = _conv_out_fwd(pre, h1, cw, pw2_f, comm=(group2, False))
    g_kva_w, g_kvb_w, g_w13, g_w2 = merged(blocks2, "merge_group2")
    wkva = g_kva_w.reshape(D, KV_LORA + ROPE)
    wkva_f = jnp.concatenate([wkva[:, :KV_LORA], jnp.zeros((D, 64), BF), wkva[:, KV_LORA:], jnp.zeros((D, 32), BF)], axis=1)
    wkvb_f = jnp.transpose(g_kvb_w, (1, 0, 2)).reshape(KV_LORA, NH * HP)
    w13_10, w2_10 = full_w13(g_w13), full_w2(g_w2)
    group3 = [bf(w_q_a[0]), bf(w_q_b[0]), bf(w_o[0]), bf(ffn_w13[1, 1]), bf(ffn_w2[1, 1])]
    (h3, u01, a01, b01), blocks3 = _ffn_fwd(h2, sub_vec(0, 2), w13_01, w2_01, comm=(group3, False))
    g_qa_w, g_qb_w, g_wo, g_w13, g_w2 = merged(blocks3, "merge_group3")
    wqa_f = g_qa_w.reshape(D, Q_LORA)
    wqb = jnp.transpose(g_qb_w, (1, 0, 2)).reshape(Q_LORA, NH, 96)
    wqb_f = jnp.pad(wqb, ((0, 0), (0, 0), (0, HP - 96))).reshape(Q_LORA, NH * HP)
    wo_f = jnp.pad(g_wo.reshape(NH, 64, D), ((0, 0), (64, 0), (0, 0))).reshape(NH * HP, D)
    w13_11, w2_11 = full_w13(g_w13), full_w2(g_w2)
    hn_kv, ckv, ckn, k_all, v_all = _kv_fwd(h3, vec_kv, wkva_f, g_kva, wkvb_f, rope)
    (h4, u10, a10, b10), _ = _ffn_fwd(h3, sub_vec(1, 0), w13_10, w2_10)
    hn_q, qa, qan, q_all = _q_fwd(h4, sub_vec(1, 1), wqa_f, g_qa, wqb_f, rope)
    o_all, lse = _attn_fwd(q_all, k_all, v_all)
    h5 = _attn_out_fwd(o_all, h4, wo_f, sub_vec(1, 1))
    (h6, u11, a11, b11), _ = _ffn_fwd(h5, sub_vec(1, 2), w13_11, w2_11)

    dh6, ps_fin = _final(h6, loss_target[0], fgv)
    loss = lax.psum(ps_fin[1, 0], ("x", "y", "c"))

    core = lax.axis_index("c").reshape(1).astype(jnp.int32)

    def pair_sums(sends, name):
        by_core = [s.reshape((4, 2) + s.shape[1:]).swapaxes(0, 1) for s in sends]
        from_sibling = _pair_swap(by_core, name)
        return [_pair_add(core, a.reshape(2, -1, a.shape[-1]), b.reshape(-1, b.shape[-1])).reshape(b.shape)
                for a, b in zip(by_core, from_sibling)]

    def ffn_back(dho, h_in, u, a, b, l, i, w13, w2, comm=None):
        vec = sub_vec(l, 2 * i)
        (dh, da, db, t, dhb, ps), reduced = _ffn_bwd(dho, h_in, a, b, vec, w13, w2, comm=comm)
        dwa = _wgrad(u, da, D, CH, "wgrad_w13")
        dwb = _wgrad(u, db, D, CH, "wgrad_w13")
        dw2, gs = _wgrad(t, dhb, CH, D, "wgrad_w2", gate=(w2, _vec([0.5 * vec[3]])))
        dgate = 0.5 * jnp.sum(gs[:, 0, :], axis=0)
        send13 = jnp.transpose(jnp.concatenate([dwa, dwb], axis=1).reshape(D, N_DEV, 704), (1, 0, 2))
        return dh, send13, dw2.reshape(N_DEV, 352, D), ps, dgate, reduced

    dh5, s13_11, s2_11, ps11, dg11, _ = ffn_back(dh6, h5, u11, a11, b11, 1, 1, w13_11, w2_11)
    vec_m1 = sub_vec(1, 1)
    do_all, delta, dhb5 = _attn_out_bwd(dh5, o_all, wo_f, vec_m1)
    dwo_p, gs_o = _wgrad(o_all, dhb5, D, D, "wgrad_wo", gate=(wo_f, _vec([vec_m1[3]])))
    dgm1 = jnp.sum(gs_o[:, 0, :], axis=0)
    dq_all, dk_all, dv_all = _attn_bwd(q_all, k_all, v_all, do_all, lse, delta)
    dh4, dqb, dqab, ps_q, ps_q2 = _q_bwd(dq_all, h4, qa, dh5, vec_m1, wqa_f, g_qa, wqb_f, rope)
    dwqb_p = _wgrad(qan, dqb, Q_LORA, D, "wgrad_wqb")
    dwqa = _wgrad(hn_q, dqab, D, Q_LORA, "wgrad_wqa")
    dh3a, s13_10, s2_10, ps10, dg10, _ = ffn_back(dh4, h3, u10, a10, b10, 1, 0, w13_10, w2_10)
    sums_a = pair_sums([
        s13_11, s2_11, s13_10, s2_10, dwqa.reshape(N_DEV, 128, Q_LORA),
        jnp.transpose(dwqb_p.reshape(Q_LORA, NH, HP)[:, :, :96].reshape(Q_LORA, N_DEV, 192), (1, 0, 2)),
        dwo_p.reshape(NH, HP, D)[:, 64:, :].reshape(N_DEV, 128, D)], "pair_swap_a")
    dh3, dkva, dkvb, ps_kv, ps_kv2 = _kv_bwd(dk_all, dv_all, h3, ckv, dh3a, vec_kv, wkva_f, g_kva, wkvb_f, rope)
    dwkva_p = _wgrad(hn_kv, dkva, D, KVA_P, "wgrad_wkva")
    dwkvb = _wgrad(ckn, dkvb, KV_LORA, D, "wgrad_wkvb")
    dh2, s13_01, s2_01, ps01, dg01, red_a = ffn_back(dh3, h2, u01, a01, b01, 0, 1, w13_01, w2_01, comm=(sums_a, True))
    vec_m0 = sub_vec(0, 1)
    du2, dhb2, ps_co = _conv_out_bwd(dh2, u2, cw, pw2_f)
    dpw2, gs_c = _wgrad(z_c, dhb2, D, D, "wgrad_pw2", gate=(pw2_f, _vec([vec_m0[3]])))
    dgm0 = jnp.sum(gs_c[:, 0, :], axis=0) + b_pw2_f[0] * ps_co[2]
    dh1, dpre, ps_ci, ps_dw, ps_b1 = _conv_in_bwd(du2, pre, h1, dh2, vec_m0, cw, pw1_f)
    dpw1 = _wgrad(hn_c, dpre, D, D, "wgrad_pw1")
    sums_b = pair_sums([
        s13_01, s2_01, jnp.transpose(dpw1.reshape(D, N_DEV, 256), (1, 0, 2)), dpw2.reshape(N_DEV, 128, D),
        jnp.concatenate([dwkva_p[:, :KV_LORA], dwkva_p[:, KV_LORA + 64:KV_LORA + 96]], axis=1).reshape(N_DEV, 128, KV_LORA + ROPE),
        jnp.transpose(dwkvb.reshape(KV_LORA, N_DEV, 256), (1, 0, 2))], "pair_swap_b")
    dh0, s13_00, s2_00, ps00, dg00, red_b = ffn_back(dh1, h0, u00, a00, b00, 0, 0, w13_00, w2_00, comm=(sums_b, True))
    red_c = _chip_scatter(pair_sums([s13_00, s2_00], "pair_swap_c"), "chip_scatter_last")
    r13_11, r2_11, r13_10, r2_10, r_wqa, r_wqb, r_wo = red_a
    r13_01, r2_01, r_pw1, r_pw2, r_wkva, r_wkvb = red_b
    r13_00, r2_00 = red_c

    dmod = jnp.stack([
        jnp.stack([ps00[1], ps00[2], dg00, ps_ci[1], ps_ci[2], dgm0, ps01[1], ps01[2], dg01]),
        jnp.stack([ps10[1], ps10[2], dg10, ps_q[1], ps_q[2], dgm1, ps11[1], ps11[2], dg11])])
    dnorm = jnp.stack([ps00[0], ps_ci[0], ps01[0], ps10[0], ps_q[0], ps11[0]])
    pieces = [dnorm, ps_b1[0], ps_dw[0:31], ps_dw[31], ps_co[0], ps_co[1], vec_m0[3] * ps_co[2],
              ps_kv[0], ps_kv2[0], ps_q2[0], ps_fin[0], dmod, ps_kv[1], ps_kv[2]]
    sizes = [int(np.prod(p.shape)) for p in pieces]
    offs = np.concatenate([[0], np.cumsum(sizes)]).astype(int)
    flat = jnp.concatenate([p.reshape(-1) for p in pieces]).reshape(-1, 128)
    (part,) = _exchange([(flat, "gather")], "gather_partials")
    part = part.reshape(N_DEV, -1)

    def piece(i, rows, cols):
        return part[:, offs[i]:offs[i + 1]].reshape(N_DEV, rows, cols)

    def mine(i, rows):
        return lax.dynamic_slice_in_dim(piece(i, rows, D), me * 128, 128, axis=2)

    dmod_all = piece(11, 2, 9 * D)
    c_t = jnp.transpose(c_all)
    g_ada = jnp.stack([_mod_wgrad(c_t, lax.dynamic_slice_in_dim(dmod_all[:, l], me * n_ada, n_ada, axis=1))
                       for l in range(2)])
    dkvmod_all = jnp.concatenate([piece(12, 1, D)[:, 0], piece(13, 1, D)[:, 0]], axis=1)
    g_kvada = _mod_wgrad(c_t, lax.dynamic_slice_in_dim(dkvmod_all, me * n_kva, n_kva, axis=1))

    parts = {
        'ada_w': g_ada.reshape(1, 2 * D, n_ada),
        'ada_b': dmod_all,
        'norm_g': mine(0, 6),
        'ffn_w13': jnp.stack([r13_00, r13_01, r13_10, r13_11], axis=1).reshape(4, 4 * D, 704),
        'ffn_w2': jnp.stack([r2_00, r2_01, r2_10, r2_11], axis=1).reshape(4, 4 * 352, D),
        'conv_w_pw1': r_pw1,
        'conv_b_pw1': lax.dynamic_slice_in_dim(piece(1, 1, 2 * D), me * 256, 256, axis=2),
        'conv_w_dw': mine(2, 31),
        'conv_b_dw': mine(3, 1),
        'conv_ln_g': mine(4, 1),
        'conv_ln_b': mine(5, 1),
        'conv_w_pw2': r_pw2,
        'conv_b_pw2': mine(6, 1),
        'kv_ada_w': g_kvada.reshape(1, D, n_kva),
        'kv_ada_b': dkvmod_all.reshape(N_DEV, 1, 2 * D),
        'kv_norm_g': piece(7, 1, D),
        'w_kv_a': r_wkva,
        'kv_a_norm_g': piece(8, 1, KV_LORA),
        'w_kv_b': r_wkvb,
        'w_q_a': r_wqa,
        'q_a_norm_g': piece(9, 1, Q_LORA),
        'w_q_b': r_wqb,
        'w_o': r_wo,
        'final_norm_g': piece(10, 1, D),
    }
    grads, deltas, new_m, new_v = [], [], [], []
    for name in _WEIGHTS:
        w = given[name]
        p = parts[name]
        shape2 = p.shape[1:]
        g, dlt, m2, v2 = _adamw(p, w.reshape(shape2), given['m_' + name].reshape(shape2), given['v_' + name].reshape(shape2))
        grads.append(g.reshape(w.shape))
        deltas.append(dlt.reshape(w.shape))
        new_m.append(m2.reshape(w.shape))
        new_v.append(v2.reshape(w.shape))
    return (loss, dh0.reshape(1, S, D), *grads, *deltas, *new_m, *new_v)
```

```python
import functools

import numpy as np
import jax
import jax.numpy as jnp
from jax import lax
from jax.experimental import pallas as pl
from jax.experimental.pallas import tpu as pltpu

F32, BF = jnp.float32, jnp.bfloat16

D = 1024
DFF = 2816
CH = 1408
NH = 16
HP = 128
KV_LORA, Q_LORA, ROPE = 256, 512, 32
KVA_P = 384
CONV_W = 31
HALO = 32
EPS = 1e-6
SM_SCALE = float((64 + 32) ** -0.5)
LOG2E = 1.4426950408889634
EXP2_SCALE = SM_SCALE * LOG2E
NEG = -1e30
N_DEV = 8
MESH = pl.DeviceIdType.MESH

TS = 256
TA = 1024
TA_FWD = 1024
ATT_ROWS = 256
TW = 2048
VMEM_LIMIT = 56 * 1024 * 1024

LR, B1, B2, EPS_ADAM, WD, STEP = 0.001, 0.9, 0.999, 1e-08, 0.01, 10

VMEM_FULL = pl.BlockSpec(memory_space=pltpu.VMEM)
HBM_FULL = pl.BlockSpec(memory_space=pltpu.HBM)


def _params(n_grid):
    return pltpu.CompilerParams(dimension_semantics=("arbitrary",) * n_grid, vmem_limit_bytes=VMEM_LIMIT)


def _dot(a, b):
    return jnp.dot(a, b, preferred_element_type=F32)


def _dot_nt(a, b):
    return lax.dot_general(a, b, (((1,), (1,)), ((), ())), preferred_element_type=F32)


def _dot_tn(a, b):
    return lax.dot_general(a, b, (((0,), (0,)), ((), ())), preferred_element_type=F32)


def _sum0(x):
    return jnp.sum(x, axis=0, keepdims=True)


def _mean1(x):
    return jnp.mean(x, axis=-1, keepdims=True)


def _sigmoid(x):
    return jax.nn.sigmoid(x)


def _rows(shape, imap):
    return pl.BlockSpec(shape, imap)


def _norm_mod(h, vec):
    r = lax.rsqrt(_mean1(h * h) + EPS)
    xhat = h * r
    u = (xhat * vec[0:1]) * (1.0 + vec[2:3]) + vec[1:2]
    return u, xhat, r


def _norm_mod_bwd(du, xhat, r, vec):
    g = vec[0:1]
    dxn = du * (1.0 + vec[2:3])
    dsh = _sum0(du)
    dsc = _sum0(du * (xhat * g))
    dg = _sum0(dxn * xhat)
    dxhat = dxn * g
    dh = r * (dxhat - xhat * _mean1(dxhat * xhat))
    return dh, dg, dsh, dsc


def _rms_bwd(dy, x, r, g):
    xhat = x * r
    dg = _sum0(dy * xhat)
    dxhat = dy * g
    return r * (dxhat - xhat * _mean1(dxhat * xhat)), dg


def _accumulate(ref, first, rows):
    @pl.when(first)
    def _():
        ref[...] = jnp.zeros(ref.shape, ref.dtype)

    for i, row in enumerate(rows):
        ref[i:i + 1, :] += row


def _rope(x, tab):
    return x * tab[:, 0:HP] + pltpu.roll(x, 16, 1) * tab[:, HP:2 * HP] + pltpu.roll(x, HP - 16, 1) * tab[:, 2 * HP:3 * HP]


def _rope_t(dy, tab):
    return (dy * tab[:, 0:HP] + pltpu.roll(dy * tab[:, HP:2 * HP], HP - 16, 1)
            + pltpu.roll(dy * tab[:, 2 * HP:3 * HP], 16, 1))


def _exchange(items, name):
    n = len(items)

    def body(*refs):
        ins, outs = refs[:n], refs[n:2 * n]
        send_sems, recv_sems, local_sems = refs[2 * n:]
        x, y, c = lax.axis_index("x"), lax.axis_index("y"), lax.axis_index("c")
        me = 4 * x + 2 * y + c

        def source(j, dev):
            return ins[j] if items[j][1] == "gather" else ins[j].at[dev]

        own = [pltpu.make_async_copy(source(j, me), outs[j].at[me], local_sems.at[j]) for j in range(n)]
        for cp in own:
            cp.start()
        remote = []
        for d in range(1, N_DEV):
            px = 1 - x if d & 4 else x
            py = 1 - y if d & 2 else y
            pc = 1 - c if d & 1 else c
            peer = 4 * px + 2 * py + pc
            for j in range(n):
                pltpu.make_async_remote_copy(
                    src_ref=source(j, peer), dst_ref=outs[j].at[me], send_sem=send_sems.at[j, d - 1],
                    recv_sem=recv_sems.at[j, d - 1], device_id=(px, py, pc), device_id_type=MESH).start()
                remote.append(pltpu.make_async_remote_copy(
                    src_ref=source(j, peer), dst_ref=outs[j].at[peer], send_sem=send_sems.at[j, d - 1],
                    recv_sem=recv_sems.at[j, d - 1], device_id=(px, py, pc), device_id_type=MESH))
        for cp in remote:
            cp.wait_send()
            cp.wait_recv()
        for cp in own:
            cp.wait()

    out_shape = []
    for arr, mode in items:
        shp = (N_DEV,) + tuple(arr.shape) if mode == "gather" else tuple(arr.shape)
        out_shape.append(jax.ShapeDtypeStruct(shp, arr.dtype))
    return pl.pallas_call(
        body, name=name, out_shape=out_shape,
        in_specs=[HBM_FULL] * n, out_specs=[HBM_FULL] * n,
        scratch_shapes=[pltpu.SemaphoreType.DMA((n, N_DEV - 1)), pltpu.SemaphoreType.DMA((n, N_DEV - 1)),
                        pltpu.SemaphoreType.DMA((n,))],
        compiler_params=pltpu.CompilerParams(has_side_effects=True),
    )(*[a for a, _ in items])


def _gather_two_level(arrs, name):
    n = len(arrs)

    def body(*refs):
        ins, outs = refs[:n], refs[n:2 * n]
        send_sems, recv_sems, local_sems = refs[2 * n:]
        x, y, c = lax.axis_index("x"), lax.axis_index("y"), lax.axis_index("c")
        sibling = (x, y, 1 - c)
        chips = [(1 - x, y), (x, 1 - y), (1 - x, 1 - y)]

        def slot(j, px, py, pc):
            return outs[j].at[4 * px + 2 * py + pc]

        def copy(j, k, block, to, src=None):
            return pltpu.make_async_remote_copy(
                src_ref=slot(j, *block) if src is None else src, dst_ref=slot(j, *block),
                send_sem=send_sems.at[j, k], recv_sem=recv_sems.at[j, k], device_id=to, device_id_type=MESH)

        own = [pltpu.make_async_copy(ins[j], slot(j, x, y, c), local_sems.at[j]) for j in range(n)]
        for cp in own:
            cp.start()
        sent = []
        for j in range(n):
            sent.append(copy(j, 0, (x, y, c), sibling, src=ins[j]))
            sent += [copy(j, 1 + i, (x, y, c), (*chip, c), src=ins[j]) for i, chip in enumerate(chips)]
        for cp in sent:
            cp.start()
        for i, chip in enumerate(chips):
            for j in range(n):
                copy(j, 1 + i, (*chip, c), (x, y, c)).wait_recv()
                passed = copy(j, 4 + i, (*chip, c), sibling)
                passed.start()
                sent.append(passed)
        for j in range(n):
            copy(j, 0, (x, y, 1 - c), (x, y, c)).wait_recv()
            for i, chip in enumerate(chips):
                copy(j, 4 + i, (*chip, 1 - c), (x, y, c)).wait_recv()
        for cp in sent:
            cp.wait_send()
        for cp in own:
            cp.wait()

    return pl.pallas_call(
        body, name=name, out_shape=[jax.ShapeDtypeStruct((N_DEV,) + tuple(a.shape), a.dtype) for a in arrs],
        in_specs=[HBM_FULL] * n, out_specs=[HBM_FULL] * n,
        scratch_shapes=[pltpu.SemaphoreType.DMA((n, N_DEV - 1)), pltpu.SemaphoreType.DMA((n, N_DEV - 1)),
                        pltpu.SemaphoreType.DMA((n,))],
        compiler_params=pltpu.CompilerParams(has_side_effects=True),
    )(*arrs)


def _pair_swap(arrs, name):
    n = len(arrs)

    def body(*refs):
        ins, outs = refs[:n], refs[n:2 * n]
        send_sems, recv_sems = refs[2 * n:]
        x, y, c = lax.axis_index("x"), lax.axis_index("y"), lax.axis_index("c")
        copies = [pltpu.make_async_remote_copy(
            src_ref=ins[j].at[1 - c], dst_ref=outs[j], send_sem=send_sems.at[j], recv_sem=recv_sems.at[j],
            device_id=(x, y, 1 - c), device_id_type=MESH) for j in range(n)]
        for cp in copies:
            cp.start()
        for cp in copies:
            cp.wait_send()
            cp.wait_recv()

    return pl.pallas_call(
        body, name=name, out_shape=[jax.ShapeDtypeStruct(tuple(a.shape[1:]), a.dtype) for a in arrs],
        in_specs=[HBM_FULL] * n, out_specs=[HBM_FULL] * n,
        scratch_shapes=[pltpu.SemaphoreType.DMA((n,)), pltpu.SemaphoreType.DMA((n,))],
        compiler_params=pltpu.CompilerParams(has_side_effects=True),
    )(*arrs)


def _chip_comm(arrs, scatter):
    n = len(arrs)
    out_shapes = [jax.ShapeDtypeStruct(tuple(a.shape) if scatter else (2, 4) + tuple(a.shape), a.dtype) for a in arrs]
    scratch = [pltpu.SemaphoreType.DMA((n, 3)), pltpu.SemaphoreType.DMA((n, 3)), pltpu.SemaphoreType.DMA((n,))]

    def copies(ins, outs, send_sems, recv_sems, local_sems, arriving):
        x, y, c = lax.axis_index("x"), lax.axis_index("y"), lax.axis_index("c")
        here = 2 * x + y

        def source(j, chip):
            return ins[j].at[chip] if scatter else ins[j]

        def slot(j, chip):
            return outs[j].at[chip] if scatter else outs[j].at[c].at[chip]

        own = [pltpu.make_async_copy(source(j, here), slot(j, here), local_sems.at[j]) for j in range(n)]
        remote = []
        for i, (px, py) in enumerate([(1 - x, y), (x, 1 - y), (1 - x, 1 - y)]):
            there = 2 * px + py
            for j in range(n):
                remote.append(pltpu.make_async_remote_copy(
                    src_ref=source(j, there), dst_ref=slot(j, there if arriving else here),
                    send_sem=send_sems.at[j, i], recv_sem=recv_sems.at[j, i], device_id=(px, py, c),
                    device_id_type=MESH))
        return own, remote

    def start(*refs):
        own, remote = copies(*refs, arriving=False)
        for cp in own + remote:
            cp.start()

    def finish(*refs):
        own, remote = copies(*refs, arriving=True)
        for cp in remote:
            cp.wait_send()
            cp.wait_recv()
        for cp in own:
            cp.wait()

    return out_shapes, scratch, start, finish


def _chip_scatter(arrs, name):
    n = len(arrs)
    out_shapes, scratch, start, finish = _chip_comm(arrs, True)

    def body(*refs):
        start(refs[:n], refs[n:2 * n], *refs[2 * n:])
        finish(refs[:n], refs[n:2 * n], *refs[2 * n:])

    return pl.pallas_call(
        body, name=name, out_shape=out_shapes, in_specs=[HBM_FULL] * n, out_specs=[HBM_FULL] * n,
        scratch_shapes=scratch, compiler_params=pltpu.CompilerParams(has_side_effects=True),
    )(*arrs)


def _hosted_call(body, args, comm, *, name, n_steps, in_specs, out_specs, out_shape, scratch_shapes=()):
    if comm is None:
        res = pl.pallas_call(
            body, name=name, grid=(n_steps,), out_shape=list(out_shape), in_specs=list(in_specs),
            out_specs=list(out_specs), scratch_shapes=list(scratch_shapes), compiler_params=_params(1))(*args)
        return res, []
    arrs, scatter = comm
    c_shapes, c_scratch, start, finish = _chip_comm(arrs, scatter)
    n_in, n_out, n_sc, k = len(in_specs), len(out_specs), len(scratch_shapes), len(arrs)

    def hosting(*refs):
        ins, cin = refs[:n_in], refs[n_in:n_in + k]
        outs, cout = refs[n_in + k:n_in + k + n_out], refs[n_in + k + n_out:n_in + 2 * k + n_out]
        scratch = refs[n_in + 2 * k + n_out:n_in + 2 * k + n_out + n_sc]
        sems = refs[n_in + 2 * k + n_out + n_sc:]
        step = pl.program_id(0)

        @pl.when(step == 0)
        def _():
            start(cin, cout, *sems)

        body(*ins, *outs, *scratch)

        @pl.when(step == n_steps - 1)
        def _():
            finish(cin, cout, *sems)

    res = pl.pallas_call(
        hosting, name=name + "_hosting", grid=(n_steps,), out_shape=list(out_shape) + c_shapes,
        in_specs=list(in_specs) + [HBM_FULL] * k, out_specs=list(out_specs) + [HBM_FULL] * k,
        scratch_shapes=list(scratch_shapes) + c_scratch,
        compiler_params=pltpu.CompilerParams(dimension_semantics=("arbitrary",), vmem_limit_bytes=VMEM_LIMIT,
                                             has_side_effects=True))(*args, *arrs)
    return res[:n_out], res[n_out:]


def _sibling_merge(arrs, name):
    n = len(arrs)

    def body(*refs):
        bufs = refs[n:2 * n]
        send_sems, recv_sems = refs[2 * n:]
        x, y, c = lax.axis_index("x"), lax.axis_index("y"), lax.axis_index("c")
        waits = []
        for j in range(n):
            pltpu.make_async_remote_copy(
                src_ref=bufs[j].at[c], dst_ref=bufs[j].at[c], send_sem=send_sems.at[j], recv_sem=recv_sems.at[j],
                device_id=(x, y, 1 - c), device_id_type=MESH).start()
            waits.append(pltpu.make_async_remote_copy(
                src_ref=bufs[j].at[c], dst_ref=bufs[j].at[1 - c], send_sem=send_sems.at[j],
                recv_sem=recv_sems.at[j], device_id=(x, y, 1 - c), device_id_type=MESH))
        for cp in waits:
            cp.wait_send()
            cp.wait_recv()

    return pl.pallas_call(
        body, name=name, out_shape=[jax.ShapeDtypeStruct(tuple(a.shape), a.dtype) for a in arrs],
        in_specs=[HBM_FULL] * n, out_specs=[HBM_FULL] * n, input_output_aliases={j: j for j in range(n)},
        scratch_shapes=[pltpu.SemaphoreType.DMA((n,)), pltpu.SemaphoreType.DMA((n,))],
        compiler_params=pltpu.CompilerParams(has_side_effects=True),
    )(*arrs)


def _pair_add(core, a, b):
    _, R, C = a.shape
    tr = _row_tile(R)

    def body(core_ref, a_ref, b_ref, o_ref):
        o_ref[...] = (a_ref[...].astype(F32) + b_ref[...].astype(F32)).astype(BF)

    blk = _rows((tr, C), lambda i, core: (i, 0))
    grid_spec = pltpu.PrefetchScalarGridSpec(
        num_scalar_prefetch=1, grid=(R // tr,),
        in_specs=[_rows((None, tr, C), lambda i, core: (core[0], i, 0)), blk], out_specs=blk)
    return pl.pallas_call(
        body, name="pair_add", grid_spec=grid_spec, out_shape=jax.ShapeDtypeStruct((R, C), BF),
        compiler_params=_params(1),
    )(core, a, b)


def _mod_fwd(c_all, w0, w1, wkv):
    n0, n1, n2 = w0.shape[1], w1.shape[1], wkv.shape[1]

    def body(c_ref, w0_ref, w1_ref, w2_ref, o_ref):
        cc = c_ref[...]
        s = cc * _sigmoid(cc)
        o_ref[:, 0:n0] = _dot(s, w0_ref[...])
        o_ref[:, n0:n0 + n1] = _dot(s, w1_ref[...])
        o_ref[:, n0 + n1:n0 + n1 + n2] = _dot(s, w2_ref[...])

    return pl.pallas_call(
        body, name="mod_fwd", out_shape=jax.ShapeDtypeStruct((N_DEV, n0 + n1 + n2), F32),
        in_specs=[VMEM_FULL] * 4, out_specs=VMEM_FULL,
        compiler_params=pltpu.CompilerParams(vmem_limit_bytes=VMEM_LIMIT),
    )(c_all, w0, w1, wkv)


def _mod_wgrad(c_t, dm):
    C = dm.shape[1]
    tr = 256

    def body(ct_ref, dm_ref, o_ref):
        ct = ct_ref[...]
        s = ct * _sigmoid(ct)
        dmv = dm_ref[...]
        lane = lax.broadcasted_iota(jnp.int32, (tr, N_DEV), 1)
        acc = jnp.zeros((tr, C), F32)
        for r in range(N_DEV):
            col = jnp.sum(jnp.where(lane == r, s, 0.0), axis=1, keepdims=True)
            acc = acc + col * dmv[r:r + 1, :]
        o_ref[...] = acc

    return pl.pallas_call(
        body, name="mod_wgrad", grid=(D // tr,), out_shape=jax.ShapeDtypeStruct((D, C), F32),
        in_specs=[_rows((tr, N_DEV), lambda i: (i, 0)), _rows((N_DEV, C), lambda i: (0, 0))],
        out_specs=_rows((tr, C), lambda i: (i, 0)), compiler_params=_params(1),
    )(c_t, dm)


def _ffn_fwd(h, vec, w13, w2, comm=None):
    S = h.shape[0]

    def body(h_ref, vec_ref, w13_ref, w2_ref, ho_ref, u_ref, a_ref, b_ref):
        hv, vec = h_ref[...], vec_ref[...]
        u, _, _ = _norm_mod(hv, vec)
        ub = u.astype(BF)
        u_ref[...] = ub
        y = jnp.zeros((TS, D), F32)
        for k in range(DFF // CH):
            c0 = k * CH
            a = _dot(ub, w13_ref[:, c0:c0 + CH])
            b = _dot(ub, w13_ref[:, DFF + c0:DFF + c0 + CH])
            a_ref[:, c0:c0 + CH] = a.astype(BF)
            b_ref[:, c0:c0 + CH] = b.astype(BF)
            t = (a * _sigmoid(a)) * b
            y = y + _dot(t.astype(BF), w2_ref[c0:c0 + CH, :])
        ho_ref[...] = hv + (0.5 * vec[3:4]) * y

    tok = lambda i: (i, 0)
    return _hosted_call(
        body, (h, vec, w13, w2), comm, name="ffn_fwd", n_steps=S // TS,
        out_shape=[jax.ShapeDtypeStruct((S, D), F32), jax.ShapeDtypeStruct((S, D), BF),
                   jax.ShapeDtypeStruct((S, DFF), BF), jax.ShapeDtypeStruct((S, DFF), BF)],
        in_specs=[_rows((TS, D), tok), _rows((8, D), lambda i: (0, 0)), VMEM_FULL, VMEM_FULL],
        out_specs=[_rows((TS, D), tok), _rows((TS, D), tok), _rows((TS, DFF), tok), _rows((TS, DFF), tok)])


def _ffn_bwd(dho, h, a, b, vec, w13, w2, comm=None):
    S = h.shape[0]

    def body(dho_ref, h_ref, a_ref, b_ref, vec_ref, w13_ref, w2_ref,
             dh_ref, da_ref, db_ref, t_ref, dhb_ref, ps_ref):
        dho_v, vec = dho_ref[...], vec_ref[...]
        dhb_ref[...] = dho_v.astype(BF)
        dyb = ((0.5 * vec[3:4]) * dho_v).astype(BF)
        du = jnp.zeros((TS, D), F32)
        for k in range(DFF // CH):
            c0 = k * CH
            av = a_ref[:, c0:c0 + CH].astype(F32)
            bv = b_ref[:, c0:c0 + CH].astype(F32)
            dt = _dot_nt(dyb, w2_ref[c0:c0 + CH, :])
            sig = _sigmoid(av)
            sl = av * sig
            t_ref[:, c0:c0 + CH] = (sl * bv).astype(BF)
            dab = (dt * bv * (sig * (1.0 + av * (1.0 - sig)))).astype(BF)
            dbb = (dt * sl).astype(BF)
            da_ref[:, c0:c0 + CH] = dab
            db_ref[:, c0:c0 + CH] = dbb
            du = du + _dot_nt(dab, w13_ref[:, c0:c0 + CH]) + _dot_nt(dbb, w13_ref[:, DFF + c0:DFF + c0 + CH])
        _, xhat, r = _norm_mod(h_ref[...], vec)
        dhn, dg, dsh, dsc = _norm_mod_bwd(du, xhat, r, vec)
        dh_ref[...] = dho_v + dhn
        _accumulate(ps_ref, pl.program_id(0) == 0, [dg, dsh, dsc])

    tok = lambda i: (i, 0)
    return _hosted_call(
        body, (dho, h, a, b, vec, w13, w2), comm, name="ffn_bwd", n_steps=S // TS,
        out_shape=[jax.ShapeDtypeStruct((S, D), F32), jax.ShapeDtypeStruct((S, DFF), BF),
                   jax.ShapeDtypeStruct((S, DFF), BF), jax.ShapeDtypeStruct((S, DFF), BF),
                   jax.ShapeDtypeStruct((S, D), BF), jax.ShapeDtypeStruct((8, D), F32)],
        in_specs=[_rows((TS, D), tok), _rows((TS, D), tok), _rows((TS, DFF), tok), _rows((TS, DFF), tok),
                  _rows((8, D), lambda i: (0, 0)), VMEM_FULL, VMEM_FULL],
        out_specs=[_rows((TS, D), tok), _rows((TS, DFF), tok), _rows((TS, DFF), tok), _rows((TS, DFF), tok),
                   _rows((TS, D), tok), _rows((8, D), lambda i: (0, 0))])


def _wgrad(a, b, tm, tn, name, gate=None):
    S, M = a.shape
    N = b.shape[1]
    n_s = S // TW

    def body(*refs):
        if gate is None:
            a_ref, b_ref, o_ref, acc_ref = refs
        else:
            a_ref, b_ref, w_ref, sc_ref, o_ref, gs_ref, acc_ref = refs
        s = pl.program_id(2)

        @pl.when(s == 0)
        def _():
            acc_ref[...] = jnp.zeros((tm, tn), F32)

        acc_ref[...] += _dot_tn(a_ref[...], b_ref[...])

        @pl.when(s == n_s - 1)
        def _():
            acc = acc_ref[...]
            if gate is None:
                o_ref[...] = acc.astype(BF)
            else:
                o_ref[...] = (acc * sc_ref[0:1, :]).astype(BF)
                gs_ref[...] = jnp.broadcast_to(_sum0(acc * w_ref[...].astype(F32)), (8, tn))

    in_specs = [_rows((TW, tm), lambda m, n, s: (s, m)), _rows((TW, tn), lambda m, n, s: (s, n))]
    out_shape = [jax.ShapeDtypeStruct((M, N), BF)]
    out_specs = [_rows((tm, tn), lambda m, n, s: (m, n))]
    args = [a, b]
    if gate is not None:
        in_specs += [_rows((tm, tn), lambda m, n, s: (m, n)), _rows((8, tn), lambda m, n, s: (0, n))]
        out_shape.append(jax.ShapeDtypeStruct((M // tm, 8, N), F32))
        out_specs.append(_rows((None, 8, tn), lambda m, n, s: (m, 0, n)))
        args += list(gate)
    res = pl.pallas_call(
        body, name=name, grid=(M // tm, N // tn, n_s), out_shape=out_shape, in_specs=in_specs,
        out_specs=out_specs, scratch_shapes=[pltpu.VMEM((tm, tn), F32)], compiler_params=_params(3),
    )(*args)
    return res[0] if gate is None else (res[0], res[1])


def _conv_in_fwd(h, vec, w1, b1):
    S = h.shape[0]

    def body(h_ref, vec_ref, w_ref, b1_ref, hn_ref, pre_ref):
        u, _, _ = _norm_mod(h_ref[...], vec_ref[...])
        ub = u.astype(BF)
        hn_ref[...] = ub
        pre_ref[...] = _dot(ub, w_ref[...]) + b1_ref[0:1, :]

    tok = lambda i: (i, 0)
    return pl.pallas_call(
        body, name="conv_in_fwd", grid=(S // TS,),
        out_shape=[jax.ShapeDtypeStruct((S, D), BF), jax.ShapeDtypeStruct((S, 2 * D), F32)],
        in_specs=[_rows((TS, D), tok), _rows((8, D), lambda i: (0, 0)), VMEM_FULL, _rows((8, 2 * D), lambda i: (0, 0))],
        out_specs=[_rows((TS, D), tok), _rows((TS, 2 * D), tok)], compiler_params=_params(1),
    )(h, vec, w1, b1)


def _glu(pre):
    return pre[:, :D] * _sigmoid(pre[:, D:])


def _tap_groups(offset):
    groups = {}
    for j in range(CONV_W):
        off = offset(j)
        groups.setdefault(off % 8, []).append((off - off % 8, j))
    return [(phase, sorted(taps)) for phase, taps in sorted(groups.items())]


def _shift_window(dst_ref, win_ref, phase, rows):
    dst_ref[0:rows, :] = win_ref[phase:phase + rows, :]


def _layernorm(u2):
    mu = _mean1(u2)
    xc = u2 - mu
    rstd = lax.rsqrt(_mean1(xc * xc) + EPS)
    return xc * rstd, rstd


def _conv_out_fwd(pre, h, cw, w2, comm=None):
    S = h.shape[0]
    hb = TS // HALO

    def body(pre_ref, ph_ref, h_ref, cw_ref, w2_ref, u2_ref, z_ref, ho_ref, win_ref, sh_ref):
        i = pl.program_id(0)
        cwv = cw_ref[...]
        win_ref[0:HALO, :] = jnp.where(i > 0, _glu(ph_ref[...]), 0.0)
        win_ref[HALO:HALO + TS, :] = _glu(pre_ref[...])
        u2 = jnp.broadcast_to(cwv[31:32], (TS, D))
        for phase, taps in _tap_groups(lambda j: HALO - (CONV_W - 1) + j):
            _shift_window(sh_ref, win_ref, phase, taps[-1][0] + TS)
            for lo, j in taps:
                u2 = u2 + cwv[j:j + 1] * sh_ref[lo:lo + TS, :]
        u2_ref[...] = u2
        xh, _ = _layernorm(u2)
        un = xh * cwv[32:33] + cwv[33:34]
        zb = (un * _sigmoid(un)).astype(BF)
        z_ref[...] = zb
        y = _dot(zb, w2_ref[...]) + cwv[34:35]
        ho_ref[...] = h_ref[...] + cwv[35:36] * y

    tok = lambda i: (i, 0)
    return _hosted_call(
        body, (pre, pre, h, cw, w2), comm, name="conv_out_fwd", n_steps=S // TS,
        out_shape=[jax.ShapeDtypeStruct((S, D), F32), jax.ShapeDtypeStruct((S, D), BF), jax.ShapeDtypeStruct((S, D), F32)],
        in_specs=[_rows((TS, 2 * D), tok), _rows((HALO, 2 * D), lambda i: (jnp.maximum(i * hb - 1, 0), 0)),
                  _rows((TS, D), tok), _rows((40, D), lambda i: (0, 0)), VMEM_FULL],
        out_specs=[_rows((TS, D), tok), _rows((TS, D), tok), _rows((TS, D), tok)],
        scratch_shapes=[pltpu.VMEM((TS + HALO, D), F32)] * 2)


def _conv_out_bwd(dho, u2, cw, w2):
    S = dho.shape[0]

    def body(dho_ref, u2_ref, cw_ref, w2_ref, du2_ref, dhb_ref, ps_ref):
        dho_v, cwv = dho_ref[...], cw_ref[...]
        dhb_ref[...] = dho_v.astype(BF)
        dz = _dot_nt((cwv[35:36] * dho_v).astype(BF), w2_ref[...])
        xh, rstd = _layernorm(u2_ref[...])
        un = xh * cwv[32:33] + cwv[33:34]
        sig = _sigmoid(un)
        dun = dz * (sig * (1.0 + un * (1.0 - sig)))
        dxh = dun * cwv[32:33]
        du2_ref[...] = rstd * (dxh - _mean1(dxh) - xh * _mean1(dxh * xh))
        _accumulate(ps_ref, pl.program_id(0) == 0, [_sum0(dun * xh), _sum0(dun), _sum0(dho_v)])

    tok = lambda i: (i, 0)
    return pl.pallas_call(
        body, name="conv_out_bwd", grid=(S // TS,),
        out_shape=[jax.ShapeDtypeStruct((S, D), F32), jax.ShapeDtypeStruct((S, D), BF), jax.ShapeDtypeStruct((8, D), F32)],
        in_specs=[_rows((TS, D), tok), _rows((TS, D), tok), _rows((40, D), lambda i: (0, 0)), VMEM_FULL],
        out_specs=[_rows((TS, D), tok), _rows((TS, D), tok), _rows((8, D), lambda i: (0, 0))],
        compiler_params=_params(1),
    )(dho, u2, cw, w2)


def _conv_in_bwd(du2, pre, h, dho, vec, cw, w1):
    S = h.shape[0]
    n_t = S // TS
    hb = TS // HALO

    def body(du2_ref, dh2h_ref, pre_ref, ph_ref, h_ref, dho_ref, vec_ref, cw_ref, w1_ref,
             dh_ref, dpre_ref, ps_ref, pw_ref, pb_ref, winu_ref, wind_ref, sh_ref):
        i = pl.program_id(0)
        vec, cwv = vec_ref[...], cw_ref[...]
        pre = pre_ref[...]
        av, sg = pre[:, :D], _sigmoid(pre[:, D:])
        winu_ref[0:HALO, :] = jnp.where(i > 0, _glu(ph_ref[...]), 0.0)
        winu_ref[HALO:HALO + TS, :] = av * sg
        du2v = du2_ref[...]
        wind_ref[0:TS, :] = du2v
        wind_ref[TS:TS + HALO, :] = jnp.where(i < n_t - 1, dh2h_ref[...], 0.0)

        @pl.when(i == 0)
        def _():
            pw_ref[...] = jnp.zeros((32, D), F32)

        for phase, taps in _tap_groups(lambda j: HALO - (CONV_W - 1) + j):
            _shift_window(sh_ref, winu_ref, phase, taps[-1][0] + TS)
            for lo, j in taps:
                pw_ref[j:j + 1, :] += _sum0(du2v * sh_ref[lo:lo + TS, :])
        pw_ref[31:32, :] += _sum0(du2v)
        du1 = jnp.zeros((TS, D), F32)
        for phase, taps in _tap_groups(lambda j: CONV_W - 1 - j):
            _shift_window(sh_ref, wind_ref, phase, taps[-1][0] + TS)
            for lo, j in taps:
                du1 = du1 + cwv[j:j + 1] * sh_ref[lo:lo + TS, :]
        da = du1 * sg
        dg = du1 * av * sg * (1.0 - sg)
        dab, dgb = da.astype(BF), dg.astype(BF)
        dpre_ref[:, :D] = dab
        dpre_ref[:, D:] = dgb

        @pl.when(i == 0)
        def _():
            pb_ref[...] = jnp.zeros((8, 2 * D), F32)

        pb_ref[0:1, :D] += _sum0(da)
        pb_ref[0:1, D:] += _sum0(dg)
        du = _dot_nt(dab, w1_ref[:, :D]) + _dot_nt(dgb, w1_ref[:, D:])
        _, xhat, r = _norm_mod(h_ref[...], vec)
        dhn, dgn, dsh, dsc = _norm_mod_bwd(du, xhat, r, vec)
        dh_ref[...] = dho_ref[...] + dhn
        _accumulate(ps_ref, i == 0, [dgn, dsh, dsc])

    tok = lambda i: (i, 0)
    fixed = lambda i: (0, 0)
    return pl.pallas_call(
        body, name="conv_in_bwd", grid=(n_t,),
        out_shape=[jax.ShapeDtypeStruct((S, D), F32), jax.ShapeDtypeStruct((S, 2 * D), BF),
                   jax.ShapeDtypeStruct((8, D), F32), jax.ShapeDtypeStruct((32, D), F32),
                   jax.ShapeDtypeStruct((8, 2 * D), F32)],
        in_specs=[_rows((TS, D), tok), _rows((HALO, D), lambda i: (jnp.minimum((i + 1) * hb, S // HALO - 1), 0)),
                  _rows((TS, 2 * D), tok), _rows((HALO, 2 * D), lambda i: (jnp.maximum(i * hb - 1, 0), 0)),
                  _rows((TS, D), tok), _rows((TS, D), tok), _rows((8, D), fixed), _rows((40, D), fixed), VMEM_FULL],
        out_specs=[_rows((TS, D), tok), _rows((TS, 2 * D), tok), _rows((8, D), fixed), _rows((32, D), fixed),
                   _rows((8, 2 * D), fixed)],
        scratch_shapes=[pltpu.VMEM((TS + HALO, D), F32)] * 3,
        compiler_params=_params(1),
    )(du2, du2, pre, pre, h, dho, vec, cw, w1)


def _lane():
    return lax.broadcasted_iota(jnp.int32, (TS, HP), 1)


def _kv_fwd(h, vec, wkva, g2, wkvb, rope):
    S = h.shape[0]

    def body(h_ref, vec_ref, wa_ref, g2_ref, wb_ref, rope_ref, hn_ref, ckv_ref, ckn_ref, k_ref, v_ref):
        u, _, _ = _norm_mod(h_ref[...], vec_ref[...])
        ub = u.astype(BF)
        hn_ref[...] = ub
        kva = _dot(ub, wa_ref[...])
        ckv = kva[:, :KV_LORA]
        ckv_ref[...] = ckv
        r2 = lax.rsqrt(_mean1(ckv * ckv) + EPS)
        cknb = ((ckv * r2) * g2_ref[0:1, :]).astype(BF)
        ckn_ref[...] = cknb
        kvb = _dot(cknb, wb_ref[...])
        kpe = _rope(kva[:, KV_LORA:KVA_P], rope_ref[...])
        lane = _lane()
        for hd in range(NH):
            blk = kvb[:, hd * HP:(hd + 1) * HP]
            k_ref[:, hd * HP:(hd + 1) * HP] = jnp.where(lane < 64, blk, kpe).astype(BF)
            v_ref[:, hd * HP:(hd + 1) * HP] = jnp.where(lane >= 64, blk, 0.0).astype(BF)

    tok = lambda i: (i, 0)
    fixed = lambda i: (0, 0)
    return pl.pallas_call(
        body, name="kv_fwd", grid=(S // TS,),
        out_shape=[jax.ShapeDtypeStruct((S, D), BF), jax.ShapeDtypeStruct((S, KV_LORA), F32),
                   jax.ShapeDtypeStruct((S, KV_LORA), BF), jax.ShapeDtypeStruct((S, NH * HP), BF),
                   jax.ShapeDtypeStruct((S, NH * HP), BF)],
        in_specs=[_rows((TS, D), tok), _rows((8, D), fixed), VMEM_FULL, _rows((8, KV_LORA), fixed), VMEM_FULL,
                  _rows((TS, 3 * HP), tok)],
        out_specs=[_rows((TS, D), tok), _rows((TS, KV_LORA), tok), _rows((TS, KV_LORA), tok),
                   _rows((TS, NH * HP), tok), _rows((TS, NH * HP), tok)],
        compiler_params=_params(1),
    )(h, vec, wkva, g2, wkvb, rope)


def _kv_bwd(dk, dv, h, ckv, dho, vec, wkva, g2, wkvb, rope):
    S = h.shape[0]

    def body(dk_ref, dv_ref, h_ref, ckv_ref, dho_ref, vec_ref, wa_ref, g2_ref, wb_ref, rope_ref,
             dh_ref, dkva_ref, dkvb_ref, ps_ref, ps2_ref):
        i = pl.program_id(0)
        vec = vec_ref[...]
        lane = _lane()
        dkpe = jnp.zeros((TS, HP), F32)
        for hd in range(NH):
            dkh = dk_ref[:, hd * HP:(hd + 1) * HP]
            dvh = dv_ref[:, hd * HP:(hd + 1) * HP]
            dkvb_ref[:, hd * HP:(hd + 1) * HP] = jnp.where(lane < 64, dkh, dvh).astype(BF)
            dkpe = dkpe + jnp.where(lane >= 64, dkh, 0.0)
        dkpe = _rope_t(dkpe, rope_ref[...])
        dckn = _dot_nt(dkvb_ref[...], wb_ref[...])
        ckv = ckv_ref[...]
        r2 = lax.rsqrt(_mean1(ckv * ckv) + EPS)
        dckv, dg2 = _rms_bwd(dckn, ckv, r2, g2_ref[0:1, :])
        dkva_ref[:, :KV_LORA] = dckv.astype(BF)
        dkva_ref[:, KV_LORA:KVA_P] = dkpe.astype(BF)
        du = _dot_nt(dkva_ref[...], wa_ref[...])
        _, xhat, r = _norm_mod(h_ref[...], vec)
        dhn, dgn, dsh, dsc = _norm_mod_bwd(du, xhat, r, vec)
        dh_ref[...] = dho_ref[...] + dhn
        _accumulate(ps_ref, i == 0, [dgn, dsh, dsc])
        _accumulate(ps2_ref, i == 0, [dg2])

    tok = lambda i: (i, 0)
    fixed = lambda i: (0, 0)
    return pl.pallas_call(
        body, name="kv_bwd", grid=(S // TS,),
        out_shape=[jax.ShapeDtypeStruct((S, D), F32), jax.ShapeDtypeStruct((S, KVA_P), BF),
                   jax.ShapeDtypeStruct((S, NH * HP), BF), jax.ShapeDtypeStruct((8, D), F32),
                   jax.ShapeDtypeStruct((8, KV_LORA), F32)],
        in_specs=[_rows((TS, NH * HP), tok), _rows((TS, NH * HP), tok), _rows((TS, D), tok), _rows((TS, KV_LORA), tok),
                  _rows((TS, D), tok), _rows((8, D), fixed), VMEM_FULL, _rows((8, KV_LORA), fixed), VMEM_FULL,
                  _rows((TS, 3 * HP), tok)],
        out_specs=[_rows((TS, D), tok), _rows((TS, KVA_P), tok), _rows((TS, NH * HP), tok), _rows((8, D), fixed),
                   _rows((8, KV_LORA), fixed)],
        compiler_params=_params(1),
    )(dk, dv, h, ckv, dho, vec, wkva, g2, wkvb, rope)


def _q_fwd(h, vec, wqa, g2, wqb, rope):
    S = h.shape[0]

    def body(h_ref, vec_ref, wa_ref, g2_ref, wb_ref, rope_ref, hn_ref, qa_ref, qan_ref, q_ref):
        u, _, _ = _norm_mod(h_ref[...], vec_ref[...])
        ub = u.astype(BF)
        hn_ref[...] = ub
        qa = _dot(ub, wa_ref[...])
        qa_ref[...] = qa
        r2 = lax.rsqrt(_mean1(qa * qa) + EPS)
        qanb = ((qa * r2) * g2_ref[0:1, :]).astype(BF)
        qan_ref[...] = qanb
        q = _dot(qanb, wb_ref[...])
        tab = rope_ref[...]
        for hd in range(NH):
            q_ref[:, hd * HP:(hd + 1) * HP] = _rope(q[:, hd * HP:(hd + 1) * HP], tab).astype(BF)

    tok = lambda i: (i, 0)
    fixed = lambda i: (0, 0)
    return pl.pallas_call(
        body, name="q_fwd", grid=(S // TS,),
        out_shape=[jax.ShapeDtypeStruct((S, D), BF), jax.ShapeDtypeStruct((S, Q_LORA), F32),
                   jax.ShapeDtypeStruct((S, Q_LORA), BF), jax.ShapeDtypeStruct((S, NH * HP), BF)],
        in_specs=[_rows((TS, D), tok), _rows((8, D), fixed), VMEM_FULL, _rows((8, Q_LORA), fixed), VMEM_FULL,
                  _rows((TS, 3 * HP), tok)],
        out_specs=[_rows((TS, D), tok), _rows((TS, Q_LORA), tok), _rows((TS, Q_LORA), tok), _rows((TS, NH * HP), tok)],
        compiler_params=_params(1),
    )(h, vec, wqa, g2, wqb, rope)


def _q_bwd(dq, h, qa, dho, vec, wqa, g2, wqb, rope):
    S = h.shape[0]

    def body(dq_ref, h_ref, qa_ref, dho_ref, vec_ref, wa_ref, g2_ref, wb_ref, rope_ref,
             dh_ref, dqb_ref, dqa_ref, ps_ref, ps2_ref):
        i = pl.program_id(0)
        vec, tab = vec_ref[...], rope_ref[...]
        for hd in range(NH):
            dqb_ref[:, hd * HP:(hd + 1) * HP] = _rope_t(dq_ref[:, hd * HP:(hd + 1) * HP], tab).astype(BF)
        dqan = _dot_nt(dqb_ref[...], wb_ref[...])
        qa = qa_ref[...]
        r2 = lax.rsqrt(_mean1(qa * qa) + EPS)
        dqa, dg2 = _rms_bwd(dqan, qa, r2, g2_ref[0:1, :])
        dqab = dqa.astype(BF)
        dqa_ref[...] = dqab
        du = _dot_nt(dqab, wa_ref[...])
        _, xhat, r = _norm_mod(h_ref[...], vec)
        dhn, dgn, dsh, dsc = _norm_mod_bwd(du, xhat, r, vec)
        dh_ref[...] = dho_ref[...] + dhn
        _accumulate(ps_ref, i == 0, [dgn, dsh, dsc])
        _accumulate(ps2_ref, i == 0, [dg2])

    tok = lambda i: (i, 0)
    fixed = lambda i: (0, 0)
    return pl.pallas_call(
        body, name="q_bwd", grid=(S // TS,),
        out_shape=[jax.ShapeDtypeStruct((S, D), F32), jax.ShapeDtypeStruct((S, NH * HP), BF),
                   jax.ShapeDtypeStruct((S, Q_LORA), BF), jax.ShapeDtypeStruct((8, D), F32),
                   jax.ShapeDtypeStruct((8, Q_LORA), F32)],
        in_specs=[_rows((TS, NH * HP), tok), _rows((TS, D), tok), _rows((TS, Q_LORA), tok), _rows((TS, D), tok),
                  _rows((8, D), fixed), VMEM_FULL, _rows((8, Q_LORA), fixed), VMEM_FULL, _rows((TS, 3 * HP), tok)],
        out_specs=[_rows((TS, D), tok), _rows((TS, NH * HP), tok), _rows((TS, Q_LORA), tok), _rows((8, D), fixed),
                   _rows((8, Q_LORA), fixed)],
        compiler_params=_params(1),
    )(dq, h, qa, dho, vec, wqa, g2, wqb, rope)


def _attn_fwd(q, k, v):
    S = q.shape[0]
    TA = TA_FWD
    nq = S // TA
    rg = min(TA, ATT_ROWS)
    groups = TA // rg

    def softmax_pv(scores, vt, state, masks):
        out = []
        for g in range(groups):
            m, l, acc = state[g]
            s = scores[g] if masks is None else jnp.where(masks[g], scores[g], NEG)
            m_new = jnp.maximum(m, jnp.max(s, axis=1, keepdims=True))
            p = jnp.exp2((s - m_new) * EXP2_SCALE)
            alpha = jnp.exp2((m - m_new) * EXP2_SCALE)
            vg = vt[g] if isinstance(vt, list) else vt
            out.append((m_new, alpha * l + jnp.sum(p, axis=1, keepdims=True), alpha * acc + _dot(p.astype(BF), vg)))
        return tuple(out)

    def body(q_ref, k_ref, v_ref, o_ref, lse_ref):
        qi = pl.program_id(1)
        qs = [q_ref[g * rg:(g + 1) * rg, :] for g in range(groups)]

        def keys(j):
            return pl.ds(pl.multiple_of(j * TA, TA), TA)

        def scores_of(j):
            kt = k_ref[keys(j), :]
            return tuple(_dot_nt(qs[g], kt) for g in range(groups))

        state = tuple((jnp.full((rg, 1), NEG, F32), jnp.zeros((rg, 1), F32), jnp.zeros((rg, HP), F32))
                      for _ in range(groups))

        def step(kj, state):
            return softmax_pv(scores_of(kj), v_ref[keys(kj), :], state, None)

        state = lax.fori_loop(0, qi, step, state)
        kd, vd = k_ref[keys(qi), :], v_ref[keys(qi), :]
        ends = [(g + 1) * rg for g in range(groups)]
        diag_scores = tuple(_dot_nt(qs[g], kd[:ends[g]]) for g in range(groups))
        masks = [lax.broadcasted_iota(jnp.int32, (rg, ends[g]), 1)
                 <= lax.broadcasted_iota(jnp.int32, (rg, ends[g]), 0) + g * rg for g in range(groups)]
        final = softmax_pv(diag_scores, [vd[:ends[g]] for g in range(groups)], state, masks)
        for g in range(groups):
            m, l, acc = final[g]
            o_ref[g * rg:(g + 1) * rg, :] = (acc / l).astype(BF)
            lse_ref[g * rg:(g + 1) * rg, :] = m * EXP2_SCALE + jnp.log(l) * LOG2E

    return pl.pallas_call(
        body, name="attn_fwd", grid=(NH, nq),
        out_shape=[jax.ShapeDtypeStruct((S, NH * HP), BF), jax.ShapeDtypeStruct((NH, S, 1), F32)],
        in_specs=[_rows((TA, HP), lambda h, i: (i, h)), _rows((S, HP), lambda h, i: (0, h)),
                  _rows((S, HP), lambda h, i: (0, h))],
        out_specs=[_rows((TA, HP), lambda h, i: (i, h)), _rows((None, TA, 1), lambda h, i: (h, i, 0))],
        compiler_params=_params(2),
    )(q, k, v)


def _attn_bwd(q, k, v, do, lse, delta):
    S = q.shape[0]
    nq = S // TA
    rg = min(TA, ATT_ROWS)
    groups = TA // rg

    def body(q_ref, do_ref, lse_ref, dl_ref, k_ref, v_ref, dq_ref, dk_ref, dv_ref, dka_ref, dva_ref):
        kj = pl.program_id(1)

        @pl.when(kj == 0)
        def _():
            dq_ref[...] = jnp.zeros((S, HP), F32)

        dka_ref[...] = jnp.zeros((TA, HP), F32)
        dva_ref[...] = jnp.zeros((TA, HP), F32)
        kt, vt = k_ref[...], v_ref[...]

        def tile(qi, diagonal):
            rows = [pl.ds(pl.multiple_of(qi * TA + g * rg, rg), rg) for g in range(groups)]
            ends = [(g + 1) * rg if diagonal else TA for g in range(groups)]
            qg = [q_ref[r, :] for r in rows]
            dog = [do_ref[r, :] for r in rows]
            scores = [_dot_nt(qg[g], kt[:ends[g]]) for g in range(groups)]
            dps = [_dot_nt(dog[g], vt[:ends[g]]) for g in range(groups)]
            for g in range(groups):
                p = jnp.exp2(scores[g] * EXP2_SCALE - lse_ref[rows[g], :])
                if diagonal:
                    col = lax.broadcasted_iota(jnp.int32, (rg, ends[g]), 1)
                    row = lax.broadcasted_iota(jnp.int32, (rg, ends[g]), 0)
                    p = jnp.where(col <= row + g * rg, p, 0.0)
                ds = p * (dps[g] - dl_ref[rows[g], :])
                pb, dsb = p.astype(BF), ds.astype(BF)
                dva_ref[0:ends[g], :] += _dot_tn(pb, dog[g])
                dka_ref[0:ends[g], :] += _dot_tn(dsb, qg[g])
                dq_ref[rows[g], :] += _dot(dsb, kt[:ends[g]]) * SM_SCALE

        tile(kj, True)

        def step(qi, carry):
            tile(qi, False)
            return carry

        lax.fori_loop(kj + 1, nq, step, 0)
        dk_ref[...] = dka_ref[...] * SM_SCALE
        dv_ref[...] = dva_ref[...]

    head = lambda h, j: (0, h)
    col1 = lambda h, j: (h, 0, 0)
    return pl.pallas_call(
        body, name="attn_bwd", grid=(NH, nq), out_shape=[jax.ShapeDtypeStruct((S, NH * HP), F32)] * 3,
        in_specs=[_rows((S, HP), head), _rows((S, HP), head), _rows((None, S, 1), col1), _rows((None, S, 1), col1),
                  _rows((TA, HP), lambda h, j: (j, h)), _rows((TA, HP), lambda h, j: (j, h))],
        out_specs=[_rows((S, HP), head), _rows((TA, HP), lambda h, j: (j, h)), _rows((TA, HP), lambda h, j: (j, h))],
        scratch_shapes=[pltpu.VMEM((TA, HP), F32), pltpu.VMEM((TA, HP), F32)], compiler_params=_params(2),
    )(q, do, lse, delta, k, v)


def _attn_out_fwd(o, h, wo, vec):
    S = h.shape[0]

    def body(o_ref, h_ref, wo_ref, vec_ref, ho_ref):
        ho_ref[...] = h_ref[...] + vec_ref[3:4, :] * _dot(o_ref[...], wo_ref[...])

    tok = lambda i: (i, 0)
    return pl.pallas_call(
        body, name="attn_out_fwd", grid=(S // TS,), out_shape=jax.ShapeDtypeStruct((S, D), F32),
        in_specs=[_rows((TS, NH * HP), tok), _rows((TS, D), tok), VMEM_FULL, _rows((8, D), lambda i: (0, 0))],
        out_specs=_rows((TS, D), tok), compiler_params=_params(1),
    )(o, h, wo, vec)


def _attn_out_bwd(dho, o, wo, vec):
    S = dho.shape[0]

    def body(dho_ref, o_ref, wo_ref, vec_ref, do_ref, dl_ref, dhb_ref):
        dho_v = dho_ref[...]
        dhb_ref[...] = dho_v.astype(BF)
        do = _dot_nt((vec_ref[3:4, :] * dho_v).astype(BF), wo_ref[...])
        do_ref[...] = do.astype(BF)
        prod = do * o_ref[...].astype(F32)
        for hd in range(NH):
            dl_ref[hd] = jnp.sum(prod[:, hd * HP:(hd + 1) * HP], axis=1, keepdims=True)

    tok = lambda i: (i, 0)
    return pl.pallas_call(
        body, name="attn_out_bwd", grid=(S // TS,),
        out_shape=[jax.ShapeDtypeStruct((S, NH * HP), BF), jax.ShapeDtypeStruct((NH, S, 1), F32),
                   jax.ShapeDtypeStruct((S, D), BF)],
        in_specs=[_rows((TS, D), tok), _rows((TS, NH * HP), tok), VMEM_FULL, _rows((8, D), lambda i: (0, 0))],
        out_specs=[_rows((TS, NH * HP), tok), _rows((NH, TS, 1), lambda i: (0, i, 0)), _rows((TS, D), tok)],
        compiler_params=_params(1),
    )(dho, o, wo, vec)


def _final(h, target, fg):
    S = h.shape[0]

    def body(h_ref, t_ref, g_ref, dh_ref, ps_ref):
        hv, g = h_ref[...], g_ref[0:1, :]
        r = lax.rsqrt(_mean1(hv * hv) + EPS)
        xhat = hv * r
        err = xhat * g - t_ref[...]
        loss = 0.5 * jnp.sum(_mean1(err * err), axis=0, keepdims=True)
        dy = err * (1.0 / D)
        dxhat = dy * g
        dh_ref[...] = r * (dxhat - xhat * _mean1(dxhat * xhat))
        _accumulate(ps_ref, pl.program_id(0) == 0, [_sum0(dy * xhat), jnp.broadcast_to(loss, (1, D))])

    tok = lambda i: (i, 0)
    return pl.pallas_call(
        body, name="final_loss", grid=(S // TS,),
        out_shape=[jax.ShapeDtypeStruct((S, D), F32), jax.ShapeDtypeStruct((8, D), F32)],
        in_specs=[_rows((TS, D), tok), _rows((TS, D), tok), _rows((8, D), lambda i: (0, 0))],
        out_specs=[_rows((TS, D), tok), _rows((8, D), lambda i: (0, 0))], compiler_params=_params(1),
    )(h, target, fg)


def _row_tile(r):
    if r <= 256:
        return r
    for t in range(256, 7, -8):
        if r % t == 0:
            return t
    return r


def _adamw(parts, w, m, v):
    P, R, C = parts.shape
    tr = _row_tile(R)

    def body(p_ref, w_ref, m_ref, v_ref, g_ref, d_ref, mo_ref, vo_ref):
        g = p_ref[0].astype(F32)
        for k in range(1, P):
            g = g + p_ref[k].astype(F32)
        g_ref[...] = g
        m2 = B1 * m_ref[...] + (1.0 - B1) * g
        v2 = B2 * v_ref[...] + (1.0 - B2) * (g * g)
        mo_ref[...] = m2
        vo_ref[...] = v2
        m_hat = m2 / (1.0 - B1 ** STEP)
        v_hat = v2 / (1.0 - B2 ** STEP)
        d_ref[...] = -LR * (m_hat / (jnp.sqrt(v_hat) + EPS_ADAM) + WD * w_ref[...])

    blk = _rows((tr, C), lambda i: (i, 0))
    return pl.pallas_call(
        body, name="adamw", grid=(R // tr,), out_shape=[jax.ShapeDtypeStruct((R, C), F32)] * 4,
        in_specs=[_rows((P, tr, C), lambda i: (0, i, 0)), blk, blk, blk], out_specs=[blk] * 4,
        compiler_params=_params(1),
    )(parts, w, m, v)


_WEIGHTS = ['ada_w', 'ada_b', 'norm_g', 'ffn_w13', 'ffn_w2', 'conv_w_pw1', 'conv_b_pw1', 'conv_w_dw', 'conv_b_dw',
            'conv_ln_g', 'conv_ln_b', 'conv_w_pw2', 'conv_b_pw2', 'kv_ada_w', 'kv_ada_b', 'kv_norm_g', 'w_kv_a',
            'kv_a_norm_g', 'w_kv_b', 'w_q_a', 'q_a_norm_g', 'w_q_b', 'w_o', 'final_norm_g']


def _vec(rows):
    rows = [r.reshape(1, -1).astype(F32) for r in rows]
    return jnp.concatenate(rows + [jnp.zeros((8 - len(rows), rows[0].shape[1]), F32)], axis=0)


def kernel(x, c, positions, ada_w, ada_b, norm_g, ffn_w13, ffn_w2, conv_w_pw1, conv_b_pw1, conv_w_dw, conv_b_dw, conv_ln_g, conv_ln_b, conv_w_pw2, conv_b_pw2, kv_ada_w, kv_ada_b, kv_norm_g, w_kv_a, kv_a_norm_g, w_kv_b, w_q_a, q_a_norm_g, w_q_b, w_o, final_norm_g, loss_target, m_ada_w, m_ada_b, m_norm_g, m_ffn_w13, m_ffn_w2, m_conv_w_pw1, m_conv_b_pw1, m_conv_w_dw, m_conv_b_dw, m_conv_ln_g, m_conv_ln_b, m_conv_w_pw2, m_conv_b_pw2, m_kv_ada_w, m_kv_ada_b, m_kv_norm_g, m_w_kv_a, m_kv_a_norm_g, m_w_kv_b, m_w_q_a, m_q_a_norm_g, m_w_q_b, m_w_o, m_final_norm_g, v_ada_w, v_ada_b, v_norm_g, v_ffn_w13, v_ffn_w2, v_conv_w_pw1, v_conv_b_pw1, v_conv_w_dw, v_conv_b_dw, v_conv_ln_g, v_conv_ln_b, v_conv_w_pw2, v_conv_b_pw2, v_kv_ada_w, v_kv_ada_b, v_kv_norm_g, v_w_kv_a, v_kv_a_norm_g, v_w_kv_b, v_w_q_a, v_q_a_norm_g, v_w_q_b, v_w_o, v_final_norm_g):
    given = dict(locals())
    S = x.shape[1]
    me = 4 * lax.axis_index("x") + 2 * lax.axis_index("y") + lax.axis_index("c")

    small = jnp.concatenate([
        conv_w_dw[0], conv_b_dw, conv_ln_g, conv_ln_b, conv_b_pw2,
        norm_g.reshape(6, 128), conv_b_pw1.reshape(2, 128),
        c.reshape(8, 128), jnp.zeros((5, 128), F32)], axis=0)
    bf = lambda w: w.astype(BF)
    full_w13 = lambda g: jnp.transpose(g.reshape(N_DEV, D, 704), (1, 0, 2)).reshape(D, 2 * DFF)
    full_w2 = lambda g: g.reshape(DFF, D)
    got = _gather_two_level([small, bf(ffn_w13[0, 0]), bf(ffn_w2[0, 0])], "gather_first")
    w13_00, w2_00 = full_w13(got[1]), full_w2(got[2])
    sm = got[0]
    chan = lambda lo, hi: jnp.moveaxis(sm[:, lo:hi, :], 0, 1).reshape(hi - lo, D)
    w_dw_f, b_dw_f, ln_g_f, ln_b_f, b_pw2_f = chan(0, 31), chan(31, 32), chan(32, 33), chan(33, 34), chan(34, 35)
    norm_f = chan(35, 41).reshape(2, 3, D)
    b_pw1_f = sm[:, 41:43, :].reshape(1, 2 * D)
    c_all = sm[:, 43:51, :].reshape(N_DEV, D)

    n_ada = ada_w.shape[2]
    n_kva = kv_ada_w.shape[1]
    modp = _mod_fwd(c_all, ada_w[0], ada_w[1], kv_ada_w)
    (modr,) = _exchange([(modp.reshape(N_DEV, 1, 2 * n_ada + n_kva), "scatter")], "scatter_mod")
    modr = modr[:, 0, :]
    mod = jnp.transpose(modr[:, :2 * n_ada].reshape(N_DEV, 2, n_ada), (1, 0, 2)).reshape(2, 9 * D) + ada_b
    mod = mod.reshape(2, 9, D)
    kvmod = (modr[:, 2 * n_ada:].reshape(2 * D) + kv_ada_b).reshape(2, D)

    def sub_vec(l, idx):
        return _vec([norm_f[l, idx], mod[l, 3 * idx], mod[l, 3 * idx + 1], mod[l, 3 * idx + 2]])

    vec_kv = _vec([kv_norm_g, kvmod[0], kvmod[1]])
    cw = jnp.concatenate([w_dw_f, b_dw_f, ln_g_f, ln_b_f, b_pw2_f, mod[0, 5].reshape(1, D), jnp.zeros((4, D), F32)], axis=0)
    b1v = _vec([b_pw1_f])
    g_kva = _vec([kv_a_norm_g])
    g_qa = _vec([q_a_norm_g[0]])
    fgv = _vec([final_norm_g])

    inv_freq = 10000.0 ** (-jnp.arange(0, ROPE, 2, dtype=F32) / ROPE)
    ang = positions[0].astype(F32)[:, None] * inv_freq
    cs, sn = jnp.cos(ang), jnp.sin(ang)
    z16, z32, z64 = jnp.zeros((S, 16), F32), jnp.zeros((S, 32), F32), jnp.zeros((S, 64), F32)
    rope = jnp.concatenate([jnp.ones((S, 64), F32), cs, cs, z32,
                            z64, z16, sn, z32,
                            z64, -sn, z16, z32], axis=1)

    def merged(blocks, name):
        return [jnp.swapaxes(m, 0, 1).reshape((N_DEV,) + m.shape[2:]) for m in _sibling_merge(blocks, name)]

    h0 = x[0]
    group1 = [bf(conv_w_pw1[0]), bf(conv_w_pw2[0]), bf(ffn_w13[0, 1]), bf(ffn_w2[0, 1])]
    (h1, u00, a00, b00), blocks1 = _ffn_fwd(h0, sub_vec(0, 0), w13_00, w2_00, comm=(group1, False))
    g_pw1, g_pw2, g_w13, g_w2 = merged(blocks1, "merge_group1")
    pw1_f = jnp.transpose(g_pw1, (1, 0, 2)).reshape(D, 2 * D)
    pw2_f = g_pw2.reshape(D, D)
    w13_01, w2_01 = full_w13(g_w13), full_w2(g_w2)
    hn_c, pre = _conv_in_fwd(h1, sub_vec(0, 1), pw1_f, b1v)
    group2 = [bf(w_kv_a), bf(w_kv_b), bf(ffn_w13[1, 0]), bf(ffn_w2[1, 0])]
    (u2, z_c, h2), blocks2 = _conv_out_fwd(pre, h1, cw, pw2_f, comm=(group2, False))
    g_kva_w, g_kvb_w, g_w13, g_w2 = merged(blocks2, "merge_group2")
    wkva = g_kva_w.reshape(D, KV_LORA + ROPE)
    wkva_f = jnp.concatenate([wkva[:, :KV_LORA], jnp.zeros((D, 64), BF), wkva[:, KV_LORA:], jnp.zeros((D, 32), BF)], axis=1)
    wkvb_f = jnp.transpose(g_kvb_w, (1, 0, 2)).reshape(KV_LORA, NH * HP)
    w13_10, w2_10 = full_w13(g_w13), full_w2(g_w2)
    group3 = [bf(w_q_a[0]), bf(w_q_b[0]), bf(w_o[0]), bf(ffn_w13[1, 1]), bf(ffn_w2[1, 1])]
    (h3, u01, a01, b01), blocks3 = _ffn_fwd(h2, sub_vec(0, 2), w13_01, w2_01, comm=(group3, False))
    g_qa_w, g_qb_w, g_wo, g_w13, g_w2 = merged(blocks3, "merge_group3")
    wqa_f = g_qa_w.reshape(D, Q_LORA)
    wqb = jnp.transpose(g_qb_w, (1, 0, 2)).reshape(Q_LORA, NH, 96)
    wqb_f = jnp.pad(wqb, ((0, 0), (0, 0), (0, HP - 96))).reshape(Q_LORA, NH * HP)
    wo_f = jnp.pad(g_wo.reshape(NH, 64, D), ((0, 0), (64, 0), (0, 0))).reshape(NH * HP, D)
    w13_11, w2_11 = full_w13(g_w13), full_w2(g_w2)
    hn_kv, ckv, ckn, k_all, v_all = _kv_fwd(h3, vec_kv, wkva_f, g_kva, wkvb_f, rope)
    (h4, u10, a10, b10), _ = _ffn_fwd(h3, sub_vec(1, 0), w13_10, w2_10)
    hn_q, qa, qan, q_all = _q_fwd(h4, sub_vec(1, 1), wqa_f, g_qa, wqb_f, rope)
    o_all, lse = _attn_fwd(q_all, k_all, v_all)
    h5 = _attn_out_fwd(o_all, h4, wo_f, sub_vec(1, 1))
    (h6, u11, a11, b11), _ = _ffn_fwd(h5, sub_vec(1, 2), w13_11, w2_11)

    dh6, ps_fin = _final(h6, loss_target[0], fgv)
    loss = lax.psum(ps_fin[1, 0], ("x", "y", "c"))

    core = lax.axis_index("c").reshape(1).astype(jnp.int32)

    def pair_sums(sends, name):
        by_core = [s.reshape((4, 2) + s.shape[1:]).swapaxes(0, 1) for s in sends]
        from_sibling = _pair_swap(by_core, name)
        return [_pair_add(core, a.reshape(2, -1, a.shape[-1]), b.reshape(-1, b.shape[-1])).reshape(b.shape)
                for a, b in zip(by_core, from_sibling)]

    def ffn_back(dho, h_in, u, a, b, l, i, w13, w2, comm=None):
        vec = sub_vec(l, 2 * i)
        (dh, da, db, t, dhb, ps), reduced = _ffn_bwd(dho, h_in, a, b, vec, w13, w2, comm=comm)
        dwa = _wgrad(u, da, D, CH, "wgrad_w13")
        dwb = _wgrad(u, db, D, CH, "wgrad_w13")
        dw2, gs = _wgrad(t, dhb, CH, D, "wgrad_w2", gate=(w2, _vec([0.5 * vec[3]])))
        dgate = 0.5 * jnp.sum(gs[:, 0, :], axis=0)
        send13 = jnp.transpose(jnp.concatenate([dwa, dwb], axis=1).reshape(D, N_DEV, 704), (1, 0, 2))
        return dh, send13, dw2.reshape(N_DEV, 352, D), ps, dgate, reduced

    dh5, s13_11, s2_11, ps11, dg11, _ = ffn_back(dh6, h5, u11, a11, b11, 1, 1, w13_11, w2_11)
    vec_m1 = sub_vec(1, 1)
    do_all, delta, dhb5 = _attn_out_bwd(dh5, o_all, wo_f, vec_m1)
    dwo_p, gs_o = _wgrad(o_all, dhb5, D, D, "wgrad_wo", gate=(wo_f, _vec([vec_m1[3]])))
    dgm1 = jnp.sum(gs_o[:, 0, :], axis=0)
    dq_all, dk_all, dv_all = _attn_bwd(q_all, k_all, v_all, do_all, lse, delta)
    dh4, dqb, dqab, ps_q, ps_q2 = _q_bwd(dq_all, h4, qa, dh5, vec_m1, wqa_f, g_qa, wqb_f, rope)
    dwqb_p = _wgrad(qan, dqb, Q_LORA, D, "wgrad_wqb")
    dwqa = _wgrad(hn_q, dqab, D, Q_LORA, "wgrad_wqa")
    dh3a, s13_10, s2_10, ps10, dg10, _ = ffn_back(dh4, h3, u10, a10, b10, 1, 0, w13_10, w2_10)
    sums_a = pair_sums([
        s13_11, s2_11, s13_10, s2_10, dwqa.reshape(N_DEV, 128, Q_LORA),
        jnp.transpose(dwqb_p.reshape(Q_LORA, NH, HP)[:, :, :96].reshape(Q_LORA, N_DEV, 192), (1, 0, 2)),
        dwo_p.reshape(NH, HP, D)[:, 64:, :].reshape(N_DEV, 128, D)], "pair_swap_a")
    dh3, dkva, dkvb, ps_kv, ps_kv2 = _kv_bwd(dk_all, dv_all, h3, ckv, dh3a, vec_kv, wkva_f, g_kva, wkvb_f, rope)
    dwkva_p = _wgrad(hn_kv, dkva, D, KVA_P, "wgrad_wkva")
    dwkvb = _wgrad(ckn, dkvb, KV_LORA, D, "wgrad_wkvb")
    dh2, s13_01, s2_01, ps01, dg01, red_a = ffn_back(dh3, h2, u01, a01, b01, 0, 1, w13_01, w2_01, comm=(sums_a, True))
    vec_m0 = sub_vec(0, 1)
    du2, dhb2, ps_co = _conv_out_bwd(dh2, u2, cw, pw2_f)
    dpw2, gs_c = _wgrad(z_c, dhb2, D, D, "wgrad_pw2", gate=(pw2_f, _vec([vec_m0[3]])))
    dgm0 = jnp.sum(gs_c[:, 0, :], axis=0) + b_pw2_f[0] * ps_co[2]
    dh1, dpre, ps_ci, ps_dw, ps_b1 = _conv_in_bwd(du2, pre, h1, dh2, vec_m0, cw, pw1_f)
    dpw1 = _wgrad(hn_c, dpre, D, D, "wgrad_pw1")
    sums_b = pair_sums([
        s13_01, s2_01, jnp.transpose(dpw1.reshape(D, N_DEV, 256), (1, 0, 2)), dpw2.reshape(N_DEV, 128, D),
        jnp.concatenate([dwkva_p[:, :KV_LORA], dwkva_p[:, KV_LORA + 64:KV_LORA + 96]], axis=1).reshape(N_DEV, 128, KV_LORA + ROPE),
        jnp.transpose(dwkvb.reshape(KV_LORA, N_DEV, 256), (1, 0, 2))], "pair_swap_b")
    dh0, s13_00, s2_00, ps00, dg00, red_b = ffn_back(dh1, h0, u00, a00, b00, 0, 0, w13_00, w2_00, comm=(sums_b, True))
    red_c = _chip_scatter(pair_sums([s13_00, s2_00], "pair_swap_c"), "chip_scatter_last")
    r13_11, r2_11, r13_10, r2_10, r_wqa, r_wqb, r_wo = red_a
    r13_01, r2_01, r_pw1, r_pw2, r_wkva, r_wkvb = red_b
    r13_00, r2_00 = red_c

    dmod = jnp.stack([
        jnp.stack([ps00[1], ps00[2], dg00, ps_ci[1], ps_ci[2], dgm0, ps01[1], ps01[2], dg01]),
        jnp.stack([ps10[1], ps10[2], dg10, ps_q[1], ps_q[2], dgm1, ps11[1], ps11[2], dg11])])
    dnorm = jnp.stack([ps00[0], ps_ci[0], ps01[0], ps10[0], ps_q[0], ps11[0]])
    pieces = [dnorm, ps_b1[0], ps_dw[0:31], ps_dw[31], ps_co[0], ps_co[1], vec_m0[3] * ps_co[2],
              ps_kv[0], ps_kv2[0], ps_q2[0], ps_fin[0], dmod, ps_kv[1], ps_kv[2]]
    sizes = [int(np.prod(p.shape)) for p in pieces]
    offs = np.concatenate([[0], np.cumsum(sizes)]).astype(int)
    flat = jnp.concatenate([p.reshape(-1) for p in pieces]).reshape(-1, 128)
    (part,) = _exchange([(flat, "gather")], "gather_partials")
    part = part.reshape(N_DEV, -1)

    def piece(i, rows, cols):
        return part[:, offs[i]:offs[i + 1]].reshape(N_DEV, rows, cols)

    def mine(i, rows):
        return lax.dynamic_slice_in_dim(piece(i, rows, D), me * 128, 128, axis=2)

    dmod_all = piece(11, 2, 9 * D)
    c_t = jnp.transpose(c_all)
    g_ada = jnp.stack([_mod_wgrad(c_t, lax.dynamic_slice_in_dim(dmod_all[:, l], me * n_ada, n_ada, axis=1))
                       for l in range(2)])
    dkvmod_all = jnp.concatenate([piece(12, 1, D)[:, 0], piece(13, 1, D)[:, 0]], axis=1)
    g_kvada = _mod_wgrad(c_t, lax.dynamic_slice_in_dim(dkvmod_all, me * n_kva, n_kva, axis=1))

    parts = {
        'ada_w': g_ada.reshape(1, 2 * D, n_ada),
        'ada_b': dmod_all,
        'norm_g': mine(0, 6),
        'ffn_w13': jnp.stack([r13_00, r13_01, r13_10, r13_11], axis=1).reshape(4, 4 * D, 704),
        'ffn_w2': jnp.stack([r2_00, r2_01, r2_10, r2_11], axis=1).reshape(4, 4 * 352, D),
        'conv_w_pw1': r_pw1,
        'conv_b_pw1': lax.dynamic_slice_in_dim(piece(1, 1, 2 * D), me * 256, 256, axis=2),
        'conv_w_dw': mine(2, 31),
        'conv_b_dw': mine(3, 1),
        'conv_ln_g': mine(4, 1),
        'conv_ln_b': mine(5, 1),
        'conv_w_pw2': r_pw2,
        'conv_b_pw2': mine(6, 1),
        'kv_ada_w': g_kvada.reshape(1, D, n_kva),
        'kv_ada_b': dkvmod_all.reshape(N_DEV, 1, 2 * D),
        'kv_norm_g': piece(7, 1, D),
        'w_kv_a': r_wkva,
        'kv_a_norm_g': piece(8, 1, KV_LORA),
        'w_kv_b': r_wkvb,
        'w_q_a': r_wqa,
        'q_a_norm_g': piece(9, 1, Q_LORA),
        'w_q_b': r_wqb,
        'w_o': r_wo,
        'final_norm_g': piece(10, 1, D),
    }
    grads, deltas, new_m, new_v = [], [], [], []
    for name in _WEIGHTS:
        w = given[name]
        p = parts[name]
        shape2 = p.shape[1:]
        g, dlt, m2, v2 = _adamw(p, w.reshape(shape2), given['m_' + name].reshape(shape2), given['v_' + name].reshape(shape2))
        grads.append(g.reshape(w.shape))
        deltas.append(dlt.reshape(w.shape))
        new_m.append(m2.reshape(w.shape))
        new_v.append(v2.reshape(w.shape))
    return (loss, dh0.reshape(1, S, D), *grads, *deltas, *new_m, *new_v)
```

```python
import functools

import numpy as np
import jax
import jax.numpy as jnp
from jax import lax
from jax.experimental import pallas as pl
from jax.experimental.pallas import tpu as pltpu

F32, BF = jnp.float32, jnp.bfloat16

D = 1024
DFF = 2816
CH = 1408
NH = 16
HP = 128
KV_LORA, Q_LORA, ROPE = 256, 512, 32
KVA_P = 384
CONV_W = 31
HALO = 32
EPS = 1e-6
SM_SCALE = float((64 + 32) ** -0.5)
LOG2E = 1.4426950408889634
LN2 = 0.6931471805599453
EXP2_SCALE = SM_SCALE * LOG2E
NEG = -1e30
N_DEV = 8
MESH = pl.DeviceIdType.MESH

TS = 256
TS_FFN_FWD = 512
TA = 1024
TA_FWD = 2048
ATT_ROWS = 256
TW = 2048
VMEM_LIMIT = 56 * 1024 * 1024

LR, B1, B2, EPS_ADAM, WD, STEP = 0.001, 0.9, 0.999, 1e-08, 0.01, 10

VMEM_FULL = pl.BlockSpec(memory_space=pltpu.VMEM)
HBM_FULL = pl.BlockSpec(memory_space=pltpu.HBM)


def _params(n_grid):
    return pltpu.CompilerParams(dimension_semantics=("arbitrary",) * n_grid, vmem_limit_bytes=VMEM_LIMIT)


def _dot(a, b):
    return jnp.dot(a, b, preferred_element_type=F32)


def _dot_nt(a, b):
    return lax.dot_general(a, b, (((1,), (1,)), ((), ())), preferred_element_type=F32)


def _dot_tn(a, b):
    return lax.dot_general(a, b, (((0,), (0,)), ((), ())), preferred_element_type=F32)


def _sum0(x):
    return jnp.sum(x, axis=0, keepdims=True)


def _mean1(x):
    return jnp.mean(x, axis=-1, keepdims=True)


def _sigmoid(x):
    return jax.nn.sigmoid(x)


def _rows(shape, imap):
    return pl.BlockSpec(shape, imap)


def _norm_mod(h, vec):
    r = lax.rsqrt(_mean1(h * h) + EPS)
    xhat = h * r
    u = (xhat * vec[0:1]) * (1.0 + vec[2:3]) + vec[1:2]
    return u, xhat, r


def _norm_mod_bwd(du, xhat, r, vec):
    g = vec[0:1]
    dxn = du * (1.0 + vec[2:3])
    dsh = _sum0(du)
    dsc = _sum0(du * (xhat * g))
    dg = _sum0(dxn * xhat)
    dxhat = dxn * g
    dh = r * (dxhat - xhat * _mean1(dxhat * xhat))
    return dh, dg, dsh, dsc


def _rms_bwd(dy, x, r, g):
    xhat = x * r
    dg = _sum0(dy * xhat)
    dxhat = dy * g
    return r * (dxhat - xhat * _mean1(dxhat * xhat)), dg


def _accumulate(ref, first, rows):
    @pl.when(first)
    def _():
        ref[...] = jnp.zeros(ref.shape, ref.dtype)

    for i, row in enumerate(rows):
        ref[i:i + 1, :] += row


def _rope(x, tab):
    return x * tab[:, 0:HP] + pltpu.roll(x, 16, 1) * tab[:, HP:2 * HP] + pltpu.roll(x, HP - 16, 1) * tab[:, 2 * HP:3 * HP]


def _rope_t(dy, tab):
    return (dy * tab[:, 0:HP] + pltpu.roll(dy * tab[:, HP:2 * HP], HP - 16, 1)
            + pltpu.roll(dy * tab[:, 2 * HP:3 * HP], 16, 1))


def _exchange(items, name):
    n = len(items)

    def body(*refs):
        ins, outs = refs[:n], refs[n:2 * n]
        send_sems, recv_sems, local_sems = refs[2 * n:]
        x, y, c = lax.axis_index("x"), lax.axis_index("y"), lax.axis_index("c")
        me = 4 * x + 2 * y + c

        def source(j, dev):
            return ins[j] if items[j][1] == "gather" else ins[j].at[dev]

        own = [pltpu.make_async_copy(source(j, me), outs[j].at[me], local_sems.at[j]) for j in range(n)]
        for cp in own:
            cp.start()
        remote = []
        for d in range(1, N_DEV):
            px = 1 - x if d & 4 else x
            py = 1 - y if d & 2 else y
            pc = 1 - c if d & 1 else c
            peer = 4 * px + 2 * py + pc
            for j in range(n):
                pltpu.make_async_remote_copy(
                    src_ref=source(j, peer), dst_ref=outs[j].at[me], send_sem=send_sems.at[j, d - 1],
                    recv_sem=recv_sems.at[j, d - 1], device_id=(px, py, pc), device_id_type=MESH).start()
                remote.append(pltpu.make_async_remote_copy(
                    src_ref=source(j, peer), dst_ref=outs[j].at[peer], send_sem=send_sems.at[j, d - 1],
                    recv_sem=recv_sems.at[j, d - 1], device_id=(px, py, pc), device_id_type=MESH))
        for cp in remote:
            cp.wait_send()
            cp.wait_recv()
        for cp in own:
            cp.wait()

    out_shape = []
    for arr, mode in items:
        shp = (N_DEV,) + tuple(arr.shape) if mode == "gather" else tuple(arr.shape)
        out_shape.append(jax.ShapeDtypeStruct(shp, arr.dtype))
    return pl.pallas_call(
        body, name=name, out_shape=out_shape,
        in_specs=[HBM_FULL] * n, out_specs=[HBM_FULL] * n,
        scratch_shapes=[pltpu.SemaphoreType.DMA((n, N_DEV - 1)), pltpu.SemaphoreType.DMA((n, N_DEV - 1)),
                        pltpu.SemaphoreType.DMA((n,))],
        compiler_params=pltpu.CompilerParams(has_side_effects=True),
    )(*[a for a, _ in items])


def _gather_two_level(arrs, name):
    n = len(arrs)

    def body(*refs):
        ins, outs = refs[:n], refs[n:2 * n]
        send_sems, recv_sems, local_sems = refs[2 * n:]
        x, y, c = lax.axis_index("x"), lax.axis_index("y"), lax.axis_index("c")
        sibling = (x, y, 1 - c)
        chips = [(1 - x, y), (x, 1 - y), (1 - x, 1 - y)]

        def slot(j, px, py, pc):
            return outs[j].at[4 * px + 2 * py + pc]

        def copy(j, k, block, to, src=None):
            return pltpu.make_async_remote_copy(
                src_ref=slot(j, *block) if src is None else src, dst_ref=slot(j, *block),
                send_sem=send_sems.at[j, k], recv_sem=recv_sems.at[j, k], device_id=to, device_id_type=MESH)

        own = [pltpu.make_async_copy(ins[j], slot(j, x, y, c), local_sems.at[j]) for j in range(n)]
        for cp in own:
            cp.start()
        sent = []
        for j in range(n):
            sent.append(copy(j, 0, (x, y, c), sibling, src=ins[j]))
            sent += [copy(j, 1 + i, (x, y, c), (*chip, c), src=ins[j]) for i, chip in enumerate(chips)]
        for cp in sent:
            cp.start()
        for i, chip in enumerate(chips):
            for j in range(n):
                copy(j, 1 + i, (*chip, c), (x, y, c)).wait_recv()
                passed = copy(j, 4 + i, (*chip, c), sibling)
                passed.start()
                sent.append(passed)
        for j in range(n):
            copy(j, 0, (x, y, 1 - c), (x, y, c)).wait_recv()
            for i, chip in enumerate(chips):
                copy(j, 4 + i, (*chip, 1 - c), (x, y, c)).wait_recv()
        for cp in sent:
            cp.wait_send()
        for cp in own:
            cp.wait()

    return pl.pallas_call(
        body, name=name, out_shape=[jax.ShapeDtypeStruct((N_DEV,) + tuple(a.shape), a.dtype) for a in arrs],
        in_specs=[HBM_FULL] * n, out_specs=[HBM_FULL] * n,
        scratch_shapes=[pltpu.SemaphoreType.DMA((n, N_DEV - 1)), pltpu.SemaphoreType.DMA((n, N_DEV - 1)),
                        pltpu.SemaphoreType.DMA((n,))],
        compiler_params=pltpu.CompilerParams(has_side_effects=True),
    )(*arrs)


def _pair_swap(arrs, name):
    n = len(arrs)

    def body(*refs):
        ins, outs = refs[:n], refs[n:2 * n]
        send_sems, recv_sems = refs[2 * n:]
        x, y, c = lax.axis_index("x"), lax.axis_index("y"), lax.axis_index("c")
        copies = [pltpu.make_async_remote_copy(
            src_ref=ins[j].at[1 - c], dst_ref=outs[j], send_sem=send_sems.at[j], recv_sem=recv_sems.at[j],
            device_id=(x, y, 1 - c), device_id_type=MESH) for j in range(n)]
        for cp in copies:
            cp.start()
        for cp in copies:
            cp.wait_send()
            cp.wait_recv()

    return pl.pallas_call(
        body, name=name, out_shape=[jax.ShapeDtypeStruct(tuple(a.shape[1:]), a.dtype) for a in arrs],
        in_specs=[HBM_FULL] * n, out_specs=[HBM_FULL] * n,
        scratch_shapes=[pltpu.SemaphoreType.DMA((n,)), pltpu.SemaphoreType.DMA((n,))],
        compiler_params=pltpu.CompilerParams(has_side_effects=True),
    )(*arrs)


def _chip_comm(arrs, scatter):
    n = len(arrs)
    out_shapes = [jax.ShapeDtypeStruct(tuple(a.shape) if scatter else (2, 4) + tuple(a.shape), a.dtype) for a in arrs]
    scratch = [pltpu.SemaphoreType.DMA((n, 3)), pltpu.SemaphoreType.DMA((n, 3)), pltpu.SemaphoreType.DMA((n,))]

    def copies(ins, outs, send_sems, recv_sems, local_sems, arriving):
        x, y, c = lax.axis_index("x"), lax.axis_index("y"), lax.axis_index("c")
        here = 2 * x + y

        def source(j, chip):
            return ins[j].at[chip] if scatter else ins[j]

        def slot(j, chip):
            return outs[j].at[chip] if scatter else outs[j].at[c].at[chip]

        own = [pltpu.make_async_copy(source(j, here), slot(j, here), local_sems.at[j]) for j in range(n)]
        remote = []
        for i, (px, py) in enumerate([(1 - x, y), (x, 1 - y), (1 - x, 1 - y)]):
            there = 2 * px + py
            for j in range(n):
                remote.append(pltpu.make_async_remote_copy(
                    src_ref=source(j, there), dst_ref=slot(j, there if arriving else here),
                    send_sem=send_sems.at[j, i], recv_sem=recv_sems.at[j, i], device_id=(px, py, c),
                    device_id_type=MESH))
        return own, remote

    def start(*refs):
        own, remote = copies(*refs, arriving=False)
        for cp in own + remote:
            cp.start()

    def finish(*refs):
        own, remote = copies(*refs, arriving=True)
        for cp in remote:
            cp.wait_send()
            cp.wait_recv()
        for cp in own:
            cp.wait()

    return out_shapes, scratch, start, finish


def _chip_scatter(arrs, name):
    n = len(arrs)
    out_shapes, scratch, start, finish = _chip_comm(arrs, True)

    def body(*refs):
        start(refs[:n], refs[n:2 * n], *refs[2 * n:])
        finish(refs[:n], refs[n:2 * n], *refs[2 * n:])

    return pl.pallas_call(
        body, name=name, out_shape=out_shapes, in_specs=[HBM_FULL] * n, out_specs=[HBM_FULL] * n,
        scratch_shapes=scratch, compiler_params=pltpu.CompilerParams(has_side_effects=True),
    )(*arrs)


def _hosted_call(body, args, comm, *, name, n_steps, in_specs, out_specs, out_shape, scratch_shapes=()):
    if comm is None:
        res = pl.pallas_call(
            body, name=name, grid=(n_steps,), out_shape=list(out_shape), in_specs=list(in_specs),
            out_specs=list(out_specs), scratch_shapes=list(scratch_shapes), compiler_params=_params(1))(*args)
        return res, []
    arrs, scatter = comm
    c_shapes, c_scratch, start, finish = _chip_comm(arrs, scatter)
    n_in, n_out, n_sc, k = len(in_specs), len(out_specs), len(scratch_shapes), len(arrs)

    def hosting(*refs):
        ins, cin = refs[:n_in], refs[n_in:n_in + k]
        outs, cout = refs[n_in + k:n_in + k + n_out], refs[n_in + k + n_out:n_in + 2 * k + n_out]
        scratch = refs[n_in + 2 * k + n_out:n_in + 2 * k + n_out + n_sc]
        sems = refs[n_in + 2 * k + n_out + n_sc:]
        step = pl.program_id(0)

        @pl.when(step == 0)
        def _():
            start(cin, cout, *sems)

        body(*ins, *outs, *scratch)

        @pl.when(step == n_steps - 1)
        def _():
            finish(cin, cout, *sems)

    res = pl.pallas_call(
        hosting, name=name + "_hosting", grid=(n_steps,), out_shape=list(out_shape) + c_shapes,
        in_specs=list(in_specs) + [HBM_FULL] * k, out_specs=list(out_specs) + [HBM_FULL] * k,
        scratch_shapes=list(scratch_shapes) + c_scratch,
        compiler_params=pltpu.CompilerParams(dimension_semantics=("arbitrary",), vmem_limit_bytes=VMEM_LIMIT,
                                             has_side_effects=True))(*args, *arrs)
    return res[:n_out], res[n_out:]


def _sibling_merge(arrs, name):
    n = len(arrs)

    def body(*refs):
        bufs = refs[n:2 * n]
        send_sems, recv_sems = refs[2 * n:]
        x, y, c = lax.axis_index("x"), lax.axis_index("y"), lax.axis_index("c")
        waits = []
        for j in range(n):
            pltpu.make_async_remote_copy(
                src_ref=bufs[j].at[c], dst_ref=bufs[j].at[c], send_sem=send_sems.at[j], recv_sem=recv_sems.at[j],
                device_id=(x, y, 1 - c), device_id_type=MESH).start()
            waits.append(pltpu.make_async_remote_copy(
                src_ref=bufs[j].at[c], dst_ref=bufs[j].at[1 - c], send_sem=send_sems.at[j],
                recv_sem=recv_sems.at[j], device_id=(x, y, 1 - c), device_id_type=MESH))
        for cp in waits:
            cp.wait_send()
            cp.wait_recv()

    return pl.pallas_call(
        body, name=name, out_shape=[jax.ShapeDtypeStruct(tuple(a.shape), a.dtype) for a in arrs],
        in_specs=[HBM_FULL] * n, out_specs=[HBM_FULL] * n, input_output_aliases={j: j for j in range(n)},
        scratch_shapes=[pltpu.SemaphoreType.DMA((n,)), pltpu.SemaphoreType.DMA((n,))],
        compiler_params=pltpu.CompilerParams(has_side_effects=True),
    )(*arrs)


def _pair_add(core, a, b):
    _, R, C = a.shape
    tr = _row_tile(R)

    def body(core_ref, a_ref, b_ref, o_ref):
        o_ref[...] = (a_ref[...].astype(F32) + b_ref[...].astype(F32)).astype(BF)

    blk = _rows((tr, C), lambda i, core: (i, 0))
    grid_spec = pltpu.PrefetchScalarGridSpec(
        num_scalar_prefetch=1, grid=(R // tr,),
        in_specs=[_rows((None, tr, C), lambda i, core: (core[0], i, 0)), blk], out_specs=blk)
    return pl.pallas_call(
        body, name="pair_add", grid_spec=grid_spec, out_shape=jax.ShapeDtypeStruct((R, C), BF),
        compiler_params=_params(1),
    )(core, a, b)


def _mod_fwd(c_all, w0, w1, wkv):
    n0, n1, n2 = w0.shape[1], w1.shape[1], wkv.shape[1]

    def body(c_ref, w0_ref, w1_ref, w2_ref, o_ref):
        cc = c_ref[...]
        s = cc * _sigmoid(cc)
        o_ref[:, 0:n0] = _dot(s, w0_ref[...])
        o_ref[:, n0:n0 + n1] = _dot(s, w1_ref[...])
        o_ref[:, n0 + n1:n0 + n1 + n2] = _dot(s, w2_ref[...])

    return pl.pallas_call(
        body, name="mod_fwd", out_shape=jax.ShapeDtypeStruct((N_DEV, n0 + n1 + n2), F32),
        in_specs=[VMEM_FULL] * 4, out_specs=VMEM_FULL,
        compiler_params=pltpu.CompilerParams(vmem_limit_bytes=VMEM_LIMIT),
    )(c_all, w0, w1, wkv)


def _mod_wgrad(c_t, dm):
    C = dm.shape[1]
    tr = 256

    def body(ct_ref, dm_ref, o_ref):
        ct = ct_ref[...]
        s = ct * _sigmoid(ct)
        dmv = dm_ref[...]
        lane = lax.broadcasted_iota(jnp.int32, (tr, N_DEV), 1)
        acc = jnp.zeros((tr, C), F32)
        for r in range(N_DEV):
            col = jnp.sum(jnp.where(lane == r, s, 0.0), axis=1, keepdims=True)
            acc = acc + col * dmv[r:r + 1, :]
        o_ref[...] = acc

    return pl.pallas_call(
        body, name="mod_wgrad", grid=(D // tr,), out_shape=jax.ShapeDtypeStruct((D, C), F32),
        in_specs=[_rows((tr, N_DEV), lambda i: (i, 0)), _rows((N_DEV, C), lambda i: (0, 0))],
        out_specs=_rows((tr, C), lambda i: (i, 0)), compiler_params=_params(1),
    )(c_t, dm)


def _ffn_fwd(h, vec, w13, w2, comm=None):
    S = h.shape[0]
    ts = min(S, TS_FFN_FWD)

    def body(h_ref, vec_ref, w13_ref, w2_ref, ho_ref, u_ref, a_ref, b_ref):
        hv, vec = h_ref[...], vec_ref[...]
        u, _, _ = _norm_mod(hv, vec)
        ub = u.astype(BF)
        u_ref[...] = ub
        y = jnp.zeros((ts, D), F32)
        for k in range(DFF // CH):
            c0 = k * CH
            a = _dot(ub, w13_ref[:, c0:c0 + CH])
            b = _dot(ub, w13_ref[:, DFF + c0:DFF + c0 + CH])
            a_ref[:, c0:c0 + CH] = a.astype(BF)
            b_ref[:, c0:c0 + CH] = b.astype(BF)
            t = (a * _sigmoid(a)) * b
            y = y + _dot(t.astype(BF), w2_ref[c0:c0 + CH, :])
        ho_ref[...] = hv + (0.5 * vec[3:4]) * y

    tok = lambda i: (i, 0)
    return _hosted_call(
        body, (h, vec, w13, w2), comm, name="ffn_fwd", n_steps=S // ts,
        out_shape=[jax.ShapeDtypeStruct((S, D), F32), jax.ShapeDtypeStruct((S, D), BF),
                   jax.ShapeDtypeStruct((S, DFF), BF), jax.ShapeDtypeStruct((S, DFF), BF)],
        in_specs=[_rows((ts, D), tok), _rows((8, D), lambda i: (0, 0)), VMEM_FULL, VMEM_FULL],
        out_specs=[_rows((ts, D), tok), _rows((ts, D), tok), _rows((ts, DFF), tok), _rows((ts, DFF), tok)])


def _ffn_bwd(dho, h, a, b, vec, w13, w2, comm=None):
    S = h.shape[0]

    def body(dho_ref, h_ref, a_ref, b_ref, vec_ref, w13_ref, w2_ref,
             dh_ref, da_ref, db_ref, t_ref, dhb_ref, ps_ref):
        dho_v, vec = dho_ref[...], vec_ref[...]
        dhb_ref[...] = dho_v.astype(BF)
        dyb = ((0.5 * vec[3:4]) * dho_v).astype(BF)
        du = jnp.zeros((TS, D), F32)
        for k in range(DFF // CH):
            c0 = k * CH
            av = a_ref[:, c0:c0 + CH].astype(F32)
            bv = b_ref[:, c0:c0 + CH].astype(F32)
            dt = _dot_nt(dyb, w2_ref[c0:c0 + CH, :])
            sig = _sigmoid(av)
            sl = av * sig
            t_ref[:, c0:c0 + CH] = (sl * bv).astype(BF)
            dab = (dt * bv * (sig * (1.0 + av * (1.0 - sig)))).astype(BF)
            dbb = (dt * sl).astype(BF)
            da_ref[:, c0:c0 + CH] = dab
            db_ref[:, c0:c0 + CH] = dbb
            du = du + _dot_nt(dab, w13_ref[:, c0:c0 + CH]) + _dot_nt(dbb, w13_ref[:, DFF + c0:DFF + c0 + CH])
        _, xhat, r = _norm_mod(h_ref[...], vec)
        dhn, dg, dsh, dsc = _norm_mod_bwd(du, xhat, r, vec)
        dh_ref[...] = dho_v + dhn
        _accumulate(ps_ref, pl.program_id(0) == 0, [dg, dsh, dsc])

    tok = lambda i: (i, 0)
    return _hosted_call(
        body, (dho, h, a, b, vec, w13, w2), comm, name="ffn_bwd", n_steps=S // TS,
        out_shape=[jax.ShapeDtypeStruct((S, D), F32), jax.ShapeDtypeStruct((S, DFF), BF),
                   jax.ShapeDtypeStruct((S, DFF), BF), jax.ShapeDtypeStruct((S, DFF), BF),
                   jax.ShapeDtypeStruct((S, D), BF), jax.ShapeDtypeStruct((8, D), F32)],
        in_specs=[_rows((TS, D), tok), _rows((TS, D), tok), _rows((TS, DFF), tok), _rows((TS, DFF), tok),
                  _rows((8, D), lambda i: (0, 0)), VMEM_FULL, VMEM_FULL],
        out_specs=[_rows((TS, D), tok), _rows((TS, DFF), tok), _rows((TS, DFF), tok), _rows((TS, DFF), tok),
                   _rows((TS, D), tok), _rows((8, D), lambda i: (0, 0))])


def _wgrad(a, b, tm, tn, name, gate=None):
    S, M = a.shape
    N = b.shape[1]
    n_s = S // TW

    def body(*refs):
        if gate is None:
            a_ref, b_ref, o_ref, acc_ref = refs
        else:
            a_ref, b_ref, w_ref, sc_ref, o_ref, gs_ref, acc_ref = refs
        s = pl.program_id(2)

        @pl.when(s == 0)
        def _():
            acc_ref[...] = jnp.zeros((tm, tn), F32)

        acc_ref[...] += _dot_tn(a_ref[...], b_ref[...])

        @pl.when(s == n_s - 1)
        def _():
            acc = acc_ref[...]
            if gate is None:
                o_ref[...] = acc.astype(BF)
            else:
                o_ref[...] = (acc * sc_ref[0:1, :]).astype(BF)
                gs_ref[...] = jnp.broadcast_to(_sum0(acc * w_ref[...].astype(F32)), (8, tn))

    in_specs = [_rows((TW, tm), lambda m, n, s: (s, m)), _rows((TW, tn), lambda m, n, s: (s, n))]
    out_shape = [jax.ShapeDtypeStruct((M, N), BF)]
    out_specs = [_rows((tm, tn), lambda m, n, s: (m, n))]
    args = [a, b]
    if gate is not None:
        in_specs += [_rows((tm, tn), lambda m, n, s: (m, n)), _rows((8, tn), lambda m, n, s: (0, n))]
        out_shape.append(jax.ShapeDtypeStruct((M // tm, 8, N), F32))
        out_specs.append(_rows((None, 8, tn), lambda m, n, s: (m, 0, n)))
        args += list(gate)
    res = pl.pallas_call(
        body, name=name, grid=(M // tm, N // tn, n_s), out_shape=out_shape, in_specs=in_specs,
        out_specs=out_specs, scratch_shapes=[pltpu.VMEM((tm, tn), F32)], compiler_params=_params(3),
    )(*args)
    return res[0] if gate is None else (res[0], res[1])


def _conv_in_fwd(h, vec, w1, b1):
    S = h.shape[0]

    def body(h_ref, vec_ref, w_ref, b1_ref, hn_ref, pre_ref):
        u, _, _ = _norm_mod(h_ref[...], vec_ref[...])
        ub = u.astype(BF)
        hn_ref[...] = ub
        pre_ref[...] = _dot(ub, w_ref[...]) + b1_ref[0:1, :]

    tok = lambda i: (i, 0)
    return pl.pallas_call(
        body, name="conv_in_fwd", grid=(S // TS,),
        out_shape=[jax.ShapeDtypeStruct((S, D), BF), jax.ShapeDtypeStruct((S, 2 * D), F32)],
        in_specs=[_rows((TS, D), tok), _rows((8, D), lambda i: (0, 0)), VMEM_FULL, _rows((8, 2 * D), lambda i: (0, 0))],
        out_specs=[_rows((TS, D), tok), _rows((TS, 2 * D), tok)], compiler_params=_params(1),
    )(h, vec, w1, b1)


def _glu(pre):
    return pre[:, :D] * _sigmoid(pre[:, D:])


def _tap_groups(offset):
    groups = {}
    for j in range(CONV_W):
        off = offset(j)
        groups.setdefault(off % 8, []).append((off - off % 8, j))
    return [(phase, sorted(taps)) for phase, taps in sorted(groups.items())]


def _shift_window(dst_ref, win_ref, phase, rows):
    dst_ref[0:rows, :] = win_ref[phase:phase + rows, :]


def _layernorm(u2):
    mu = _mean1(u2)
    xc = u2 - mu
    rstd = lax.rsqrt(_mean1(xc * xc) + EPS)
    return xc * rstd, rstd


def _conv_out_fwd(pre, h, cw, w2, comm=None):
    S = h.shape[0]
    hb = TS // HALO

    def body(pre_ref, ph_ref, h_ref, cw_ref, w2_ref, u2_ref, z_ref, ho_ref, win_ref, sh_ref):
        i = pl.program_id(0)
        cwv = cw_ref[...]
        win_ref[0:HALO, :] = jnp.where(i > 0, _glu(ph_ref[...]), 0.0)
        win_ref[HALO:HALO + TS, :] = _glu(pre_ref[...])
        u2 = jnp.broadcast_to(cwv[31:32], (TS, D))
        for phase, taps in _tap_groups(lambda j: HALO - (CONV_W - 1) + j):
            _shift_window(sh_ref, win_ref, phase, taps[-1][0] + TS)
            for lo, j in taps:
                u2 = u2 + cwv[j:j + 1] * sh_ref[lo:lo + TS, :]
        u2_ref[...] = u2
        xh, _ = _layernorm(u2)
        un = xh * cwv[32:33] + cwv[33:34]
        zb = (un * _sigmoid(un)).astype(BF)
        z_ref[...] = zb
        y = _dot(zb, w2_ref[...]) + cwv[34:35]
        ho_ref[...] = h_ref[...] + cwv[35:36] * y

    tok = lambda i: (i, 0)
    return _hosted_call(
        body, (pre, pre, h, cw, w2), comm, name="conv_out_fwd", n_steps=S // TS,
        out_shape=[jax.ShapeDtypeStruct((S, D), F32), jax.ShapeDtypeStruct((S, D), BF), jax.ShapeDtypeStruct((S, D), F32)],
        in_specs=[_rows((TS, 2 * D), tok), _rows((HALO, 2 * D), lambda i: (jnp.maximum(i * hb - 1, 0), 0)),
                  _rows((TS, D), tok), _rows((40, D), lambda i: (0, 0)), VMEM_FULL],
        out_specs=[_rows((TS, D), tok), _rows((TS, D), tok), _rows((TS, D), tok)],
        scratch_shapes=[pltpu.VMEM((TS + HALO, D), F32)] * 2)


def _conv_out_bwd(dho, u2, cw, w2):
    S = dho.shape[0]

    def body(dho_ref, u2_ref, cw_ref, w2_ref, du2_ref, dhb_ref, ps_ref):
        dho_v, cwv = dho_ref[...], cw_ref[...]
        dhb_ref[...] = dho_v.astype(BF)
        dz = _dot_nt((cwv[35:36] * dho_v).astype(BF), w2_ref[...])
        xh, rstd = _layernorm(u2_ref[...])
        un = xh * cwv[32:33] + cwv[33:34]
        sig = _sigmoid(un)
        dun = dz * (sig * (1.0 + un * (1.0 - sig)))
        dxh = dun * cwv[32:33]
        du2_ref[...] = rstd * (dxh - _mean1(dxh) - xh * _mean1(dxh * xh))
        _accumulate(ps_ref, pl.program_id(0) == 0, [_sum0(dun * xh), _sum0(dun), _sum0(dho_v)])

    tok = lambda i: (i, 0)
    return pl.pallas_call(
        body, name="conv_out_bwd", grid=(S // TS,),
        out_shape=[jax.ShapeDtypeStruct((S, D), F32), jax.ShapeDtypeStruct((S, D), BF), jax.ShapeDtypeStruct((8, D), F32)],
        in_specs=[_rows((TS, D), tok), _rows((TS, D), tok), _rows((40, D), lambda i: (0, 0)), VMEM_FULL],
        out_specs=[_rows((TS, D), tok), _rows((TS, D), tok), _rows((8, D), lambda i: (0, 0))],
        compiler_params=_params(1),
    )(dho, u2, cw, w2)


def _conv_in_bwd(du2, pre, h, dho, vec, cw, w1):
    S = h.shape[0]
    n_t = S // TS
    hb = TS // HALO

    def body(du2_ref, dh2h_ref, pre_ref, ph_ref, h_ref, dho_ref, vec_ref, cw_ref, w1_ref,
             dh_ref, dpre_ref, ps_ref, pw_ref, pb_ref, winu_ref, wind_ref, sh_ref):
        i = pl.program_id(0)
        vec, cwv = vec_ref[...], cw_ref[...]
        pre = pre_ref[...]
        av, sg = pre[:, :D], _sigmoid(pre[:, D:])
        winu_ref[0:HALO, :] = jnp.where(i > 0, _glu(ph_ref[...]), 0.0)
        winu_ref[HALO:HALO + TS, :] = av * sg
        du2v = du2_ref[...]
        wind_ref[0:TS, :] = du2v
        wind_ref[TS:TS + HALO, :] = jnp.where(i < n_t - 1, dh2h_ref[...], 0.0)

        @pl.when(i == 0)
        def _():
            pw_ref[...] = jnp.zeros((32, D), F32)

        for phase, taps in _tap_groups(lambda j: HALO - (CONV_W - 1) + j):
            _shift_window(sh_ref, winu_ref, phase, taps[-1][0] + TS)
            for lo, j in taps:
                pw_ref[j:j + 1, :] += _sum0(du2v * sh_ref[lo:lo + TS, :])
        pw_ref[31:32, :] += _sum0(du2v)
        du1 = jnp.zeros((TS, D), F32)
        for phase, taps in _tap_groups(lambda j: CONV_W - 1 - j):
            _shift_window(sh_ref, wind_ref, phase, taps[-1][0] + TS)
            for lo, j in taps:
                du1 = du1 + cwv[j:j + 1] * sh_ref[lo:lo + TS, :]
        da = du1 * sg
        dg = du1 * av * sg * (1.0 - sg)
        dab, dgb = da.astype(BF), dg.astype(BF)
        dpre_ref[:, :D] = dab
        dpre_ref[:, D:] = dgb

        @pl.when(i == 0)
        def _():
            pb_ref[...] = jnp.zeros((8, 2 * D), F32)

        pb_ref[0:1, :D] += _sum0(da)
        pb_ref[0:1, D:] += _sum0(dg)
        du = _dot_nt(dab, w1_ref[:, :D]) + _dot_nt(dgb, w1_ref[:, D:])
        _, xhat, r = _norm_mod(h_ref[...], vec)
        dhn, dgn, dsh, dsc = _norm_mod_bwd(du, xhat, r, vec)
        dh_ref[...] = dho_ref[...] + dhn
        _accumulate(ps_ref, i == 0, [dgn, dsh, dsc])

    tok = lambda i: (i, 0)
    fixed = lambda i: (0, 0)
    return pl.pallas_call(
        body, name="conv_in_bwd", grid=(n_t,),
        out_shape=[jax.ShapeDtypeStruct((S, D), F32), jax.ShapeDtypeStruct((S, 2 * D), BF),
                   jax.ShapeDtypeStruct((8, D), F32), jax.ShapeDtypeStruct((32, D), F32),
                   jax.ShapeDtypeStruct((8, 2 * D), F32)],
        in_specs=[_rows((TS, D), tok), _rows((HALO, D), lambda i: (jnp.minimum((i + 1) * hb, S // HALO - 1), 0)),
                  _rows((TS, 2 * D), tok), _rows((HALO, 2 * D), lambda i: (jnp.maximum(i * hb - 1, 0), 0)),
                  _rows((TS, D), tok), _rows((TS, D), tok), _rows((8, D), fixed), _rows((40, D), fixed), VMEM_FULL],
        out_specs=[_rows((TS, D), tok), _rows((TS, 2 * D), tok), _rows((8, D), fixed), _rows((32, D), fixed),
                   _rows((8, 2 * D), fixed)],
        scratch_shapes=[pltpu.VMEM((TS + HALO, D), F32)] * 3,
        compiler_params=_params(1),
    )(du2, du2, pre, pre, h, dho, vec, cw, w1)


def _lane():
    return lax.broadcasted_iota(jnp.int32, (TS, HP), 1)


def _kv_fwd(h, vec, wkva, g2, wkvb, rope):
    S = h.shape[0]

    def body(h_ref, vec_ref, wa_ref, g2_ref, wb_ref, rope_ref, hn_ref, ckv_ref, ckn_ref, k_ref, v_ref):
        u, _, _ = _norm_mod(h_ref[...], vec_ref[...])
        ub = u.astype(BF)
        hn_ref[...] = ub
        kva = _dot(ub, wa_ref[...])
        ckv = kva[:, :KV_LORA]
        ckv_ref[...] = ckv
        r2 = lax.rsqrt(_mean1(ckv * ckv) + EPS)
        cknb = ((ckv * r2) * g2_ref[0:1, :]).astype(BF)
        ckn_ref[...] = cknb
        kvb = _dot(cknb, wb_ref[...])
        kpe = _rope(kva[:, KV_LORA:KVA_P], rope_ref[...])
        lane = _lane()
        ones_lane0 = jnp.where(lane == 0, 1.0, 0.0)
        for hd in range(NH):
            blk = kvb[:, hd * HP:(hd + 1) * HP]
            k_ref[:, hd * HP:(hd + 1) * HP] = jnp.where(lane < 64, blk, kpe).astype(BF)
            v_ref[:, hd * HP:(hd + 1) * HP] = jnp.where(lane >= 64, blk, ones_lane0).astype(BF)

    tok = lambda i: (i, 0)
    fixed = lambda i: (0, 0)
    return pl.pallas_call(
        body, name="kv_fwd", grid=(S // TS,),
        out_shape=[jax.ShapeDtypeStruct((S, D), BF), jax.ShapeDtypeStruct((S, KV_LORA), F32),
                   jax.ShapeDtypeStruct((S, KV_LORA), BF), jax.ShapeDtypeStruct((S, NH * HP), BF),
                   jax.ShapeDtypeStruct((S, NH * HP), BF)],
        in_specs=[_rows((TS, D), tok), _rows((8, D), fixed), VMEM_FULL, _rows((8, KV_LORA), fixed), VMEM_FULL,
                  _rows((TS, 3 * HP), tok)],
        out_specs=[_rows((TS, D), tok), _rows((TS, KV_LORA), tok), _rows((TS, KV_LORA), tok),
                   _rows((TS, NH * HP), tok), _rows((TS, NH * HP), tok)],
        compiler_params=_params(1),
    )(h, vec, wkva, g2, wkvb, rope)


def _kv_bwd(dk, dv, h, ckv, dho, vec, wkva, g2, wkvb, rope):
    S = h.shape[0]

    def body(dk_ref, dv_ref, h_ref, ckv_ref, dho_ref, vec_ref, wa_ref, g2_ref, wb_ref, rope_ref,
             dh_ref, dkva_ref, dkvb_ref, ps_ref, ps2_ref):
        i = pl.program_id(0)
        vec = vec_ref[...]
        lane = _lane()
        dkpe = jnp.zeros((TS, HP), F32)
        for hd in range(NH):
            dkh = dk_ref[:, hd * HP:(hd + 1) * HP]
            dvh = dv_ref[:, hd * HP:(hd + 1) * HP]
            dkvb_ref[:, hd * HP:(hd + 1) * HP] = jnp.where(lane < 64, dkh, dvh).astype(BF)
            dkpe = dkpe + jnp.where(lane >= 64, dkh, 0.0)
        dkpe = _rope_t(dkpe, rope_ref[...])
        dckn = _dot_nt(dkvb_ref[...], wb_ref[...])
        ckv = ckv_ref[...]
        r2 = lax.rsqrt(_mean1(ckv * ckv) + EPS)
        dckv, dg2 = _rms_bwd(dckn, ckv, r2, g2_ref[0:1, :])
        dkva_ref[:, :KV_LORA] = dckv.astype(BF)
        dkva_ref[:, KV_LORA:KVA_P] = dkpe.astype(BF)
        du = _dot_nt(dkva_ref[...], wa_ref[...])
        _, xhat, r = _norm_mod(h_ref[...], vec)
        dhn, dgn, dsh, dsc = _norm_mod_bwd(du, xhat, r, vec)
        dh_ref[...] = dho_ref[...] + dhn
        _accumulate(ps_ref, i == 0, [dgn, dsh, dsc])
        _accumulate(ps2_ref, i == 0, [dg2])

    tok = lambda i: (i, 0)
    fixed = lambda i: (0, 0)
    return pl.pallas_call(
        body, name="kv_bwd", grid=(S // TS,),
        out_shape=[jax.ShapeDtypeStruct((S, D), F32), jax.ShapeDtypeStruct((S, KVA_P), BF),
                   jax.ShapeDtypeStruct((S, NH * HP), BF), jax.ShapeDtypeStruct((8, D), F32),
                   jax.ShapeDtypeStruct((8, KV_LORA), F32)],
        in_specs=[_rows((TS, NH * HP), tok), _rows((TS, NH * HP), tok), _rows((TS, D), tok), _rows((TS, KV_LORA), tok),
                  _rows((TS, D), tok), _rows((8, D), fixed), VMEM_FULL, _rows((8, KV_LORA), fixed), VMEM_FULL,
                  _rows((TS, 3 * HP), tok)],
        out_specs=[_rows((TS, D), tok), _rows((TS, KVA_P), tok), _rows((TS, NH * HP), tok), _rows((8, D), fixed),
                   _rows((8, KV_LORA), fixed)],
        compiler_params=_params(1),
    )(dk, dv, h, ckv, dho, vec, wkva, g2, wkvb, rope)


def _q_fwd(h, vec, wqa, g2, wqb, rope):
    S = h.shape[0]

    def body(h_ref, vec_ref, wa_ref, g2_ref, wb_ref, rope_ref, hn_ref, qa_ref, qan_ref, q_ref):
        u, _, _ = _norm_mod(h_ref[...], vec_ref[...])
        ub = u.astype(BF)
        hn_ref[...] = ub
        qa = _dot(ub, wa_ref[...])
        qa_ref[...] = qa
        r2 = lax.rsqrt(_mean1(qa * qa) + EPS)
        qanb = ((qa * r2) * g2_ref[0:1, :]).astype(BF)
        qan_ref[...] = qanb
        q = _dot(qanb, wb_ref[...])
        tab = rope_ref[...]
        for hd in range(NH):
            q_ref[:, hd * HP:(hd + 1) * HP] = (_rope(q[:, hd * HP:(hd + 1) * HP], tab) * EXP2_SCALE).astype(BF)

    tok = lambda i: (i, 0)
    fixed = lambda i: (0, 0)
    return pl.pallas_call(
        body, name="q_fwd", grid=(S // TS,),
        out_shape=[jax.ShapeDtypeStruct((S, D), BF), jax.ShapeDtypeStruct((S, Q_LORA), F32),
                   jax.ShapeDtypeStruct((S, Q_LORA), BF), jax.ShapeDtypeStruct((S, NH * HP), BF)],
        in_specs=[_rows((TS, D), tok), _rows((8, D), fixed), VMEM_FULL, _rows((8, Q_LORA), fixed), VMEM_FULL,
                  _rows((TS, 3 * HP), tok)],
        out_specs=[_rows((TS, D), tok), _rows((TS, Q_LORA), tok), _rows((TS, Q_LORA), tok), _rows((TS, NH * HP), tok)],
        compiler_params=_params(1),
    )(h, vec, wqa, g2, wqb, rope)


def _q_bwd(dq, h, qa, dho, vec, wqa, g2, wqb, rope):
    S = h.shape[0]

    def body(dq_ref, h_ref, qa_ref, dho_ref, vec_ref, wa_ref, g2_ref, wb_ref, rope_ref,
             dh_ref, dqb_ref, dqa_ref, ps_ref, ps2_ref):
        i = pl.program_id(0)
        vec, tab = vec_ref[...], rope_ref[...]
        for hd in range(NH):
            dqb_ref[:, hd * HP:(hd + 1) * HP] = _rope_t(dq_ref[:, hd * HP:(hd + 1) * HP], tab).astype(BF)
        dqan = _dot_nt(dqb_ref[...], wb_ref[...])
        qa = qa_ref[...]
        r2 = lax.rsqrt(_mean1(qa * qa) + EPS)
        dqa, dg2 = _rms_bwd(dqan, qa, r2, g2_ref[0:1, :])
        dqab = dqa.astype(BF)
        dqa_ref[...] = dqab
        du = _dot_nt(dqab, wa_ref[...])
        _, xhat, r = _norm_mod(h_ref[...], vec)
        dhn, dgn, dsh, dsc = _norm_mod_bwd(du, xhat, r, vec)
        dh_ref[...] = dho_ref[...] + dhn
        _accumulate(ps_ref, i == 0, [dgn, dsh, dsc])
        _accumulate(ps2_ref, i == 0, [dg2])

    tok = lambda i: (i, 0)
    fixed = lambda i: (0, 0)
    return pl.pallas_call(
        body, name="q_bwd", grid=(S // TS,),
        out_shape=[jax.ShapeDtypeStruct((S, D), F32), jax.ShapeDtypeStruct((S, NH * HP), BF),
                   jax.ShapeDtypeStruct((S, Q_LORA), BF), jax.ShapeDtypeStruct((8, D), F32),
                   jax.ShapeDtypeStruct((8, Q_LORA), F32)],
        in_specs=[_rows((TS, NH * HP), tok), _rows((TS, D), tok), _rows((TS, Q_LORA), tok), _rows((TS, D), tok),
                  _rows((8, D), fixed), VMEM_FULL, _rows((8, Q_LORA), fixed), VMEM_FULL, _rows((TS, 3 * HP), tok)],
        out_specs=[_rows((TS, D), tok), _rows((TS, NH * HP), tok), _rows((TS, Q_LORA), tok), _rows((8, D), fixed),
                   _rows((8, Q_LORA), fixed)],
        compiler_params=_params(1),
    )(dq, h, qa, dho, vec, wqa, g2, wqb, rope)


def _attn_fwd(q, k, v):
    S = q.shape[0]
    TA = TA_FWD
    nq = S // TA
    rg = min(TA, ATT_ROWS)
    groups = TA // rg

    def softmax_pv(scores, vt, state, masks):
        out = []
        for g in range(groups):
            m, acc = state[g]
            s = scores[g] if masks is None else jnp.where(masks[g], scores[g], NEG)
            m_new = jnp.maximum(m, jnp.max(s, axis=1, keepdims=True))
            p = jnp.exp2(s - m_new)
            vg = vt[g] if isinstance(vt, list) else vt
            out.append((m_new, jnp.exp2(m - m_new) * acc + _dot(p.astype(BF), vg)))
        return tuple(out)

    def body(q_ref, k_ref, v_ref, o_ref, lse_ref):
        qi = pl.program_id(1)
        qs = [q_ref[g * rg:(g + 1) * rg, :] for g in range(groups)]

        def keys(j):
            return pl.ds(pl.multiple_of(j * TA, TA), TA)

        def scores_of(j):
            kt = k_ref[keys(j), :]
            return tuple(_dot_nt(qs[g], kt) for g in range(groups))

        state = tuple((jnp.full((rg, 1), NEG, F32), jnp.zeros((rg, HP), F32)) for _ in range(groups))

        def step(kj, state):
            return softmax_pv(scores_of(kj), v_ref[keys(kj), :], state, None)

        state = lax.fori_loop(0, qi, step, state)
        kd, vd = k_ref[keys(qi), :], v_ref[keys(qi), :]
        ends = [(g + 1) * rg for g in range(groups)]
        diag_scores = tuple(_dot_nt(qs[g], kd[:ends[g]]) for g in range(groups))
        masks = [lax.broadcasted_iota(jnp.int32, (rg, ends[g]), 1)
                 <= lax.broadcasted_iota(jnp.int32, (rg, ends[g]), 0) + g * rg for g in range(groups)]
        final = softmax_pv(diag_scores, [vd[:ends[g]] for g in range(groups)], state, masks)
        lane = lax.broadcasted_iota(jnp.int32, (rg, HP), 1)
        for g in range(groups):
            m, acc = final[g]
            l = jnp.sum(jnp.where(lane == 0, acc, 0.0), axis=1, keepdims=True)
            o_ref[g * rg:(g + 1) * rg, :] = (acc / l).astype(BF)
            lse_ref[g * rg:(g + 1) * rg, :] = m + jnp.log(l) * LOG2E

    return pl.pallas_call(
        body, name="attn_fwd", grid=(NH, nq),
        out_shape=[jax.ShapeDtypeStruct((S, NH * HP), BF), jax.ShapeDtypeStruct((NH, S, 1), F32)],
        in_specs=[_rows((TA, HP), lambda h, i: (i, h)), _rows((S, HP), lambda h, i: (0, h)),
                  _rows((S, HP), lambda h, i: (0, h))],
        out_specs=[_rows((TA, HP), lambda h, i: (i, h)), _rows((None, TA, 1), lambda h, i: (h, i, 0))],
        compiler_params=_params(2),
    )(q, k, v)


def _attn_bwd(q, k, v, do, lse, delta):
    S = q.shape[0]
    nq = S // TA
    rg = min(TA, ATT_ROWS)
    groups = TA // rg

    def body(q_ref, do_ref, lse_ref, dl_ref, k_ref, v_ref, dq_ref, dk_ref, dv_ref, dka_ref, dva_ref):
        kj = pl.program_id(1)

        @pl.when(kj == 0)
        def _():
            dq_ref[...] = jnp.zeros((S, HP), F32)

        dka_ref[...] = jnp.zeros((TA, HP), F32)
        dva_ref[...] = jnp.zeros((TA, HP), F32)
        kt, vt = k_ref[...], v_ref[...]

        def tile(qi, diagonal):
            rows = [pl.ds(pl.multiple_of(qi * TA + g * rg, rg), rg) for g in range(groups)]
            ends = [(g + 1) * rg if diagonal else TA for g in range(groups)]
            qg = [q_ref[r, :] for r in rows]
            dog = [do_ref[r, :] for r in rows]
            scores = [_dot_nt(qg[g], kt[:ends[g]]) for g in range(groups)]
            dps = [_dot_nt(dog[g], vt[:ends[g]]) for g in range(groups)]
            for g in range(groups):
                p = jnp.exp2(scores[g] - lse_ref[rows[g], :])
                if diagonal:
                    col = lax.broadcasted_iota(jnp.int32, (rg, ends[g]), 1)
                    row = lax.broadcasted_iota(jnp.int32, (rg, ends[g]), 0)
                    p = jnp.where(col <= row + g * rg, p, 0.0)
                ds = p * (dps[g] - dl_ref[rows[g], :])
                pb, dsb = p.astype(BF), ds.astype(BF)
                dva_ref[0:ends[g], :] += _dot_tn(pb, dog[g])
                dka_ref[0:ends[g], :] += _dot_tn(dsb, qg[g])
                dq_ref[rows[g], :] += _dot(dsb, kt[:ends[g]]) * SM_SCALE

        tile(kj, True)

        def step(qi, carry):
            tile(qi, False)
            return carry

        lax.fori_loop(kj + 1, nq, step, 0)
        dk_ref[...] = dka_ref[...] * LN2
        dv_ref[...] = dva_ref[...]

    head = lambda h, j: (0, h)
    col1 = lambda h, j: (h, 0, 0)
    return pl.pallas_call(
        body, name="attn_bwd", grid=(NH, nq), out_shape=[jax.ShapeDtypeStruct((S, NH * HP), F32)] * 3,
        in_specs=[_rows((S, HP), head), _rows((S, HP), head), _rows((None, S, 1), col1), _rows((None, S, 1), col1),
                  _rows((TA, HP), lambda h, j: (j, h)), _rows((TA, HP), lambda h, j: (j, h))],
        out_specs=[_rows((S, HP), head), _rows((TA, HP), lambda h, j: (j, h)), _rows((TA, HP), lambda h, j: (j, h))],
        scratch_shapes=[pltpu.VMEM((TA, HP), F32), pltpu.VMEM((TA, HP), F32)], compiler_params=_params(2),
    )(q, do, lse, delta, k, v)


def _attn_out_fwd(o, h, wo, vec):
    S = h.shape[0]

    def body(o_ref, h_ref, wo_ref, vec_ref, ho_ref):
        ho_ref[...] = h_ref[...] + vec_ref[3:4, :] * _dot(o_ref[...], wo_ref[...])

    tok = lambda i: (i, 0)
    return pl.pallas_call(
        body, name="attn_out_fwd", grid=(S // TS,), out_shape=jax.ShapeDtypeStruct((S, D), F32),
        in_specs=[_rows((TS, NH * HP), tok), _rows((TS, D), tok), VMEM_FULL, _rows((8, D), lambda i: (0, 0))],
        out_specs=_rows((TS, D), tok), compiler_params=_params(1),
    )(o, h, wo, vec)


def _attn_out_bwd(dho, o, wo, vec):
    S = dho.shape[0]

    def body(dho_ref, o_ref, wo_ref, vec_ref, do_ref, dl_ref, dhb_ref):
        dho_v = dho_ref[...]
        dhb_ref[...] = dho_v.astype(BF)
        do = _dot_nt((vec_ref[3:4, :] * dho_v).astype(BF), wo_ref[...])
        do_ref[...] = do.astype(BF)
        prod = do * o_ref[...].astype(F32)
        for hd in range(NH):
            dl_ref[hd] = jnp.sum(prod[:, hd * HP:(hd + 1) * HP], axis=1, keepdims=True)

    tok = lambda i: (i, 0)
    return pl.pallas_call(
        body, name="attn_out_bwd", grid=(S // TS,),
        out_shape=[jax.ShapeDtypeStruct((S, NH * HP), BF), jax.ShapeDtypeStruct((NH, S, 1), F32),
                   jax.ShapeDtypeStruct((S, D), BF)],
        in_specs=[_rows((TS, D), tok), _rows((TS, NH * HP), tok), VMEM_FULL, _rows((8, D), lambda i: (0, 0))],
        out_specs=[_rows((TS, NH * HP), tok), _rows((NH, TS, 1), lambda i: (0, i, 0)), _rows((TS, D), tok)],
        compiler_params=_params(1),
    )(dho, o, wo, vec)


def _final(h, target, fg):
    S = h.shape[0]

    def body(h_ref, t_ref, g_ref, dh_ref, ps_ref):
        hv, g = h_ref[...], g_ref[0:1, :]
        r = lax.rsqrt(_mean1(hv * hv) + EPS)
        xhat = hv * r
        err = xhat * g - t_ref[...]
        loss = 0.5 * jnp.sum(_mean1(err * err), axis=0, keepdims=True)
        dy = err * (1.0 / D)
        dxhat = dy * g
        dh_ref[...] = r * (dxhat - xhat * _mean1(dxhat * xhat))
        _accumulate(ps_ref, pl.program_id(0) == 0, [_sum0(dy * xhat), jnp.broadcast_to(loss, (1, D))])

    tok = lambda i: (i, 0)
    return pl.pallas_call(
        body, name="final_loss", grid=(S // TS,),
        out_shape=[jax.ShapeDtypeStruct((S, D), F32), jax.ShapeDtypeStruct((8, D), F32)],
        in_specs=[_rows((TS, D), tok), _rows((TS, D), tok), _rows((8, D), lambda i: (0, 0))],
        out_specs=[_rows((TS, D), tok), _rows((8, D), lambda i: (0, 0))], compiler_params=_params(1),
    )(h, target, fg)


def _row_tile(r):
    if r <= 256:
        return r
    for t in range(256, 7, -8):
        if r % t == 0:
            return t
    return r


def _adamw(parts, w, m, v):
    P, R, C = parts.shape
    tr = _row_tile(R)

    def body(p_ref, w_ref, m_ref, v_ref, g_ref, d_ref, mo_ref, vo_ref):
        g = p_ref[0].astype(F32)
        for k in range(1, P):
            g = g + p_ref[k].astype(F32)
        g_ref[...] = g
        m2 = B1 * m_ref[...] + (1.0 - B1) * g
        v2 = B2 * v_ref[...] + (1.0 - B2) * (g * g)
        mo_ref[...] = m2
        vo_ref[...] = v2
        m_hat = m2 / (1.0 - B1 ** STEP)
        v_hat = v2 / (1.0 - B2 ** STEP)
        d_ref[...] = -LR * (m_hat / (jnp.sqrt(v_hat) + EPS_ADAM) + WD * w_ref[...])

    blk = _rows((tr, C), lambda i: (i, 0))
    return pl.pallas_call(
        body, name="adamw", grid=(R // tr,), out_shape=[jax.ShapeDtypeStruct((R, C), F32)] * 4,
        in_specs=[_rows((P, tr, C), lambda i: (0, i, 0)), blk, blk, blk], out_specs=[blk] * 4,
        compiler_params=_params(1),
    )(parts, w, m, v)


_WEIGHTS = ['ada_w', 'ada_b', 'norm_g', 'ffn_w13', 'ffn_w2', 'conv_w_pw1', 'conv_b_pw1', 'conv_w_dw', 'conv_b_dw',
            'conv_ln_g', 'conv_ln_b', 'conv_w_pw2', 'conv_b_pw2', 'kv_ada_w', 'kv_ada_b', 'kv_norm_g', 'w_kv_a',
            'kv_a_norm_g', 'w_kv_b', 'w_q_a', 'q_a_norm_g', 'w_q_b', 'w_o', 'final_norm_g']


def _vec(rows):
    rows = [r.reshape(1, -1).astype(F32) for r in rows]
    return jnp.concatenate(rows + [jnp.zeros((8 - len(rows), rows[0].shape[1]), F32)], axis=0)


def kernel(x, c, positions, ada_w, ada_b, norm_g, ffn_w13, ffn_w2, conv_w_pw1, conv_b_pw1, conv_w_dw, conv_b_dw, conv_ln_g, conv_ln_b, conv_w_pw2, conv_b_pw2, kv_ada_w, kv_ada_b, kv_norm_g, w_kv_a, kv_a_norm_g, w_kv_b, w_q_a, q_a_norm_g, w_q_b, w_o, final_norm_g, loss_target, m_ada_w, m_ada_b, m_norm_g, m_ffn_w13, m_ffn_w2, m_conv_w_pw1, m_conv_b_pw1, m_conv_w_dw, m_conv_b_dw, m_conv_ln_g, m_conv_ln_b, m_conv_w_pw2, m_conv_b_pw2, m_kv_ada_w, m_kv_ada_b, m_kv_norm_g, m_w_kv_a, m_kv_a_norm_g, m_w_kv_b, m_w_q_a, m_q_a_norm_g, m_w_q_b, m_w_o, m_final_norm_g, v_ada_w, v_ada_b, v_norm_g, v_ffn_w13, v_ffn_w2, v_conv_w_pw1, v_conv_b_pw1, v_conv_w_dw, v_conv_b_dw, v_conv_ln_g, v_conv_ln_b, v_conv_w_pw2, v_conv_b_pw2, v_kv_ada_w, v_kv_ada_b, v_kv_norm_g, v_w_kv_a, v_kv_a_norm_g, v_w_kv_b, v_w_q_a, v_q_a_norm_g, v_w_q_b, v_w_o, v_final_norm_g):
    given = dict(locals())
    S = x.shape[1]
    me = 4 * lax.axis_index("x") + 2 * lax.axis_index("y") + lax.axis_index("c")

    small = jnp.concatenate([
        conv_w_dw[0], conv_b_dw, conv_ln_g, conv_ln_b, conv_b_pw2,
        norm_g.reshape(6, 128), conv_b_pw1.reshape(2, 128),
        c.reshape(8, 128), jnp.zeros((5, 128), F32)], axis=0)
    bf = lambda w: w.astype(BF)
    full_w13 = lambda g: jnp.transpose(g.reshape(N_DEV, D, 704), (1, 0, 2)).reshape(D, 2 * DFF)
    full_w2 = lambda g: g.reshape(DFF, D)
    got = _gather_two_level([small, bf(ffn_w13[0, 0]), bf(ffn_w2[0, 0])], "gather_first")
    w13_00, w2_00 = full_w13(got[1]), full_w2(got[2])
    sm = got[0]
    chan = lambda lo, hi: jnp.moveaxis(sm[:, lo:hi, :], 0, 1).reshape(hi - lo, D)
    w_dw_f, b_dw_f, ln_g_f, ln_b_f, b_pw2_f = chan(0, 31), chan(31, 32), chan(32, 33), chan(33, 34), chan(34, 35)
    norm_f = chan(35, 41).reshape(2, 3, D)
    b_pw1_f = sm[:, 41:43, :].reshape(1, 2 * D)
    c_all = sm[:, 43:51, :].reshape(N_DEV, D)

    n_ada = ada_w.shape[2]
    n_kva = kv_ada_w.shape[1]
    modp = _mod_fwd(c_all, ada_w[0], ada_w[1], kv_ada_w)
    (modr,) = _exchange([(modp.reshape(N_DEV, 1, 2 * n_ada + n_kva), "scatter")], "scatter_mod")
    modr = modr[:, 0, :]
    mod = jnp.transpose(modr[:, :2 * n_ada].reshape(N_DEV, 2, n_ada), (1, 0, 2)).reshape(2, 9 * D) + ada_b
    mod = mod.reshape(2, 9, D)
    kvmod = (modr[:, 2 * n_ada:].reshape(2 * D) + kv_ada_b).reshape(2, D)

    def sub_vec(l, idx):
        return _vec([norm_f[l, idx], mod[l, 3 * idx], mod[l, 3 * idx + 1], mod[l, 3 * idx + 2]])

    vec_kv = _vec([kv_norm_g, kvmod[0], kvmod[1]])
    cw = jnp.concatenate([w_dw_f, b_dw_f, ln_g_f, ln_b_f, b_pw2_f, mod[0, 5].reshape(1, D), jnp.zeros((4, D), F32)], axis=0)
    b1v = _vec([b_pw1_f])
    g_kva = _vec([kv_a_norm_g])
    g_qa = _vec([q_a_norm_g[0]])
    fgv = _vec([final_norm_g])

    inv_freq = 10000.0 ** (-jnp.arange(0, ROPE, 2, dtype=F32) / ROPE)
    ang = positions[0].astype(F32)[:, None] * inv_freq
    cs, sn = jnp.cos(ang), jnp.sin(ang)
    z16, z32, z64 = jnp.zeros((S, 16), F32), jnp.zeros((S, 32), F32), jnp.zeros((S, 64), F32)
    rope = jnp.concatenate([jnp.ones((S, 64), F32), cs, cs, z32,
                            z64, z16, sn, z32,
                            z64, -sn, z16, z32], axis=1)

    def merged(blocks, name):
        return [jnp.swapaxes(m, 0, 1).reshape((N_DEV,) + m.shape[2:]) for m in _sibling_merge(blocks, name)]

    h0 = x[0]
    group1 = [bf(conv_w_pw1[0]), bf(conv_w_pw2[0]), bf(ffn_w13[0, 1]), bf(ffn_w2[0, 1])]
    (h1, u00, a00, b00), blocks1 = _ffn_fwd(h0, sub_vec(0, 0), w13_00, w2_00, comm=(group1, False))
    g_pw1, g_pw2, g_w13, g_w2 = merged(blocks1, "merge_group1")
    pw1_f = jnp.transpose(g_pw1, (1, 0, 2)).reshape(D, 2 * D)
    pw2_f = g_pw2.reshape(D, D)
    w13_01, w2_01 = full_w13(g_w13), full_w2(g_w2)
    hn_c, pre = _conv_in_fwd(h1, sub_vec(0, 1), pw1_f, b1v)
    group2 = [bf(w_kv_a), bf(w_kv_b), bf(ffn_w13[1, 0]), bf(ffn_w2[1, 0])]
    (u2, z_c, h2), blocks2 = _conv_out_fwd(pre, h1, cw, pw2_f, comm=(group2, False))
    g_kva_w, g_kvb_w, g_w13, g_w2 = merged(blocks2, "merge_group2")
    wkva = g_kva_w.reshape(D, KV_LORA + ROPE)
    wkva_f = jnp.concatenate([wkva[:, :KV_LORA], jnp.zeros((D, 64), BF), wkva[:, KV_LORA:], jnp.zeros((D, 32), BF)], axis=1)
    wkvb_f = jnp.transpose(g_kvb_w, (1, 0, 2)).reshape(KV_LORA, NH * HP)
    w13_10, w2_10 = full_w13(g_w13), full_w2(g_w2)
    group3 = [bf(w_q_a[0]), bf(w_q_b[0]), bf(w_o[0]), bf(ffn_w13[1, 1]), bf(ffn_w2[1, 1])]
    (h3, u01, a01, b01), blocks3 = _ffn_fwd(h2, sub_vec(0, 2), w13_01, w2_01, comm=(group3, False))
    g_qa_w, g_qb_w, g_wo, g_w13, g_w2 = merged(blocks3, "merge_group3")
    wqa_f = g_qa_w.reshape(D, Q_LORA)
    wqb = jnp.transpose(g_qb_w, (1, 0, 2)).reshape(Q_LORA, NH, 96)
    wqb_f = jnp.pad(wqb, ((0, 0), (0, 0), (0, HP - 96))).reshape(Q_LORA, NH * HP)
    wo_f = jnp.pad(g_wo.reshape(NH, 64, D), ((0, 0), (64, 0), (0, 0))).reshape(NH * HP, D)
    w13_11, w2_11 = full_w13(g_w13), full_w2(g_w2)
    hn_kv, ckv, ckn, k_all, v_all = _kv_fwd(h3, vec_kv, wkva_f, g_kva, wkvb_f, rope)
    (h4, u10, a10, b10), _ = _ffn_fwd(h3, sub_vec(1, 0), w13_10, w2_10)
    hn_q, qa, qan, q_all = _q_fwd(h4, sub_vec(1, 1), wqa_f, g_qa, wqb_f, rope)
    o_all, lse = _attn_fwd(q_all, k_all, v_all)
    h5 = _attn_out_fwd(o_all, h4, wo_f, sub_vec(1, 1))
    (h6, u11, a11, b11), _ = _ffn_fwd(h5, sub_vec(1, 2), w13_11, w2_11)

    dh6, ps_fin = _final(h6, loss_target[0], fgv)
    loss = lax.psum(ps_fin[1, 0], ("x", "y", "c"))

    core = lax.axis_index("c").reshape(1).astype(jnp.int32)

    def pair_sums(sends, name):
        by_core = [s.reshape((4, 2) + s.shape[1:]).swapaxes(0, 1) for s in sends]
        from_sibling = _pair_swap(by_core, name)
        return [_pair_add(core, a.reshape(2, -1, a.shape[-1]), b.reshape(-1, b.shape[-1])).reshape(b.shape)
                for a, b in zip(by_core, from_sibling)]

    def ffn_back(dho, h_in, u, a, b, l, i, w13, w2, comm=None):
        vec = sub_vec(l, 2 * i)
        (dh, da, db, t, dhb, ps), reduced = _ffn_bwd(dho, h_in, a, b, vec, w13, w2, comm=comm)
        dwa = _wgrad(u, da, D, CH, "wgrad_w13")
        dwb = _wgrad(u, db, D, CH, "wgrad_w13")
        dw2, gs = _wgrad(t, dhb, CH, D, "wgrad_w2", gate=(w2, _vec([0.5 * vec[3]])))
        dgate = 0.5 * jnp.sum(gs[:, 0, :], axis=0)
        send13 = jnp.transpose(jnp.concatenate([dwa, dwb], axis=1).reshape(D, N_DEV, 704), (1, 0, 2))
        return dh, send13, dw2.reshape(N_DEV, 352, D), ps, dgate, reduced

    dh5, s13_11, s2_11, ps11, dg11, _ = ffn_back(dh6, h5, u11, a11, b11, 1, 1, w13_11, w2_11)
    vec_m1 = sub_vec(1, 1)
    do_all, delta, dhb5 = _attn_out_bwd(dh5, o_all, wo_f, vec_m1)
    dwo_p, gs_o = _wgrad(o_all, dhb5, D, D, "wgrad_wo", gate=(wo_f, _vec([vec_m1[3]])))
    dgm1 = jnp.sum(gs_o[:, 0, :], axis=0)
    dq_all, dk_all, dv_all = _attn_bwd(q_all, k_all, v_all, do_all, lse, delta)
    dh4, dqb, dqab, ps_q, ps_q2 = _q_bwd(dq_all, h4, qa, dh5, vec_m1, wqa_f, g_qa, wqb_f, rope)
    dwqb_p = _wgrad(qan, dqb, Q_LORA, D, "wgrad_wqb")
    dwqa = _wgrad(hn_q, dqab, D, Q_LORA, "wgrad_wqa")
    dh3a, s13_10, s2_10, ps10, dg10, _ = ffn_back(dh4, h3, u10, a10, b10, 1, 0, w13_10, w2_10)
    sums_a = pair_sums([
        s13_11, s2_11, s13_10, s2_10, dwqa.reshape(N_DEV, 128, Q_LORA),
        jnp.transpose(dwqb_p.reshape(Q_LORA, NH, HP)[:, :, :96].reshape(Q_LORA, N_DEV, 192), (1, 0, 2)),
        dwo_p.reshape(NH, HP, D)[:, 64:, :].reshape(N_DEV, 128, D)], "pair_swap_a")
    dh3, dkva, dkvb, ps_kv, ps_kv2 = _kv_bwd(dk_all, dv_all, h3, ckv, dh3a, vec_kv, wkva_f, g_kva, wkvb_f, rope)
    dwkva_p = _wgrad(hn_kv, dkva, D, KVA_P, "wgrad_wkva")
    dwkvb = _wgrad(ckn, dkvb, KV_LORA, D, "wgrad_wkvb")
    dh2, s13_01, s2_01, ps01, dg01, red_a = ffn_back(dh3, h2, u01, a01, b01, 0, 1, w13_01, w2_01, comm=(sums_a, True))
    vec_m0 = sub_vec(0, 1)
    du2, dhb2, ps_co = _conv_out_bwd(dh2, u2, cw, pw2_f)
    dpw2, gs_c = _wgrad(z_c, dhb2, D, D, "wgrad_pw2", gate=(pw2_f, _vec([vec_m0[3]])))
    dgm0 = jnp.sum(gs_c[:, 0, :], axis=0) + b_pw2_f[0] * ps_co[2]
    dh1, dpre, ps_ci, ps_dw, ps_b1 = _conv_in_bwd(du2, pre, h1, dh2, vec_m0, cw, pw1_f)
    dpw1 = _wgrad(hn_c, dpre, D, D, "wgrad_pw1")
    sums_b = pair_sums([
        s13_01, s2_01, jnp.transpose(dpw1.reshape(D, N_DEV, 256), (1, 0, 2)), dpw2.reshape(N_DEV, 128, D),
        jnp.concatenate([dwkva_p[:, :KV_LORA], dwkva_p[:, KV_LORA + 64:KV_LORA + 96]], axis=1).reshape(N_DEV, 128, KV_LORA + ROPE),
        jnp.transpose(dwkvb.reshape(KV_LORA, N_DEV, 256), (1, 0, 2))], "pair_swap_b")
    dh0, s13_00, s2_00, ps00, dg00, red_b = ffn_back(dh1, h0, u00, a00, b00, 0, 0, w13_00, w2_00, comm=(sums_b, True))
    red_c = _chip_scatter(pair_sums([s13_00, s2_00], "pair_swap_c"), "chip_scatter_last")
    r13_11, r2_11, r13_10, r2_10, r_wqa, r_wqb, r_wo = red_a
    r13_01, r2_01, r_pw1, r_pw2, r_wkva, r_wkvb = red_b
    r13_00, r2_00 = red_c

    dmod = jnp.stack([
        jnp.stack([ps00[1], ps00[2], dg00, ps_ci[1], ps_ci[2], dgm0, ps01[1], ps01[2], dg01]),
        jnp.stack([ps10[1], ps10[2], dg10, ps_q[1], ps_q[2], dgm1, ps11[1], ps11[2], dg11])])
    dnorm = jnp.stack([ps00[0], ps_ci[0], ps01[0], ps10[0], ps_q[0], ps11[0]])
    pieces = [dnorm, ps_b1[0], ps_dw[0:31], ps_dw[31], ps_co[0], ps_co[1], vec_m0[3] * ps_co[2],
              ps_kv[0], ps_kv2[0], ps_q2[0], ps_fin[0], dmod, ps_kv[1], ps_kv[2]]
    sizes = [int(np.prod(p.shape)) for p in pieces]
    offs = np.concatenate([[0], np.cumsum(sizes)]).astype(int)
    flat = jnp.concatenate([p.reshape(-1) for p in pieces]).reshape(-1, 128)
    (part,) = _exchange([(flat, "gather")], "gather_partials")
    part = part.reshape(N_DEV, -1)

    def piece(i, rows, cols):
        return part[:, offs[i]:offs[i + 1]].reshape(N_DEV, rows, cols)

    def mine(i, rows):
        return lax.dynamic_slice_in_dim(piece(i, rows, D), me * 128, 128, axis=2)

    dmod_all = piece(11, 2, 9 * D)
    c_t = jnp.transpose(c_all)
    g_ada = jnp.stack([_mod_wgrad(c_t, lax.dynamic_slice_in_dim(dmod_all[:, l], me * n_ada, n_ada, axis=1))
                       for l in range(2)])
    dkvmod_all = jnp.concatenate([piece(12, 1, D)[:, 0], piece(13, 1, D)[:, 0]], axis=1)
    g_kvada = _mod_wgrad(c_t, lax.dynamic_slice_in_dim(dkvmod_all, me * n_kva, n_kva, axis=1))

    parts = {
        'ada_w': g_ada.reshape(1, 2 * D, n_ada),
        'ada_b': dmod_all,
        'norm_g': mine(0, 6),
        'ffn_w13': jnp.stack([r13_00, r13_01, r13_10, r13_11], axis=1).reshape(4, 4 * D, 704),
        'ffn_w2': jnp.stack([r2_00, r2_01, r2_10, r2_11], axis=1).reshape(4, 4 * 352, D),
        'conv_w_pw1': r_pw1,
        'conv_b_pw1': lax.dynamic_slice_in_dim(piece(1, 1, 2 * D), me * 256, 256, axis=2),
        'conv_w_dw': mine(2, 31),
        'conv_b_dw': mine(3, 1),
        'conv_ln_g': mine(4, 1),
        'conv_ln_b': mine(5, 1),
        'conv_w_pw2': r_pw2,
        'conv_b_pw2': mine(6, 1),
        'kv_ada_w': g_kvada.reshape(1, D, n_kva),
        'kv_ada_b': dkvmod_all.reshape(N_DEV, 1, 2 * D),
        'kv_norm_g': piece(7, 1, D),
        'w_kv_a': r_wkva,
        'kv_a_norm_g': piece(8, 1, KV_LORA),
        'w_kv_b': r_wkvb,
        'w_q_a': r_wqa,
        'q_a_norm_g': piece(9, 1, Q_LORA),
        'w_q_b': r_wqb,
        'w_o': r_wo,
        'final_norm_g': piece(10, 1, D),
    }
    grads, deltas, new_m, new_v = [], [], [], []
    for name in _WEIGHTS:
        w = given[name]
        p = parts[name]
        shape2 = p.shape[1:]
        g, dlt, m2, v2 = _adamw(p, w.reshape(shape2), given['m_' + name].reshape(shape2), given['v_' + name].reshape(shape2))
        grads.append(g.reshape(w.shape))
        deltas.append(dlt.reshape(w.shape))
        new_m.append(m2.reshape(w.shape))
        new_v.append(v2.reshape(w.shape))
    return (loss, dh0.reshape(1, S, D), *grads, *deltas, *new_m, *new_v)
```

```python
import functools

import numpy as np
import jax
import jax.numpy as jnp
from jax import lax
from jax.experimental import pallas as pl
from jax.experimental.pallas import tpu as pltpu

F32, BF = jnp.float32, jnp.bfloat16

D = 1024
DFF = 2816
CH = 1408
FFN_CHUNKS = ((0, 1536), (1536, 1280))
NH = 16
HP = 128
KV_LORA, Q_LORA, ROPE = 256, 512, 32
KVA_P = 384
CONV_W = 31
HALO = 32
EPS = 1e-6
SM_SCALE = float((64 + 32) ** -0.5)
LOG2E = 1.4426950408889634
LN2 = 0.6931471805599453
EXP2_SCALE = SM_SCALE * LOG2E
NEG = -1e30
N_DEV = 8
MESH = pl.DeviceIdType.MESH

TS = 256
TS_FFN_FWD = 512
TA = 1024
TA_FWD = 2048
ATT_ROWS = 256
ATT_ROWS_BWD = 512
TW = 2048
VMEM_LIMIT = 56 * 1024 * 1024

LR, B1, B2, EPS_ADAM, WD, STEP = 0.001, 0.9, 0.999, 1e-08, 0.01, 10

VMEM_FULL = pl.BlockSpec(memory_space=pltpu.VMEM)
HBM_FULL = pl.BlockSpec(memory_space=pltpu.HBM)


def _params(n_grid):
    return pltpu.CompilerParams(dimension_semantics=("arbitrary",) * n_grid, vmem_limit_bytes=VMEM_LIMIT)


def _dot(a, b):
    return jnp.dot(a, b, preferred_element_type=F32)


def _dot_nt(a, b):
    return lax.dot_general(a, b, (((1,), (1,)), ((), ())), preferred_element_type=F32)


def _dot_tn(a, b):
    return lax.dot_general(a, b, (((0,), (0,)), ((), ())), preferred_element_type=F32)


def _sum0(x):
    return jnp.sum(x, axis=0, keepdims=True)


def _mean1(x):
    return jnp.mean(x, axis=-1, keepdims=True)


def _sigmoid(x):
    return jax.nn.sigmoid(x)


def _rows(shape, imap):
    return pl.BlockSpec(shape, imap)


def _norm_mod(h, vec):
    r = lax.rsqrt(_mean1(h * h) + EPS)
    xhat = h * r
    u = (xhat * vec[0:1]) * (1.0 + vec[2:3]) + vec[1:2]
    return u, xhat, r


def _norm_mod_bwd(du, xhat, r, vec):
    g = vec[0:1]
    dxn = du * (1.0 + vec[2:3])
    dsh = _sum0(du)
    dsc = _sum0(du * (xhat * g))
    dg = _sum0(dxn * xhat)
    dxhat = dxn * g
    dh = r * (dxhat - xhat * _mean1(dxhat * xhat))
    return dh, dg, dsh, dsc


def _rms_bwd(dy, x, r, g):
    xhat = x * r
    dg = _sum0(dy * xhat)
    dxhat = dy * g
    return r * (dxhat - xhat * _mean1(dxhat * xhat)), dg


def _accumulate(ref, first, rows):
    @pl.when(first)
    def _():
        ref[...] = jnp.zeros(ref.shape, ref.dtype)

    for i, row in enumerate(rows):
        ref[i:i + 1, :] += row


def _rope(x, tab):
    return x * tab[:, 0:HP] + pltpu.roll(x, 16, 1) * tab[:, HP:2 * HP] + pltpu.roll(x, HP - 16, 1) * tab[:, 2 * HP:3 * HP]


def _rope_t(dy, tab):
    return (dy * tab[:, 0:HP] + pltpu.roll(dy * tab[:, HP:2 * HP], HP - 16, 1)
            + pltpu.roll(dy * tab[:, 2 * HP:3 * HP], 16, 1))


def _exchange(items, name):
    n = len(items)

    def body(*refs):
        ins, outs = refs[:n], refs[n:2 * n]
        send_sems, recv_sems, local_sems = refs[2 * n:]
        x, y, c = lax.axis_index("x"), lax.axis_index("y"), lax.axis_index("c")
        me = 4 * x + 2 * y + c

        def source(j, dev):
            return ins[j] if items[j][1] == "gather" else ins[j].at[dev]

        own = [pltpu.make_async_copy(source(j, me), outs[j].at[me], local_sems.at[j]) for j in range(n)]
        for cp in own:
            cp.start()
        remote = []
        for d in range(1, N_DEV):
            px = 1 - x if d & 4 else x
            py = 1 - y if d & 2 else y
            pc = 1 - c if d & 1 else c
            peer = 4 * px + 2 * py + pc
            for j in range(n):
                pltpu.make_async_remote_copy(
                    src_ref=source(j, peer), dst_ref=outs[j].at[me], send_sem=send_sems.at[j, d - 1],
                    recv_sem=recv_sems.at[j, d - 1], device_id=(px, py, pc), device_id_type=MESH).start()
                remote.append(pltpu.make_async_remote_copy(
                    src_ref=source(j, peer), dst_ref=outs[j].at[peer], send_sem=send_sems.at[j, d - 1],
                    recv_sem=recv_sems.at[j, d - 1], device_id=(px, py, pc), device_id_type=MESH))
        for cp in remote:
            cp.wait_send()
            cp.wait_recv()
        for cp in own:
            cp.wait()

    out_shape = []
    for arr, mode in items:
        shp = (N_DEV,) + tuple(arr.shape) if mode == "gather" else tuple(arr.shape)
        out_shape.append(jax.ShapeDtypeStruct(shp, arr.dtype))
    return pl.pallas_call(
        body, name=name, out_shape=out_shape,
        in_specs=[HBM_FULL] * n, out_specs=[HBM_FULL] * n,
        scratch_shapes=[pltpu.SemaphoreType.DMA((n, N_DEV - 1)), pltpu.SemaphoreType.DMA((n, N_DEV - 1)),
                        pltpu.SemaphoreType.DMA((n,))],
        compiler_params=pltpu.CompilerParams(has_side_effects=True),
    )(*[a for a, _ in items])


def _gather_two_level(arrs, name):
    n = len(arrs)

    def body(*refs):
        ins, outs = refs[:n], refs[n:2 * n]
        send_sems, recv_sems, local_sems = refs[2 * n:]
        x, y, c = lax.axis_index("x"), lax.axis_index("y"), lax.axis_index("c")
        sibling = (x, y, 1 - c)
        chips = [(1 - x, y), (x, 1 - y), (1 - x, 1 - y)]

        def slot(j, px, py, pc):
            return outs[j].at[4 * px + 2 * py + pc]

        def copy(j, k, block, to, src=None):
            return pltpu.make_async_remote_copy(
                src_ref=slot(j, *block) if src is None else src, dst_ref=slot(j, *block),
                send_sem=send_sems.at[j, k], recv_sem=recv_sems.at[j, k], device_id=to, device_id_type=MESH)

        own = [pltpu.make_async_copy(ins[j], slot(j, x, y, c), local_sems.at[j]) for j in range(n)]
        for cp in own:
            cp.start()
        sent = []
        for j in range(n):
            sent.append(copy(j, 0, (x, y, c), sibling, src=ins[j]))
            sent += [copy(j, 1 + i, (x, y, c), (*chip, c), src=ins[j]) for i, chip in enumerate(chips)]
        for cp in sent:
            cp.start()
        for i, chip in enumerate(chips):
            for j in range(n):
                copy(j, 1 + i, (*chip, c), (x, y, c)).wait_recv()
                passed = copy(j, 4 + i, (*chip, c), sibling)
                passed.start()
                sent.append(passed)
        for j in range(n):
            copy(j, 0, (x, y, 1 - c), (x, y, c)).wait_recv()
            for i, chip in enumerate(chips):
                copy(j, 4 + i, (*chip, 1 - c), (x, y, c)).wait_recv()
        for cp in sent:
            cp.wait_send()
        for cp in own:
            cp.wait()

    return pl.pallas_call(
        body, name=name, out_shape=[jax.ShapeDtypeStruct((N_DEV,) + tuple(a.shape), a.dtype) for a in arrs],
        in_specs=[HBM_FULL] * n, out_specs=[HBM_FULL] * n,
        scratch_shapes=[pltpu.SemaphoreType.DMA((n, N_DEV - 1)), pltpu.SemaphoreType.DMA((n, N_DEV - 1)),
                        pltpu.SemaphoreType.DMA((n,))],
        compiler_params=pltpu.CompilerParams(has_side_effects=True),
    )(*arrs)


def _pair_swap(arrs, name):
    n = len(arrs)

    def body(*refs):
        ins, outs = refs[:n], refs[n:2 * n]
        send_sems, recv_sems = refs[2 * n:]
        x, y, c = lax.axis_index("x"), lax.axis_index("y"), lax.axis_index("c")
        copies = [pltpu.make_async_remote_copy(
            src_ref=ins[j].at[1 - c], dst_ref=outs[j], send_sem=send_sems.at[j], recv_sem=recv_sems.at[j],
            device_id=(x, y, 1 - c), device_id_type=MESH) for j in range(n)]
        for cp in copies:
            cp.start()
        for cp in copies:
            cp.wait_send()
            cp.wait_recv()

    return pl.pallas_call(
        body, name=name, out_shape=[jax.ShapeDtypeStruct(tuple(a.shape[1:]), a.dtype) for a in arrs],
        in_specs=[HBM_FULL] * n, out_specs=[HBM_FULL] * n,
        scratch_shapes=[pltpu.SemaphoreType.DMA((n,)), pltpu.SemaphoreType.DMA((n,))],
        compiler_params=pltpu.CompilerParams(has_side_effects=True),
    )(*arrs)


def _chip_comm(arrs, scatter):
    n = len(arrs)
    out_shapes = [jax.ShapeDtypeStruct(tuple(a.shape) if scatter else (2, 4) + tuple(a.shape), a.dtype) for a in arrs]
    scratch = [pltpu.SemaphoreType.DMA((n, 3)), pltpu.SemaphoreType.DMA((n, 3)), pltpu.SemaphoreType.DMA((n,))]

    def copies(ins, outs, send_sems, recv_sems, local_sems, arriving):
        x, y, c = lax.axis_index("x"), lax.axis_index("y"), lax.axis_index("c")
        here = 2 * x + y

        def source(j, chip):
            return ins[j].at[chip] if scatter else ins[j]

        def slot(j, chip):
            return outs[j].at[chip] if scatter else outs[j].at[c].at[chip]

        own = [pltpu.make_async_copy(source(j, here), slot(j, here), local_sems.at[j]) for j in range(n)]
        remote = []
        for i, (px, py) in enumerate([(1 - x, y), (x, 1 - y), (1 - x, 1 - y)]):
            there = 2 * px + py
            for j in range(n):
                remote.append(pltpu.make_async_remote_copy(
                    src_ref=source(j, there), dst_ref=slot(j, there if arriving else here),
                    send_sem=send_sems.at[j, i], recv_sem=recv_sems.at[j, i], device_id=(px, py, c),
                    device_id_type=MESH))
        return own, remote

    def start(*refs):
        own, remote = copies(*refs, arriving=False)
        for cp in own + remote:
            cp.start()

    def finish(*refs):
        own, remote = copies(*refs, arriving=True)
        for cp in remote:
            cp.wait_send()
            cp.wait_recv()
        for cp in own:
            cp.wait()

    return out_shapes, scratch, start, finish


def _chip_scatter(arrs, name):
    n = len(arrs)
    out_shapes, scratch, start, finish = _chip_comm(arrs, True)

    def body(*refs):
        start(refs[:n], refs[n:2 * n], *refs[2 * n:])
        finish(refs[:n], refs[n:2 * n], *refs[2 * n:])

    return pl.pallas_call(
        body, name=name, out_shape=out_shapes, in_specs=[HBM_FULL] * n, out_specs=[HBM_FULL] * n,
        scratch_shapes=scratch, compiler_params=pltpu.CompilerParams(has_side_effects=True),
    )(*arrs)


def _hosted_call(body, args, comm, *, name, n_steps, in_specs, out_specs, out_shape, scratch_shapes=()):
    if comm is None:
        res = pl.pallas_call(
            body, name=name, grid=(n_steps,), out_shape=list(out_shape), in_specs=list(in_specs),
            out_specs=list(out_specs), scratch_shapes=list(scratch_shapes), compiler_params=_params(1))(*args)
        return res, []
    arrs, scatter = comm
    c_shapes, c_scratch, start, finish = _chip_comm(arrs, scatter)
    n_in, n_out, n_sc, k = len(in_specs), len(out_specs), len(scratch_shapes), len(arrs)

    def hosting(*refs):
        ins, cin = refs[:n_in], refs[n_in:n_in + k]
        outs, cout = refs[n_in + k:n_in + k + n_out], refs[n_in + k + n_out:n_in + 2 * k + n_out]
        scratch = refs[n_in + 2 * k + n_out:n_in + 2 * k + n_out + n_sc]
        sems = refs[n_in + 2 * k + n_out + n_sc:]
        step = pl.program_id(0)

        @pl.when(step == 0)
        def _():
            start(cin, cout, *sems)

        body(*ins, *outs, *scratch)

        @pl.when(step == n_steps - 1)
        def _():
            finish(cin, cout, *sems)

    res = pl.pallas_call(
        hosting, name=name + "_hosting", grid=(n_steps,), out_shape=list(out_shape) + c_shapes,
        in_specs=list(in_specs) + [HBM_FULL] * k, out_specs=list(out_specs) + [HBM_FULL] * k,
        scratch_shapes=list(scratch_shapes) + c_scratch,
        compiler_params=pltpu.CompilerParams(dimension_semantics=("arbitrary",), vmem_limit_bytes=VMEM_LIMIT,
                                             has_side_effects=True))(*args, *arrs)
    return res[:n_out], res[n_out:]


def _sibling_merge(arrs, name):
    n = len(arrs)

    def body(*refs):
        bufs = refs[n:2 * n]
        send_sems, recv_sems = refs[2 * n:]
        x, y, c = lax.axis_index("x"), lax.axis_index("y"), lax.axis_index("c")
        waits = []
        for j in range(n):
            pltpu.make_async_remote_copy(
                src_ref=bufs[j].at[c], dst_ref=bufs[j].at[c], send_sem=send_sems.at[j], recv_sem=recv_sems.at[j],
                device_id=(x, y, 1 - c), device_id_type=MESH).start()
            waits.append(pltpu.make_async_remote_copy(
                src_ref=bufs[j].at[c], dst_ref=bufs[j].at[1 - c], send_sem=send_sems.at[j],
                recv_sem=recv_sems.at[j], device_id=(x, y, 1 - c), device_id_type=MESH))
        for cp in waits:
            cp.wait_send()
            cp.wait_recv()

    return pl.pallas_call(
        body, name=name, out_shape=[jax.ShapeDtypeStruct(tuple(a.shape), a.dtype) for a in arrs],
        in_specs=[HBM_FULL] * n, out_specs=[HBM_FULL] * n, input_output_aliases={j: j for j in range(n)},
        scratch_shapes=[pltpu.SemaphoreType.DMA((n,)), pltpu.SemaphoreType.DMA((n,))],
        compiler_params=pltpu.CompilerParams(has_side_effects=True),
    )(*arrs)


def _pair_add(core, a, b):
    _, R, C = a.shape
    tr = _row_tile(R)

    def body(core_ref, a_ref, b_ref, o_ref):
        o_ref[...] = (a_ref[...].astype(F32) + b_ref[...].astype(F32)).astype(BF)

    blk = _rows((tr, C), lambda i, core: (i, 0))
    grid_spec = pltpu.PrefetchScalarGridSpec(
        num_scalar_prefetch=1, grid=(R // tr,),
        in_specs=[_rows((None, tr, C), lambda i, core: (core[0], i, 0)), blk], out_specs=blk)
    return pl.pallas_call(
        body, name="pair_add", grid_spec=grid_spec, out_shape=jax.ShapeDtypeStruct((R, C), BF),
        compiler_params=_params(1),
    )(core, a, b)


def _mod_fwd(c_all, w0, w1, wkv):
    n0, n1, n2 = w0.shape[1], w1.shape[1], wkv.shape[1]

    def body(c_ref, w0_ref, w1_ref, w2_ref, o_ref):
        cc = c_ref[...]
        s = cc * _sigmoid(cc)
        o_ref[:, 0:n0] = _dot(s, w0_ref[...])
        o_ref[:, n0:n0 + n1] = _dot(s, w1_ref[...])
        o_ref[:, n0 + n1:n0 + n1 + n2] = _dot(s, w2_ref[...])

    return pl.pallas_call(
        body, name="mod_fwd", out_shape=jax.ShapeDtypeStruct((N_DEV, n0 + n1 + n2), F32),
        in_specs=[VMEM_FULL] * 4, out_specs=VMEM_FULL,
        compiler_params=pltpu.CompilerParams(vmem_limit_bytes=VMEM_LIMIT),
    )(c_all, w0, w1, wkv)


def _mod_wgrad(c_t, dm):
    C = dm.shape[1]
    tr = 256

    def body(ct_ref, dm_ref, o_ref):
        ct = ct_ref[...]
        s = ct * _sigmoid(ct)
        dmv = dm_ref[...]
        lane = lax.broadcasted_iota(jnp.int32, (tr, N_DEV), 1)
        acc = jnp.zeros((tr, C), F32)
        for r in range(N_DEV):
            col = jnp.sum(jnp.where(lane == r, s, 0.0), axis=1, keepdims=True)
            acc = acc + col * dmv[r:r + 1, :]
        o_ref[...] = acc

    return pl.pallas_call(
        body, name="mod_wgrad", grid=(D // tr,), out_shape=jax.ShapeDtypeStruct((D, C), F32),
        in_specs=[_rows((tr, N_DEV), lambda i: (i, 0)), _rows((N_DEV, C), lambda i: (0, 0))],
        out_specs=_rows((tr, C), lambda i: (i, 0)), compiler_params=_params(1),
    )(c_t, dm)


def _ffn_fwd(h, vec, w13, w2, comm=None):
    S = h.shape[0]
    ts = min(S, TS_FFN_FWD)

    def body(h_ref, vec_ref, w13_ref, w2_ref, ho_ref, u_ref, a_ref, b_ref):
        hv, vec = h_ref[...], vec_ref[...]
        u, _, _ = _norm_mod(hv, vec)
        ub = u.astype(BF)
        u_ref[...] = ub
        y = jnp.zeros((ts, D), F32)
        for c0, cw in FFN_CHUNKS:
            a = _dot(ub, w13_ref[:, c0:c0 + cw])
            b = _dot(ub, w13_ref[:, DFF + c0:DFF + c0 + cw])
            a_ref[:, c0:c0 + cw] = a.astype(BF)
            b_ref[:, c0:c0 + cw] = b.astype(BF)
            t = (a * _sigmoid(a)) * b
            y = y + _dot(t.astype(BF), w2_ref[c0:c0 + cw, :])
        ho_ref[...] = hv + (0.5 * vec[3:4]) * y

    tok = lambda i: (i, 0)
    return _hosted_call(
        body, (h, vec, w13, w2), comm, name="ffn_fwd", n_steps=S // ts,
        out_shape=[jax.ShapeDtypeStruct((S, D), F32), jax.ShapeDtypeStruct((S, D), BF),
                   jax.ShapeDtypeStruct((S, DFF), BF), jax.ShapeDtypeStruct((S, DFF), BF)],
        in_specs=[_rows((ts, D), tok), _rows((8, D), lambda i: (0, 0)), VMEM_FULL, VMEM_FULL],
        out_specs=[_rows((ts, D), tok), _rows((ts, D), tok), _rows((ts, DFF), tok), _rows((ts, DFF), tok)])


def _ffn_bwd(dho, h, a, b, vec, w13, w2, comm=None):
    S = h.shape[0]

    def body(dho_ref, h_ref, a_ref, b_ref, vec_ref, w13_ref, w2_ref,
             dh_ref, da_ref, db_ref, t_ref, dhb_ref, ps_ref):
        dho_v, vec = dho_ref[...], vec_ref[...]
        dhb_ref[...] = dho_v.astype(BF)
        dyb = ((0.5 * vec[3:4]) * dho_v).astype(BF)
        du = jnp.zeros((TS, D), F32)
        for c0, cw in FFN_CHUNKS:
            av = a_ref[:, c0:c0 + cw].astype(F32)
            bv = b_ref[:, c0:c0 + cw].astype(F32)
            dt = _dot_nt(dyb, w2_ref[c0:c0 + cw, :])
            sig = _sigmoid(av)
            sl = av * sig
            t_ref[:, c0:c0 + cw] = (sl * bv).astype(BF)
            dab = (dt * bv * (sig * (1.0 + av * (1.0 - sig)))).astype(BF)
            dbb = (dt * sl).astype(BF)
            da_ref[:, c0:c0 + cw] = dab
            db_ref[:, c0:c0 + cw] = dbb
            du = du + _dot_nt(dab, w13_ref[:, c0:c0 + cw]) + _dot_nt(dbb, w13_ref[:, DFF + c0:DFF + c0 + cw])
        _, xhat, r = _norm_mod(h_ref[...], vec)
        dhn, dg, dsh, dsc = _norm_mod_bwd(du, xhat, r, vec)
        dh_ref[...] = dho_v + dhn
        _accumulate(ps_ref, pl.program_id(0) == 0, [dg, dsh, dsc])

    tok = lambda i: (i, 0)
    return _hosted_call(
        body, (dho, h, a, b, vec, w13, w2), comm, name="ffn_bwd", n_steps=S // TS,
        out_shape=[jax.ShapeDtypeStruct((S, D), F32), jax.ShapeDtypeStruct((S, DFF), BF),
                   jax.ShapeDtypeStruct((S, DFF), BF), jax.ShapeDtypeStruct((S, DFF), BF),
                   jax.ShapeDtypeStruct((S, D), BF), jax.ShapeDtypeStruct((8, D), F32)],
        in_specs=[_rows((TS, D), tok), _rows((TS, D), tok), _rows((TS, DFF), tok), _rows((TS, DFF), tok),
                  _rows((8, D), lambda i: (0, 0)), VMEM_FULL, VMEM_FULL],
        out_specs=[_rows((TS, D), tok), _rows((TS, DFF), tok), _rows((TS, DFF), tok), _rows((TS, DFF), tok),
                   _rows((TS, D), tok), _rows((8, D), lambda i: (0, 0))])


def _wgrad(a, b, tm, tn, name, gate=None):
    S, M = a.shape
    N = b.shape[1]
    n_s = S // TW

    def body(*refs):
        if gate is None:
            a_ref, b_ref, o_ref, acc_ref = refs
        else:
            a_ref, b_ref, w_ref, sc_ref, o_ref, gs_ref, acc_ref = refs
        s = pl.program_id(2)

        @pl.when(s == 0)
        def _():
            acc_ref[...] = jnp.zeros((tm, tn), F32)

        acc_ref[...] += _dot_tn(a_ref[...], b_ref[...])

        @pl.when(s == n_s - 1)
        def _():
            acc = acc_ref[...]
            if gate is None:
                o_ref[...] = acc.astype(BF)
            else:
                o_ref[...] = (acc * sc_ref[0:1, :]).astype(BF)
                gs_ref[...] = jnp.broadcast_to(_sum0(acc * w_ref[...].astype(F32)), (8, tn))

    in_specs = [_rows((TW, tm), lambda m, n, s: (s, m)), _rows((TW, tn), lambda m, n, s: (s, n))]
    out_shape = [jax.ShapeDtypeStruct((M, N), BF)]
    out_specs = [_rows((tm, tn), lambda m, n, s: (m, n))]
    args = [a, b]
    if gate is not None:
        in_specs += [_rows((tm, tn), lambda m, n, s: (m, n)), _rows((8, tn), lambda m, n, s: (0, n))]
        out_shape.append(jax.ShapeDtypeStruct((M // tm, 8, N), F32))
        out_specs.append(_rows((None, 8, tn), lambda m, n, s: (m, 0, n)))
        args += list(gate)
    res = pl.pallas_call(
        body, name=name, grid=(M // tm, N // tn, n_s), out_shape=out_shape, in_specs=in_specs,
        out_specs=out_specs, scratch_shapes=[pltpu.VMEM((tm, tn), F32)], compiler_params=_params(3),
    )(*args)
    return res[0] if gate is None else (res[0], res[1])


def _conv_in_fwd(h, vec, w1, b1):
    S = h.shape[0]

    def body(h_ref, vec_ref, w_ref, b1_ref, hn_ref, pre_ref):
        u, _, _ = _norm_mod(h_ref[...], vec_ref[...])
        ub = u.astype(BF)
        hn_ref[...] = ub
        pre_ref[...] = _dot(ub, w_ref[...]) + b1_ref[0:1, :]

    tok = lambda i: (i, 0)
    return pl.pallas_call(
        body, name="conv_in_fwd", grid=(S // TS,),
        out_shape=[jax.ShapeDtypeStruct((S, D), BF), jax.ShapeDtypeStruct((S, 2 * D), F32)],
        in_specs=[_rows((TS, D), tok), _rows((8, D), lambda i: (0, 0)), VMEM_FULL, _rows((8, 2 * D), lambda i: (0, 0))],
        out_specs=[_rows((TS, D), tok), _rows((TS, 2 * D), tok)], compiler_params=_params(1),
    )(h, vec, w1, b1)


def _glu(pre):
    return pre[:, :D] * _sigmoid(pre[:, D:])


def _tap_groups(offset):
    groups = {}
    for j in range(CONV_W):
        off = offset(j)
        groups.setdefault(off % 8, []).append((off - off % 8, j))
    return [(phase, sorted(taps)) for phase, taps in sorted(groups.items())]


def _shift_window(dst_ref, win_ref, phase, rows):
    dst_ref[0:rows, :] = win_ref[phase:phase + rows, :]


def _layernorm(u2):
    mu = _mean1(u2)
    xc = u2 - mu
    rstd = lax.rsqrt(_mean1(xc * xc) + EPS)
    return xc * rstd, rstd


def _conv_out_fwd(pre, h, cw, w2, comm=None):
    S = h.shape[0]
    hb = TS // HALO

    def body(pre_ref, ph_ref, h_ref, cw_ref, w2_ref, u2_ref, z_ref, ho_ref, win_ref, sh_ref):
        i = pl.program_id(0)
        cwv = cw_ref[...]
        win_ref[0:HALO, :] = jnp.where(i > 0, _glu(ph_ref[...]), 0.0)
        win_ref[HALO:HALO + TS, :] = _glu(pre_ref[...])
        u2 = jnp.broadcast_to(cwv[31:32], (TS, D))
        for phase, taps in _tap_groups(lambda j: HALO - (CONV_W - 1) + j):
            _shift_window(sh_ref, win_ref, phase, taps[-1][0] + TS)
            for lo, j in taps:
                u2 = u2 + cwv[j:j + 1] * sh_ref[lo:lo + TS, :]
        u2_ref[...] = u2
        xh, _ = _layernorm(u2)
        un = xh * cwv[32:33] + cwv[33:34]
        zb = (un * _sigmoid(un)).astype(BF)
        z_ref[...] = zb
        y = _dot(zb, w2_ref[...]) + cwv[34:35]
        ho_ref[...] = h_ref[...] + cwv[35:36] * y

    tok = lambda i: (i, 0)
    return _hosted_call(
        body, (pre, pre, h, cw, w2), comm, name="conv_out_fwd", n_steps=S // TS,
        out_shape=[jax.ShapeDtypeStruct((S, D), F32), jax.ShapeDtypeStruct((S, D), BF), jax.ShapeDtypeStruct((S, D), F32)],
        in_specs=[_rows((TS, 2 * D), tok), _rows((HALO, 2 * D), lambda i: (jnp.maximum(i * hb - 1, 0), 0)),
                  _rows((TS, D), tok), _rows((40, D), lambda i: (0, 0)), VMEM_FULL],
        out_specs=[_rows((TS, D), tok), _rows((TS, D), tok), _rows((TS, D), tok)],
        scratch_shapes=[pltpu.VMEM((TS + HALO, D), F32)] * 2)


def _conv_out_bwd(dho, u2, cw, w2):
    S = dho.shape[0]

    def body(dho_ref, u2_ref, cw_ref, w2_ref, du2_ref, dhb_ref, ps_ref):
        dho_v, cwv = dho_ref[...], cw_ref[...]
        dhb_ref[...] = dho_v.astype(BF)
        dz = _dot_nt((cwv[35:36] * dho_v).astype(BF), w2_ref[...])
        xh, rstd = _layernorm(u2_ref[...])
        un = xh * cwv[32:33] + cwv[33:34]
        sig = _sigmoid(un)
        dun = dz * (sig * (1.0 + un * (1.0 - sig)))
        dxh = dun * cwv[32:33]
        du2_ref[...] = rstd * (dxh - _mean1(dxh) - xh * _mean1(dxh * xh))
        _accumulate(ps_ref, pl.program_id(0) == 0, [_sum0(dun * xh), _sum0(dun), _sum0(dho_v)])

    tok = lambda i: (i, 0)
    return pl.pallas_call(
        body, name="conv_out_bwd", grid=(S // TS,),
        out_shape=[jax.ShapeDtypeStruct((S, D), F32), jax.ShapeDtypeStruct((S, D), BF), jax.ShapeDtypeStruct((8, D), F32)],
        in_specs=[_rows((TS, D), tok), _rows((TS, D), tok), _rows((40, D), lambda i: (0, 0)), VMEM_FULL],
        out_specs=[_rows((TS, D), tok), _rows((TS, D), tok), _rows((8, D), lambda i: (0, 0))],
        compiler_params=_params(1),
    )(dho, u2, cw, w2)


def _conv_in_bwd(du2, pre, h, dho, vec, cw, w1):
    S = h.shape[0]
    n_t = S // TS
    hb = TS // HALO

    def body(du2_ref, dh2h_ref, pre_ref, ph_ref, h_ref, dho_ref, vec_ref, cw_ref, w1_ref,
             dh_ref, dpre_ref, ps_ref, pw_ref, pb_ref, winu_ref, wind_ref, sh_ref):
        i = pl.program_id(0)
        vec, cwv = vec_ref[...], cw_ref[...]
        pre = pre_ref[...]
        av, sg = pre[:, :D], _sigmoid(pre[:, D:])
        winu_ref[0:HALO, :] = jnp.where(i > 0, _glu(ph_ref[...]), 0.0)
        winu_ref[HALO:HALO + TS, :] = av * sg
        du2v = du2_ref[...]
        wind_ref[0:TS, :] = du2v
        wind_ref[TS:TS + HALO, :] = jnp.where(i < n_t - 1, dh2h_ref[...], 0.0)

        @pl.when(i == 0)
        def _():
            pw_ref[...] = jnp.zeros((32, D), F32)

        for phase, taps in _tap_groups(lambda j: HALO - (CONV_W - 1) + j):
            _shift_window(sh_ref, winu_ref, phase, taps[-1][0] + TS)
            for lo, j in taps:
                pw_ref[j:j + 1, :] += _sum0(du2v * sh_ref[lo:lo + TS, :])
        pw_ref[31:32, :] += _sum0(du2v)
        du1 = jnp.zeros((TS, D), F32)
        for phase, taps in _tap_groups(lambda j: CONV_W - 1 - j):
            _shift_window(sh_ref, wind_ref, phase, taps[-1][0] + TS)
            for lo, j in taps:
                du1 = du1 + cwv[j:j + 1] * sh_ref[lo:lo + TS, :]
        da = du1 * sg
        dg = du1 * av * sg * (1.0 - sg)
        dab, dgb = da.astype(BF), dg.astype(BF)
        dpre_ref[:, :D] = dab
        dpre_ref[:, D:] = dgb

        @pl.when(i == 0)
        def _():
            pb_ref[...] = jnp.zeros((8, 2 * D), F32)

        pb_ref[0:1, :D] += _sum0(da)
        pb_ref[0:1, D:] += _sum0(dg)
        du = _dot_nt(dab, w1_ref[:, :D]) + _dot_nt(dgb, w1_ref[:, D:])
        _, xhat, r = _norm_mod(h_ref[...], vec)
        dhn, dgn, dsh, dsc = _norm_mod_bwd(du, xhat, r, vec)
        dh_ref[...] = dho_ref[...] + dhn
        _accumulate(ps_ref, i == 0, [dgn, dsh, dsc])

    tok = lambda i: (i, 0)
    fixed = lambda i: (0, 0)
    return pl.pallas_call(
        body, name="conv_in_bwd", grid=(n_t,),
        out_shape=[jax.ShapeDtypeStruct((S, D), F32), jax.ShapeDtypeStruct((S, 2 * D), BF),
                   jax.ShapeDtypeStruct((8, D), F32), jax.ShapeDtypeStruct((32, D), F32),
                   jax.ShapeDtypeStruct((8, 2 * D), F32)],
        in_specs=[_rows((TS, D), tok), _rows((HALO, D), lambda i: (jnp.minimum((i + 1) * hb, S // HALO - 1), 0)),
                  _rows((TS, 2 * D), tok), _rows((HALO, 2 * D), lambda i: (jnp.maximum(i * hb - 1, 0), 0)),
                  _rows((TS, D), tok), _rows((TS, D), tok), _rows((8, D), fixed), _rows((40, D), fixed), VMEM_FULL],
        out_specs=[_rows((TS, D), tok), _rows((TS, 2 * D), tok), _rows((8, D), fixed), _rows((32, D), fixed),
                   _rows((8, 2 * D), fixed)],
        scratch_shapes=[pltpu.VMEM((TS + HALO, D), F32)] * 3,
        compiler_params=_params(1),
    )(du2, du2, pre, pre, h, dho, vec, cw, w1)


def _lane():
    return lax.broadcasted_iota(jnp.int32, (TS, HP), 1)


def _kv_fwd(h, vec, wkva, g2, wkvb, rope):
    S = h.shape[0]

    def body(h_ref, vec_ref, wa_ref, g2_ref, wb_ref, rope_ref, hn_ref, ckv_ref, ckn_ref, k_ref, v_ref):
        u, _, _ = _norm_mod(h_ref[...], vec_ref[...])
        ub = u.astype(BF)
        hn_ref[...] = ub
        kva = _dot(ub, wa_ref[...])
        ckv = kva[:, :KV_LORA]
        ckv_ref[...] = ckv
        r2 = lax.rsqrt(_mean1(ckv * ckv) + EPS)
        cknb = ((ckv * r2) * g2_ref[0:1, :]).astype(BF)
        ckn_ref[...] = cknb
        kvb = _dot(cknb, wb_ref[...])
        kpe = _rope(kva[:, KV_LORA:KVA_P], rope_ref[...])
        lane = _lane()
        ones_lane0 = jnp.where(lane == 0, 1.0, 0.0)
        for hd in range(NH):
            blk = kvb[:, hd * HP:(hd + 1) * HP]
            k_ref[:, hd * HP:(hd + 1) * HP] = jnp.where(lane < 64, blk, kpe).astype(BF)
            v_ref[:, hd * HP:(hd + 1) * HP] = jnp.where(lane >= 64, blk, ones_lane0).astype(BF)

    tok = lambda i: (i, 0)
    fixed = lambda i: (0, 0)
    return pl.pallas_call(
        body, name="kv_fwd", grid=(S // TS,),
        out_shape=[jax.ShapeDtypeStruct((S, D), BF), jax.ShapeDtypeStruct((S, KV_LORA), F32),
                   jax.ShapeDtypeStruct((S, KV_LORA), BF), jax.ShapeDtypeStruct((S, NH * HP), BF),
                   jax.ShapeDtypeStruct((S, NH * HP), BF)],
        in_specs=[_rows((TS, D), tok), _rows((8, D), fixed), VMEM_FULL, _rows((8, KV_LORA), fixed), VMEM_FULL,
                  _rows((TS, 3 * HP), tok)],
        out_specs=[_rows((TS, D), tok), _rows((TS, KV_LORA), tok), _rows((TS, KV_LORA), tok),
                   _rows((TS, NH * HP), tok), _rows((TS, NH * HP), tok)],
        compiler_params=_params(1),
    )(h, vec, wkva, g2, wkvb, rope)


def _kv_bwd(dk, dv, h, ckv, dho, vec, wkva, g2, wkvb, rope):
    S = h.shape[0]

    def body(dk_ref, dv_ref, h_ref, ckv_ref, dho_ref, vec_ref, wa_ref, g2_ref, wb_ref, rope_ref,
             dh_ref, dkva_ref, dkvb_ref, ps_ref, ps2_ref):
        i = pl.program_id(0)
        vec = vec_ref[...]
        lane = _lane()
        dkpe = jnp.zeros((TS, HP), F32)
        for hd in range(NH):
            dkh = dk_ref[:, hd * HP:(hd + 1) * HP]
            dvh = dv_ref[:, hd * HP:(hd + 1) * HP]
            dkvb_ref[:, hd * HP:(hd + 1) * HP] = jnp.where(lane < 64, dkh, dvh).astype(BF)
            dkpe = dkpe + jnp.where(lane >= 64, dkh, 0.0)
        dkpe = _rope_t(dkpe, rope_ref[...])
        dckn = _dot_nt(dkvb_ref[...], wb_ref[...])
        ckv = ckv_ref[...]
        r2 = lax.rsqrt(_mean1(ckv * ckv) + EPS)
        dckv, dg2 = _rms_bwd(dckn, ckv, r2, g2_ref[0:1, :])
        dkva_ref[:, :KV_LORA] = dckv.astype(BF)
        dkva_ref[:, KV_LORA:KVA_P] = dkpe.astype(BF)
        du = _dot_nt(dkva_ref[...], wa_ref[...])
        _, xhat, r = _norm_mod(h_ref[...], vec)
        dhn, dgn, dsh, dsc = _norm_mod_bwd(du, xhat, r, vec)
        dh_ref[...] = dho_ref[...] + dhn
        _accumulate(ps_ref, i == 0, [dgn, dsh, dsc])
        _accumulate(ps2_ref, i == 0, [dg2])

    tok = lambda i: (i, 0)
    fixed = lambda i: (0, 0)
    return pl.pallas_call(
        body, name="kv_bwd", grid=(S // TS,),
        out_shape=[jax.ShapeDtypeStruct((S, D), F32), jax.ShapeDtypeStruct((S, KVA_P), BF),
                   jax.ShapeDtypeStruct((S, NH * HP), BF), jax.ShapeDtypeStruct((8, D), F32),
                   jax.ShapeDtypeStruct((8, KV_LORA), F32)],
        in_specs=[_rows((TS, NH * HP), tok), _rows((TS, NH * HP), tok), _rows((TS, D), tok), _rows((TS, KV_LORA), tok),
                  _rows((TS, D), tok), _rows((8, D), fixed), VMEM_FULL, _rows((8, KV_LORA), fixed), VMEM_FULL,
                  _rows((TS, 3 * HP), tok)],
        out_specs=[_rows((TS, D), tok), _rows((TS, KVA_P), tok), _rows((TS, NH * HP), tok), _rows((8, D), fixed),
                   _rows((8, KV_LORA), fixed)],
        compiler_params=_params(1),
    )(dk, dv, h, ckv, dho, vec, wkva, g2, wkvb, rope)


def _q_fwd(h, vec, wqa, g2, wqb, rope):
    S = h.shape[0]

    def body(h_ref, vec_ref, wa_ref, g2_ref, wb_ref, rope_ref, hn_ref, qa_ref, qan_ref, q_ref):
        u, _, _ = _norm_mod(h_ref[...], vec_ref[...])
        ub = u.astype(BF)
        hn_ref[...] = ub
        qa = _dot(ub, wa_ref[...])
        qa_ref[...] = qa
        r2 = lax.rsqrt(_mean1(qa * qa) + EPS)
        qanb = ((qa * r2) * g2_ref[0:1, :]).astype(BF)
        qan_ref[...] = qanb
        q = _dot(qanb, wb_ref[...])
        tab = rope_ref[...]
        for hd in range(NH):
            q_ref[:, hd * HP:(hd + 1) * HP] = (_rope(q[:, hd * HP:(hd + 1) * HP], tab) * EXP2_SCALE).astype(BF)

    tok = lambda i: (i, 0)
    fixed = lambda i: (0, 0)
    return pl.pallas_call(
        body, name="q_fwd", grid=(S // TS,),
        out_shape=[jax.ShapeDtypeStruct((S, D), BF), jax.ShapeDtypeStruct((S, Q_LORA), F32),
                   jax.ShapeDtypeStruct((S, Q_LORA), BF), jax.ShapeDtypeStruct((S, NH * HP), BF)],
        in_specs=[_rows((TS, D), tok), _rows((8, D), fixed), VMEM_FULL, _rows((8, Q_LORA), fixed), VMEM_FULL,
                  _rows((TS, 3 * HP), tok)],
        out_specs=[_rows((TS, D), tok), _rows((TS, Q_LORA), tok), _rows((TS, Q_LORA), tok), _rows((TS, NH * HP), tok)],
        compiler_params=_params(1),
    )(h, vec, wqa, g2, wqb, rope)


def _q_bwd(dq, h, qa, dho, vec, wqa, g2, wqb, rope):
    S = h.shape[0]

    def body(dq_ref, h_ref, qa_ref, dho_ref, vec_ref, wa_ref, g2_ref, wb_ref, rope_ref,
             dh_ref, dqb_ref, dqa_ref, ps_ref, ps2_ref):
        i = pl.program_id(0)
        vec, tab = vec_ref[...], rope_ref[...]
        for hd in range(NH):
            dqb_ref[:, hd * HP:(hd + 1) * HP] = _rope_t(dq_ref[:, hd * HP:(hd + 1) * HP], tab).astype(BF)
        dqan = _dot_nt(dqb_ref[...], wb_ref[...])
        qa = qa_ref[...]
        r2 = lax.rsqrt(_mean1(qa * qa) + EPS)
        dqa, dg2 = _rms_bwd(dqan, qa, r2, g2_ref[0:1, :])
        dqab = dqa.astype(BF)
        dqa_ref[...] = dqab
        du = _dot_nt(dqab, wa_ref[...])
        _, xhat, r = _norm_mod(h_ref[...], vec)
        dhn, dgn, dsh, dsc = _norm_mod_bwd(du, xhat, r, vec)
        dh_ref[...] = dho_ref[...] + dhn
        _accumulate(ps_ref, i == 0, [dgn, dsh, dsc])
        _accumulate(ps2_ref, i == 0, [dg2])

    tok = lambda i: (i, 0)
    fixed = lambda i: (0, 0)
    return pl.pallas_call(
        body, name="q_bwd", grid=(S // TS,),
        out_shape=[jax.ShapeDtypeStruct((S, D), F32), jax.ShapeDtypeStruct((S, NH * HP), BF),
                   jax.ShapeDtypeStruct((S, Q_LORA), BF), jax.ShapeDtypeStruct((8, D), F32),
                   jax.ShapeDtypeStruct((8, Q_LORA), F32)],
        in_specs=[_rows((TS, NH * HP), tok), _rows((TS, D), tok), _rows((TS, Q_LORA), tok), _rows((TS, D), tok),
                  _rows((8, D), fixed), VMEM_FULL, _rows((8, Q_LORA), fixed), VMEM_FULL, _rows((TS, 3 * HP), tok)],
        out_specs=[_rows((TS, D), tok), _rows((TS, NH * HP), tok), _rows((TS, Q_LORA), tok), _rows((8, D), fixed),
                   _rows((8, Q_LORA), fixed)],
        compiler_params=_params(1),
    )(dq, h, qa, dho, vec, wqa, g2, wqb, rope)


def _attn_fwd(q, k, v):
    S = q.shape[0]
    TA = TA_FWD
    nq = S // TA
    rg = min(TA, ATT_ROWS)
    groups = TA // rg

    def softmax_pv(scores, vt, state, masks):
        out = []
        for g in range(groups):
            m, acc = state[g]
            s = scores[g] if masks is None else jnp.where(masks[g], scores[g], NEG)
            m_new = jnp.maximum(m, jnp.max(s, axis=1, keepdims=True))
            p = jnp.exp2(s - m_new)
            vg = vt[g] if isinstance(vt, list) else vt
            out.append((m_new, jnp.exp2(m - m_new) * acc + _dot(p.astype(BF), vg)))
        return tuple(out)

    def body(q_ref, k_ref, v_ref, o_ref, lse_ref):
        qi = pl.program_id(1)
        qs = [q_ref[g * rg:(g + 1) * rg, :] for g in range(groups)]

        def keys(j):
            return pl.ds(pl.multiple_of(j * TA, TA), TA)

        def scores_of(j):
            kt = k_ref[keys(j), :]
            return tuple(_dot_nt(qs[g], kt) for g in range(groups))

        state = tuple((jnp.full((rg, 1), NEG, F32), jnp.zeros((rg, HP), F32)) for _ in range(groups))

        def step(kj, state):
            return softmax_pv(scores_of(kj), v_ref[keys(kj), :], state, None)

        state = lax.fori_loop(0, qi, step, state)
        kd, vd = k_ref[keys(qi), :], v_ref[keys(qi), :]
        ends = [(g + 1) * rg for g in range(groups)]
        diag_scores = tuple(_dot_nt(qs[g], kd[:ends[g]]) for g in range(groups))
        masks = [lax.broadcasted_iota(jnp.int32, (rg, ends[g]), 1)
                 <= lax.broadcasted_iota(jnp.int32, (rg, ends[g]), 0) + g * rg for g in range(groups)]
        final = softmax_pv(diag_scores, [vd[:ends[g]] for g in range(groups)], state, masks)
        lane = lax.broadcasted_iota(jnp.int32, (rg, HP), 1)
        for g in range(groups):
            m, acc = final[g]
            l = jnp.sum(jnp.where(lane == 0, acc, 0.0), axis=1, keepdims=True)
            o_ref[g * rg:(g + 1) * rg, :] = (acc / l).astype(BF)
            lse_ref[g * rg:(g + 1) * rg, :] = m + jnp.log(l) * LOG2E

    return pl.pallas_call(
        body, name="attn_fwd", grid=(NH, nq),
        out_shape=[jax.ShapeDtypeStruct((S, NH * HP), BF), jax.ShapeDtypeStruct((NH, S, 1), F32)],
        in_specs=[_rows((TA, HP), lambda h, i: (i, h)), _rows((S, HP), lambda h, i: (0, h)),
                  _rows((S, HP), lambda h, i: (0, h))],
        out_specs=[_rows((TA, HP), lambda h, i: (i, h)), _rows((None, TA, 1), lambda h, i: (h, i, 0))],
        compiler_params=_params(2),
    )(q, k, v)


def _attn_bwd(q, k, v, do, lse, delta):
    S = q.shape[0]
    nq = S // TA
    rg_loop = min(TA, ATT_ROWS_BWD)
    rg_diag = min(TA, ATT_ROWS)

    def body(q_ref, do_ref, lse_ref, dl_ref, k_ref, v_ref, dq_ref, dk_ref, dv_ref, dka_ref, dva_ref):
        kj = pl.program_id(1)

        @pl.when(kj == 0)
        def _():
            dq_ref[...] = jnp.zeros((S, HP), F32)

        dka_ref[...] = jnp.zeros((TA, HP), F32)
        dva_ref[...] = jnp.zeros((TA, HP), F32)
        kt, vt = k_ref[...], v_ref[...]

        def tile(qi, diagonal):
            rg = rg_diag if diagonal else rg_loop
            groups = TA // rg
            rows = [pl.ds(pl.multiple_of(qi * TA + g * rg, rg), rg) for g in range(groups)]
            ends = [(g + 1) * rg if diagonal else TA for g in range(groups)]
            qg = [q_ref[r, :] for r in rows]
            dog = [do_ref[r, :] for r in rows]
            scores = [_dot_nt(qg[g], kt[:ends[g]]) for g in range(groups)]
            dps = [_dot_nt(dog[g], vt[:ends[g]]) for g in range(groups)]
            for g in range(groups):
                p = jnp.exp2(scores[g] - lse_ref[rows[g], :])
                if diagonal:
                    col = lax.broadcasted_iota(jnp.int32, (rg, ends[g]), 1)
                    row = lax.broadcasted_iota(jnp.int32, (rg, ends[g]), 0)
                    p = jnp.where(col <= row + g * rg, p, 0.0)
                ds = p * (dps[g] - dl_ref[rows[g], :])
                pb, dsb = p.astype(BF), ds.astype(BF)
                dva_ref[0:ends[g], :] += _dot_tn(pb, dog[g])
                dka_ref[0:ends[g], :] += _dot_tn(dsb, qg[g])
                dq_ref[rows[g], :] += _dot(dsb, kt[:ends[g]]) * SM_SCALE

        tile(kj, True)

        def step(qi, carry):
            tile(qi, False)
            return carry

        lax.fori_loop(kj + 1, nq, step, 0)
        dk_ref[...] = dka_ref[...] * LN2
        dv_ref[...] = dva_ref[...]

    head = lambda h, j: (0, h)
    col1 = lambda h, j: (h, 0, 0)
    return pl.pallas_call(
        body, name="attn_bwd", grid=(NH, nq), out_shape=[jax.ShapeDtypeStruct((S, NH * HP), F32)] * 3,
        in_specs=[_rows((S, HP), head), _rows((S, HP), head), _rows((None, S, 1), col1), _rows((None, S, 1), col1),
                  _rows((TA, HP), lambda h, j: (j, h)), _rows((TA, HP), lambda h, j: (j, h))],
        out_specs=[_rows((S, HP), head), _rows((TA, HP), lambda h, j: (j, h)), _rows((TA, HP), lambda h, j: (j, h))],
        scratch_shapes=[pltpu.VMEM((TA, HP), F32), pltpu.VMEM((TA, HP), F32)], compiler_params=_params(2),
    )(q, do, lse, delta, k, v)


def _attn_out_fwd(o, h, wo, vec):
    S = h.shape[0]

    def body(o_ref, h_ref, wo_ref, vec_ref, ho_ref):
        ho_ref[...] = h_ref[...] + vec_ref[3:4, :] * _dot(o_ref[...], wo_ref[...])

    tok = lambda i: (i, 0)
    return pl.pallas_call(
        body, name="attn_out_fwd", grid=(S // TS,), out_shape=jax.ShapeDtypeStruct((S, D), F32),
        in_specs=[_rows((TS, NH * HP), tok), _rows((TS, D), tok), VMEM_FULL, _rows((8, D), lambda i: (0, 0))],
        out_specs=_rows((TS, D), tok), compiler_params=_params(1),
    )(o, h, wo, vec)


def _attn_out_bwd(dho, o, wo, vec):
    S = dho.shape[0]

    def body(dho_ref, o_ref, wo_ref, vec_ref, do_ref, dl_ref, dhb_ref):
        dho_v = dho_ref[...]
        dhb_ref[...] = dho_v.astype(BF)
        do = _dot_nt((vec_ref[3:4, :] * dho_v).astype(BF), wo_ref[...])
        do_ref[...] = do.astype(BF)
        prod = do * o_ref[...].astype(F32)
        for hd in range(NH):
            dl_ref[hd] = jnp.sum(prod[:, hd * HP:(hd + 1) * HP], axis=1, keepdims=True)

    tok = lambda i: (i, 0)
    return pl.pallas_call(
        body, name="attn_out_bwd", grid=(S // TS,),
        out_shape=[jax.ShapeDtypeStruct((S, NH * HP), BF), jax.ShapeDtypeStruct((NH, S, 1), F32),
                   jax.ShapeDtypeStruct((S, D), BF)],
        in_specs=[_rows((TS, D), tok), _rows((TS, NH * HP), tok), VMEM_FULL, _rows((8, D), lambda i: (0, 0))],
        out_specs=[_rows((TS, NH * HP), tok), _rows((NH, TS, 1), lambda i: (0, i, 0)), _rows((TS, D), tok)],
        compiler_params=_params(1),
    )(dho, o, wo, vec)


def _final(h, target, fg):
    S = h.shape[0]

    def body(h_ref, t_ref, g_ref, dh_ref, ps_ref):
        hv, g = h_ref[...], g_ref[0:1, :]
        r = lax.rsqrt(_mean1(hv * hv) + EPS)
        xhat = hv * r
        err = xhat * g - t_ref[...]
        loss = 0.5 * jnp.sum(_mean1(err * err), axis=0, keepdims=True)
        dy = err * (1.0 / D)
        dxhat = dy * g
        dh_ref[...] = r * (dxhat - xhat * _mean1(dxhat * xhat))
        _accumulate(ps_ref, pl.program_id(0) == 0, [_sum0(dy * xhat), jnp.broadcast_to(loss, (1, D))])

    tok = lambda i: (i, 0)
    return pl.pallas_call(
        body, name="final_loss", grid=(S // TS,),
        out_shape=[jax.ShapeDtypeStruct((S, D), F32), jax.ShapeDtypeStruct((8, D), F32)],
        in_specs=[_rows((TS, D), tok), _rows((TS, D), tok), _rows((8, D), lambda i: (0, 0))],
        out_specs=[_rows((TS, D), tok), _rows((8, D), lambda i: (0, 0))], compiler_params=_params(1),
    )(h, target, fg)


def _row_tile(r):
    if r <= 256:
        return r
    for t in range(256, 7, -8):
        if r % t == 0:
            return t
    return r


def _adamw(parts, w, m, v):
    P, R, C = parts.shape
    tr = _row_tile(R)

    def body(p_ref, w_ref, m_ref, v_ref, g_ref, d_ref, mo_ref, vo_ref):
        g = p_ref[0].astype(F32)
        for k in range(1, P):
            g = g + p_ref[k].astype(F32)
        g_ref[...] = g
        m2 = B1 * m_ref[...] + (1.0 - B1) * g
        v2 = B2 * v_ref[...] + (1.0 - B2) * (g * g)
        mo_ref[...] = m2
        vo_ref[...] = v2
        m_hat = m2 / (1.0 - B1 ** STEP)
        v_hat = v2 / (1.0 - B2 ** STEP)
        d_ref[...] = -LR * (m_hat / (jnp.sqrt(v_hat) + EPS_ADAM) + WD * w_ref[...])

    blk = _rows((tr, C), lambda i: (i, 0))
    return pl.pallas_call(
        body, name="adamw", grid=(R // tr,), out_shape=[jax.ShapeDtypeStruct((R, C), F32)] * 4,
        in_specs=[_rows((P, tr, C), lambda i: (0, i, 0)), blk, blk, blk], out_specs=[blk] * 4,
        compiler_params=_params(1),
    )(parts, w, m, v)


_WEIGHTS = ['ada_w', 'ada_b', 'norm_g', 'ffn_w13', 'ffn_w2', 'conv_w_pw1', 'conv_b_pw1', 'conv_w_dw', 'conv_b_dw',
            'conv_ln_g', 'conv_ln_b', 'conv_w_pw2', 'conv_b_pw2', 'kv_ada_w', 'kv_ada_b', 'kv_norm_g', 'w_kv_a',
            'kv_a_norm_g', 'w_kv_b', 'w_q_a', 'q_a_norm_g', 'w_q_b', 'w_o', 'final_norm_g']


def _vec(rows):
    rows = [r.reshape(1, -1).astype(F32) for r in rows]
    return jnp.concatenate(rows + [jnp.zeros((8 - len(rows), rows[0].shape[1]), F32)], axis=0)


def kernel(x, c, positions, ada_w, ada_b, norm_g, ffn_w13, ffn_w2, conv_w_pw1, conv_b_pw1, conv_w_dw, conv_b_dw, conv_ln_g, conv_ln_b, conv_w_pw2, conv_b_pw2, kv_ada_w, kv_ada_b, kv_norm_g, w_kv_a, kv_a_norm_g, w_kv_b, w_q_a, q_a_norm_g, w_q_b, w_o, final_norm_g, loss_target, m_ada_w, m_ada_b, m_norm_g, m_ffn_w13, m_ffn_w2, m_conv_w_pw1, m_conv_b_pw1, m_conv_w_dw, m_conv_b_dw, m_conv_ln_g, m_conv_ln_b, m_conv_w_pw2, m_conv_b_pw2, m_kv_ada_w, m_kv_ada_b, m_kv_norm_g, m_w_kv_a, m_kv_a_norm_g, m_w_kv_b, m_w_q_a, m_q_a_norm_g, m_w_q_b, m_w_o, m_final_norm_g, v_ada_w, v_ada_b, v_norm_g, v_ffn_w13, v_ffn_w2, v_conv_w_pw1, v_conv_b_pw1, v_conv_w_dw, v_conv_b_dw, v_conv_ln_g, v_conv_ln_b, v_conv_w_pw2, v_conv_b_pw2, v_kv_ada_w, v_kv_ada_b, v_kv_norm_g, v_w_kv_a, v_kv_a_norm_g, v_w_kv_b, v_w_q_a, v_q_a_norm_g, v_w_q_b, v_w_o, v_final_norm_g):
    given = dict(locals())
    S = x.shape[1]
    me = 4 * lax.axis_index("x") + 2 * lax.axis_index("y") + lax.axis_index("c")

    small = jnp.concatenate([
        conv_w_dw[0], conv_b_dw, conv_ln_g, conv_ln_b, conv_b_pw2,
        norm_g.reshape(6, 128), conv_b_pw1.reshape(2, 128),
        c.reshape(8, 128), jnp.zeros((5, 128), F32)], axis=0)
    bf = lambda w: w.astype(BF)
    full_w13 = lambda g: jnp.transpose(g.reshape(N_DEV, D, 704), (1, 0, 2)).reshape(D, 2 * DFF)
    full_w2 = lambda g: g.reshape(DFF, D)
    got = _gather_two_level([small, bf(ffn_w13[0, 0]), bf(ffn_w2[0, 0])], "gather_first")
    w13_00, w2_00 = full_w13(got[1]), full_w2(got[2])
    sm = got[0]
    chan = lambda lo, hi: jnp.moveaxis(sm[:, lo:hi, :], 0, 1).reshape(hi - lo, D)
    w_dw_f, b_dw_f, ln_g_f, ln_b_f, b_pw2_f = chan(0, 31), chan(31, 32), chan(32, 33), chan(33, 34), chan(34, 35)
    norm_f = chan(35, 41).reshape(2, 3, D)
    b_pw1_f = sm[:, 41:43, :].reshape(1, 2 * D)
    c_all = sm[:, 43:51, :].reshape(N_DEV, D)

    n_ada = ada_w.shape[2]
    n_kva = kv_ada_w.shape[1]
    modp = _mod_fwd(c_all, ada_w[0], ada_w[1], kv_ada_w)
    (modr,) = _exchange([(modp.reshape(N_DEV, 1, 2 * n_ada + n_kva), "scatter")], "scatter_mod")
    modr = modr[:, 0, :]
    mod = jnp.transpose(modr[:, :2 * n_ada].reshape(N_DEV, 2, n_ada), (1, 0, 2)).reshape(2, 9 * D) + ada_b
    mod = mod.reshape(2, 9, D)
    kvmod = (modr[:, 2 * n_ada:].reshape(2 * D) + kv_ada_b).reshape(2, D)

    def sub_vec(l, idx):
        return _vec([norm_f[l, idx], mod[l, 3 * idx], mod[l, 3 * idx + 1], mod[l, 3 * idx + 2]])

    vec_kv = _vec([kv_norm_g, kvmod[0], kvmod[1]])
    cw = jnp.concatenate([w_dw_f, b_dw_f, ln_g_f, ln_b_f, b_pw2_f, mod[0, 5].reshape(1, D), jnp.zeros((4, D), F32)], axis=0)
    b1v = _vec([b_pw1_f])
    g_kva = _vec([kv_a_norm_g])
    g_qa = _vec([q_a_norm_g[0]])
    fgv = _vec([final_norm_g])

    inv_freq = 10000.0 ** (-jnp.arange(0, ROPE, 2, dtype=F32) / ROPE)
    ang = positions[0].astype(F32)[:, None] * inv_freq
    cs, sn = jnp.cos(ang), jnp.sin(ang)
    z16, z32, z64 = jnp.zeros((S, 16), F32), jnp.zeros((S, 32), F32), jnp.zeros((S, 64), F32)
    rope = jnp.concatenate([jnp.ones((S, 64), F32), cs, cs, z32,
                            z64, z16, sn, z32,
                            z64, -sn, z16, z32], axis=1)

    def merged(blocks, name):
        return [jnp.swapaxes(m, 0, 1).reshape((N_DEV,) + m.shape[2:]) for m in _sibling_merge(blocks, name)]

    h0 = x[0]
    group1 = [bf(conv_w_pw1[0]), bf(conv_w_pw2[0]), bf(ffn_w13[0, 1]), bf(ffn_w2[0, 1])]
    (h1, u00, a00, b00), blocks1 = _ffn_fwd(h0, sub_vec(0, 0), w13_00, w2_00, comm=(group1, False))
    g_pw1, g_pw2, g_w13, g_w2 = merged(blocks1, "merge_group1")
    pw1_f = jnp.transpose(g_pw1, (1, 0, 2)).reshape(D, 2 * D)
    pw2_f = g_pw2.reshape(D, D)
    w13_01, w2_01 = full_w13(g_w13), full_w2(g_w2)
    hn_c, pre = _conv_in_fwd(h1, sub_vec(0, 1), pw1_f, b1v)
    group2 = [bf(w_kv_a), bf(w_kv_b), bf(ffn_w13[1, 0]), bf(ffn_w2[1, 0])]
    (u2, z_c, h2), blocks2 = _conv_out_fwd(pre, h1, cw, pw2_f, comm=(group2, False))
    g_kva_w, g_kvb_w, g_w13, g_w2 = merged(blocks2, "merge_group2")
    wkva = g_kva_w.reshape(D, KV_LORA + ROPE)
    wkva_f = jnp.concatenate([wkva[:, :KV_LORA], jnp.zeros((D, 64), BF), wkva[:, KV_LORA:], jnp.zeros((D, 32), BF)], axis=1)
    wkvb_f = jnp.transpose(g_kvb_w, (1, 0, 2)).reshape(KV_LORA, NH * HP)
    w13_10, w2_10 = full_w13(g_w13), full_w2(g_w2)
    group3 = [bf(w_q_a[0]), bf(w_q_b[0]), bf(w_o[0]), bf(ffn_w13[1, 1]), bf(ffn_w2[1, 1])]
    (h3, u01, a01, b01), blocks3 = _ffn_fwd(h2, sub_vec(0, 2), w13_01, w2_01, comm=(group3, False))
    g_qa_w, g_qb_w, g_wo, g_w13, g_w2 = merged(blocks3, "merge_group3")
    wqa_f = g_qa_w.reshape(D, Q_LORA)
    wqb = jnp.transpose(g_qb_w, (1, 0, 2)).reshape(Q_LORA, NH, 96)
    wqb_f = jnp.pad(wqb, ((0, 0), (0, 0), (0, HP - 96))).reshape(Q_LORA, NH * HP)
    wo_f = jnp.pad(g_wo.reshape(NH, 64, D), ((0, 0), (64, 0), (0, 0))).reshape(NH * HP, D)
    w13_11, w2_11 = full_w13(g_w13), full_w2(g_w2)
    hn_kv, ckv, ckn, k_all, v_all = _kv_fwd(h3, vec_kv, wkva_f, g_kva, wkvb_f, rope)
    (h4, u10, a10, b10), _ = _ffn_fwd(h3, sub_vec(1, 0), w13_10, w2_10)
    hn_q, qa, qan, q_all = _q_fwd(h4, sub_vec(1, 1), wqa_f, g_qa, wqb_f, rope)
    o_all, lse = _attn_fwd(q_all, k_all, v_all)
    h5 = _attn_out_fwd(o_all, h4, wo_f, sub_vec(1, 1))
    (h6, u11, a11, b11), _ = _ffn_fwd(h5, sub_vec(1, 2), w13_11, w2_11)

    dh6, ps_fin = _final(h6, loss_target[0], fgv)
    loss = lax.psum(ps_fin[1, 0], ("x", "y", "c"))

    core = lax.axis_index("c").reshape(1).astype(jnp.int32)

    def pair_sums(sends, name):
        by_core = [s.reshape((4, 2) + s.shape[1:]).swapaxes(0, 1) for s in sends]
        from_sibling = _pair_swap(by_core, name)
        return [_pair_add(core, a.reshape(2, -1, a.shape[-1]), b.reshape(-1, b.shape[-1])).reshape(b.shape)
                for a, b in zip(by_core, from_sibling)]

    def ffn_back(dho, h_in, u, a, b, l, i, w13, w2, comm=None):
        vec = sub_vec(l, 2 * i)
        (dh, da, db, t, dhb, ps), reduced = _ffn_bwd(dho, h_in, a, b, vec, w13, w2, comm=comm)
        dwa = _wgrad(u, da, 512, DFF, "wgrad_w13")
        dwb = _wgrad(u, db, 512, DFF, "wgrad_w13")
        dw2, gs = _wgrad(t, dhb, CH, D, "wgrad_w2", gate=(w2, _vec([0.5 * vec[3]])))
        dgate = 0.5 * jnp.sum(gs[:, 0, :], axis=0)
        send13 = jnp.transpose(jnp.concatenate([dwa, dwb], axis=1).reshape(D, N_DEV, 704), (1, 0, 2))
        return dh, send13, dw2.reshape(N_DEV, 352, D), ps, dgate, reduced

    dh5, s13_11, s2_11, ps11, dg11, _ = ffn_back(dh6, h5, u11, a11, b11, 1, 1, w13_11, w2_11)
    vec_m1 = sub_vec(1, 1)
    do_all, delta, dhb5 = _attn_out_bwd(dh5, o_all, wo_f, vec_m1)
    dwo_p, gs_o = _wgrad(o_all, dhb5, D, D, "wgrad_wo", gate=(wo_f, _vec([vec_m1[3]])))
    dgm1 = jnp.sum(gs_o[:, 0, :], axis=0)
    dq_all, dk_all, dv_all = _attn_bwd(q_all, k_all, v_all, do_all, lse, delta)
    dh4, dqb, dqab, ps_q, ps_q2 = _q_bwd(dq_all, h4, qa, dh5, vec_m1, wqa_f, g_qa, wqb_f, rope)
    dwqb_p = _wgrad(qan, dqb, Q_LORA, D, "wgrad_wqb")
    dwqa = _wgrad(hn_q, dqab, D, Q_LORA, "wgrad_wqa")
    dh3a, s13_10, s2_10, ps10, dg10, _ = ffn_back(dh4, h3, u10, a10, b10, 1, 0, w13_10, w2_10)
    sums_a = pair_sums([
        s13_11, s2_11, s13_10, s2_10, dwqa.reshape(N_DEV, 128, Q_LORA),
        jnp.transpose(dwqb_p.reshape(Q_LORA, NH, HP)[:, :, :96].reshape(Q_LORA, N_DEV, 192), (1, 0, 2)),
        dwo_p.reshape(NH, HP, D)[:, 64:, :].reshape(N_DEV, 128, D)], "pair_swap_a")
    dh3, dkva, dkvb, ps_kv, ps_kv2 = _kv_bwd(dk_all, dv_all, h3, ckv, dh3a, vec_kv, wkva_f, g_kva, wkvb_f, rope)
    dwkva_p = _wgrad(hn_kv, dkva, D, KVA_P, "wgrad_wkva")
    dwkvb = _wgrad(ckn, dkvb, KV_LORA, D, "wgrad_wkvb")
    dh2, s13_01, s2_01, ps01, dg01, red_a = ffn_back(dh3, h2, u01, a01, b01, 0, 1, w13_01, w2_01, comm=(sums_a, True))
    vec_m0 = sub_vec(0, 1)
    du2, dhb2, ps_co = _conv_out_bwd(dh2, u2, cw, pw2_f)
    dpw2, gs_c = _wgrad(z_c, dhb2, D, D, "wgrad_pw2", gate=(pw2_f, _vec([vec_m0[3]])))
    dgm0 = jnp.sum(gs_c[:, 0, :], axis=0) + b_pw2_f[0] * ps_co[2]
    dh1, dpre, ps_ci, ps_dw, ps_b1 = _conv_in_bwd(du2, pre, h1, dh2, vec_m0, cw, pw1_f)
    dpw1 = _wgrad(hn_c, dpre, D, D, "wgrad_pw1")
    sums_b = pair_sums([
        s13_01, s2_01, jnp.transpose(dpw1.reshape(D, N_DEV, 256), (1, 0, 2)), dpw2.reshape(N_DEV, 128, D),
        jnp.concatenate([dwkva_p[:, :KV_LORA], dwkva_p[:, KV_LORA + 64:KV_LORA + 96]], axis=1).reshape(N_DEV, 128, KV_LORA + ROPE),
        jnp.transpose(dwkvb.reshape(KV_LORA, N_DEV, 256), (1, 0, 2))], "pair_swap_b")
    dh0, s13_00, s2_00, ps00, dg00, red_b = ffn_back(dh1, h0, u00, a00, b00, 0, 0, w13_00, w2_00, comm=(sums_b, True))
    red_c = _chip_scatter(pair_sums([s13_00, s2_00], "pair_swap_c"), "chip_scatter_last")
    r13_11, r2_11, r13_10, r2_10, r_wqa, r_wqb, r_wo = red_a
    r13_01, r2_01, r_pw1, r_pw2, r_wkva, r_wkvb = red_b
    r13_00, r2_00 = red_c

    dmod = jnp.stack([
        jnp.stack([ps00[1], ps00[2], dg00, ps_ci[1], ps_ci[2], dgm0, ps01[1], ps01[2], dg01]),
        jnp.stack([ps10[1], ps10[2], dg10, ps_q[1], ps_q[2], dgm1, ps11[1], ps11[2], dg11])])
    dnorm = jnp.stack([ps00[0], ps_ci[0], ps01[0], ps10[0], ps_q[0], ps11[0]])
    pieces = [dnorm, ps_b1[0], ps_dw[0:31], ps_dw[31], ps_co[0], ps_co[1], vec_m0[3] * ps_co[2],
              ps_kv[0], ps_kv2[0], ps_q2[0], ps_fin[0], dmod, ps_kv[1], ps_kv[2]]
    sizes = [int(np.prod(p.shape)) for p in pieces]
    offs = np.concatenate([[0], np.cumsum(sizes)]).astype(int)
    flat = jnp.concatenate([p.reshape(-1) for p in pieces]).reshape(-1, 128)
    (part,) = _exchange([(flat, "gather")], "gather_partials")
    part = part.reshape(N_DEV, -1)

    def piece(i, rows, cols):
        return part[:, offs[i]:offs[i + 1]].reshape(N_DEV, rows, cols)

    def mine(i, rows):
        return lax.dynamic_slice_in_dim(piece(i, rows, D), me * 128, 128, axis=2)

    dmod_all = piece(11, 2, 9 * D)
    c_t = jnp.transpose(c_all)
    g_ada = jnp.stack([_mod_wgrad(c_t, lax.dynamic_slice_in_dim(dmod_all[:, l], me * n_ada, n_ada, axis=1))
                       for l in range(2)])
    dkvmod_all = jnp.concatenate([piece(12, 1, D)[:, 0], piece(13, 1, D)[:, 0]], axis=1)
    g_kvada = _mod_wgrad(c_t, lax.dynamic_slice_in_dim(dkvmod_all, me * n_kva, n_kva, axis=1))

    parts = {
        'ada_w': g_ada.reshape(1, 2 * D, n_ada),
        'ada_b': dmod_all,
        'norm_g': mine(0, 6),
        'ffn_w13': jnp.stack([r13_00, r13_01, r13_10, r13_11], axis=1).reshape(4, 4 * D, 704),
        'ffn_w2': jnp.stack([r2_00, r2_01, r2_10, r2_11], axis=1).reshape(4, 4 * 352, D),
        'conv_w_pw1': r_pw1,
        'conv_b_pw1': lax.dynamic_slice_in_dim(piece(1, 1, 2 * D), me * 256, 256, axis=2),
        'conv_w_dw': mine(2, 31),
        'conv_b_dw': mine(3, 1),
        'conv_ln_g': mine(4, 1),
        'conv_ln_b': mine(5, 1),
        'conv_w_pw2': r_pw2,
        'conv_b_pw2': mine(6, 1),
        'kv_ada_w': g_kvada.reshape(1, D, n_kva),
        'kv_ada_b': dkvmod_all.reshape(N_DEV, 1, 2 * D),
        'kv_norm_g': piece(7, 1, D),
        'w_kv_a': r_wkva,
        'kv_a_norm_g': piece(8, 1, KV_LORA),
        'w_kv_b': r_wkvb,
        'w_q_a': r_wqa,
        'q_a_norm_g': piece(9, 1, Q_LORA),
        'w_q_b': r_wqb,
        'w_o': r_wo,
        'final_norm_g': piece(10, 1, D),
    }
    grads, deltas, new_m, new_v = [], [], [], []
    for name in _WEIGHTS:
        w = given[name]
        p = parts[name]
        shape2 = p.shape[1:]
        g, dlt, m2, v2 = _adamw(p, w.reshape(shape2), given['m_' + name].reshape(shape2), given['v_' + name].reshape(shape2))
        grads.append(g.reshape(w.shape))
        deltas.append(dlt.reshape(w.shape))
        new_m.append(m2.reshape(w.shape))
        new_v.append(v2.reshape(w.shape))
    return (loss, dh0.reshape(1, S, D), *grads, *deltas, *new_m, *new_v)
```

```python
import functools

import numpy as np
import jax
import jax.numpy as jnp
from jax import lax
from jax.experimental import pallas as pl
from jax.experimental.pallas import tpu as pltpu

F32, BF = jnp.float32, jnp.bfloat16

D = 1024
DFF = 2816
CH = 1408
FFN_CHUNKS = ((0, 1536), (1536, 1280))
NH = 16
HP = 128
KV_LORA, Q_LORA, ROPE = 256, 512, 32
KVA_P = 384
CONV_W = 31
HALO = 32
EPS = 1e-6
SM_SCALE = float((64 + 32) ** -0.5)
LOG2E = 1.4426950408889634
LN2 = 0.6931471805599453
EXP2_SCALE = SM_SCALE * LOG2E
NEG = -1e30
N_DEV = 8
MESH = pl.DeviceIdType.MESH

TS = 512
TS_FFN_FWD = 512
TS_FFN_BWD = 256
TA = 1024
TA_FWD = 2048
ATT_ROWS = 256
ATT_ROWS_BWD = 512
TW = 2048
VMEM_LIMIT = 56 * 1024 * 1024

LR, B1, B2, EPS_ADAM, WD, STEP = 0.001, 0.9, 0.999, 1e-08, 0.01, 10

VMEM_FULL = pl.BlockSpec(memory_space=pltpu.VMEM)
HBM_FULL = pl.BlockSpec(memory_space=pltpu.HBM)


def _params(n_grid):
    return pltpu.CompilerParams(dimension_semantics=("arbitrary",) * n_grid, vmem_limit_bytes=VMEM_LIMIT)


def _dot(a, b):
    return jnp.dot(a, b, preferred_element_type=F32)


def _dot_nt(a, b):
    return lax.dot_general(a, b, (((1,), (1,)), ((), ())), preferred_element_type=F32)


def _dot_tn(a, b):
    return lax.dot_general(a, b, (((0,), (0,)), ((), ())), preferred_element_type=F32)


def _sum0(x):
    return jnp.sum(x, axis=0, keepdims=True)


def _mean1(x):
    return jnp.mean(x, axis=-1, keepdims=True)


def _sigmoid(x):
    return jax.nn.sigmoid(x)


def _rows(shape, imap):
    return pl.BlockSpec(shape, imap)


def _norm_mod(h, vec):
    r = lax.rsqrt(_mean1(h * h) + EPS)
    xhat = h * r
    u = (xhat * vec[0:1]) * (1.0 + vec[2:3]) + vec[1:2]
    return u, xhat, r


def _norm_mod_bwd(du, xhat, r, vec):
    g = vec[0:1]
    dxn = du * (1.0 + vec[2:3])
    dsh = _sum0(du)
    dsc = _sum0(du * (xhat * g))
    dg = _sum0(dxn * xhat)
    dxhat = dxn * g
    dh = r * (dxhat - xhat * _mean1(dxhat * xhat))
    return dh, dg, dsh, dsc


def _rms_bwd(dy, x, r, g):
    xhat = x * r
    dg = _sum0(dy * xhat)
    dxhat = dy * g
    return r * (dxhat - xhat * _mean1(dxhat * xhat)), dg


def _accumulate(ref, first, rows):
    @pl.when(first)
    def _():
        ref[...] = jnp.zeros(ref.shape, ref.dtype)

    for i, row in enumerate(rows):
        ref[i:i + 1, :] += row


def _rope(x, tab):
    return x * tab[:, 0:HP] + pltpu.roll(x, 16, 1) * tab[:, HP:2 * HP] + pltpu.roll(x, HP - 16, 1) * tab[:, 2 * HP:3 * HP]


def _rope_t(dy, tab):
    return (dy * tab[:, 0:HP] + pltpu.roll(dy * tab[:, HP:2 * HP], HP - 16, 1)
            + pltpu.roll(dy * tab[:, 2 * HP:3 * HP], 16, 1))


def _exchange(items, name):
    n = len(items)

    def body(*refs):
        ins, outs = refs[:n], refs[n:2 * n]
        send_sems, recv_sems, local_sems = refs[2 * n:]
        x, y, c = lax.axis_index("x"), lax.axis_index("y"), lax.axis_index("c")
        me = 4 * x + 2 * y + c

        def source(j, dev):
            return ins[j] if items[j][1] == "gather" else ins[j].at[dev]

        own = [pltpu.make_async_copy(source(j, me), outs[j].at[me], local_sems.at[j]) for j in range(n)]
        for cp in own:
            cp.start()
        remote = []
        for d in range(1, N_DEV):
            px = 1 - x if d & 4 else x
            py = 1 - y if d & 2 else y
            pc = 1 - c if d & 1 else c
            peer = 4 * px + 2 * py + pc
            for j in range(n):
                pltpu.make_async_remote_copy(
                    src_ref=source(j, peer), dst_ref=outs[j].at[me], send_sem=send_sems.at[j, d - 1],
                    recv_sem=recv_sems.at[j, d - 1], device_id=(px, py, pc), device_id_type=MESH).start()
                remote.append(pltpu.make_async_remote_copy(
                    src_ref=source(j, peer), dst_ref=outs[j].at[peer], send_sem=send_sems.at[j, d - 1],
                    recv_sem=recv_sems.at[j, d - 1], device_id=(px, py, pc), device_id_type=MESH))
        for cp in remote:
            cp.wait_send()
            cp.wait_recv()
        for cp in own:
            cp.wait()

    out_shape = []
    for arr, mode in items:
        shp = (N_DEV,) + tuple(arr.shape) if mode == "gather" else tuple(arr.shape)
        out_shape.append(jax.ShapeDtypeStruct(shp, arr.dtype))
    return pl.pallas_call(
        body, name=name, out_shape=out_shape,
        in_specs=[HBM_FULL] * n, out_specs=[HBM_FULL] * n,
        scratch_shapes=[pltpu.SemaphoreType.DMA((n, N_DEV - 1)), pltpu.SemaphoreType.DMA((n, N_DEV - 1)),
                        pltpu.SemaphoreType.DMA((n,))],
        compiler_params=pltpu.CompilerParams(has_side_effects=True),
    )(*[a for a, _ in items])


def _gather_two_level(arrs, name):
    n = len(arrs)

    def body(*refs):
        ins, outs = refs[:n], refs[n:2 * n]
        send_sems, recv_sems, local_sems = refs[2 * n:]
        x, y, c = lax.axis_index("x"), lax.axis_index("y"), lax.axis_index("c")
        sibling = (x, y, 1 - c)
        chips = [(1 - x, y), (x, 1 - y), (1 - x, 1 - y)]

        def slot(j, px, py, pc):
            return outs[j].at[4 * px + 2 * py + pc]

        def copy(j, k, block, to, src=None):
            return pltpu.make_async_remote_copy(
                src_ref=slot(j, *block) if src is None else src, dst_ref=slot(j, *block),
                send_sem=send_sems.at[j, k], recv_sem=recv_sems.at[j, k], device_id=to, device_id_type=MESH)

        own = [pltpu.make_async_copy(ins[j], slot(j, x, y, c), local_sems.at[j]) for j in range(n)]
        for cp in own:
            cp.start()
        sent = []
        for j in range(n):
            sent.append(copy(j, 0, (x, y, c), sibling, src=ins[j]))
            sent += [copy(j, 1 + i, (x, y, c), (*chip, c), src=ins[j]) for i, chip in enumerate(chips)]
        for cp in sent:
            cp.start()
        for i, chip in enumerate(chips):
            for j in range(n):
                copy(j, 1 + i, (*chip, c), (x, y, c)).wait_recv()
                passed = copy(j, 4 + i, (*chip, c), sibling)
                passed.start()
                sent.append(passed)
        for j in range(n):
            copy(j, 0, (x, y, 1 - c), (x, y, c)).wait_recv()
            for i, chip in enumerate(chips):
                copy(j, 4 + i, (*chip, 1 - c), (x, y, c)).wait_recv()
        for cp in sent:
            cp.wait_send()
        for cp in own:
            cp.wait()

    return pl.pallas_call(
        body, name=name, out_shape=[jax.ShapeDtypeStruct((N_DEV,) + tuple(a.shape), a.dtype) for a in arrs],
        in_specs=[HBM_FULL] * n, out_specs=[HBM_FULL] * n,
        scratch_shapes=[pltpu.SemaphoreType.DMA((n, N_DEV - 1)), pltpu.SemaphoreType.DMA((n, N_DEV - 1)),
                        pltpu.SemaphoreType.DMA((n,))],
        compiler_params=pltpu.CompilerParams(has_side_effects=True),
    )(*arrs)


def _pair_swap(arrs, name):
    n = len(arrs)

    def body(*refs):
        ins, outs = refs[:n], refs[n:2 * n]
        send_sems, recv_sems = refs[2 * n:]
        x, y, c = lax.axis_index("x"), lax.axis_index("y"), lax.axis_index("c")
        copies = [pltpu.make_async_remote_copy(
            src_ref=ins[j].at[1 - c], dst_ref=outs[j], send_sem=send_sems.at[j], recv_sem=recv_sems.at[j],
            device_id=(x, y, 1 - c), device_id_type=MESH) for j in range(n)]
        for cp in copies:
            cp.start()
        for cp in copies:
            cp.wait_send()
            cp.wait_recv()

    return pl.pallas_call(
        body, name=name, out_shape=[jax.ShapeDtypeStruct(tuple(a.shape[1:]), a.dtype) for a in arrs],
        in_specs=[HBM_FULL] * n, out_specs=[HBM_FULL] * n,
        scratch_shapes=[pltpu.SemaphoreType.DMA((n,)), pltpu.SemaphoreType.DMA((n,))],
        compiler_params=pltpu.CompilerParams(has_side_effects=True),
    )(*arrs)


def _chip_comm(arrs, scatter):
    n = len(arrs)
    out_shapes = [jax.ShapeDtypeStruct(tuple(a.shape) if scatter else (2, 4) + tuple(a.shape), a.dtype) for a in arrs]
    scratch = [pltpu.SemaphoreType.DMA((n, 3)), pltpu.SemaphoreType.DMA((n, 3)), pltpu.SemaphoreType.DMA((n,))]

    def copies(ins, outs, send_sems, recv_sems, local_sems, arriving):
        x, y, c = lax.axis_index("x"), lax.axis_index("y"), lax.axis_index("c")
        here = 2 * x + y

        def source(j, chip):
            return ins[j].at[chip] if scatter else ins[j]

        def slot(j, chip):
            return outs[j].at[chip] if scatter else outs[j].at[c].at[chip]

        own = [pltpu.make_async_copy(source(j, here), slot(j, here), local_sems.at[j]) for j in range(n)]
        remote = []
        for i, (px, py) in enumerate([(1 - x, y), (x, 1 - y), (1 - x, 1 - y)]):
            there = 2 * px + py
            for j in range(n):
                remote.append(pltpu.make_async_remote_copy(
                    src_ref=source(j, there), dst_ref=slot(j, there if arriving else here),
                    send_sem=send_sems.at[j, i], recv_sem=recv_sems.at[j, i], device_id=(px, py, c),
                    device_id_type=MESH))
        return own, remote

    def start(*refs):
        own, remote = copies(*refs, arriving=False)
        for cp in own + remote:
            cp.start()

    def finish(*refs):
        own, remote = copies(*refs, arriving=True)
        for cp in remote:
            cp.wait_send()
            cp.wait_recv()
        for cp in own:
            cp.wait()

    return out_shapes, scratch, start, finish


def _chip_scatter(arrs, name):
    n = len(arrs)
    out_shapes, scratch, start, finish = _chip_comm(arrs, True)

    def body(*refs):
        start(refs[:n], refs[n:2 * n], *refs[2 * n:])
        finish(refs[:n], refs[n:2 * n], *refs[2 * n:])

    return pl.pallas_call(
        body, name=name, out_shape=out_shapes, in_specs=[HBM_FULL] * n, out_specs=[HBM_FULL] * n,
        scratch_shapes=scratch, compiler_params=pltpu.CompilerParams(has_side_effects=True),
    )(*arrs)


def _hosted_call(body, args, comm, *, name, n_steps, in_specs, out_specs, out_shape, scratch_shapes=()):
    if comm is None:
        res = pl.pallas_call(
            body, name=name, grid=(n_steps,), out_shape=list(out_shape), in_specs=list(in_specs),
            out_specs=list(out_specs), scratch_shapes=list(scratch_shapes), compiler_params=_params(1))(*args)
        return res, []
    arrs, scatter = comm
    c_shapes, c_scratch, start, finish = _chip_comm(arrs, scatter)
    n_in, n_out, n_sc, k = len(in_specs), len(out_specs), len(scratch_shapes), len(arrs)

    def hosting(*refs):
        ins, cin = refs[:n_in], refs[n_in:n_in + k]
        outs, cout = refs[n_in + k:n_in + k + n_out], refs[n_in + k + n_out:n_in + 2 * k + n_out]
        scratch = refs[n_in + 2 * k + n_out:n_in + 2 * k + n_out + n_sc]
        sems = refs[n_in + 2 * k + n_out + n_sc:]
        step = pl.program_id(0)

        @pl.when(step == 0)
        def _():
            start(cin, cout, *sems)

        body(*ins, *outs, *scratch)

        @pl.when(step == n_steps - 1)
        def _():
            finish(cin, cout, *sems)

    res = pl.pallas_call(
        hosting, name=name + "_hosting", grid=(n_steps,), out_shape=list(out_shape) + c_shapes,
        in_specs=list(in_specs) + [HBM_FULL] * k, out_specs=list(out_specs) + [HBM_FULL] * k,
        scratch_shapes=list(scratch_shapes) + c_scratch,
        compiler_params=pltpu.CompilerParams(dimension_semantics=("arbitrary",), vmem_limit_bytes=VMEM_LIMIT,
                                             has_side_effects=True))(*args, *arrs)
    return res[:n_out], res[n_out:]


def _sibling_merge(arrs, name):
    n = len(arrs)

    def body(*refs):
        bufs = refs[n:2 * n]
        send_sems, recv_sems = refs[2 * n:]
        x, y, c = lax.axis_index("x"), lax.axis_index("y"), lax.axis_index("c")
        waits = []
        for j in range(n):
            pltpu.make_async_remote_copy(
                src_ref=bufs[j].at[c], dst_ref=bufs[j].at[c], send_sem=send_sems.at[j], recv_sem=recv_sems.at[j],
                device_id=(x, y, 1 - c), device_id_type=MESH).start()
            waits.append(pltpu.make_async_remote_copy(
                src_ref=bufs[j].at[c], dst_ref=bufs[j].at[1 - c], send_sem=send_sems.at[j],
                recv_sem=recv_sems.at[j], device_id=(x, y, 1 - c), device_id_type=MESH))
        for cp in waits:
            cp.wait_send()
            cp.wait_recv()

    return pl.pallas_call(
        body, name=name, out_shape=[jax.ShapeDtypeStruct(tuple(a.shape), a.dtype) for a in arrs],
        in_specs=[HBM_FULL] * n, out_specs=[HBM_FULL] * n, input_output_aliases={j: j for j in range(n)},
        scratch_shapes=[pltpu.SemaphoreType.DMA((n,)), pltpu.SemaphoreType.DMA((n,))],
        compiler_params=pltpu.CompilerParams(has_side_effects=True),
    )(*arrs)


def _pair_add(core, a, b):
    _, R, C = a.shape
    tr = _row_tile(R)

    def body(core_ref, a_ref, b_ref, o_ref):
        o_ref[...] = (a_ref[...].astype(F32) + b_ref[...].astype(F32)).astype(BF)

    blk = _rows((tr, C), lambda i, core: (i, 0))
    grid_spec = pltpu.PrefetchScalarGridSpec(
        num_scalar_prefetch=1, grid=(R // tr,),
        in_specs=[_rows((None, tr, C), lambda i, core: (core[0], i, 0)), blk], out_specs=blk)
    return pl.pallas_call(
        body, name="pair_add", grid_spec=grid_spec, out_shape=jax.ShapeDtypeStruct((R, C), BF),
        compiler_params=_params(1),
    )(core, a, b)


def _mod_fwd(c_all, w0, w1, wkv):
    n0, n1, n2 = w0.shape[1], w1.shape[1], wkv.shape[1]

    def body(c_ref, w0_ref, w1_ref, w2_ref, o_ref):
        cc = c_ref[...]
        s = cc * _sigmoid(cc)
        o_ref[:, 0:n0] = _dot(s, w0_ref[...])
        o_ref[:, n0:n0 + n1] = _dot(s, w1_ref[...])
        o_ref[:, n0 + n1:n0 + n1 + n2] = _dot(s, w2_ref[...])

    return pl.pallas_call(
        body, name="mod_fwd", out_shape=jax.ShapeDtypeStruct((N_DEV, n0 + n1 + n2), F32),
        in_specs=[VMEM_FULL] * 4, out_specs=VMEM_FULL,
        compiler_params=pltpu.CompilerParams(vmem_limit_bytes=VMEM_LIMIT),
    )(c_all, w0, w1, wkv)


def _mod_wgrad(c_t, dm):
    C = dm.shape[1]
    tr = 256

    def body(ct_ref, dm_ref, o_ref):
        ct = ct_ref[...]
        s = ct * _sigmoid(ct)
        dmv = dm_ref[...]
        lane = lax.broadcasted_iota(jnp.int32, (tr, N_DEV), 1)
        acc = jnp.zeros((tr, C), F32)
        for r in range(N_DEV):
            col = jnp.sum(jnp.where(lane == r, s, 0.0), axis=1, keepdims=True)
            acc = acc + col * dmv[r:r + 1, :]
        o_ref[...] = acc

    return pl.pallas_call(
        body, name="mod_wgrad", grid=(D // tr,), out_shape=jax.ShapeDtypeStruct((D, C), F32),
        in_specs=[_rows((tr, N_DEV), lambda i: (i, 0)), _rows((N_DEV, C), lambda i: (0, 0))],
        out_specs=_rows((tr, C), lambda i: (i, 0)), compiler_params=_params(1),
    )(c_t, dm)


def _ffn_fwd(h, vec, w13, w2, comm=None):
    S = h.shape[0]
    ts = min(S, TS_FFN_FWD)

    def body(h_ref, vec_ref, w13_ref, w2_ref, ho_ref, u_ref, a_ref, b_ref):
        hv, vec = h_ref[...], vec_ref[...]
        u, _, _ = _norm_mod(hv, vec)
        ub = u.astype(BF)
        u_ref[...] = ub
        y = jnp.zeros((ts, D), F32)
        for c0, cw in FFN_CHUNKS:
            a = _dot(ub, w13_ref[:, c0:c0 + cw])
            b = _dot(ub, w13_ref[:, DFF + c0:DFF + c0 + cw])
            a_ref[:, c0:c0 + cw] = a.astype(BF)
            b_ref[:, c0:c0 + cw] = b.astype(BF)
            t = (a * _sigmoid(a)) * b
            y = y + _dot(t.astype(BF), w2_ref[c0:c0 + cw, :])
        ho_ref[...] = hv + (0.5 * vec[3:4]) * y

    tok = lambda i: (i, 0)
    return _hosted_call(
        body, (h, vec, w13, w2), comm, name="ffn_fwd", n_steps=S // ts,
        out_shape=[jax.ShapeDtypeStruct((S, D), F32), jax.ShapeDtypeStruct((S, D), BF),
                   jax.ShapeDtypeStruct((S, DFF), BF), jax.ShapeDtypeStruct((S, DFF), BF)],
        in_specs=[_rows((ts, D), tok), _rows((8, D), lambda i: (0, 0)), VMEM_FULL, VMEM_FULL],
        out_specs=[_rows((ts, D), tok), _rows((ts, D), tok), _rows((ts, DFF), tok), _rows((ts, DFF), tok)])


def _ffn_bwd(dho, h, a, b, vec, w13, w2, comm=None):
    S = h.shape[0]
    ts = min(S, TS_FFN_BWD)

    def body(dho_ref, h_ref, a_ref, b_ref, vec_ref, w13_ref, w2_ref,
             dh_ref, da_ref, db_ref, t_ref, dhb_ref, ps_ref):
        dho_v, vec = dho_ref[...], vec_ref[...]
        dhb_ref[...] = dho_v.astype(BF)
        dyb = ((0.5 * vec[3:4]) * dho_v).astype(BF)
        du = jnp.zeros((ts, D), F32)
        for c0, cw in FFN_CHUNKS:
            av = a_ref[:, c0:c0 + cw].astype(F32)
            bv = b_ref[:, c0:c0 + cw].astype(F32)
            dt = _dot_nt(dyb, w2_ref[c0:c0 + cw, :])
            sig = _sigmoid(av)
            sl = av * sig
            t_ref[:, c0:c0 + cw] = (sl * bv).astype(BF)
            dab = (dt * bv * (sig * (1.0 + av * (1.0 - sig)))).astype(BF)
            dbb = (dt * sl).astype(BF)
            da_ref[:, c0:c0 + cw] = dab
            db_ref[:, c0:c0 + cw] = dbb
            du = du + _dot_nt(dab, w13_ref[:, c0:c0 + cw]) + _dot_nt(dbb, w13_ref[:, DFF + c0:DFF + c0 + cw])
        _, xhat, r = _norm_mod(h_ref[...], vec)
        dhn, dg, dsh, dsc = _norm_mod_bwd(du, xhat, r, vec)
        dh_ref[...] = dho_v + dhn
        _accumulate(ps_ref, pl.program_id(0) == 0, [dg, dsh, dsc])

    tok = lambda i: (i, 0)
    return _hosted_call(
        body, (dho, h, a, b, vec, w13, w2), comm, name="ffn_bwd", n_steps=S // ts,
        out_shape=[jax.ShapeDtypeStruct((S, D), F32), jax.ShapeDtypeStruct((S, DFF), BF),
                   jax.ShapeDtypeStruct((S, DFF), BF), jax.ShapeDtypeStruct((S, DFF), BF),
                   jax.ShapeDtypeStruct((S, D), BF), jax.ShapeDtypeStruct((8, D), F32)],
        in_specs=[_rows((ts, D), tok), _rows((ts, D), tok), _rows((ts, DFF), tok), _rows((ts, DFF), tok),
                  _rows((8, D), lambda i: (0, 0)), VMEM_FULL, VMEM_FULL],
        out_specs=[_rows((ts, D), tok), _rows((ts, DFF), tok), _rows((ts, DFF), tok), _rows((ts, DFF), tok),
                   _rows((ts, D), tok), _rows((8, D), lambda i: (0, 0))])


def _wgrad(a, b, tm, tn, name, gate=None):
    S, M = a.shape
    N = b.shape[1]
    n_s = S // TW

    def body(*refs):
        if gate is None:
            a_ref, b_ref, o_ref, acc_ref = refs
        else:
            a_ref, b_ref, w_ref, sc_ref, o_ref, gs_ref, acc_ref = refs
        s = pl.program_id(2)

        @pl.when(s == 0)
        def _():
            acc_ref[...] = jnp.zeros((tm, tn), F32)

        acc_ref[...] += _dot_tn(a_ref[...], b_ref[...])

        @pl.when(s == n_s - 1)
        def _():
            acc = acc_ref[...]
            if gate is None:
                o_ref[...] = acc.astype(BF)
            else:
                o_ref[...] = (acc * sc_ref[0:1, :]).astype(BF)
                gs_ref[...] = jnp.broadcast_to(_sum0(acc * w_ref[...].astype(F32)), (8, tn))

    in_specs = [_rows((TW, tm), lambda m, n, s: (s, m)), _rows((TW, tn), lambda m, n, s: (s, n))]
    out_shape = [jax.ShapeDtypeStruct((M, N), BF)]
    out_specs = [_rows((tm, tn), lambda m, n, s: (m, n))]
    args = [a, b]
    if gate is not None:
        in_specs += [_rows((tm, tn), lambda m, n, s: (m, n)), _rows((8, tn), lambda m, n, s: (0, n))]
        out_shape.append(jax.ShapeDtypeStruct((M // tm, 8, N), F32))
        out_specs.append(_rows((None, 8, tn), lambda m, n, s: (m, 0, n)))
        args += list(gate)
    res = pl.pallas_call(
        body, name=name, grid=(M // tm, N // tn, n_s), out_shape=out_shape, in_specs=in_specs,
        out_specs=out_specs, scratch_shapes=[pltpu.VMEM((tm, tn), F32)], compiler_params=_params(3),
    )(*args)
    return res[0] if gate is None else (res[0], res[1])


def _conv_in_fwd(h, vec, w1, b1):
    S = h.shape[0]

    def body(h_ref, vec_ref, w_ref, b1_ref, hn_ref, pre_ref):
        u, _, _ = _norm_mod(h_ref[...], vec_ref[...])
        ub = u.astype(BF)
        hn_ref[...] = ub
        pre_ref[...] = _dot(ub, w_ref[...]) + b1_ref[0:1, :]

    tok = lambda i: (i, 0)
    return pl.pallas_call(
        body, name="conv_in_fwd", grid=(S // TS,),
        out_shape=[jax.ShapeDtypeStruct((S, D), BF), jax.ShapeDtypeStruct((S, 2 * D), F32)],
        in_specs=[_rows((TS, D), tok), _rows((8, D), lambda i: (0, 0)), VMEM_FULL, _rows((8, 2 * D), lambda i: (0, 0))],
        out_specs=[_rows((TS, D), tok), _rows((TS, 2 * D), tok)], compiler_params=_params(1),
    )(h, vec, w1, b1)


def _glu(pre):
    return pre[:, :D] * _sigmoid(pre[:, D:])


def _tap_groups(offset):
    groups = {}
    for j in range(CONV_W):
        off = offset(j)
        groups.setdefault(off % 8, []).append((off - off % 8, j))
    return [(phase, sorted(taps)) for phase, taps in sorted(groups.items())]


def _shift_window(dst_ref, win_ref, phase, rows):
    dst_ref[0:rows, :] = win_ref[phase:phase + rows, :]


def _layernorm(u2):
    mu = _mean1(u2)
    xc = u2 - mu
    rstd = lax.rsqrt(_mean1(xc * xc) + EPS)
    return xc * rstd, rstd


def _conv_out_fwd(pre, h, cw, w2, comm=None):
    S = h.shape[0]
    hb = TS // HALO

    def body(pre_ref, ph_ref, h_ref, cw_ref, w2_ref, u2_ref, z_ref, ho_ref, win_ref, sh_ref):
        i = pl.program_id(0)
        cwv = cw_ref[...]
        win_ref[0:HALO, :] = jnp.where(i > 0, _glu(ph_ref[...]), 0.0)
        win_ref[HALO:HALO + TS, :] = _glu(pre_ref[...])
        u2 = jnp.broadcast_to(cwv[31:32], (TS, D))
        for phase, taps in _tap_groups(lambda j: HALO - (CONV_W - 1) + j):
            _shift_window(sh_ref, win_ref, phase, taps[-1][0] + TS)
            for lo, j in taps:
                u2 = u2 + cwv[j:j + 1] * sh_ref[lo:lo + TS, :]
        u2_ref[...] = u2
        xh, _ = _layernorm(u2)
        un = xh * cwv[32:33] + cwv[33:34]
        zb = (un * _sigmoid(un)).astype(BF)
        z_ref[...] = zb
        y = _dot(zb, w2_ref[...]) + cwv[34:35]
        ho_ref[...] = h_ref[...] + cwv[35:36] * y

    tok = lambda i: (i, 0)
    return _hosted_call(
        body, (pre, pre, h, cw, w2), comm, name="conv_out_fwd", n_steps=S // TS,
        out_shape=[jax.ShapeDtypeStruct((S, D), F32), jax.ShapeDtypeStruct((S, D), BF), jax.ShapeDtypeStruct((S, D), F32)],
        in_specs=[_rows((TS, 2 * D), tok), _rows((HALO, 2 * D), lambda i: (jnp.maximum(i * hb - 1, 0), 0)),
                  _rows((TS, D), tok), _rows((40, D), lambda i: (0, 0)), VMEM_FULL],
        out_specs=[_rows((TS, D), tok), _rows((TS, D), tok), _rows((TS, D), tok)],
        scratch_shapes=[pltpu.VMEM((TS + HALO, D), F32)] * 2)


def _conv_out_bwd(dho, u2, cw, w2):
    S = dho.shape[0]

    def body(dho_ref, u2_ref, cw_ref, w2_ref, du2_ref, dhb_ref, ps_ref):
        dho_v, cwv = dho_ref[...], cw_ref[...]
        dhb_ref[...] = dho_v.astype(BF)
        dz = _dot_nt((cwv[35:36] * dho_v).astype(BF), w2_ref[...])
        xh, rstd = _layernorm(u2_ref[...])
        un = xh * cwv[32:33] + cwv[33:34]
        sig = _sigmoid(un)
        dun = dz * (sig * (1.0 + un * (1.0 - sig)))
        dxh = dun * cwv[32:33]
        du2_ref[...] = rstd * (dxh - _mean1(dxh) - xh * _mean1(dxh * xh))
        _accumulate(ps_ref, pl.program_id(0) == 0, [_sum0(dun * xh), _sum0(dun), _sum0(dho_v)])

    tok = lambda i: (i, 0)
    return pl.pallas_call(
        body, name="conv_out_bwd", grid=(S // TS,),
        out_shape=[jax.ShapeDtypeStruct((S, D), F32), jax.ShapeDtypeStruct((S, D), BF), jax.ShapeDtypeStruct((8, D), F32)],
        in_specs=[_rows((TS, D), tok), _rows((TS, D), tok), _rows((40, D), lambda i: (0, 0)), VMEM_FULL],
        out_specs=[_rows((TS, D), tok), _rows((TS, D), tok), _rows((8, D), lambda i: (0, 0))],
        compiler_params=_params(1),
    )(dho, u2, cw, w2)


def _conv_in_bwd(du2, pre, h, dho, vec, cw, w1):
    S = h.shape[0]
    n_t = S // TS
    hb = TS // HALO

    def body(du2_ref, dh2h_ref, pre_ref, ph_ref, h_ref, dho_ref, vec_ref, cw_ref, w1_ref,
             dh_ref, dpre_ref, ps_ref, pw_ref, pb_ref, winu_ref, wind_ref, sh_ref):
        i = pl.program_id(0)
        vec, cwv = vec_ref[...], cw_ref[...]
        pre = pre_ref[...]
        av, sg = pre[:, :D], _sigmoid(pre[:, D:])
        winu_ref[0:HALO, :] = jnp.where(i > 0, _glu(ph_ref[...]), 0.0)
        winu_ref[HALO:HALO + TS, :] = av * sg
        du2v = du2_ref[...]
        wind_ref[0:TS, :] = du2v
        wind_ref[TS:TS + HALO, :] = jnp.where(i < n_t - 1, dh2h_ref[...], 0.0)

        @pl.when(i == 0)
        def _():
            pw_ref[...] = jnp.zeros((32, D), F32)

        for phase, taps in _tap_groups(lambda j: HALO - (CONV_W - 1) + j):
            _shift_window(sh_ref, winu_ref, phase, taps[-1][0] + TS)
            for lo, j in taps:
                pw_ref[j:j + 1, :] += _sum0(du2v * sh_ref[lo:lo + TS, :])
        pw_ref[31:32, :] += _sum0(du2v)
        du1 = jnp.zeros((TS, D), F32)
        for phase, taps in _tap_groups(lambda j: CONV_W - 1 - j):
            _shift_window(sh_ref, wind_ref, phase, taps[-1][0] + TS)
            for lo, j in taps:
                du1 = du1 + cwv[j:j + 1] * sh_ref[lo:lo + TS, :]
        da = du1 * sg
        dg = du1 * av * sg * (1.0 - sg)
        dab, dgb = da.astype(BF), dg.astype(BF)
        dpre_ref[:, :D] = dab
        dpre_ref[:, D:] = dgb

        @pl.when(i == 0)
        def _():
            pb_ref[...] = jnp.zeros((8, 2 * D), F32)

        pb_ref[0:1, :D] += _sum0(da)
        pb_ref[0:1, D:] += _sum0(dg)
        du = _dot_nt(dab, w1_ref[:, :D]) + _dot_nt(dgb, w1_ref[:, D:])
        _, xhat, r = _norm_mod(h_ref[...], vec)
        dhn, dgn, dsh, dsc = _norm_mod_bwd(du, xhat, r, vec)
        dh_ref[...] = dho_ref[...] + dhn
        _accumulate(ps_ref, i == 0, [dgn, dsh, dsc])

    tok = lambda i: (i, 0)
    fixed = lambda i: (0, 0)
    return pl.pallas_call(
        body, name="conv_in_bwd", grid=(n_t,),
        out_shape=[jax.ShapeDtypeStruct((S, D), F32), jax.ShapeDtypeStruct((S, 2 * D), BF),
                   jax.ShapeDtypeStruct((8, D), F32), jax.ShapeDtypeStruct((32, D), F32),
                   jax.ShapeDtypeStruct((8, 2 * D), F32)],
        in_specs=[_rows((TS, D), tok), _rows((HALO, D), lambda i: (jnp.minimum((i + 1) * hb, S // HALO - 1), 0)),
                  _rows((TS, 2 * D), tok), _rows((HALO, 2 * D), lambda i: (jnp.maximum(i * hb - 1, 0), 0)),
                  _rows((TS, D), tok), _rows((TS, D), tok), _rows((8, D), fixed), _rows((40, D), fixed), VMEM_FULL],
        out_specs=[_rows((TS, D), tok), _rows((TS, 2 * D), tok), _rows((8, D), fixed), _rows((32, D), fixed),
                   _rows((8, 2 * D), fixed)],
        scratch_shapes=[pltpu.VMEM((TS + HALO, D), F32)] * 3,
        compiler_params=_params(1),
    )(du2, du2, pre, pre, h, dho, vec, cw, w1)


def _lane():
    return lax.broadcasted_iota(jnp.int32, (TS, HP), 1)


def _kv_fwd(h, vec, wkva, g2, wkvb, rope):
    S = h.shape[0]

    def body(h_ref, vec_ref, wa_ref, g2_ref, wb_ref, rope_ref, hn_ref, ckv_ref, ckn_ref, k_ref, v_ref):
        u, _, _ = _norm_mod(h_ref[...], vec_ref[...])
        ub = u.astype(BF)
        hn_ref[...] = ub
        kva = _dot(ub, wa_ref[...])
        ckv = kva[:, :KV_LORA]
        ckv_ref[...] = ckv
        r2 = lax.rsqrt(_mean1(ckv * ckv) + EPS)
        cknb = ((ckv * r2) * g2_ref[0:1, :]).astype(BF)
        ckn_ref[...] = cknb
        kvb = _dot(cknb, wb_ref[...])
        kpe = _rope(kva[:, KV_LORA:KVA_P], rope_ref[...])
        lane = _lane()
        ones_lane0 = jnp.where(lane == 0, 1.0, 0.0)
        for hd in range(NH):
            blk = kvb[:, hd * HP:(hd + 1) * HP]
            k_ref[:, hd * HP:(hd + 1) * HP] = jnp.where(lane < 64, blk, kpe).astype(BF)
            v_ref[:, hd * HP:(hd + 1) * HP] = jnp.where(lane >= 64, blk, ones_lane0).astype(BF)

    tok = lambda i: (i, 0)
    fixed = lambda i: (0, 0)
    return pl.pallas_call(
        body, name="kv_fwd", grid=(S // TS,),
        out_shape=[jax.ShapeDtypeStruct((S, D), BF), jax.ShapeDtypeStruct((S, KV_LORA), F32),
                   jax.ShapeDtypeStruct((S, KV_LORA), BF), jax.ShapeDtypeStruct((S, NH * HP), BF),
                   jax.ShapeDtypeStruct((S, NH * HP), BF)],
        in_specs=[_rows((TS, D), tok), _rows((8, D), fixed), VMEM_FULL, _rows((8, KV_LORA), fixed), VMEM_FULL,
                  _rows((TS, 3 * HP), tok)],
        out_specs=[_rows((TS, D), tok), _rows((TS, KV_LORA), tok), _rows((TS, KV_LORA), tok),
                   _rows((TS, NH * HP), tok), _rows((TS, NH * HP), tok)],
        compiler_params=_params(1),
    )(h, vec, wkva, g2, wkvb, rope)


def _kv_bwd(dk, dv, h, ckv, dho, vec, wkva, g2, wkvb, rope):
    S = h.shape[0]

    def body(dk_ref, dv_ref, h_ref, ckv_ref, dho_ref, vec_ref, wa_ref, g2_ref, wb_ref, rope_ref,
             dh_ref, dkva_ref, dkvb_ref, ps_ref, ps2_ref):
        i = pl.program_id(0)
        vec = vec_ref[...]
        lane = _lane()
        dkpe = jnp.zeros((TS, HP), F32)
        for hd in range(NH):
            dkh = dk_ref[:, hd * HP:(hd + 1) * HP]
            dvh = dv_ref[:, hd * HP:(hd + 1) * HP]
            dkvb_ref[:, hd * HP:(hd + 1) * HP] = jnp.where(lane < 64, dkh, dvh).astype(BF)
            dkpe = dkpe + jnp.where(lane >= 64, dkh, 0.0)
        dkpe = _rope_t(dkpe, rope_ref[...])
        dckn = _dot_nt(dkvb_ref[...], wb_ref[...])
        ckv = ckv_ref[...]
        r2 = lax.rsqrt(_mean1(ckv * ckv) + EPS)
        dckv, dg2 = _rms_bwd(dckn, ckv, r2, g2_ref[0:1, :])
        dkva_ref[:, :KV_LORA] = dckv.astype(BF)
        dkva_ref[:, KV_LORA:KVA_P] = dkpe.astype(BF)
        du = _dot_nt(dkva_ref[...], wa_ref[...])
        _, xhat, r = _norm_mod(h_ref[...], vec)
        dhn, dgn, dsh, dsc = _norm_mod_bwd(du, xhat, r, vec)
        dh_ref[...] = dho_ref[...] + dhn
        _accumulate(ps_ref, i == 0, [dgn, dsh, dsc])
        _accumulate(ps2_ref, i == 0, [dg2])

    tok = lambda i: (i, 0)
    fixed = lambda i: (0, 0)
    return pl.pallas_call(
        body, name="kv_bwd", grid=(S // TS,),
        out_shape=[jax.ShapeDtypeStruct((S, D), F32), jax.ShapeDtypeStruct((S, KVA_P), BF),
                   jax.ShapeDtypeStruct((S, NH * HP), BF), jax.ShapeDtypeStruct((8, D), F32),
                   jax.ShapeDtypeStruct((8, KV_LORA), F32)],
        in_specs=[_rows((TS, NH * HP), tok), _rows((TS, NH * HP), tok), _rows((TS, D), tok), _rows((TS, KV_LORA), tok),
                  _rows((TS, D), tok), _rows((8, D), fixed), VMEM_FULL, _rows((8, KV_LORA), fixed), VMEM_FULL,
                  _rows((TS, 3 * HP), tok)],
        out_specs=[_rows((TS, D), tok), _rows((TS, KVA_P), tok), _rows((TS, NH * HP), tok), _rows((8, D), fixed),
                   _rows((8, KV_LORA), fixed)],
        compiler_params=_params(1),
    )(dk, dv, h, ckv, dho, vec, wkva, g2, wkvb, rope)


def _q_fwd(h, vec, wqa, g2, wqb, rope):
    S = h.shape[0]

    def body(h_ref, vec_ref, wa_ref, g2_ref, wb_ref, rope_ref, hn_ref, qa_ref, qan_ref, q_ref):
        u, _, _ = _norm_mod(h_ref[...], vec_ref[...])
        ub = u.astype(BF)
        hn_ref[...] = ub
        qa = _dot(ub, wa_ref[...])
        qa_ref[...] = qa
        r2 = lax.rsqrt(_mean1(qa * qa) + EPS)
        qanb = ((qa * r2) * g2_ref[0:1, :]).astype(BF)
        qan_ref[...] = qanb
        q = _dot(qanb, wb_ref[...])
        tab = rope_ref[...]
        for hd in range(NH):
            q_ref[:, hd * HP:(hd + 1) * HP] = (_rope(q[:, hd * HP:(hd + 1) * HP], tab) * EXP2_SCALE).astype(BF)

    tok = lambda i: (i, 0)
    fixed = lambda i: (0, 0)
    return pl.pallas_call(
        body, name="q_fwd", grid=(S // TS,),
        out_shape=[jax.ShapeDtypeStruct((S, D), BF), jax.ShapeDtypeStruct((S, Q_LORA), F32),
                   jax.ShapeDtypeStruct((S, Q_LORA), BF), jax.ShapeDtypeStruct((S, NH * HP), BF)],
        in_specs=[_rows((TS, D), tok), _rows((8, D), fixed), VMEM_FULL, _rows((8, Q_LORA), fixed), VMEM_FULL,
                  _rows((TS, 3 * HP), tok)],
        out_specs=[_rows((TS, D), tok), _rows((TS, Q_LORA), tok), _rows((TS, Q_LORA), tok), _rows((TS, NH * HP), tok)],
        compiler_params=_params(1),
    )(h, vec, wqa, g2, wqb, rope)


def _q_bwd(dq, h, qa, dho, vec, wqa, g2, wqb, rope):
    S = h.shape[0]

    def body(dq_ref, h_ref, qa_ref, dho_ref, vec_ref, wa_ref, g2_ref, wb_ref, rope_ref,
             dh_ref, dqb_ref, dqa_ref, ps_ref, ps2_ref):
        i = pl.program_id(0)
        vec, tab = vec_ref[...], rope_ref[...]
        for hd in range(NH):
            dqb_ref[:, hd * HP:(hd + 1) * HP] = _rope_t(dq_ref[:, hd * HP:(hd + 1) * HP], tab).astype(BF)
        dqan = _dot_nt(dqb_ref[...], wb_ref[...])
        qa = qa_ref[...]
        r2 = lax.rsqrt(_mean1(qa * qa) + EPS)
        dqa, dg2 = _rms_bwd(dqan, qa, r2, g2_ref[0:1, :])
        dqab = dqa.astype(BF)
        dqa_ref[...] = dqab
        du = _dot_nt(dqab, wa_ref[...])
        _, xhat, r = _norm_mod(h_ref[...], vec)
        dhn, dgn, dsh, dsc = _norm_mod_bwd(du, xhat, r, vec)
        dh_ref[...] = dho_ref[...] + dhn
        _accumulate(ps_ref, i == 0, [dgn, dsh, dsc])
        _accumulate(ps2_ref, i == 0, [dg2])

    tok = lambda i: (i, 0)
    fixed = lambda i: (0, 0)
    return pl.pallas_call(
        body, name="q_bwd", grid=(S // TS,),
        out_shape=[jax.ShapeDtypeStruct((S, D), F32), jax.ShapeDtypeStruct((S, NH * HP), BF),
                   jax.ShapeDtypeStruct((S, Q_LORA), BF), jax.ShapeDtypeStruct((8, D), F32),
                   jax.ShapeDtypeStruct((8, Q_LORA), F32)],
        in_specs=[_rows((TS, NH * HP), tok), _rows((TS, D), tok), _rows((TS, Q_LORA), tok), _rows((TS, D), tok),
                  _rows((8, D), fixed), VMEM_FULL, _rows((8, Q_LORA), fixed), VMEM_FULL, _rows((TS, 3 * HP), tok)],
        out_specs=[_rows((TS, D), tok), _rows((TS, NH * HP), tok), _rows((TS, Q_LORA), tok), _rows((8, D), fixed),
                   _rows((8, Q_LORA), fixed)],
        compiler_params=_params(1),
    )(dq, h, qa, dho, vec, wqa, g2, wqb, rope)


def _attn_fwd(q, k, v):
    S = q.shape[0]
    TA = TA_FWD
    nq = S // TA
    rg = min(TA, ATT_ROWS)
    groups = TA // rg

    def softmax_pv(scores, vt, state, masks):
        out = []
        for g in range(groups):
            m, acc = state[g]
            s = scores[g] if masks is None else jnp.where(masks[g], scores[g], NEG)
            m_new = jnp.maximum(m, jnp.max(s, axis=1, keepdims=True))
            p = jnp.exp2(s - m_new)
            vg = vt[g] if isinstance(vt, list) else vt
            out.append((m_new, jnp.exp2(m - m_new) * acc + _dot(p.astype(BF), vg)))
        return tuple(out)

    def body(q_ref, k_ref, v_ref, o_ref, lse_ref):
        qi = pl.program_id(1)
        qs = [q_ref[g * rg:(g + 1) * rg, :] for g in range(groups)]

        def keys(j):
            return pl.ds(pl.multiple_of(j * TA, TA), TA)

        def scores_of(j):
            kt = k_ref[keys(j), :]
            return tuple(_dot_nt(qs[g], kt) for g in range(groups))

        state = tuple((jnp.full((rg, 1), NEG, F32), jnp.zeros((rg, HP), F32)) for _ in range(groups))

        def step(kj, state):
            return softmax_pv(scores_of(kj), v_ref[keys(kj), :], state, None)

        state = lax.fori_loop(0, qi, step, state)
        kd, vd = k_ref[keys(qi), :], v_ref[keys(qi), :]
        ends = [(g + 1) * rg for g in range(groups)]
        diag_scores = tuple(_dot_nt(qs[g], kd[:ends[g]]) for g in range(groups))
        masks = [lax.broadcasted_iota(jnp.int32, (rg, ends[g]), 1)
                 <= lax.broadcasted_iota(jnp.int32, (rg, ends[g]), 0) + g * rg for g in range(groups)]
        final = softmax_pv(diag_scores, [vd[:ends[g]] for g in range(groups)], state, masks)
        lane = lax.broadcasted_iota(jnp.int32, (rg, HP), 1)
        for g in range(groups):
            m, acc = final[g]
            l = jnp.sum(jnp.where(lane == 0, acc, 0.0), axis=1, keepdims=True)
            o_ref[g * rg:(g + 1) * rg, :] = (acc / l).astype(BF)
            lse_ref[g * rg:(g + 1) * rg, :] = m + jnp.log(l) * LOG2E

    return pl.pallas_call(
        body, name="attn_fwd", grid=(NH, nq),
        out_shape=[jax.ShapeDtypeStruct((S, NH * HP), BF), jax.ShapeDtypeStruct((NH, S, 1), F32)],
        in_specs=[_rows((TA, HP), lambda h, i: (i, h)), _rows((S, HP), lambda h, i: (0, h)),
                  _rows((S, HP), lambda h, i: (0, h))],
        out_specs=[_rows((TA, HP), lambda h, i: (i, h)), _rows((None, TA, 1), lambda h, i: (h, i, 0))],
        compiler_params=_params(2),
    )(q, k, v)


def _attn_bwd(q, k, v, do, lse, delta):
    S = q.shape[0]
    nq = S // TA
    rg_loop = min(TA, ATT_ROWS_BWD)
    rg_diag = min(TA, ATT_ROWS)

    def body(q_ref, do_ref, lse_ref, dl_ref, k_ref, v_ref, dq_ref, dk_ref, dv_ref, dka_ref, dva_ref):
        kj = pl.program_id(1)

        @pl.when(kj == 0)
        def _():
            dq_ref[...] = jnp.zeros((S, HP), F32)

        dka_ref[...] = jnp.zeros((TA, HP), F32)
        dva_ref[...] = jnp.zeros((TA, HP), F32)
        kt, vt = k_ref[...], v_ref[...]

        def tile(qi, diagonal):
            rg = rg_diag if diagonal else rg_loop
            groups = TA // rg
            rows = [pl.ds(pl.multiple_of(qi * TA + g * rg, rg), rg) for g in range(groups)]
            ends = [(g + 1) * rg if diagonal else TA for g in range(groups)]
            qg = [q_ref[r, :] for r in rows]
            dog = [do_ref[r, :] for r in rows]
            scores = [_dot_nt(qg[g], kt[:ends[g]]) for g in range(groups)]
            dps = [_dot_nt(dog[g], vt[:ends[g]]) for g in range(groups)]
            for g in range(groups):
                p = jnp.exp2(scores[g] - lse_ref[rows[g], :])
                if diagonal:
                    col = lax.broadcasted_iota(jnp.int32, (rg, ends[g]), 1)
                    row = lax.broadcasted_iota(jnp.int32, (rg, ends[g]), 0)
                    p = jnp.where(col <= row + g * rg, p, 0.0)
                ds = p * (dps[g] - dl_ref[rows[g], :])
                pb, dsb = p.astype(BF), ds.astype(BF)
                dva_ref[0:ends[g], :] += _dot_tn(pb, dog[g])
                dka_ref[0:ends[g], :] += _dot_tn(dsb, qg[g])
                dq_ref[rows[g], :] += _dot(dsb, kt[:ends[g]]) * SM_SCALE

        tile(kj, True)

        def step(qi, carry):
            tile(qi, False)
            return carry

        lax.fori_loop(kj + 1, nq, step, 0)
        dk_ref[...] = dka_ref[...] * LN2
        dv_ref[...] = dva_ref[...]

    head = lambda h, j: (0, h)
    col1 = lambda h, j: (h, 0, 0)
    return pl.pallas_call(
        body, name="attn_bwd", grid=(NH, nq), out_shape=[jax.ShapeDtypeStruct((S, NH * HP), F32)] * 3,
        in_specs=[_rows((S, HP), head), _rows((S, HP), head), _rows((None, S, 1), col1), _rows((None, S, 1), col1),
                  _rows((TA, HP), lambda h, j: (j, h)), _rows((TA, HP), lambda h, j: (j, h))],
        out_specs=[_rows((S, HP), head), _rows((TA, HP), lambda h, j: (j, h)), _rows((TA, HP), lambda h, j: (j, h))],
        scratch_shapes=[pltpu.VMEM((TA, HP), F32), pltpu.VMEM((TA, HP), F32)], compiler_params=_params(2),
    )(q, do, lse, delta, k, v)


def _attn_out_fwd(o, h, wo, vec):
    S = h.shape[0]

    def body(o_ref, h_ref, wo_ref, vec_ref, ho_ref):
        ho_ref[...] = h_ref[...] + vec_ref[3:4, :] * _dot(o_ref[...], wo_ref[...])

    tok = lambda i: (i, 0)
    return pl.pallas_call(
        body, name="attn_out_fwd", grid=(S // TS,), out_shape=jax.ShapeDtypeStruct((S, D), F32),
        in_specs=[_rows((TS, NH * HP), tok), _rows((TS, D), tok), VMEM_FULL, _rows((8, D), lambda i: (0, 0))],
        out_specs=_rows((TS, D), tok), compiler_params=_params(1),
    )(o, h, wo, vec)


def _attn_out_bwd(dho, o, wo, vec):
    S = dho.shape[0]

    def body(dho_ref, o_ref, wo_ref, vec_ref, do_ref, dl_ref, dhb_ref):
        dho_v = dho_ref[...]
        dhb_ref[...] = dho_v.astype(BF)
        do = _dot_nt((vec_ref[3:4, :] * dho_v).astype(BF), wo_ref[...])
        do_ref[...] = do.astype(BF)
        prod = do * o_ref[...].astype(F32)
        for hd in range(NH):
            dl_ref[hd] = jnp.sum(prod[:, hd * HP:(hd + 1) * HP], axis=1, keepdims=True)

    tok = lambda i: (i, 0)
    return pl.pallas_call(
        body, name="attn_out_bwd", grid=(S // TS,),
        out_shape=[jax.ShapeDtypeStruct((S, NH * HP), BF), jax.ShapeDtypeStruct((NH, S, 1), F32),
                   jax.ShapeDtypeStruct((S, D), BF)],
        in_specs=[_rows((TS, D), tok), _rows((TS, NH * HP), tok), VMEM_FULL, _rows((8, D), lambda i: (0, 0))],
        out_specs=[_rows((TS, NH * HP), tok), _rows((NH, TS, 1), lambda i: (0, i, 0)), _rows((TS, D), tok)],
        compiler_params=_params(1),
    )(dho, o, wo, vec)


def _final(h, target, fg):
    S = h.shape[0]

    def body(h_ref, t_ref, g_ref, dh_ref, ps_ref):
        hv, g = h_ref[...], g_ref[0:1, :]
        r = lax.rsqrt(_mean1(hv * hv) + EPS)
        xhat = hv * r
        err = xhat * g - t_ref[...]
        loss = 0.5 * jnp.sum(_mean1(err * err), axis=0, keepdims=True)
        dy = err * (1.0 / D)
        dxhat = dy * g
        dh_ref[...] = r * (dxhat - xhat * _mean1(dxhat * xhat))
        _accumulate(ps_ref, pl.program_id(0) == 0, [_sum0(dy * xhat), jnp.broadcast_to(loss, (1, D))])

    tok = lambda i: (i, 0)
    return pl.pallas_call(
        body, name="final_loss", grid=(S // TS,),
        out_shape=[jax.ShapeDtypeStruct((S, D), F32), jax.ShapeDtypeStruct((8, D), F32)],
        in_specs=[_rows((TS, D), tok), _rows((TS, D), tok), _rows((8, D), lambda i: (0, 0))],
        out_specs=[_rows((TS, D), tok), _rows((8, D), lambda i: (0, 0))], compiler_params=_params(1),
    )(h, target, fg)


def _row_tile(r):
    if r <= 512:
        return r
    for t in range(512, 7, -8):
        if r % t == 0:
            return t
    return r


def _adamw(parts, w, m, v):
    P, R, C = parts.shape
    tr = _row_tile(R)

    def body(p_ref, w_ref, m_ref, v_ref, g_ref, d_ref, mo_ref, vo_ref):
        g = p_ref[0].astype(F32)
        for k in range(1, P):
            g = g + p_ref[k].astype(F32)
        g_ref[...] = g
        m2 = B1 * m_ref[...] + (1.0 - B1) * g
        v2 = B2 * v_ref[...] + (1.0 - B2) * (g * g)
        mo_ref[...] = m2
        vo_ref[...] = v2
        m_hat = m2 / (1.0 - B1 ** STEP)
        v_hat = v2 / (1.0 - B2 ** STEP)
        d_ref[...] = -LR * (m_hat / (jnp.sqrt(v_hat) + EPS_ADAM) + WD * w_ref[...])

    blk = _rows((tr, C), lambda i: (i, 0))
    return pl.pallas_call(
        body, name="adamw", grid=(R // tr,), out_shape=[jax.ShapeDtypeStruct((R, C), F32)] * 4,
        in_specs=[_rows((P, tr, C), lambda i: (0, i, 0)), blk, blk, blk], out_specs=[blk] * 4,
        compiler_params=_params(1),
    )(parts, w, m, v)


_WEIGHTS = ['ada_w', 'ada_b', 'norm_g', 'ffn_w13', 'ffn_w2', 'conv_w_pw1', 'conv_b_pw1', 'conv_w_dw', 'conv_b_dw',
            'conv_ln_g', 'conv_ln_b', 'conv_w_pw2', 'conv_b_pw2', 'kv_ada_w', 'kv_ada_b', 'kv_norm_g', 'w_kv_a',
            'kv_a_norm_g', 'w_kv_b', 'w_q_a', 'q_a_norm_g', 'w_q_b', 'w_o', 'final_norm_g']


def _vec(rows):
    rows = [r.reshape(1, -1).astype(F32) for r in rows]
    return jnp.concatenate(rows + [jnp.zeros((8 - len(rows), rows[0].shape[1]), F32)], axis=0)


def kernel(x, c, positions, ada_w, ada_b, norm_g, ffn_w13, ffn_w2, conv_w_pw1, conv_b_pw1, conv_w_dw, conv_b_dw, conv_ln_g, conv_ln_b, conv_w_pw2, conv_b_pw2, kv_ada_w, kv_ada_b, kv_norm_g, w_kv_a, kv_a_norm_g, w_kv_b, w_q_a, q_a_norm_g, w_q_b, w_o, final_norm_g, loss_target, m_ada_w, m_ada_b, m_norm_g, m_ffn_w13, m_ffn_w2, m_conv_w_pw1, m_conv_b_pw1, m_conv_w_dw, m_conv_b_dw, m_conv_ln_g, m_conv_ln_b, m_conv_w_pw2, m_conv_b_pw2, m_kv_ada_w, m_kv_ada_b, m_kv_norm_g, m_w_kv_a, m_kv_a_norm_g, m_w_kv_b, m_w_q_a, m_q_a_norm_g, m_w_q_b, m_w_o, m_final_norm_g, v_ada_w, v_ada_b, v_norm_g, v_ffn_w13, v_ffn_w2, v_conv_w_pw1, v_conv_b_pw1, v_conv_w_dw, v_conv_b_dw, v_conv_ln_g, v_conv_ln_b, v_conv_w_pw2, v_conv_b_pw2, v_kv_ada_w, v_kv_ada_b, v_kv_norm_g, v_w_kv_a, v_kv_a_norm_g, v_w_kv_b, v_w_q_a, v_q_a_norm_g, v_w_q_b, v_w_o, v_final_norm_g):
    given = dict(locals())
    S = x.shape[1]
    me = 4 * lax.axis_index("x") + 2 * lax.axis_index("y") + lax.axis_index("c")

    small = jnp.concatenate([
        conv_w_dw[0], conv_b_dw, conv_ln_g, conv_ln_b, conv_b_pw2,
        norm_g.reshape(6, 128), conv_b_pw1.reshape(2, 128),
        c.reshape(8, 128), jnp.zeros((5, 128), F32)], axis=0)
    bf = lambda w: w.astype(BF)
    full_w13 = lambda g: jnp.transpose(g.reshape(N_DEV, D, 704), (1, 0, 2)).reshape(D, 2 * DFF)
    full_w2 = lambda g: g.reshape(DFF, D)
    got = _gather_two_level([small, bf(ffn_w13[0, 0]), bf(ffn_w2[0, 0])], "gather_first")
    w13_00, w2_00 = full_w13(got[1]), full_w2(got[2])
    sm = got[0]
    chan = lambda lo, hi: jnp.moveaxis(sm[:, lo:hi, :], 0, 1).reshape(hi - lo, D)
    w_dw_f, b_dw_f, ln_g_f, ln_b_f, b_pw2_f = chan(0, 31), chan(31, 32), chan(32, 33), chan(33, 34), chan(34, 35)
    norm_f = chan(35, 41).reshape(2, 3, D)
    b_pw1_f = sm[:, 41:43, :].reshape(1, 2 * D)
    c_all = sm[:, 43:51, :].reshape(N_DEV, D)

    n_ada = ada_w.shape[2]
    n_kva = kv_ada_w.shape[1]
    modp = _mod_fwd(c_all, ada_w[0], ada_w[1], kv_ada_w)
    (modr,) = _exchange([(modp.reshape(N_DEV, 1, 2 * n_ada + n_kva), "scatter")], "scatter_mod")
    modr = modr[:, 0, :]
    mod = jnp.transpose(modr[:, :2 * n_ada].reshape(N_DEV, 2, n_ada), (1, 0, 2)).reshape(2, 9 * D) + ada_b
    mod = mod.reshape(2, 9, D)
    kvmod = (modr[:, 2 * n_ada:].reshape(2 * D) + kv_ada_b).reshape(2, D)

    def sub_vec(l, idx):
        return _vec([norm_f[l, idx], mod[l, 3 * idx], mod[l, 3 * idx + 1], mod[l, 3 * idx + 2]])

    vec_kv = _vec([kv_norm_g, kvmod[0], kvmod[1]])
    cw = jnp.concatenate([w_dw_f, b_dw_f, ln_g_f, ln_b_f, b_pw2_f, mod[0, 5].reshape(1, D), jnp.zeros((4, D), F32)], axis=0)
    b1v = _vec([b_pw1_f])
    g_kva = _vec([kv_a_norm_g])
    g_qa = _vec([q_a_norm_g[0]])
    fgv = _vec([final_norm_g])

    inv_freq = 10000.0 ** (-jnp.arange(0, ROPE, 2, dtype=F32) / ROPE)
    ang = positions[0].astype(F32)[:, None] * inv_freq
    cs, sn = jnp.cos(ang), jnp.sin(ang)
    z16, z32, z64 = jnp.zeros((S, 16), F32), jnp.zeros((S, 32), F32), jnp.zeros((S, 64), F32)
    rope = jnp.concatenate([jnp.ones((S, 64), F32), cs, cs, z32,
                            z64, z16, sn, z32,
                            z64, -sn, z16, z32], axis=1)

    def merged(blocks, name):
        return [jnp.swapaxes(m, 0, 1).reshape((N_DEV,) + m.shape[2:]) for m in _sibling_merge(blocks, name)]

    h0 = x[0]
    group1 = [bf(conv_w_pw1[0]), bf(conv_w_pw2[0]), bf(ffn_w13[0, 1]), bf(ffn_w2[0, 1])]
    (h1, u00, a00, b00), blocks1 = _ffn_fwd(h0, sub_vec(0, 0), w13_00, w2_00, comm=(group1, False))
    g_pw1, g_pw2, g_w13, g_w2 = merged(blocks1, "merge_group1")
    pw1_f = jnp.transpose(g_pw1, (1, 0, 2)).reshape(D, 2 * D)
    pw2_f = g_pw2.reshape(D, D)
    w13_01, w2_01 = full_w13(g_w13), full_w2(g_w2)
    hn_c, pre = _conv_in_fwd(h1, sub_vec(0, 1), pw1_f, b1v)
    group2 = [bf(w_kv_a), bf(w_kv_b), bf(ffn_w13[1, 0]), bf(ffn_w2[1, 0])]
    (u2, z_c, h2), blocks2 = _conv_out_fwd(pre, h1, cw, pw2_f, comm=(group2, False))
    g_kva_w, g_kvb_w, g_w13, g_w2 = merged(blocks2, "merge_group2")
    wkva = g_kva_w.reshape(D, KV_LORA + ROPE)
    wkva_f = jnp.concatenate([wkva[:, :KV_LORA], jnp.zeros((D, 64), BF), wkva[:, KV_LORA:], jnp.zeros((D, 32), BF)], axis=1)
    wkvb_f = jnp.transpose(g_kvb_w, (1, 0, 2)).reshape(KV_LORA, NH * HP)
    w13_10, w2_10 = full_w13(g_w13), full_w2(g_w2)
    group3 = [bf(w_q_a[0]), bf(w_q_b[0]), bf(w_o[0]), bf(ffn_w13[1, 1]), bf(ffn_w2[1, 1])]
    (h3, u01, a01, b01), blocks3 = _ffn_fwd(h2, sub_vec(0, 2), w13_01, w2_01, comm=(group3, False))
    g_qa_w, g_qb_w, g_wo, g_w13, g_w2 = merged(blocks3, "merge_group3")
    wqa_f = g_qa_w.reshape(D, Q_LORA)
    wqb = jnp.transpose(g_qb_w, (1, 0, 2)).reshape(Q_LORA, NH, 96)
    wqb_f = jnp.pad(wqb, ((0, 0), (0, 0), (0, HP - 96))).reshape(Q_LORA, NH * HP)
    wo_f = jnp.pad(g_wo.reshape(NH, 64, D), ((0, 0), (64, 0), (0, 0))).reshape(NH * HP, D)
    w13_11, w2_11 = full_w13(g_w13), full_w2(g_w2)
    hn_kv, ckv, ckn, k_all, v_all = _kv_fwd(h3, vec_kv, wkva_f, g_kva, wkvb_f, rope)
    (h4, u10, a10, b10), _ = _ffn_fwd(h3, sub_vec(1, 0), w13_10, w2_10)
    hn_q, qa, qan, q_all = _q_fwd(h4, sub_vec(1, 1), wqa_f, g_qa, wqb_f, rope)
    o_all, lse = _attn_fwd(q_all, k_all, v_all)
    h5 = _attn_out_fwd(o_all, h4, wo_f, sub_vec(1, 1))
    (h6, u11, a11, b11), _ = _ffn_fwd(h5, sub_vec(1, 2), w13_11, w2_11)

    dh6, ps_fin = _final(h6, loss_target[0], fgv)
    loss = lax.psum(ps_fin[1, 0], ("x", "y", "c"))

    core = lax.axis_index("c").reshape(1).astype(jnp.int32)

    def pair_sums(sends, name):
        by_core = [s.reshape((4, 2) + s.shape[1:]).swapaxes(0, 1) for s in sends]
        from_sibling = _pair_swap(by_core, name)
        return [_pair_add(core, a.reshape(2, -1, a.shape[-1]), b.reshape(-1, b.shape[-1])).reshape(b.shape)
                for a, b in zip(by_core, from_sibling)]

    def ffn_back(dho, h_in, u, a, b, l, i, w13, w2, comm=None):
        vec = sub_vec(l, 2 * i)
        (dh, da, db, t, dhb, ps), reduced = _ffn_bwd(dho, h_in, a, b, vec, w13, w2, comm=comm)
        dwa = _wgrad(u, da, 512, DFF, "wgrad_w13")
        dwb = _wgrad(u, db, 512, DFF, "wgrad_w13")
        dw2, gs = _wgrad(t, dhb, CH, D, "wgrad_w2", gate=(w2, _vec([0.5 * vec[3]])))
        dgate = 0.5 * jnp.sum(gs[:, 0, :], axis=0)
        send13 = jnp.transpose(jnp.concatenate([dwa, dwb], axis=1).reshape(D, N_DEV, 704), (1, 0, 2))
        return dh, send13, dw2.reshape(N_DEV, 352, D), ps, dgate, reduced

    dh5, s13_11, s2_11, ps11, dg11, _ = ffn_back(dh6, h5, u11, a11, b11, 1, 1, w13_11, w2_11)
    vec_m1 = sub_vec(1, 1)
    do_all, delta, dhb5 = _attn_out_bwd(dh5, o_all, wo_f, vec_m1)
    dwo_p, gs_o = _wgrad(o_all, dhb5, D, D, "wgrad_wo", gate=(wo_f, _vec([vec_m1[3]])))
    dgm1 = jnp.sum(gs_o[:, 0, :], axis=0)
    dq_all, dk_all, dv_all = _attn_bwd(q_all, k_all, v_all, do_all, lse, delta)
    dh4, dqb, dqab, ps_q, ps_q2 = _q_bwd(dq_all, h4, qa, dh5, vec_m1, wqa_f, g_qa, wqb_f, rope)
    dwqb_p = _wgrad(qan, dqb, Q_LORA, D, "wgrad_wqb")
    dwqa = _wgrad(hn_q, dqab, D, Q_LORA, "wgrad_wqa")
    dh3a, s13_10, s2_10, ps10, dg10, _ = ffn_back(dh4, h3, u10, a10, b10, 1, 0, w13_10, w2_10)
    sums_a = pair_sums([
        s13_11, s2_11, s13_10, s2_10, dwqa.reshape(N_DEV, 128, Q_LORA),
        jnp.transpose(dwqb_p.reshape(Q_LORA, NH, HP)[:, :, :96].reshape(Q_LORA, N_DEV, 192), (1, 0, 2)),
        dwo_p.reshape(NH, HP, D)[:, 64:, :].reshape(N_DEV, 128, D)], "pair_swap_a")
    dh3, dkva, dkvb, ps_kv, ps_kv2 = _kv_bwd(dk_all, dv_all, h3, ckv, dh3a, vec_kv, wkva_f, g_kva, wkvb_f, rope)
    dwkva_p = _wgrad(hn_kv, dkva, D, KVA_P, "wgrad_wkva")
    dwkvb = _wgrad(ckn, dkvb, KV_LORA, D, "wgrad_wkvb")
    dh2, s13_01, s2_01, ps01, dg01, red_a = ffn_back(dh3, h2, u01, a01, b01, 0, 1, w13_01, w2_01, comm=(sums_a, True))
    vec_m0 = sub_vec(0, 1)
    du2, dhb2, ps_co = _conv_out_bwd(dh2, u2, cw, pw2_f)
    dpw2, gs_c = _wgrad(z_c, dhb2, D, D, "wgrad_pw2", gate=(pw2_f, _vec([vec_m0[3]])))
    dgm0 = jnp.sum(gs_c[:, 0, :], axis=0) + b_pw2_f[0] * ps_co[2]
    dh1, dpre, ps_ci, ps_dw, ps_b1 = _conv_in_bwd(du2, pre, h1, dh2, vec_m0, cw, pw1_f)
    dpw1 = _wgrad(hn_c, dpre, D, D, "wgrad_pw1")
    sums_b = pair_sums([
        s13_01, s2_01, jnp.transpose(dpw1.reshape(D, N_DEV, 256), (1, 0, 2)), dpw2.reshape(N_DEV, 128, D),
        jnp.concatenate([dwkva_p[:, :KV_LORA], dwkva_p[:, KV_LORA + 64:KV_LORA + 96]], axis=1).reshape(N_DEV, 128, KV_LORA + ROPE),
        jnp.transpose(dwkvb.reshape(KV_LORA, N_DEV, 256), (1, 0, 2))], "pair_swap_b")
    dh0, s13_00, s2_00, ps00, dg00, red_b = ffn_back(dh1, h0, u00, a00, b00, 0, 0, w13_00, w2_00, comm=(sums_b, True))
    red_c = _chip_scatter(pair_sums([s13_00, s2_00], "pair_swap_c"), "chip_scatter_last")
    r13_11, r2_11, r13_10, r2_10, r_wqa, r_wqb, r_wo = red_a
    r13_01, r2_01, r_pw1, r_pw2, r_wkva, r_wkvb = red_b
    r13_00, r2_00 = red_c

    dmod = jnp.stack([
        jnp.stack([ps00[1], ps00[2], dg00, ps_ci[1], ps_ci[2], dgm0, ps01[1], ps01[2], dg01]),
        jnp.stack([ps10[1], ps10[2], dg10, ps_q[1], ps_q[2], dgm1, ps11[1], ps11[2], dg11])])
    dnorm = jnp.stack([ps00[0], ps_ci[0], ps01[0], ps10[0], ps_q[0], ps11[0]])
    pieces = [dnorm, ps_b1[0], ps_dw[0:31], ps_dw[31], ps_co[0], ps_co[1], vec_m0[3] * ps_co[2],
              ps_kv[0], ps_kv2[0], ps_q2[0], ps_fin[0], dmod, ps_kv[1], ps_kv[2]]
    sizes = [int(np.prod(p.shape)) for p in pieces]
    offs = np.concatenate([[0], np.cumsum(sizes)]).astype(int)
    flat = jnp.concatenate([p.reshape(-1) for p in pieces]).reshape(-1, 128)
    (part,) = _exchange([(flat, "gather")], "gather_partials")
    part = part.reshape(N_DEV, -1)

    def piece(i, rows, cols):
        return part[:, offs[i]:offs[i + 1]].reshape(N_DEV, rows, cols)

    def mine(i, rows):
        return lax.dynamic_slice_in_dim(piece(i, rows, D), me * 128, 128, axis=2)

    dmod_all = piece(11, 2, 9 * D)
    c_t = jnp.transpose(c_all)
    g_ada = jnp.stack([_mod_wgrad(c_t, lax.dynamic_slice_in_dim(dmod_all[:, l], me * n_ada, n_ada, axis=1))
                       for l in range(2)])
    dkvmod_all = jnp.concatenate([piece(12, 1, D)[:, 0], piece(13, 1, D)[:, 0]], axis=1)
    g_kvada = _mod_wgrad(c_t, lax.dynamic_slice_in_dim(dkvmod_all, me * n_kva, n_kva, axis=1))

    parts = {
        'ada_w': g_ada.reshape(1, 2 * D, n_ada),
        'ada_b': dmod_all,
        'norm_g': mine(0, 6),
        'ffn_w13': jnp.stack([r13_00, r13_01, r13_10, r13_11], axis=1).reshape(4, 4 * D, 704),
        'ffn_w2': jnp.stack([r2_00, r2_01, r2_10, r2_11], axis=1).reshape(4, 4 * 352, D),
        'conv_w_pw1': r_pw1,
        'conv_b_pw1': lax.dynamic_slice_in_dim(piece(1, 1, 2 * D), me * 256, 256, axis=2),
        'conv_w_dw': mine(2, 31),
        'conv_b_dw': mine(3, 1),
        'conv_ln_g': mine(4, 1),
        'conv_ln_b': mine(5, 1),
        'conv_w_pw2': r_pw2,
        'conv_b_pw2': mine(6, 1),
        'kv_ada_w': g_kvada.reshape(1, D, n_kva),
        'kv_ada_b': dkvmod_all.reshape(N_DEV, 1, 2 * D),
        'kv_norm_g': piece(7, 1, D),
        'w_kv_a': r_wkva,
        'kv_a_norm_g': piece(8, 1, KV_LORA),
        'w_kv_b': r_wkvb,
        'w_q_a': r_wqa,
        'q_a_norm_g': piece(9, 1, Q_LORA),
        'w_q_b': r_wqb,
        'w_o': r_wo,
        'final_norm_g': piece(10, 1, D),
    }
    grads, deltas, new_m, new_v = [], [], [], []
    for name in _WEIGHTS:
        w = given[name]
        p = parts[name]
        shape2 = p.shape[1:]
        g, dlt, m2, v2 = _adamw(p, w.reshape(shape2), given['m_' + name].reshape(shape2), given['v_' + name].reshape(shape2))
        grads.append(g.reshape(w.shape))
        deltas.append(dlt.reshape(w.shape))
        new_m.append(m2.reshape(w.shape))
        new_v.append(v2.reshape(w.shape))
    return (loss, dh0.reshape(1, S, D), *grads, *deltas, *new_m, *new_v)
```

```python
import functools

import numpy as np
import jax
import jax.numpy as jnp
from jax import lax
from jax.experimental import pallas as pl
from jax.experimental.pallas import tpu as pltpu

F32, BF = jnp.float32, jnp.bfloat16

D = 1024
DFF = 2816
CH = 1408
FFN_CHUNKS = ((0, 1536), (1536, 1280))
NH = 16
HP = 128
KV_LORA, Q_LORA, ROPE = 256, 512, 32
KVA_P = 384
CONV_W = 31
HALO = 32
EPS = 1e-6
SM_SCALE = float((64 + 32) ** -0.5)
LOG2E = 1.4426950408889634
LN2 = 0.6931471805599453
EXP2_SCALE = SM_SCALE * LOG2E
NEG = -1e30
N_DEV = 8
MESH = pl.DeviceIdType.MESH

TS = 512
TS_FFN_FWD = 512
TS_FFN_BWD = 256
TA = 1024
TA_FWD = 2048
ATT_ROWS = 256
ATT_ROWS_BWD = 512
TW = 2048
VMEM_LIMIT = 56 * 1024 * 1024

LR, B1, B2, EPS_ADAM, WD, STEP = 0.001, 0.9, 0.999, 1e-08, 0.01, 10

VMEM_FULL = pl.BlockSpec(memory_space=pltpu.VMEM)
HBM_FULL = pl.BlockSpec(memory_space=pltpu.HBM)


def _params(n_grid):
    return pltpu.CompilerParams(dimension_semantics=("arbitrary",) * n_grid, vmem_limit_bytes=VMEM_LIMIT)


def _dot(a, b):
    return jnp.dot(a, b, preferred_element_type=F32)


def _dot_nt(a, b):
    return lax.dot_general(a, b, (((1,), (1,)), ((), ())), preferred_element_type=F32)


def _dot_tn(a, b):
    return lax.dot_general(a, b, (((0,), (0,)), ((), ())), preferred_element_type=F32)


def _sum0(x):
    return jnp.sum(x, axis=0, keepdims=True)


def _mean1(x):
    return jnp.mean(x, axis=-1, keepdims=True)


def _sigmoid(x):
    return jax.nn.sigmoid(x)


def _rows(shape, imap):
    return pl.BlockSpec(shape, imap)


def _norm_mod(h, vec):
    r = lax.rsqrt(_mean1(h * h) + EPS)
    xhat = h * r
    u = (xhat * vec[0:1]) * (1.0 + vec[2:3]) + vec[1:2]
    return u, xhat, r


def _norm_mod_bwd(du, xhat, r, vec):
    g = vec[0:1]
    dxn = du * (1.0 + vec[2:3])
    dsh = _sum0(du)
    dsc = _sum0(du * (xhat * g))
    dg = _sum0(dxn * xhat)
    dxhat = dxn * g
    dh = r * (dxhat - xhat * _mean1(dxhat * xhat))
    return dh, dg, dsh, dsc


def _rms_bwd(dy, x, r, g):
    xhat = x * r
    dg = _sum0(dy * xhat)
    dxhat = dy * g
    return r * (dxhat - xhat * _mean1(dxhat * xhat)), dg


def _accumulate(ref, first, rows):
    @pl.when(first)
    def _():
        ref[...] = jnp.zeros(ref.shape, ref.dtype)

    for i, row in enumerate(rows):
        ref[i:i + 1, :] += row


def _rope(x, tab):
    return x * tab[:, 0:HP] + pltpu.roll(x, 16, 1) * tab[:, HP:2 * HP] + pltpu.roll(x, HP - 16, 1) * tab[:, 2 * HP:3 * HP]


def _rope_t(dy, tab):
    return (dy * tab[:, 0:HP] + pltpu.roll(dy * tab[:, HP:2 * HP], HP - 16, 1)
            + pltpu.roll(dy * tab[:, 2 * HP:3 * HP], 16, 1))


def _exchange(items, name):
    n = len(items)

    def body(*refs):
        ins, outs = refs[:n], refs[n:2 * n]
        send_sems, recv_sems, local_sems = refs[2 * n:]
        x, y, c = lax.axis_index("x"), lax.axis_index("y"), lax.axis_index("c")
        me = 4 * x + 2 * y + c

        def source(j, dev):
            return ins[j] if items[j][1] == "gather" else ins[j].at[dev]

        own = [pltpu.make_async_copy(source(j, me), outs[j].at[me], local_sems.at[j]) for j in range(n)]
        for cp in own:
            cp.start()
        remote = []
        for d in range(1, N_DEV):
            px = 1 - x if d & 4 else x
            py = 1 - y if d & 2 else y
            pc = 1 - c if d & 1 else c
            peer = 4 * px + 2 * py + pc
            for j in range(n):
                pltpu.make_async_remote_copy(
                    src_ref=source(j, peer), dst_ref=outs[j].at[me], send_sem=send_sems.at[j, d - 1],
                    recv_sem=recv_sems.at[j, d - 1], device_id=(px, py, pc), device_id_type=MESH).start()
                remote.append(pltpu.make_async_remote_copy(
                    src_ref=source(j, peer), dst_ref=outs[j].at[peer], send_sem=send_sems.at[j, d - 1],
                    recv_sem=recv_sems.at[j, d - 1], device_id=(px, py, pc), device_id_type=MESH))
        for cp in remote:
            cp.wait_send()
            cp.wait_recv()
        for cp in own:
            cp.wait()

    out_shape = []
    for arr, mode in items:
        shp = (N_DEV,) + tuple(arr.shape) if mode == "gather" else tuple(arr.shape)
        out_shape.append(jax.ShapeDtypeStruct(shp, arr.dtype))
    return pl.pallas_call(
        body, name=name, out_shape=out_shape,
        in_specs=[HBM_FULL] * n, out_specs=[HBM_FULL] * n,
        scratch_shapes=[pltpu.SemaphoreType.DMA((n, N_DEV - 1)), pltpu.SemaphoreType.DMA((n, N_DEV - 1)),
                        pltpu.SemaphoreType.DMA((n,))],
        compiler_params=pltpu.CompilerParams(has_side_effects=True),
    )(*[a for a, _ in items])


def _gather_two_level(arrs, name):
    n = len(arrs)

    def body(*refs):
        ins, outs = refs[:n], refs[n:2 * n]
        send_sems, recv_sems, local_sems = refs[2 * n:]
        x, y, c = lax.axis_index("x"), lax.axis_index("y"), lax.axis_index("c")
        sibling = (x, y, 1 - c)
        chips = [(1 - x, y), (x, 1 - y), (1 - x, 1 - y)]

        def slot(j, px, py, pc):
            return outs[j].at[4 * px + 2 * py + pc]

        def copy(j, k, block, to, src=None):
            return pltpu.make_async_remote_copy(
                src_ref=slot(j, *block) if src is None else src, dst_ref=slot(j, *block),
                send_sem=send_sems.at[j, k], recv_sem=recv_sems.at[j, k], device_id=to, device_id_type=MESH)

        own = [pltpu.make_async_copy(ins[j], slot(j, x, y, c), local_sems.at[j]) for j in range(n)]
        for cp in own:
            cp.start()
        sent = []
        for j in range(n):
            sent.append(copy(j, 0, (x, y, c), sibling, src=ins[j]))
            sent += [copy(j, 1 + i, (x, y, c), (*chip, c), src=ins[j]) for i, chip in enumerate(chips)]
        for cp in sent:
            cp.start()
        for i, chip in enumerate(chips):
            for j in range(n):
                copy(j, 1 + i, (*chip, c), (x, y, c)).wait_recv()
                passed = copy(j, 4 + i, (*chip, c), sibling)
                passed.start()
                sent.append(passed)
        for j in range(n):
            copy(j, 0, (x, y, 1 - c), (x, y, c)).wait_recv()
            for i, chip in enumerate(chips):
                copy(j, 4 + i, (*chip, 1 - c), (x, y, c)).wait_recv()
        for cp in sent:
            cp.wait_send()
        for cp in own:
            cp.wait()

    return pl.pallas_call(
        body, name=name, out_shape=[jax.ShapeDtypeStruct((N_DEV,) + tuple(a.shape), a.dtype) for a in arrs],
        in_specs=[HBM_FULL] * n, out_specs=[HBM_FULL] * n,
        scratch_shapes=[pltpu.SemaphoreType.DMA((n, N_DEV - 1)), pltpu.SemaphoreType.DMA((n, N_DEV - 1)),
                        pltpu.SemaphoreType.DMA((n,))],
        compiler_params=pltpu.CompilerParams(has_side_effects=True),
    )(*arrs)


def _pair_swap(arrs, name):
    n = len(arrs)

    def body(*refs):
        ins, outs = refs[:n], refs[n:2 * n]
        send_sems, recv_sems = refs[2 * n:]
        x, y, c = lax.axis_index("x"), lax.axis_index("y"), lax.axis_index("c")
        copies = [pltpu.make_async_remote_copy(
            src_ref=ins[j].at[1 - c], dst_ref=outs[j], send_sem=send_sems.at[j], recv_sem=recv_sems.at[j],
            device_id=(x, y, 1 - c), device_id_type=MESH) for j in range(n)]
        for cp in copies:
            cp.start()
        for cp in copies:
            cp.wait_send()
            cp.wait_recv()

    return pl.pallas_call(
        body, name=name, out_shape=[jax.ShapeDtypeStruct(tuple(a.shape[1:]), a.dtype) for a in arrs],
        in_specs=[HBM_FULL] * n, out_specs=[HBM_FULL] * n,
        scratch_shapes=[pltpu.SemaphoreType.DMA((n,)), pltpu.SemaphoreType.DMA((n,))],
        compiler_params=pltpu.CompilerParams(has_side_effects=True),
    )(*arrs)


def _chip_comm(arrs, scatter):
    n = len(arrs)
    out_shapes = [jax.ShapeDtypeStruct(tuple(a.shape) if scatter else (2, 4) + tuple(a.shape), a.dtype) for a in arrs]
    scratch = [pltpu.SemaphoreType.DMA((n, 3)), pltpu.SemaphoreType.DMA((n, 3)), pltpu.SemaphoreType.DMA((n,))]

    def copies(ins, outs, send_sems, recv_sems, local_sems, arriving):
        x, y, c = lax.axis_index("x"), lax.axis_index("y"), lax.axis_index("c")
        here = 2 * x + y

        def source(j, chip):
            return ins[j].at[chip] if scatter else ins[j]

        def slot(j, chip):
            return outs[j].at[chip] if scatter else outs[j].at[c].at[chip]

        own = [pltpu.make_async_copy(source(j, here), slot(j, here), local_sems.at[j]) for j in range(n)]
        remote = []
        for i, (px, py) in enumerate([(1 - x, y), (x, 1 - y), (1 - x, 1 - y)]):
            there = 2 * px + py
            for j in range(n):
                remote.append(pltpu.make_async_remote_copy(
                    src_ref=source(j, there), dst_ref=slot(j, there if arriving else here),
                    send_sem=send_sems.at[j, i], recv_sem=recv_sems.at[j, i], device_id=(px, py, c),
                    device_id_type=MESH))
        return own, remote

    def start(*refs):
        own, remote = copies(*refs, arriving=False)
        for cp in own + remote:
            cp.start()

    def finish(*refs):
        own, remote = copies(*refs, arriving=True)
        for cp in remote:
            cp.wait_send()
            cp.wait_recv()
        for cp in own:
            cp.wait()

    return out_shapes, scratch, start, finish


def _chip_scatter(arrs, name):
    n = len(arrs)
    out_shapes, scratch, start, finish = _chip_comm(arrs, True)

    def body(*refs):
        start(refs[:n], refs[n:2 * n], *refs[2 * n:])
        finish(refs[:n], refs[n:2 * n], *refs[2 * n:])

    return pl.pallas_call(
        body, name=name, out_shape=out_shapes, in_specs=[HBM_FULL] * n, out_specs=[HBM_FULL] * n,
        scratch_shapes=scratch, compiler_params=pltpu.CompilerParams(has_side_effects=True),
    )(*arrs)


def _hosted_call(body, args, comm, *, name, n_steps, in_specs, out_specs, out_shape, scratch_shapes=()):
    if comm is None:
        res = pl.pallas_call(
            body, name=name, grid=(n_steps,), out_shape=list(out_shape), in_specs=list(in_specs),
            out_specs=list(out_specs), scratch_shapes=list(scratch_shapes), compiler_params=_params(1))(*args)
        return res, []
    arrs, scatter = comm
    c_shapes, c_scratch, start, finish = _chip_comm(arrs, scatter)
    n_in, n_out, n_sc, k = len(in_specs), len(out_specs), len(scratch_shapes), len(arrs)

    def hosting(*refs):
        ins, cin = refs[:n_in], refs[n_in:n_in + k]
        outs, cout = refs[n_in + k:n_in + k + n_out], refs[n_in + k + n_out:n_in + 2 * k + n_out]
        scratch = refs[n_in + 2 * k + n_out:n_in + 2 * k + n_out + n_sc]
        sems = refs[n_in + 2 * k + n_out + n_sc:]
        step = pl.program_id(0)

        @pl.when(step == 0)
        def _():
            start(cin, cout, *sems)

        body(*ins, *outs, *scratch)

        @pl.when(step == n_steps - 1)
        def _():
            finish(cin, cout, *sems)

    res = pl.pallas_call(
        hosting, name=name + "_hosting", grid=(n_steps,), out_shape=list(out_shape) + c_shapes,
        in_specs=list(in_specs) + [HBM_FULL] * k, out_specs=list(out_specs) + [HBM_FULL] * k,
        scratch_shapes=list(scratch_shapes) + c_scratch,
        compiler_params=pltpu.CompilerParams(dimension_semantics=("arbitrary",), vmem_limit_bytes=VMEM_LIMIT,
                                             has_side_effects=True))(*args, *arrs)
    return res[:n_out], res[n_out:]


def _sibling_merge(arrs, name):
    n = len(arrs)

    def body(*refs):
        bufs = refs[n:2 * n]
        send_sems, recv_sems = refs[2 * n:]
        x, y, c = lax.axis_index("x"), lax.axis_index("y"), lax.axis_index("c")
        waits = []
        for j in range(n):
            pltpu.make_async_remote_copy(
                src_ref=bufs[j].at[c], dst_ref=bufs[j].at[c], send_sem=send_sems.at[j], recv_sem=recv_sems.at[j],
                device_id=(x, y, 1 - c), device_id_type=MESH).start()
            waits.append(pltpu.make_async_remote_copy(
                src_ref=bufs[j].at[c], dst_ref=bufs[j].at[1 - c], send_sem=send_sems.at[j],
                recv_sem=recv_sems.at[j], device_id=(x, y, 1 - c), device_id_type=MESH))
        for cp in waits:
            cp.wait_send()
            cp.wait_recv()

    return pl.pallas_call(
        body, name=name, out_shape=[jax.ShapeDtypeStruct(tuple(a.shape), a.dtype) for a in arrs],
        in_specs=[HBM_FULL] * n, out_specs=[HBM_FULL] * n, input_output_aliases={j: j for j in range(n)},
        scratch_shapes=[pltpu.SemaphoreType.DMA((n,)), pltpu.SemaphoreType.DMA((n,))],
        compiler_params=pltpu.CompilerParams(has_side_effects=True),
    )(*arrs)


def _pair_add(core, a, b):
    _, R, C = a.shape
    tr = _row_tile(R)

    def body(core_ref, a_ref, b_ref, o_ref):
        o_ref[...] = (a_ref[...].astype(F32) + b_ref[...].astype(F32)).astype(BF)

    blk = _rows((tr, C), lambda i, core: (i, 0))
    grid_spec = pltpu.PrefetchScalarGridSpec(
        num_scalar_prefetch=1, grid=(R // tr,),
        in_specs=[_rows((None, tr, C), lambda i, core: (core[0], i, 0)), blk], out_specs=blk)
    return pl.pallas_call(
        body, name="pair_add", grid_spec=grid_spec, out_shape=jax.ShapeDtypeStruct((R, C), BF),
        compiler_params=_params(1),
    )(core, a, b)


def _mod_fwd(c_all, w0, w1, wkv):
    n0, n1, n2 = w0.shape[1], w1.shape[1], wkv.shape[1]

    def body(c_ref, w0_ref, w1_ref, w2_ref, o_ref):
        cc = c_ref[...]
        s = cc * _sigmoid(cc)
        o_ref[:, 0:n0] = _dot(s, w0_ref[...])
        o_ref[:, n0:n0 + n1] = _dot(s, w1_ref[...])
        o_ref[:, n0 + n1:n0 + n1 + n2] = _dot(s, w2_ref[...])

    return pl.pallas_call(
        body, name="mod_fwd", out_shape=jax.ShapeDtypeStruct((N_DEV, n0 + n1 + n2), F32),
        in_specs=[VMEM_FULL] * 4, out_specs=VMEM_FULL,
        compiler_params=pltpu.CompilerParams(vmem_limit_bytes=VMEM_LIMIT),
    )(c_all, w0, w1, wkv)


def _mod_wgrad(c_t, dm):
    C = dm.shape[1]
    tr = 256

    def body(ct_ref, dm_ref, o_ref):
        ct = ct_ref[...]
        s = ct * _sigmoid(ct)
        dmv = dm_ref[...]
        lane = lax.broadcasted_iota(jnp.int32, (tr, N_DEV), 1)
        acc = jnp.zeros((tr, C), F32)
        for r in range(N_DEV):
            col = jnp.sum(jnp.where(lane == r, s, 0.0), axis=1, keepdims=True)
            acc = acc + col * dmv[r:r + 1, :]
        o_ref[...] = acc

    return pl.pallas_call(
        body, name="mod_wgrad", grid=(D // tr,), out_shape=jax.ShapeDtypeStruct((D, C), F32),
        in_specs=[_rows((tr, N_DEV), lambda i: (i, 0)), _rows((N_DEV, C), lambda i: (0, 0))],
        out_specs=_rows((tr, C), lambda i: (i, 0)), compiler_params=_params(1),
    )(c_t, dm)


def _ffn_fwd(h, vec, w13, w2, comm=None):
    S = h.shape[0]
    ts = min(S, TS_FFN_FWD)

    def body(h_ref, vec_ref, w13_ref, w2_ref, ho_ref, u_ref, a_ref, b_ref):
        hv, vec = h_ref[...], vec_ref[...]
        u, _, _ = _norm_mod(hv, vec)
        ub = u.astype(BF)
        u_ref[...] = ub
        y = jnp.zeros((ts, D), F32)
        for c0, cw in FFN_CHUNKS:
            a = _dot(ub, w13_ref[:, c0:c0 + cw])
            b = _dot(ub, w13_ref[:, DFF + c0:DFF + c0 + cw])
            a_ref[:, c0:c0 + cw] = a.astype(BF)
            b_ref[:, c0:c0 + cw] = b.astype(BF)
            t = (a * _sigmoid(a)) * b
            y = y + _dot(t.astype(BF), w2_ref[c0:c0 + cw, :])
        ho_ref[...] = hv + (0.5 * vec[3:4]) * y

    tok = lambda i: (i, 0)
    return _hosted_call(
        body, (h, vec, w13, w2), comm, name="ffn_fwd", n_steps=S // ts,
        out_shape=[jax.ShapeDtypeStruct((S, D), F32), jax.ShapeDtypeStruct((S, D), BF),
                   jax.ShapeDtypeStruct((S, DFF), BF), jax.ShapeDtypeStruct((S, DFF), BF)],
        in_specs=[_rows((ts, D), tok), _rows((8, D), lambda i: (0, 0)), VMEM_FULL, VMEM_FULL],
        out_specs=[_rows((ts, D), tok), _rows((ts, D), tok), _rows((ts, DFF), tok), _rows((ts, DFF), tok)])


def _ffn_bwd(dho, h, a, b, vec, w13, w2, comm=None):
    S = h.shape[0]
    ts = min(S, TS_FFN_BWD)

    def body(dho_ref, h_ref, a_ref, b_ref, vec_ref, w13_ref, w2_ref,
             dh_ref, da_ref, db_ref, t_ref, dhb_ref, ps_ref):
        dho_v, vec = dho_ref[...], vec_ref[...]
        dhb_ref[...] = dho_v.astype(BF)
        dyb = ((0.5 * vec[3:4]) * dho_v).astype(BF)
        du = jnp.zeros((ts, D), F32)
        for c0, cw in FFN_CHUNKS:
            av = a_ref[:, c0:c0 + cw].astype(F32)
            bv = b_ref[:, c0:c0 + cw].astype(F32)
            dt = _dot_nt(dyb, w2_ref[c0:c0 + cw, :])
            sig = _sigmoid(av)
            sl = av * sig
            t_ref[:, c0:c0 + cw] = (sl * bv).astype(BF)
            dab = (dt * bv * (sig * (1.0 + av * (1.0 - sig)))).astype(BF)
            dbb = (dt * sl).astype(BF)
            da_ref[:, c0:c0 + cw] = dab
            db_ref[:, c0:c0 + cw] = dbb
            du = du + _dot_nt(dab, w13_ref[:, c0:c0 + cw]) + _dot_nt(dbb, w13_ref[:, DFF + c0:DFF + c0 + cw])
        _, xhat, r = _norm_mod(h_ref[...], vec)
        dhn, dg, dsh, dsc = _norm_mod_bwd(du, xhat, r, vec)
        dh_ref[...] = dho_v + dhn
        _accumulate(ps_ref, pl.program_id(0) == 0, [dg, dsh, dsc])

    tok = lambda i: (i, 0)
    return _hosted_call(
        body, (dho, h, a, b, vec, w13, w2), comm, name="ffn_bwd", n_steps=S // ts,
        out_shape=[jax.ShapeDtypeStruct((S, D), F32), jax.ShapeDtypeStruct((S, DFF), BF),
                   jax.ShapeDtypeStruct((S, DFF), BF), jax.ShapeDtypeStruct((S, DFF), BF),
                   jax.ShapeDtypeStruct((S, D), BF), jax.ShapeDtypeStruct((8, D), F32)],
        in_specs=[_rows((ts, D), tok), _rows((ts, D), tok), _rows((ts, DFF), tok), _rows((ts, DFF), tok),
                  _rows((8, D), lambda i: (0, 0)), VMEM_FULL, VMEM_FULL],
        out_specs=[_rows((ts, D), tok), _rows((ts, DFF), tok), _rows((ts, DFF), tok), _rows((ts, DFF), tok),
                   _rows((ts, D), tok), _rows((8, D), lambda i: (0, 0))])


def _wgrad(a, b, tm, tn, name, gate=None):
    S, M = a.shape
    N = b.shape[1]
    n_s = S // TW

    def body(*refs):
        if gate is None:
            a_ref, b_ref, o_ref, acc_ref = refs
        else:
            a_ref, b_ref, w_ref, sc_ref, o_ref, gs_ref, acc_ref = refs
        s = pl.program_id(2)

        @pl.when(s == 0)
        def _():
            acc_ref[...] = jnp.zeros((tm, tn), F32)

        acc_ref[...] += _dot_tn(a_ref[...], b_ref[...])

        @pl.when(s == n_s - 1)
        def _():
            acc = acc_ref[...]
            if gate is None:
                o_ref[...] = acc.astype(BF)
            else:
                o_ref[...] = (acc * sc_ref[0:1, :]).astype(BF)
                gs_ref[...] = jnp.broadcast_to(_sum0(acc * w_ref[...].astype(F32)), (8, tn))

    in_specs = [_rows((TW, tm), lambda m, n, s: (s, m)), _rows((TW, tn), lambda m, n, s: (s, n))]
    out_shape = [jax.ShapeDtypeStruct((M, N), BF)]
    out_specs = [_rows((tm, tn), lambda m, n, s: (m, n))]
    args = [a, b]
    if gate is not None:
        in_specs += [_rows((tm, tn), lambda m, n, s: (m, n)), _rows((8, tn), lambda m, n, s: (0, n))]
        out_shape.append(jax.ShapeDtypeStruct((M // tm, 8, N), F32))
        out_specs.append(_rows((None, 8, tn), lambda m, n, s: (m, 0, n)))
        args += list(gate)
    res = pl.pallas_call(
        body, name=name, grid=(M // tm, N // tn, n_s), out_shape=out_shape, in_specs=in_specs,
        out_specs=out_specs, scratch_shapes=[pltpu.VMEM((tm, tn), F32)], compiler_params=_params(3),
    )(*args)
    return res[0] if gate is None else (res[0], res[1])


def _conv_in_fwd(h, vec, w1, b1):
    S = h.shape[0]

    def body(h_ref, vec_ref, w_ref, b1_ref, hn_ref, pre_ref):
        u, _, _ = _norm_mod(h_ref[...], vec_ref[...])
        ub = u.astype(BF)
        hn_ref[...] = ub
        pre_ref[...] = _dot(ub, w_ref[...]) + b1_ref[0:1, :]

    tok = lambda i: (i, 0)
    return pl.pallas_call(
        body, name="conv_in_fwd", grid=(S // TS,),
        out_shape=[jax.ShapeDtypeStruct((S, D), BF), jax.ShapeDtypeStruct((S, 2 * D), F32)],
        in_specs=[_rows((TS, D), tok), _rows((8, D), lambda i: (0, 0)), VMEM_FULL, _rows((8, 2 * D), lambda i: (0, 0))],
        out_specs=[_rows((TS, D), tok), _rows((TS, 2 * D), tok)], compiler_params=_params(1),
    )(h, vec, w1, b1)


def _glu(pre):
    return pre[:, :D] * _sigmoid(pre[:, D:])


def _tap_groups(offset):
    groups = {}
    for j in range(CONV_W):
        off = offset(j)
        groups.setdefault(off % 8, []).append((off - off % 8, j))
    return [(phase, sorted(taps)) for phase, taps in sorted(groups.items())]


def _shift_window(dst_ref, win_ref, phase, rows):
    dst_ref[0:rows, :] = win_ref[phase:phase + rows, :]


def _layernorm(u2):
    mu = _mean1(u2)
    xc = u2 - mu
    rstd = lax.rsqrt(_mean1(xc * xc) + EPS)
    return xc * rstd, rstd


def _conv_out_fwd(pre, h, cw, w2, comm=None):
    S = h.shape[0]
    hb = TS // HALO

    def body(pre_ref, ph_ref, h_ref, cw_ref, w2_ref, u2_ref, z_ref, ho_ref, win_ref, sh_ref):
        i = pl.program_id(0)
        cwv = cw_ref[...]
        win_ref[0:HALO, :] = jnp.where(i > 0, _glu(ph_ref[...]), 0.0)
        win_ref[HALO:HALO + TS, :] = _glu(pre_ref[...])
        u2 = jnp.broadcast_to(cwv[31:32], (TS, D))
        for phase, taps in _tap_groups(lambda j: HALO - (CONV_W - 1) + j):
            _shift_window(sh_ref, win_ref, phase, taps[-1][0] + TS)
            for lo, j in taps:
                u2 = u2 + cwv[j:j + 1] * sh_ref[lo:lo + TS, :]
        u2_ref[...] = u2
        xh, _ = _layernorm(u2)
        un = xh * cwv[32:33] + cwv[33:34]
        zb = (un * _sigmoid(un)).astype(BF)
        z_ref[...] = zb
        y = _dot(zb, w2_ref[...]) + cwv[34:35]
        ho_ref[...] = h_ref[...] + cwv[35:36] * y

    tok = lambda i: (i, 0)
    return _hosted_call(
        body, (pre, pre, h, cw, w2), comm, name="conv_out_fwd", n_steps=S // TS,
        out_shape=[jax.ShapeDtypeStruct((S, D), F32), jax.ShapeDtypeStruct((S, D), BF), jax.ShapeDtypeStruct((S, D), F32)],
        in_specs=[_rows((TS, 2 * D), tok), _rows((HALO, 2 * D), lambda i: (jnp.maximum(i * hb - 1, 0), 0)),
                  _rows((TS, D), tok), _rows((40, D), lambda i: (0, 0)), VMEM_FULL],
        out_specs=[_rows((TS, D), tok), _rows((TS, D), tok), _rows((TS, D), tok)],
        scratch_shapes=[pltpu.VMEM((TS + HALO, D), F32)] * 2)


def _conv_out_bwd(dho, u2, cw, w2):
    S = dho.shape[0]

    def body(dho_ref, u2_ref, cw_ref, w2_ref, du2_ref, dhb_ref, ps_ref):
        dho_v, cwv = dho_ref[...], cw_ref[...]
        dhb_ref[...] = dho_v.astype(BF)
        dz = _dot_nt((cwv[35:36] * dho_v).astype(BF), w2_ref[...])
        xh, rstd = _layernorm(u2_ref[...])
        un = xh * cwv[32:33] + cwv[33:34]
        sig = _sigmoid(un)
        dun = dz * (sig * (1.0 + un * (1.0 - sig)))
        dxh = dun * cwv[32:33]
        du2_ref[...] = rstd * (dxh - _mean1(dxh) - xh * _mean1(dxh * xh))
        _accumulate(ps_ref, pl.program_id(0) == 0, [_sum0(dun * xh), _sum0(dun), _sum0(dho_v)])

    tok = lambda i: (i, 0)
    return pl.pallas_call(
        body, name="conv_out_bwd", grid=(S // TS,),
        out_shape=[jax.ShapeDtypeStruct((S, D), F32), jax.ShapeDtypeStruct((S, D), BF), jax.ShapeDtypeStruct((8, D), F32)],
        in_specs=[_rows((TS, D), tok), _rows((TS, D), tok), _rows((40, D), lambda i: (0, 0)), VMEM_FULL],
        out_specs=[_rows((TS, D), tok), _rows((TS, D), tok), _rows((8, D), lambda i: (0, 0))],
        compiler_params=_params(1),
    )(dho, u2, cw, w2)


def _conv_in_bwd(du2, pre, h, dho, vec, cw, w1):
    S = h.shape[0]
    n_t = S // TS
    hb = TS // HALO

    def body(du2_ref, dh2h_ref, pre_ref, ph_ref, h_ref, dho_ref, vec_ref, cw_ref, w1_ref,
             dh_ref, dpre_ref, ps_ref, pw_ref, pb_ref, winu_ref, wind_ref, sh_ref):
        i = pl.program_id(0)
        vec, cwv = vec_ref[...], cw_ref[...]
        pre = pre_ref[...]
        av, sg = pre[:, :D], _sigmoid(pre[:, D:])
        winu_ref[0:HALO, :] = jnp.where(i > 0, _glu(ph_ref[...]), 0.0)
        winu_ref[HALO:HALO + TS, :] = av * sg
        du2v = du2_ref[...]
        wind_ref[0:TS, :] = du2v
        wind_ref[TS:TS + HALO, :] = jnp.where(i < n_t - 1, dh2h_ref[...], 0.0)

        @pl.when(i == 0)
        def _():
            pw_ref[...] = jnp.zeros((32, D), F32)

        for phase, taps in _tap_groups(lambda j: HALO - (CONV_W - 1) + j):
            _shift_window(sh_ref, winu_ref, phase, taps[-1][0] + TS)
            for lo, j in taps:
                pw_ref[j:j + 1, :] += _sum0(du2v * sh_ref[lo:lo + TS, :])
        pw_ref[31:32, :] += _sum0(du2v)
        du1 = jnp.zeros((TS, D), F32)
        for phase, taps in _tap_groups(lambda j: CONV_W - 1 - j):
            _shift_window(sh_ref, wind_ref, phase, taps[-1][0] + TS)
            for lo, j in taps:
                du1 = du1 + cwv[j:j + 1] * sh_ref[lo:lo + TS, :]
        da = du1 * sg
        dg = du1 * av * sg * (1.0 - sg)
        dab, dgb = da.astype(BF), dg.astype(BF)
        dpre_ref[:, :D] = dab
        dpre_ref[:, D:] = dgb

        @pl.when(i == 0)
        def _():
            pb_ref[...] = jnp.zeros((8, 2 * D), F32)

        pb_ref[0:1, :D] += _sum0(da)
        pb_ref[0:1, D:] += _sum0(dg)
        du = _dot_nt(dab, w1_ref[:, :D]) + _dot_nt(dgb, w1_ref[:, D:])
        _, xhat, r = _norm_mod(h_ref[...], vec)
        dhn, dgn, dsh, dsc = _norm_mod_bwd(du, xhat, r, vec)
        dh_ref[...] = dho_ref[...] + dhn
        _accumulate(ps_ref, i == 0, [dgn, dsh, dsc])

    tok = lambda i: (i, 0)
    fixed = lambda i: (0, 0)
    return pl.pallas_call(
        body, name="conv_in_bwd", grid=(n_t,),
        out_shape=[jax.ShapeDtypeStruct((S, D), F32), jax.ShapeDtypeStruct((S, 2 * D), BF),
                   jax.ShapeDtypeStruct((8, D), F32), jax.ShapeDtypeStruct((32, D), F32),
                   jax.ShapeDtypeStruct((8, 2 * D), F32)],
        in_specs=[_rows((TS, D), tok), _rows((HALO, D), lambda i: (jnp.minimum((i + 1) * hb, S // HALO - 1), 0)),
                  _rows((TS, 2 * D), tok), _rows((HALO, 2 * D), lambda i: (jnp.maximum(i * hb - 1, 0), 0)),
                  _rows((TS, D), tok), _rows((TS, D), tok), _rows((8, D), fixed), _rows((40, D), fixed), VMEM_FULL],
        out_specs=[_rows((TS, D), tok), _rows((TS, 2 * D), tok), _rows((8, D), fixed), _rows((32, D), fixed),
                   _rows((8, 2 * D), fixed)],
        scratch_shapes=[pltpu.VMEM((TS + HALO, D), F32)] * 3,
        compiler_params=_params(1),
    )(du2, du2, pre, pre, h, dho, vec, cw, w1)


def _lane():
    return lax.broadcasted_iota(jnp.int32, (TS, HP), 1)


def _kv_fwd(h, vec, wkva, g2, wkvb, rope):
    S = h.shape[0]

    def body(h_ref, vec_ref, wa_ref, g2_ref, wb_ref, rope_ref, hn_ref, ckv_ref, ckn_ref, k_ref, v_ref):
        u, _, _ = _norm_mod(h_ref[...], vec_ref[...])
        ub = u.astype(BF)
        hn_ref[...] = ub
        kva = _dot(ub, wa_ref[...])
        ckv = kva[:, :KV_LORA]
        ckv_ref[...] = ckv
        r2 = lax.rsqrt(_mean1(ckv * ckv) + EPS)
        cknb = ((ckv * r2) * g2_ref[0:1, :]).astype(BF)
        ckn_ref[...] = cknb
        kvb = _dot(cknb, wb_ref[...])
        kpe = _rope(kva[:, KV_LORA:KVA_P], rope_ref[...])
        lane = _lane()
        ones_lane0 = jnp.where(lane == 0, 1.0, 0.0)
        for hd in range(NH):
            blk = kvb[:, hd * HP:(hd + 1) * HP]
            k_ref[:, hd * HP:(hd + 1) * HP] = jnp.where(lane < 64, blk, kpe).astype(BF)
            v_ref[:, hd * HP:(hd + 1) * HP] = jnp.where(lane >= 64, blk, ones_lane0).astype(BF)

    tok = lambda i: (i, 0)
    fixed = lambda i: (0, 0)
    return pl.pallas_call(
        body, name="kv_fwd", grid=(S // TS,),
        out_shape=[jax.ShapeDtypeStruct((S, D), BF), jax.ShapeDtypeStruct((S, KV_LORA), F32),
                   jax.ShapeDtypeStruct((S, KV_LORA), BF), jax.ShapeDtypeStruct((S, NH * HP), BF),
                   jax.ShapeDtypeStruct((S, NH * HP), BF)],
        in_specs=[_rows((TS, D), tok), _rows((8, D), fixed), VMEM_FULL, _rows((8, KV_LORA), fixed), VMEM_FULL,
                  _rows((TS, 3 * HP), tok)],
        out_specs=[_rows((TS, D), tok), _rows((TS, KV_LORA), tok), _rows((TS, KV_LORA), tok),
                   _rows((TS, NH * HP), tok), _rows((TS, NH * HP), tok)],
        compiler_params=_params(1),
    )(h, vec, wkva, g2, wkvb, rope)


def _kv_bwd(dk, dv, h, ckv, dho, vec, wkva, g2, wkvb, rope):
    S = h.shape[0]

    def body(dk_ref, dv_ref, h_ref, ckv_ref, dho_ref, vec_ref, wa_ref, g2_ref, wb_ref, rope_ref,
             dh_ref, dkva_ref, dkvb_ref, ps_ref, ps2_ref):
        i = pl.program_id(0)
        vec = vec_ref[...]
        lane = _lane()
        dkpe = jnp.zeros((TS, HP), F32)
        for hd in range(NH):
            dkh = dk_ref[:, hd * HP:(hd + 1) * HP]
            dvh = dv_ref[:, hd * HP:(hd + 1) * HP]
            dkvb_ref[:, hd * HP:(hd + 1) * HP] = jnp.where(lane < 64, dkh, dvh).astype(BF)
            dkpe = dkpe + jnp.where(lane >= 64, dkh, 0.0)
        dkpe = _rope_t(dkpe, rope_ref[...])
        dckn = _dot_nt(dkvb_ref[...], wb_ref[...])
        ckv = ckv_ref[...]
        r2 = lax.rsqrt(_mean1(ckv * ckv) + EPS)
        dckv, dg2 = _rms_bwd(dckn, ckv, r2, g2_ref[0:1, :])
        dkva_ref[:, :KV_LORA] = dckv.astype(BF)
        dkva_ref[:, KV_LORA:KVA_P] = dkpe.astype(BF)
        du = _dot_nt(dkva_ref[...], wa_ref[...])
        _, xhat, r = _norm_mod(h_ref[...], vec)
        dhn, dgn, dsh, dsc = _norm_mod_bwd(du, xhat, r, vec)
        dh_ref[...] = dho_ref[...] + dhn
        _accumulate(ps_ref, i == 0, [dgn, dsh, dsc])
        _accumulate(ps2_ref, i == 0, [dg2])

    tok = lambda i: (i, 0)
    fixed = lambda i: (0, 0)
    return pl.pallas_call(
        body, name="kv_bwd", grid=(S // TS,),
        out_shape=[jax.ShapeDtypeStruct((S, D), F32), jax.ShapeDtypeStruct((S, KVA_P), BF),
                   jax.ShapeDtypeStruct((S, NH * HP), BF), jax.ShapeDtypeStruct((8, D), F32),
                   jax.ShapeDtypeStruct((8, KV_LORA), F32)],
        in_specs=[_rows((TS, NH * HP), tok), _rows((TS, NH * HP), tok), _rows((TS, D), tok), _rows((TS, KV_LORA), tok),
                  _rows((TS, D), tok), _rows((8, D), fixed), VMEM_FULL, _rows((8, KV_LORA), fixed), VMEM_FULL,
                  _rows((TS, 3 * HP), tok)],
        out_specs=[_rows((TS, D), tok), _rows((TS, KVA_P), tok), _rows((TS, NH * HP), tok), _rows((8, D), fixed),
                   _rows((8, KV_LORA), fixed)],
        compiler_params=_params(1),
    )(dk, dv, h, ckv, dho, vec, wkva, g2, wkvb, rope)


def _q_fwd(h, vec, wqa, g2, wqb, rope):
    S = h.shape[0]

    def body(h_ref, vec_ref, wa_ref, g2_ref, wb_ref, rope_ref, hn_ref, qa_ref, qan_ref, q_ref):
        u, _, _ = _norm_mod(h_ref[...], vec_ref[...])
        ub = u.astype(BF)
        hn_ref[...] = ub
        qa = _dot(ub, wa_ref[...])
        qa_ref[...] = qa
        r2 = lax.rsqrt(_mean1(qa * qa) + EPS)
        qanb = ((qa * r2) * g2_ref[0:1, :]).astype(BF)
        qan_ref[...] = qanb
        q = _dot(qanb, wb_ref[...])
        tab = rope_ref[...]
        for hd in range(NH):
            q_ref[:, hd * HP:(hd + 1) * HP] = (_rope(q[:, hd * HP:(hd + 1) * HP], tab) * EXP2_SCALE).astype(BF)

    tok = lambda i: (i, 0)
    fixed = lambda i: (0, 0)
    return pl.pallas_call(
        body, name="q_fwd", grid=(S // TS,),
        out_shape=[jax.ShapeDtypeStruct((S, D), BF), jax.ShapeDtypeStruct((S, Q_LORA), F32),
                   jax.ShapeDtypeStruct((S, Q_LORA), BF), jax.ShapeDtypeStruct((S, NH * HP), BF)],
        in_specs=[_rows((TS, D), tok), _rows((8, D), fixed), VMEM_FULL, _rows((8, Q_LORA), fixed), VMEM_FULL,
                  _rows((TS, 3 * HP), tok)],
        out_specs=[_rows((TS, D), tok), _rows((TS, Q_LORA), tok), _rows((TS, Q_LORA), tok), _rows((TS, NH * HP), tok)],
        compiler_params=_params(1),
    )(h, vec, wqa, g2, wqb, rope)


def _q_bwd(dq, h, qa, dho, vec, wqa, g2, wqb, rope):
    S = h.shape[0]

    def body(dq_ref, h_ref, qa_ref, dho_ref, vec_ref, wa_ref, g2_ref, wb_ref, rope_ref,
             dh_ref, dqb_ref, dqa_ref, ps_ref, ps2_ref):
        i = pl.program_id(0)
        vec, tab = vec_ref[...], rope_ref[...]
        for hd in range(NH):
            dqb_ref[:, hd * HP:(hd + 1) * HP] = _rope_t(dq_ref[:, hd * HP:(hd + 1) * HP], tab).astype(BF)
        dqan = _dot_nt(dqb_ref[...], wb_ref[...])
        qa = qa_ref[...]
        r2 = lax.rsqrt(_mean1(qa * qa) + EPS)
        dqa, dg2 = _rms_bwd(dqan, qa, r2, g2_ref[0:1, :])
        dqab = dqa.astype(BF)
        dqa_ref[...] = dqab
        du = _dot_nt(dqab, wa_ref[...])
        _, xhat, r = _norm_mod(h_ref[...], vec)
        dhn, dgn, dsh, dsc = _norm_mod_bwd(du, xhat, r, vec)
        dh_ref[...] = dho_ref[...] + dhn
        _accumulate(ps_ref, i == 0, [dgn, dsh, dsc])
        _accumulate(ps2_ref, i == 0, [dg2])

    tok = lambda i: (i, 0)
    fixed = lambda i: (0, 0)
    return pl.pallas_call(
        body, name="q_bwd", grid=(S // TS,),
        out_shape=[jax.ShapeDtypeStruct((S, D), F32), jax.ShapeDtypeStruct((S, NH * HP), BF),
                   jax.ShapeDtypeStruct((S, Q_LORA), BF), jax.ShapeDtypeStruct((8, D), F32),
                   jax.ShapeDtypeStruct((8, Q_LORA), F32)],
        in_specs=[_rows((TS, NH * HP), tok), _rows((TS, D), tok), _rows((TS, Q_LORA), tok), _rows((TS, D), tok),
                  _rows((8, D), fixed), VMEM_FULL, _rows((8, Q_LORA), fixed), VMEM_FULL, _rows((TS, 3 * HP), tok)],
        out_specs=[_rows((TS, D), tok), _rows((TS, NH * HP), tok), _rows((TS, Q_LORA), tok), _rows((8, D), fixed),
                   _rows((8, Q_LORA), fixed)],
        compiler_params=_params(1),
    )(dq, h, qa, dho, vec, wqa, g2, wqb, rope)


def _attn_fwd(q, k, v):
    S = q.shape[0]
    TA = TA_FWD
    nq = S // TA
    rg = min(TA, ATT_ROWS)
    groups = TA // rg

    def softmax_pv(scores, vt, state, masks):
        out = []
        for g in range(groups):
            m, acc = state[g]
            s = scores[g] if masks is None else jnp.where(masks[g], scores[g], NEG)
            m_new = jnp.maximum(m, jnp.max(s, axis=1, keepdims=True))
            p = jnp.exp2(s - m_new)
            vg = vt[g] if isinstance(vt, list) else vt
            out.append((m_new, jnp.exp2(m - m_new) * acc + _dot(p.astype(BF), vg)))
        return tuple(out)

    def body(q_ref, k_ref, v_ref, o_ref, lse_ref):
        qi = pl.program_id(1)
        qs = [q_ref[g * rg:(g + 1) * rg, :] for g in range(groups)]

        def keys(j):
            return pl.ds(pl.multiple_of(j * TA, TA), TA)

        def scores_of(j):
            kt = k_ref[keys(j), :]
            return tuple(_dot_nt(qs[g], kt) for g in range(groups))

        state = tuple((jnp.full((rg, 1), NEG, F32), jnp.zeros((rg, HP), F32)) for _ in range(groups))

        def step(kj, state):
            return softmax_pv(scores_of(kj), v_ref[keys(kj), :], state, None)

        state = lax.fori_loop(0, qi, step, state)
        kd, vd = k_ref[keys(qi), :], v_ref[keys(qi), :]
        ends = [(g + 1) * rg for g in range(groups)]
        diag_scores = tuple(_dot_nt(qs[g], kd[:ends[g]]) for g in range(groups))
        masks = [lax.broadcasted_iota(jnp.int32, (rg, ends[g]), 1)
                 <= lax.broadcasted_iota(jnp.int32, (rg, ends[g]), 0) + g * rg for g in range(groups)]
        final = softmax_pv(diag_scores, [vd[:ends[g]] for g in range(groups)], state, masks)
        lane = lax.broadcasted_iota(jnp.int32, (rg, HP), 1)
        for g in range(groups):
            m, acc = final[g]
            l = jnp.sum(jnp.where(lane == 0, acc, 0.0), axis=1, keepdims=True)
            o_ref[g * rg:(g + 1) * rg, :] = (acc / l).astype(BF)
            lse_ref[g * rg:(g + 1) * rg, :] = m + jnp.log(l) * LOG2E

    return pl.pallas_call(
        body, name="attn_fwd", grid=(NH, nq),
        out_shape=[jax.ShapeDtypeStruct((S, NH * HP), BF), jax.ShapeDtypeStruct((NH, S, 1), F32)],
        in_specs=[_rows((TA, HP), lambda h, i: (i, h)), _rows((S, HP), lambda h, i: (0, h)),
                  _rows((S, HP), lambda h, i: (0, h))],
        out_specs=[_rows((TA, HP), lambda h, i: (i, h)), _rows((None, TA, 1), lambda h, i: (h, i, 0))],
        compiler_params=_params(2),
    )(q, k, v)


def _attn_bwd(q, k, v, do, lse, delta):
    S = q.shape[0]
    nq = S // TA
    rg_loop = min(TA, ATT_ROWS_BWD)
    rg_diag = min(TA, ATT_ROWS)

    def body(q_ref, do_ref, lse_ref, dl_ref, k_ref, v_ref, dq_ref, dk_ref, dv_ref, dka_ref, dva_ref):
        kj = pl.program_id(1)

        @pl.when(kj == 0)
        def _():
            dq_ref[...] = jnp.zeros((S, HP), F32)

        dka_ref[...] = jnp.zeros((TA, HP), F32)
        dva_ref[...] = jnp.zeros((TA, HP), F32)
        kt, vt = k_ref[...], v_ref[...]

        def tile(qi, diagonal):
            rg = rg_diag if diagonal else rg_loop
            groups = TA // rg
            rows = [pl.ds(pl.multiple_of(qi * TA + g * rg, rg), rg) for g in range(groups)]
            ends = [(g + 1) * rg if diagonal else TA for g in range(groups)]
            qg = [q_ref[r, :] for r in rows]
            dog = [do_ref[r, :] for r in rows]
            scores = [_dot_nt(qg[g], kt[:ends[g]]) for g in range(groups)]
            dps = [_dot_nt(dog[g], vt[:ends[g]]) for g in range(groups)]
            for g in range(groups):
                p = jnp.exp2(scores[g] - lse_ref[rows[g], :])
                if diagonal:
                    col = lax.broadcasted_iota(jnp.int32, (rg, ends[g]), 1)
                    row = lax.broadcasted_iota(jnp.int32, (rg, ends[g]), 0)
                    p = jnp.where(col <= row + g * rg, p, 0.0)
                ds = p * (dps[g] - dl_ref[rows[g], :])
                pb, dsb = p.astype(BF), ds.astype(BF)
                dva_ref[0:ends[g], :] += _dot_tn(pb, dog[g])
                dka_ref[0:ends[g], :] += _dot_tn(dsb, qg[g])
                dq_ref[rows[g], :] += _dot(dsb, kt[:ends[g]]) * SM_SCALE

        tile(kj, True)

        def step(qi, carry):
            tile(qi, False)
            return carry

        lax.fori_loop(kj + 1, nq, step, 0)
        dk_ref[...] = dka_ref[...] * LN2
        dv_ref[...] = dva_ref[...]

    head = lambda h, j: (0, h)
    col1 = lambda h, j: (h, 0, 0)
    return pl.pallas_call(
        body, name="attn_bwd", grid=(NH, nq), out_shape=[jax.ShapeDtypeStruct((S, NH * HP), F32)] * 3,
        in_specs=[_rows((S, HP), head), _rows((S, HP), head), _rows((None, S, 1), col1), _rows((None, S, 1), col1),
                  _rows((TA, HP), lambda h, j: (j, h)), _rows((TA, HP), lambda h, j: (j, h))],
        out_specs=[_rows((S, HP), head), _rows((TA, HP), lambda h, j: (j, h)), _rows((TA, HP), lambda h, j: (j, h))],
        scratch_shapes=[pltpu.VMEM((TA, HP), F32), pltpu.VMEM((TA, HP), F32)], compiler_params=_params(2),
    )(q, do, lse, delta, k, v)


def _attn_out_fwd(o, h, wo, vec):
    S = h.shape[0]

    def body(o_ref, h_ref, wo_ref, vec_ref, ho_ref):
        ho_ref[...] = h_ref[...] + vec_ref[3:4, :] * _dot(o_ref[...], wo_ref[...])

    tok = lambda i: (i, 0)
    return pl.pallas_call(
        body, name="attn_out_fwd", grid=(S // TS,), out_shape=jax.ShapeDtypeStruct((S, D), F32),
        in_specs=[_rows((TS, NH * HP), tok), _rows((TS, D), tok), VMEM_FULL, _rows((8, D), lambda i: (0, 0))],
        out_specs=_rows((TS, D), tok), compiler_params=_params(1),
    )(o, h, wo, vec)


def _attn_out_bwd(dho, o, wo, vec):
    S = dho.shape[0]

    def body(dho_ref, o_ref, wo_ref, vec_ref, do_ref, dl_ref, dhb_ref):
        dho_v = dho_ref[...]
        dhb_ref[...] = dho_v.astype(BF)
        do = _dot_nt((vec_ref[3:4, :] * dho_v).astype(BF), wo_ref[...])
        do_ref[...] = do.astype(BF)
        prod = do * o_ref[...].astype(F32)
        for hd in range(NH):
            dl_ref[hd] = jnp.sum(prod[:, hd * HP:(hd + 1) * HP], axis=1, keepdims=True)

    tok = lambda i: (i, 0)
    return pl.pallas_call(
        body, name="attn_out_bwd", grid=(S // TS,),
        out_shape=[jax.ShapeDtypeStruct((S, NH * HP), BF), jax.ShapeDtypeStruct((NH, S, 1), F32),
                   jax.ShapeDtypeStruct((S, D), BF)],
        in_specs=[_rows((TS, D), tok), _rows((TS, NH * HP), tok), VMEM_FULL, _rows((8, D), lambda i: (0, 0))],
        out_specs=[_rows((TS, NH * HP), tok), _rows((NH, TS, 1), lambda i: (0, i, 0)), _rows((TS, D), tok)],
        compiler_params=_params(1),
    )(dho, o, wo, vec)


def _final(h, target, fg):
    S = h.shape[0]

    def body(h_ref, t_ref, g_ref, dh_ref, ps_ref):
        hv, g = h_ref[...], g_ref[0:1, :]
        r = lax.rsqrt(_mean1(hv * hv) + EPS)
        xhat = hv * r
        err = xhat * g - t_ref[...]
        loss = 0.5 * jnp.sum(_mean1(err * err), axis=0, keepdims=True)
        dy = err * (1.0 / D)
        dxhat = dy * g
        dh_ref[...] = r * (dxhat - xhat * _mean1(dxhat * xhat))
        _accumulate(ps_ref, pl.program_id(0) == 0, [_sum0(dy * xhat), jnp.broadcast_to(loss, (1, D))])

    tok = lambda i: (i, 0)
    return pl.pallas_call(
        body, name="final_loss", grid=(S // TS,),
        out_shape=[jax.ShapeDtypeStruct((S, D), F32), jax.ShapeDtypeStruct((8, D), F32)],
        in_specs=[_rows((TS, D), tok), _rows((TS, D), tok), _rows((8, D), lambda i: (0, 0))],
        out_specs=[_rows((TS, D), tok), _rows((8, D), lambda i: (0, 0))], compiler_params=_params(1),
    )(h, target, fg)


def _row_tile(r):
    if r <= 512:
        return r
    for t in range(512, 7, -8):
        if r % t == 0:
            return t
    return r


def _adamw(parts, w, m, v):
    P, R, C = parts.shape
    tr = _row_tile(R)

    def body(p_ref, w_ref, m_ref, v_ref, g_ref, d_ref, mo_ref, vo_ref):
        g = p_ref[0].astype(F32)
        for k in range(1, P):
            g = g + p_ref[k].astype(F32)
        g_ref[...] = g
        m2 = B1 * m_ref[...] + (1.0 - B1) * g
        v2 = B2 * v_ref[...] + (1.0 - B2) * (g * g)
        mo_ref[...] = m2
        vo_ref[...] = v2
        m_hat = m2 / (1.0 - B1 ** STEP)
        v_hat = v2 / (1.0 - B2 ** STEP)
        d_ref[...] = -LR * (m_hat / (jnp.sqrt(v_hat) + EPS_ADAM) + WD * w_ref[...])

    blk = _rows((tr, C), lambda i: (i, 0))
    return pl.pallas_call(
        body, name="adamw", grid=(R // tr,), out_shape=[jax.ShapeDtypeStruct((R, C), F32)] * 4,
        in_specs=[_rows((P, tr, C), lambda i: (0, i, 0)), blk, blk, blk], out_specs=[blk] * 4,
        compiler_params=_params(1),
    )(parts, w, m, v)


_WEIGHTS = ['ada_w', 'ada_b', 'norm_g', 'ffn_w13', 'ffn_w2', 'conv_w_pw1', 'conv_b_pw1', 'conv_w_dw', 'conv_b_dw',
            'conv_ln_g', 'conv_ln_b', 'conv_w_pw2', 'conv_b_pw2', 'kv_ada_w', 'kv_ada_b', 'kv_norm_g', 'w_kv_a',
            'kv_a_norm_g', 'w_kv_b', 'w_q_a', 'q_a_norm_g', 'w_q_b', 'w_o', 'final_norm_g']


def _vec(rows):
    rows = [r.reshape(1, -1).astype(F32) for r in rows]
    return jnp.concatenate(rows + [jnp.zeros((8 - len(rows), rows[0].shape[1]), F32)], axis=0)


def kernel(x, c, positions, ada_w, ada_b, norm_g, ffn_w13, ffn_w2, conv_w_pw1, conv_b_pw1, conv_w_dw, conv_b_dw, conv_ln_g, conv_ln_b, conv_w_pw2, conv_b_pw2, kv_ada_w, kv_ada_b, kv_norm_g, w_kv_a, kv_a_norm_g, w_kv_b, w_q_a, q_a_norm_g, w_q_b, w_o, final_norm_g, loss_target, m_ada_w, m_ada_b, m_norm_g, m_ffn_w13, m_ffn_w2, m_conv_w_pw1, m_conv_b_pw1, m_conv_w_dw, m_conv_b_dw, m_conv_ln_g, m_conv_ln_b, m_conv_w_pw2, m_conv_b_pw2, m_kv_ada_w, m_kv_ada_b, m_kv_norm_g, m_w_kv_a, m_kv_a_norm_g, m_w_kv_b, m_w_q_a, m_q_a_norm_g, m_w_q_b, m_w_o, m_final_norm_g, v_ada_w, v_ada_b, v_norm_g, v_ffn_w13, v_ffn_w2, v_conv_w_pw1, v_conv_b_pw1, v_conv_w_dw, v_conv_b_dw, v_conv_ln_g, v_conv_ln_b, v_conv_w_pw2, v_conv_b_pw2, v_kv_ada_w, v_kv_ada_b, v_kv_norm_g, v_w_kv_a, v_kv_a_norm_g, v_w_kv_b, v_w_q_a, v_q_a_norm_g, v_w_q_b, v_w_o, v_final_norm_g):
    given = dict(locals())
    S = x.shape[1]
    me = 4 * lax.axis_index("x") + 2 * lax.axis_index("y") + lax.axis_index("c")

    small = jnp.concatenate([
        conv_w_dw[0], conv_b_dw, conv_ln_g, conv_ln_b, conv_b_pw2,
        norm_g.reshape(6, 128), conv_b_pw1.reshape(2, 128),
        c.reshape(8, 128), jnp.zeros((5, 128), F32)], axis=0)
    bf = lambda w: w.astype(BF)
    full_w13 = lambda g: jnp.transpose(g.reshape(N_DEV, D, 704), (1, 0, 2)).reshape(D, 2 * DFF)
    full_w2 = lambda g: g.reshape(DFF, D)
    got = _gather_two_level([small, bf(ffn_w13[0, 0]), bf(ffn_w2[0, 0])], "gather_first")
    w13_00, w2_00 = full_w13(got[1]), full_w2(got[2])
    sm = got[0]
    chan = lambda lo, hi: jnp.moveaxis(sm[:, lo:hi, :], 0, 1).reshape(hi - lo, D)
    w_dw_f, b_dw_f, ln_g_f, ln_b_f, b_pw2_f = chan(0, 31), chan(31, 32), chan(32, 33), chan(33, 34), chan(34, 35)
    norm_f = chan(35, 41).reshape(2, 3, D)
    b_pw1_f = sm[:, 41:43, :].reshape(1, 2 * D)
    c_all = sm[:, 43:51, :].reshape(N_DEV, D)

    n_ada = ada_w.shape[2]
    n_kva = kv_ada_w.shape[1]
    modp = _mod_fwd(c_all, ada_w[0], ada_w[1], kv_ada_w)
    (modr,) = _exchange([(modp.reshape(N_DEV, 1, 2 * n_ada + n_kva), "scatter")], "scatter_mod")
    modr = modr[:, 0, :]
    mod = jnp.transpose(modr[:, :2 * n_ada].reshape(N_DEV, 2, n_ada), (1, 0, 2)).reshape(2, 9 * D) + ada_b
    mod = mod.reshape(2, 9, D)
    kvmod = (modr[:, 2 * n_ada:].reshape(2 * D) + kv_ada_b).reshape(2, D)

    def sub_vec(l, idx):
        return _vec([norm_f[l, idx], mod[l, 3 * idx], mod[l, 3 * idx + 1], mod[l, 3 * idx + 2]])

    vec_kv = _vec([kv_norm_g, kvmod[0], kvmod[1]])
    cw = jnp.concatenate([w_dw_f, b_dw_f, ln_g_f, ln_b_f, b_pw2_f, mod[0, 5].reshape(1, D), jnp.zeros((4, D), F32)], axis=0)
    b1v = _vec([b_pw1_f])
    g_kva = _vec([kv_a_norm_g])
    g_qa = _vec([q_a_norm_g[0]])
    fgv = _vec([final_norm_g])

    inv_freq = 10000.0 ** (-jnp.arange(0, ROPE, 2, dtype=F32) / ROPE)
    ang = positions[0].astype(F32)[:, None] * inv_freq
    cs, sn = jnp.cos(ang), jnp.sin(ang)
    z16, z32, z64 = jnp.zeros((S, 16), F32), jnp.zeros((S, 32), F32), jnp.zeros((S, 64), F32)
    rope = jnp.concatenate([jnp.ones((S, 64), F32), cs, cs, z32,
                            z64, z16, sn, z32,
                            z64, -sn, z16, z32], axis=1)

    def merged(blocks, name):
        return [jnp.swapaxes(m, 0, 1).reshape((N_DEV,) + m.shape[2:]) for m in _sibling_merge(blocks, name)]

    h0 = x[0]
    group1 = [bf(conv_w_pw1[0]), bf(conv_w_pw2[0]), bf(ffn_w13[0, 1]), bf(ffn_w2[0, 1])]
    (h1, u00, a00, b00), blocks1 = _ffn_fwd(h0, sub_vec(0, 0), w13_00, w2_00, comm=(group1, False))
    g_pw1, g_pw2, g_w13, g_w2 = merged(blocks1, "merge_group1")
    pw1_f = jnp.transpose(g_pw1, (1, 0, 2)).reshape(D, 2 * D)
    pw2_f = g_pw2.reshape(D, D)
    w13_01, w2_01 = full_w13(g_w13), full_w2(g_w2)
    hn_c, pre = _conv_in_fwd(h1, sub_vec(0, 1), pw1_f, b1v)
    group2 = [bf(w_kv_a), bf(w_kv_b), bf(ffn_w13[1, 0]), bf(ffn_w2[1, 0])]
    (u2, z_c, h2), blocks2 = _conv_out_fwd(pre, h1, cw, pw2_f, comm=(group2, False))
    g_kva_w, g_kvb_w, g_w13, g_w2 = merged(blocks2, "merge_group2")
    wkva = g_kva_w.reshape(D, KV_LORA + ROPE)
    wkva_f = jnp.concatenate([wkva[:, :KV_LORA], jnp.zeros((D, 64), BF), wkva[:, KV_LORA:], jnp.zeros((D, 32), BF)], axis=1)
    wkvb_f = jnp.transpose(g_kvb_w, (1, 0, 2)).reshape(KV_LORA, NH * HP)
    w13_10, w2_10 = full_w13(g_w13), full_w2(g_w2)
    group3 = [bf(w_q_a[0]), bf(w_q_b[0]), bf(w_o[0]), bf(ffn_w13[1, 1]), bf(ffn_w2[1, 1])]
    (h3, u01, a01, b01), blocks3 = _ffn_fwd(h2, sub_vec(0, 2), w13_01, w2_01, comm=(group3, False))
    g_qa_w, g_qb_w, g_wo, g_w13, g_w2 = merged(blocks3, "merge_group3")
    wqa_f = g_qa_w.reshape(D, Q_LORA)
    wqb = jnp.transpose(g_qb_w, (1, 0, 2)).reshape(Q_LORA, NH, 96)
    wqb_f = jnp.pad(wqb, ((0, 0), (0, 0), (0, HP - 96))).reshape(Q_LORA, NH * HP)
    wo_f = jnp.pad(g_wo.reshape(NH, 64, D), ((0, 0), (64, 0), (0, 0))).reshape(NH * HP, D)
    w13_11, w2_11 = full_w13(g_w13), full_w2(g_w2)
    hn_kv, ckv, ckn, k_all, v_all = _kv_fwd(h3, vec_kv, wkva_f, g_kva, wkvb_f, rope)
    (h4, u10, a10, b10), _ = _ffn_fwd(h3, sub_vec(1, 0), w13_10, w2_10)
    hn_q, qa, qan, q_all = _q_fwd(h4, sub_vec(1, 1), wqa_f, g_qa, wqb_f, rope)
    o_all, lse = _attn_fwd(q_all, k_all, v_all)
    h5 = _attn_out_fwd(o_all, h4, wo_f, sub_vec(1, 1))
    (h6, u11, a11, b11), _ = _ffn_fwd(h5, sub_vec(1, 2), w13_11, w2_11)

    dh6, ps_fin = _final(h6, loss_target[0], fgv)
    loss = lax.psum(ps_fin[1, 0], ("x", "y", "c"))

    core = lax.axis_index("c").reshape(1).astype(jnp.int32)

    def pair_sums(sends, name):
        by_core = [s.reshape((4, 2) + s.shape[1:]).swapaxes(0, 1) for s in sends]
        from_sibling = _pair_swap(by_core, name)
        return [_pair_add(core, a.reshape(2, -1, a.shape[-1]), b.reshape(-1, b.shape[-1])).reshape(b.shape)
                for a, b in zip(by_core, from_sibling)]

    def ffn_back(dho, h_in, u, a, b, l, i, w13, w2, comm=None):
        vec = sub_vec(l, 2 * i)
        (dh, da, db, t, dhb, ps), reduced = _ffn_bwd(dho, h_in, a, b, vec, w13, w2, comm=comm)
        dwa = _wgrad(u, da, 512, DFF, "wgrad_w13")
        dwb = _wgrad(u, db, 512, DFF, "wgrad_w13")
        dw2, gs = _wgrad(t, dhb, CH, D, "wgrad_w2", gate=(w2, _vec([0.5 * vec[3]])))
        dgate = 0.5 * jnp.sum(gs[:, 0, :], axis=0)
        send13 = jnp.transpose(jnp.concatenate([dwa, dwb], axis=1).reshape(D, N_DEV, 704), (1, 0, 2))
        return dh, send13, dw2.reshape(N_DEV, 352, D), ps, dgate, reduced

    dh5, s13_11, s2_11, ps11, dg11, _ = ffn_back(dh6, h5, u11, a11, b11, 1, 1, w13_11, w2_11)
    vec_m1 = sub_vec(1, 1)
    do_all, delta, dhb5 = _attn_out_bwd(dh5, o_all, wo_f, vec_m1)
    dwo_p, gs_o = _wgrad(o_all, dhb5, D, D, "wgrad_wo", gate=(wo_f, _vec([vec_m1[3]])))
    dgm1 = jnp.sum(gs_o[:, 0, :], axis=0)
    dq_all, dk_all, dv_all = _attn_bwd(q_all, k_all, v_all, do_all, lse, delta)
    dh4, dqb, dqab, ps_q, ps_q2 = _q_bwd(dq_all, h4, qa, dh5, vec_m1, wqa_f, g_qa, wqb_f, rope)
    dwqb_p = _wgrad(qan, dqb, Q_LORA, D, "wgrad_wqb")
    dwqa = _wgrad(hn_q, dqab, D, Q_LORA, "wgrad_wqa")
    sums_a = pair_sums([
        s13_11, s2_11, dwqa.reshape(N_DEV, 128, Q_LORA),
        jnp.transpose(dwqb_p.reshape(Q_LORA, NH, HP)[:, :, :96].reshape(Q_LORA, N_DEV, 192), (1, 0, 2)),
        dwo_p.reshape(NH, HP, D)[:, 64:, :].reshape(N_DEV, 128, D)], "pair_swap_a")
    dh3a, s13_10, s2_10, ps10, dg10, red_a = ffn_back(dh4, h3, u10, a10, b10, 1, 0, w13_10, w2_10, comm=(sums_a, True))
    sums_a2 = pair_sums([s13_10, s2_10], "pair_swap_a2")
    dh3, dkva, dkvb, ps_kv, ps_kv2 = _kv_bwd(dk_all, dv_all, h3, ckv, dh3a, vec_kv, wkva_f, g_kva, wkvb_f, rope)
    dwkva_p = _wgrad(hn_kv, dkva, D, KVA_P, "wgrad_wkva")
    dwkvb = _wgrad(ckn, dkvb, KV_LORA, D, "wgrad_wkvb")
    dh2, s13_01, s2_01, ps01, dg01, red_a2 = ffn_back(dh3, h2, u01, a01, b01, 0, 1, w13_01, w2_01, comm=(sums_a2, True))
    vec_m0 = sub_vec(0, 1)
    du2, dhb2, ps_co = _conv_out_bwd(dh2, u2, cw, pw2_f)
    dpw2, gs_c = _wgrad(z_c, dhb2, D, D, "wgrad_pw2", gate=(pw2_f, _vec([vec_m0[3]])))
    dgm0 = jnp.sum(gs_c[:, 0, :], axis=0) + b_pw2_f[0] * ps_co[2]
    dh1, dpre, ps_ci, ps_dw, ps_b1 = _conv_in_bwd(du2, pre, h1, dh2, vec_m0, cw, pw1_f)
    dpw1 = _wgrad(hn_c, dpre, D, D, "wgrad_pw1")
    sums_b = pair_sums([
        s13_01, s2_01, jnp.transpose(dpw1.reshape(D, N_DEV, 256), (1, 0, 2)), dpw2.reshape(N_DEV, 128, D),
        jnp.concatenate([dwkva_p[:, :KV_LORA], dwkva_p[:, KV_LORA + 64:KV_LORA + 96]], axis=1).reshape(N_DEV, 128, KV_LORA + ROPE),
        jnp.transpose(dwkvb.reshape(KV_LORA, N_DEV, 256), (1, 0, 2))], "pair_swap_b")
    dh0, s13_00, s2_00, ps00, dg00, red_b = ffn_back(dh1, h0, u00, a00, b00, 0, 0, w13_00, w2_00, comm=(sums_b, True))
    red_c = _chip_scatter(pair_sums([s13_00, s2_00], "pair_swap_c"), "chip_scatter_last")
    r13_11, r2_11, r_wqa, r_wqb, r_wo = red_a
    r13_10, r2_10 = red_a2
    r13_01, r2_01, r_pw1, r_pw2, r_wkva, r_wkvb = red_b
    r13_00, r2_00 = red_c

    dmod = jnp.stack([
        jnp.stack([ps00[1], ps00[2], dg00, ps_ci[1], ps_ci[2], dgm0, ps01[1], ps01[2], dg01]),
        jnp.stack([ps10[1], ps10[2], dg10, ps_q[1], ps_q[2], dgm1, ps11[1], ps11[2], dg11])])
    dnorm = jnp.stack([ps00[0], ps_ci[0], ps01[0], ps10[0], ps_q[0], ps11[0]])
    pieces = [dnorm, ps_b1[0], ps_dw[0:31], ps_dw[31], ps_co[0], ps_co[1], vec_m0[3] * ps_co[2],
              ps_kv[0], ps_kv2[0], ps_q2[0], ps_fin[0], dmod, ps_kv[1], ps_kv[2]]
    sizes = [int(np.prod(p.shape)) for p in pieces]
    offs = np.concatenate([[0], np.cumsum(sizes)]).astype(int)
    flat = jnp.concatenate([p.reshape(-1) for p in pieces]).reshape(-1, 128)
    (part,) = _exchange([(flat, "gather")], "gather_partials")
    part = part.reshape(N_DEV, -1)

    def piece(i, rows, cols):
        return part[:, offs[i]:offs[i + 1]].reshape(N_DEV, rows, cols)

    def mine(i, rows):
        return lax.dynamic_slice_in_dim(piece(i, rows, D), me * 128, 128, axis=2)

    dmod_all = piece(11, 2, 9 * D)
    c_t = jnp.transpose(c_all)
    g_ada = jnp.stack([_mod_wgrad(c_t, lax.dynamic_slice_in_dim(dmod_all[:, l], me * n_ada, n_ada, axis=1))
                       for l in range(2)])
    dkvmod_all = jnp.concatenate([piece(12, 1, D)[:, 0], piece(13, 1, D)[:, 0]], axis=1)
    g_kvada = _mod_wgrad(c_t, lax.dynamic_slice_in_dim(dkvmod_all, me * n_kva, n_kva, axis=1))

    parts = {
        'ada_w': g_ada.reshape(1, 2 * D, n_ada),
        'ada_b': dmod_all,
        'norm_g': mine(0, 6),
        'ffn_w13': jnp.stack([r13_00, r13_01, r13_10, r13_11], axis=1).reshape(4, 4 * D, 704),
        'ffn_w2': jnp.stack([r2_00, r2_01, r2_10, r2_11], axis=1).reshape(4, 4 * 352, D),
        'conv_w_pw1': r_pw1,
        'conv_b_pw1': lax.dynamic_slice_in_dim(piece(1, 1, 2 * D), me * 256, 256, axis=2),
        'conv_w_dw': mine(2, 31),
        'conv_b_dw': mine(3, 1),
        'conv_ln_g': mine(4, 1),
        'conv_ln_b': mine(5, 1),
        'conv_w_pw2': r_pw2,
        'conv_b_pw2': mine(6, 1),
        'kv_ada_w': g_kvada.reshape(1, D, n_kva),
        'kv_ada_b': dkvmod_all.reshape(N_DEV, 1, 2 * D),
        'kv_norm_g': piece(7, 1, D),
        'w_kv_a': r_wkva,
        'kv_a_norm_g': piece(8, 1, KV_LORA),
        'w_kv_b': r_wkvb,
        'w_q_a': r_wqa,
        'q_a_norm_g': piece(9, 1, Q_LORA),
        'w_q_b': r_wqb,
        'w_o': r_wo,
        'final_norm_g': piece(10, 1, D),
    }
    grads, deltas, new_m, new_v = [], [], [], []
    for name in _WEIGHTS:
        w = given[name]
        p = parts[name]
        shape2 = p.shape[1:]
        g, dlt, m2, v2 = _adamw(p, w.reshape(shape2), given['m_' + name].reshape(shape2), given['v_' + name].reshape(shape2))
        grads.append(g.reshape(w.shape))
        deltas.append(dlt.reshape(w.shape))
        new_m.append(m2.reshape(w.shape))
        new_v.append(v2.reshape(w.shape))
    return (loss, dh0.reshape(1, S, D), *grads, *deltas, *new_m, *new_v)
```
